```python
import math
import jax, jax.numpy as jnp
from jax import lax
import numpy as np


D_MODEL = 2048
BATCH = 8
SEQ = 4096
DEPTH = 2

N_META = 16
HEAD_DIM = 128
A_HEADS = 4
A_WIDTH = A_HEADS * HEAD_DIM
HGRN_CHUNK = 64
B_WIDTH = 512
CONV_W = 3
C_HEADS = 4
C_HALF = HEAD_DIM // 2
C_WIDTH = C_HEADS * HEAD_DIM
D_HEADS = 4
D_WIDTH = D_HEADS * HEAD_DIM
IDX_HEADS = 16
IDX_DIM = 64
TOPK_MAX = 256
N_BRANCH = 4
BRANCH_WIDTH = 512
D_FF = 5632
N_BUCKETS = 32
MAX_DISTANCE = 128
Q_BLOCK = 128
EPS = 1e-6

IN_SPLITS = (A_WIDTH, A_WIDTH, A_WIDTH, A_WIDTH,
             B_WIDTH, B_WIDTH, B_WIDTH,
             C_WIDTH, C_WIDTH, C_WIDTH,
             D_WIDTH, HEAD_DIM, HEAD_DIM,
             IDX_HEADS * IDX_DIM, IDX_DIM, IDX_HEADS,
             N_BRANCH * D_MODEL)
IN_COLS = sum(IN_SPLITS)

kernel_name = "hybrid_gated_hgrn2_conv_diffattn_dsa_macaron"


def rmsnorm(x, g):
    xf = x.astype(jnp.float32)
    inv = lax.rsqrt(jnp.mean(xf * xf, axis=-1, keepdims=True) + EPS)
    return (xf * inv).astype(x.dtype) * g


def swiglu_ffn(h, w_gu, w_down):
    gate, up = jnp.split(h @ w_gu, 2, axis=-1)
    return (jax.nn.silu(gate) * up) @ w_down


def rel_bucket(q_pos, k_pos):
    n = jnp.maximum(q_pos - k_pos, 0)
    max_exact = N_BUCKETS // 2
    nf = jnp.maximum(n, 1).astype(jnp.float32)
    large = max_exact + (jnp.log(nf / max_exact) / math.log(MAX_DISTANCE / max_exact)
                         * (N_BUCKETS - max_exact)).astype(jnp.int32)
    large = jnp.minimum(large, N_BUCKETS - 1)
    return jnp.where(n < max_exact, n, large)


def map_query_blocks(fn, q_arrays, seq_len):
    pad = (-seq_len) % Q_BLOCK
    pos = jnp.concatenate([jnp.zeros((pad,), jnp.int32), jnp.arange(seq_len, dtype=jnp.int32)])
    n_blk = (seq_len + pad) // Q_BLOCK

    def to_blocks(a):
        a = jnp.pad(a, [(0, 0), (pad, 0)] + [(0, 0)] * (a.ndim - 2))
        a = a.reshape((a.shape[0], n_blk, Q_BLOCK) + a.shape[2:])
        return jnp.moveaxis(a, 1, 0)

    xs = tuple(to_blocks(a) for a in q_arrays) + (pos.reshape(n_blk, Q_BLOCK),)
    out = lax.map(lambda args: fn(*args), xs)
    out = jnp.moveaxis(out, 0, 1)
    out = out.reshape((out.shape[0], n_blk * Q_BLOCK) + out.shape[3:])
    return out[:, pad:]


def hgrn2_chunk(S, q, log_f, k, v):
    b = jnp.cumsum(log_f, axis=2)
    o_inter = jnp.einsum('bhtd,bhde->bhte', q * jnp.exp(b), S)
    c = q.shape[2]
    causal = jnp.tril(jnp.ones((c, c), bool))
    diff = b[:, :, :, None, :] - b[:, :, None, :, :]
    decay = jnp.exp(jnp.where(causal[:, :, None], diff, -jnp.inf))
    scores = jnp.einsum('bhtd,bhsd,bhtsd->bhts', q, k, decay)
    o = o_inter + jnp.einsum('bhts,bhse->bhte', scores, v)
    b_last = b[:, :, -1:, :]
    k_dec = k * jnp.exp(b_last - b)
    S_new = jnp.exp(b_last[:, :, 0, :])[..., None] * S + jnp.einsum('bhsd,bhse->bhde', k_dec, v)
    return S_new, o


def hgrn2_mixer(q_raw, f_raw, i_raw, g_raw, lb, gnorm):
    Bb, L, _ = q_raw.shape

    def heads(t):
        return t.astype(jnp.float32).reshape(Bb, L, A_HEADS, HEAD_DIM).transpose(0, 2, 1, 3)

    lb = lb.astype(jnp.float32).reshape(1, A_HEADS, 1, HEAD_DIM)
    z = heads(f_raw)
    log_f = jnp.logaddexp(jnp.log(lb), jnp.log1p(-lb) + jax.nn.log_sigmoid(z))
    k = (1.0 - lb) * jax.nn.sigmoid(-z)
    q = jax.nn.silu(heads(q_raw))
    v = heads(i_raw)
    S0 = jnp.zeros((Bb, A_HEADS, HEAD_DIM, HEAD_DIM), jnp.float32)
    m = N_META
    S_meta, o_meta = hgrn2_chunk(S0, q[:, :, :m], log_f[:, :, :m], k[:, :, :m], v[:, :, :m])

    def to_chunks(t):
        r = t[:, :, m:]
        n = r.shape[2] // HGRN_CHUNK
        return jnp.moveaxis(r.reshape(Bb, A_HEADS, n, HGRN_CHUNK, HEAD_DIM), 2, 0)

    _, o_real = lax.scan(lambda S, xs: hgrn2_chunk(S, *xs), S_meta,
                         (to_chunks(q), to_chunks(log_f), to_chunks(k), to_chunks(v)))
    o_real = jnp.moveaxis(o_real, 0, 2).reshape(Bb, A_HEADS, L - m, HEAD_DIM)
    o = jnp.concatenate([o_meta, o_real], axis=2).transpose(0, 2, 1, 3)
    o = rmsnorm(o, gnorm.astype(jnp.float32).reshape(A_HEADS, HEAD_DIM)).reshape(Bb, L, A_WIDTH)
    return o.astype(g_raw.dtype) * jax.nn.silu(g_raw)


def short_conv_mixer(b_gate, c_gate, u, conv_w):
    zc = c_gate * u
    L = zc.shape[1]
    zp = jnp.pad(zc, ((0, 0), (CONV_W - 1, 0), (0, 0)))
    y = conv_w[0] * zp[:, CONV_W - 1:CONV_W - 1 + L]
    for j in range(1, CONV_W):
        y = y + conv_w[j] * zp[:, CONV_W - 1 - j:CONV_W - 1 - j + L]
    return b_gate * y


def diff_attention(c_q, c_k, c_v, q_norm, k_norm, lam_params, subln, table, layer):
    Bb, L, _ = c_q.shape
    q = rmsnorm(c_q.reshape(Bb, L, C_HEADS, 2, C_HALF), q_norm) * (C_HALF ** -0.5)
    k = rmsnorm(c_k.reshape(Bb, L, C_HEADS, 2, C_HALF), k_norm)
    v = c_v.reshape(Bb, L, C_HEADS, HEAD_DIM)
    lp = lam_params.astype(jnp.float32)
    lam_init = 0.8 - 0.6 * math.exp(-0.3 * layer)
    lam = jnp.exp(jnp.sum(lp[0] * lp[1])) - jnp.exp(jnp.sum(lp[2] * lp[3])) + lam_init
    k_pos = jnp.arange(L, dtype=jnp.int32)

    def block(q_blk, q_pos):
        bias = jnp.take(table, rel_bucket(q_pos[:, None], k_pos[None, :]), axis=0)
        logits = (jnp.einsum('bqhcd,bkhcd->bchqk', q_blk, k).astype(jnp.float32)
                  + jnp.transpose(bias, (2, 0, 1)).astype(jnp.float32))
        logits = jnp.where(k_pos[None, :] <= q_pos[:, None], logits, -jnp.inf)
        p = jax.nn.softmax(logits, axis=-1)
        p_diff = p[:, 0] - lam * p[:, 1]
        return jnp.einsum('bhqk,bkhe->bqhe', p_diff.astype(v.dtype), v)

    o = map_query_blocks(block, (q,), L)
    o = rmsnorm(o, subln) * (1.0 - lam_init)
    return o.reshape(Bb, L, C_WIDTH)


def dsa_attention(d_q, d_k, d_v, d_qi, d_ki, d_w, q_norm, k_norm, table, top_k):
    Bb, L, _ = d_q.shape
    q = rmsnorm(d_q.reshape(Bb, L, D_HEADS, HEAD_DIM), q_norm) * (HEAD_DIM ** -0.5)
    k = rmsnorm(d_k, k_norm)
    v = d_v
    qi = d_qi.reshape(Bb, L, IDX_HEADS, IDX_DIM).astype(jnp.float32) * (IDX_DIM ** -0.5)
    ki = d_ki.astype(jnp.float32)
    wi = d_w.astype(jnp.float32) * (IDX_HEADS ** -0.5)
    k_pos = jnp.arange(L, dtype=jnp.int32)
    gather = jax.vmap(lambda t, i: t[i])

    def block(q_blk, qi_blk, wi_blk, q_pos):
        score = jax.nn.relu(jnp.einsum('bqhd,bkd->bqhk', qi_blk, ki))
        index = jnp.einsum('bqhk,bqh->bqk', score, wi_blk)
        index = jnp.where(k_pos[None, None, :] <= q_pos[None, :, None], index, -jnp.inf)
        _, sel = lax.top_k(index, top_k)
        k_sel = gather(k, sel)
        v_sel = gather(v, sel)
        bias = jnp.take(table, rel_bucket(q_pos[None, :, None], sel), axis=0)
        logits = (jnp.einsum('bqhd,bqkd->bhqk', q_blk, k_sel).astype(jnp.float32)
                  + jnp.transpose(bias, (0, 3, 1, 2)).astype(jnp.float32))
        logits = jnp.where((sel <= q_pos[None, :, None])[:, None], logits, -jnp.inf)
        p = jax.nn.softmax(logits, axis=-1)
        return jnp.einsum('bhqk,bqkd->bqhd', p.astype(v.dtype), v_sel)

    o = map_query_blocks(block, (q, qi, wi), L)
    return o.reshape(Bb, L, D_WIDTH)


def hybrid_mixer(h, layer, top_k, rel_bias, w_in, lb, hgrn_gnorm, conv_w, diff_q_norm,
                 diff_k_norm, diff_lambda, diff_subln, dsa_q_norm, dsa_k_norm, w_branch, w_out):
    Bb, L, _ = h.shape
    offsets = np.cumsum(np.array(IN_SPLITS))[:-1].tolist()
    (a_q, a_f, a_i, a_g, b_b, b_c, b_u, c_q, c_k, c_v,
     d_q, d_k, d_v, d_qi, d_ki, d_w, gate_logits) = jnp.split(h @ w_in, offsets, axis=-1)
    branches = (
        hgrn2_mixer(a_q, a_f, a_i, a_g, lb, hgrn_gnorm),
        short_conv_mixer(b_b, b_c, b_u, conv_w),
        diff_attention(c_q, c_k, c_v, diff_q_norm, diff_k_norm, diff_lambda, diff_subln,
                       rel_bias[:, :C_HEADS], layer),
        dsa_attention(d_q, d_k, d_v, d_qi, d_ki, d_w, dsa_q_norm, dsa_k_norm,
                      rel_bias[:, C_HEADS:], top_k),
    )
    gates = jax.nn.sigmoid(gate_logits.reshape(Bb, L, N_BRANCH, D_MODEL))
    merged = gates[:, :, 0] * (branches[0] @ w_branch[0])
    for m in range(1, N_BRANCH):
        merged = merged + gates[:, :, m] * (branches[m] @ w_branch[m])
    return merged @ w_out


def setup_inputs(seed: int = 0) -> dict:
    key = jax.random.key(seed)
    ks = jax.random.split(key, 24)
    f32 = jnp.float32

    def nrm(k, shape, scale):
        return jax.random.normal(k, shape, f32) * scale

    def gain(k, shape):
        return 1.0 + 0.05 * jax.random.normal(k, shape, f32)

    return {
        "x": nrm(ks[0], (BATCH, SEQ, D_MODEL), 1.0),
        "meta_tokens": nrm(ks[1], (N_META, D_MODEL), 1.0),
        "rel_bias": nrm(ks[2], (N_BUCKETS, C_HEADS + D_HEADS), 0.2),
        "ffn1_norm": gain(ks[3], (DEPTH, D_MODEL)),
        "ffn1_w_gu": nrm(ks[4], (DEPTH, D_MODEL, 2 * D_FF), D_MODEL ** -0.5),
        "ffn1_w_down": nrm(ks[5], (DEPTH, D_FF, D_MODEL), D_FF ** -0.5),
        "mix_norm": gain(ks[6], (DEPTH, D_MODEL)),
        "w_in": nrm(ks[7], (DEPTH, D_MODEL, IN_COLS), D_MODEL ** -0.5),
        "hgrn_lb": nrm(ks[8], (DEPTH, A_WIDTH), 1.0),
        "hgrn_gnorm": gain(ks[9], (DEPTH, A_WIDTH)),
        "conv_w": nrm(ks[10], (DEPTH, CONV_W, B_WIDTH), CONV_W ** -0.5),
        "diff_q_norm": gain(ks[11], (DEPTH, C_HALF)),
        "diff_k_norm": gain(ks[12], (DEPTH, C_HALF)),
        "diff_lambda": nrm(ks[13], (DEPTH, 4, C_HALF), 0.1),
        "diff_subln": gain(ks[14], (DEPTH, HEAD_DIM)),
        "dsa_q_norm": gain(ks[15], (DEPTH, HEAD_DIM)),
        "dsa_k_norm": gain(ks[16], (DEPTH, HEAD_DIM)),
        "w_branch": nrm(ks[17], (DEPTH, N_BRANCH, BRANCH_WIDTH, D_MODEL), BRANCH_WIDTH ** -0.5),
        "w_out": nrm(ks[18], (DEPTH, D_MODEL, D_MODEL), D_MODEL ** -0.5),
        "ffn2_norm": gain(ks[19], (DEPTH, D_MODEL)),
        "ffn2_w_gu": nrm(ks[20], (DEPTH, D_MODEL, 2 * D_FF), D_MODEL ** -0.5),
        "ffn2_w_down": nrm(ks[21], (DEPTH, D_FF, D_MODEL), D_FF ** -0.5),
    }


def reference(x, meta_tokens, rel_bias, ffn1_norm, ffn1_w_gu, ffn1_w_down, mix_norm, w_in,
              hgrn_lb, hgrn_gnorm, conv_w, diff_q_norm, diff_k_norm, diff_lambda, diff_subln,
              dsa_q_norm, dsa_k_norm, w_branch, w_out, ffn2_norm, ffn2_w_gu, ffn2_w_down):
    Bb, seq, _ = x.shape
    top_k = min(TOPK_MAX, seq // 4)
    meta = jnp.broadcast_to(meta_tokens.astype(x.dtype)[None], (Bb, N_META, D_MODEL))
    h = jnp.concatenate([meta, x], axis=1)
    lbs = jnp.cumsum(jax.nn.softmax(hgrn_lb.astype(jnp.float32), axis=0), axis=0)
    lbs = lbs - lbs[0:1]
    for l in range(DEPTH):
        h = h + 0.5 * swiglu_ffn(rmsnorm(h, ffn1_norm[l]), ffn1_w_gu[l], ffn1_w_down[l])
        h = h + hybrid_mixer(rmsnorm(h, mix_norm[l]), l, top_k, rel_bias, w_in[l], lbs[l],
                             hgrn_gnorm[l], conv_w[l], diff_q_norm[l], diff_k_norm[l],
                             diff_lambda[l], diff_subln[l], dsa_q_norm[l], dsa_k_norm[l],
                             w_branch[l], w_out[l])
        h = h + 0.5 * swiglu_ffn(rmsnorm(h, ffn2_norm[l]), ffn2_w_gu[l], ffn2_w_down[l])
    return h[:, N_META:]
```

```python
import functools
import math

import jax
import jax.numpy as jnp
import numpy as np
from jax import lax
from jax.experimental import pallas as pl
from jax.experimental.pallas import tpu as pltpu

F32 = jnp.float32
BF16 = jnp.bfloat16
I32 = jnp.int32

EPS = 1e-6
N_META = 16
HEAD_DIM = 128
HALF = 64
N_HEADS = 4
MIX_WIDTH = N_HEADS * HEAD_DIM
IDX_HEADS = 16
IDX_DIM = 64
TOPK_MAX = 256
N_BRANCH = 4
N_BUCKETS = 32
MAX_DISTANCE = 128
CONV_W = 3
SEQ_TILE = 128
HGRN_CHUNK = 16
MASKED = -1e30
INT_MIN = -(2 ** 31)
VMEM_LIMIT_BYTES = 56 * 1024 * 1024

COL_A = 0
COL_QI = 2048
COL_B = 3072
COL_C = 4608
COL_DQ = 6144
COL_DK = 6656
COL_DV = 6784
COL_KIW = 6912
P_COLS = 7168


def _cparams(*sem):
    return pltpu.CompilerParams(dimension_semantics=sem, vmem_limit_bytes=VMEM_LIMIT_BYTES)


def _dot(a, b):
    return jnp.dot(a, b, preferred_element_type=F32)


def _dot_nt(a, b):
    return lax.dot_general(a, b, (((1,), (1,)), ((), ())), preferred_element_type=F32)


def _dot_tn(a, b):
    return lax.dot_general(a, b, (((0,), (0,)), ((), ())), preferred_element_type=F32)


def _rms_inv(x):
    return lax.rsqrt(jnp.mean(x * x, axis=-1, keepdims=True) + EPS)


def _ffn_body(h_ref, g_ref, wg_ref, wu_ref, wd_ref, o_ref, xn_ref):
    j = pl.program_id(1)

    @pl.when(j == 0)
    def _():
        x = h_ref[...]
        xn_ref[...] = ((x * _rms_inv(x)) * g_ref[...]).astype(BF16)
        o_ref[...] = x

    xn = xn_ref[...]
    g = _dot(xn, wg_ref[...])
    u = _dot(xn, wu_ref[...])
    a = (g * jax.nn.sigmoid(g) * (0.5 * u)).astype(BF16)
    o_ref[...] += _dot(a, wd_ref[...])


def _ffn(h, gain, w_gu, w_down, tm, tf):
    T, D = h.shape
    F = w_down.shape[0]
    nf = F // tf
    return pl.pallas_call(
        _ffn_body,
        grid=(T // tm, nf),
        in_specs=[
            pl.BlockSpec((tm, D), lambda i, j: (i, 0)),
            pl.BlockSpec((1, D), lambda i, j: (0, 0)),
            pl.BlockSpec((D, tf), lambda i, j: (0, j)),
            pl.BlockSpec((D, tf), lambda i, j: (0, j + nf)),
            pl.BlockSpec((tf, D), lambda i, j: (j, 0)),
        ],
        out_specs=pl.BlockSpec((tm, D), lambda i, j: (i, 0)),
        out_shape=jax.ShapeDtypeStruct((T, D), F32),
        scratch_shapes=[pltpu.VMEM((tm, D), BF16)],
        compiler_params=_cparams("parallel", "arbitrary"),
        name="ffn",
    )(h, gain.reshape(1, D), w_gu, w_gu, w_down)


def _proj_body(h_ref, g_ref, w_ref, xn_ref, p_ref):
    @pl.when(pl.program_id(1) == 0)
    def _():
        x = h_ref[...]
        xn_ref[...] = ((x * _rms_inv(x)) * g_ref[...]).astype(BF16)

    p_ref[...] = _dot(xn_ref[...], w_ref[...])


def _proj(h, gain, w, tm, tn):
    T, D = h.shape
    N = w.shape[1]
    return pl.pallas_call(
        _proj_body,
        grid=(T // tm, N // tn),
        in_specs=[
            pl.BlockSpec((tm, D), lambda i, j: (i, 0)),
            pl.BlockSpec((1, D), lambda i, j: (0, 0)),
            pl.BlockSpec((D, tn), lambda i, j: (0, j)),
        ],
        out_specs=[
            pl.BlockSpec((tm, D), lambda i, j: (i, 0)),
            pl.BlockSpec((tm, tn), lambda i, j: (i, j)),
        ],
        out_shape=[jax.ShapeDtypeStruct((T, D), BF16), jax.ShapeDtypeStruct((T, N), F32)],
        compiler_params=_cparams("parallel", "arbitrary"),
        name="proj",
    )(h, gain.reshape(1, D), w)


def _hgrn_body(q_ref, f_ref, i_ref, g_ref, loglb_ref, log1mlb_ref, omlb_ref, gn_ref, o_ref,
               st_ref, qs_ref, ks_ref, bs_ref, os_ref):
    R = q_ref.shape[0]
    C = HGRN_CHUNK

    @pl.when(pl.program_id(1) == 0)
    def _():
        st_ref[...] = jnp.zeros_like(st_ref)

    z = f_ref[...]
    log_sig = jnp.minimum(z, 0.0) - jnp.log1p(jnp.exp(-jnp.abs(z)))
    y = log1mlb_ref[...] + log_sig
    a = loglb_ref[...]
    log_f = jnp.maximum(a, y) + jnp.log1p(jnp.exp(-jnp.abs(a - y)))
    row = lax.broadcasted_iota(I32, z.shape, 0) % C
    b = log_f
    for sh in (1, 2, 4, 8):
        b = b + jnp.where(row >= sh, pltpu.roll(b, sh, axis=0), 0.0)
    bs_ref[...] = b
    ks_ref[...] = omlb_ref[...] * jax.nn.sigmoid(-z)
    qr = q_ref[...]
    qs_ref[...] = qr * jax.nn.sigmoid(qr)

    t_iota = lax.broadcasted_iota(I32, (C, 1), 0)

    def chunk(c, carry):
        r0 = pl.multiple_of(c * C, C)
        for h in range(N_HEADS):
            hs = slice(h * HEAD_DIM, (h + 1) * HEAD_DIM)
            qc = qs_ref[pl.ds(r0, C), hs]
            kc = ks_ref[pl.ds(r0, C), hs]
            bc = bs_ref[pl.ds(r0, C), hs]
            vc = i_ref[pl.ds(r0, C), hs]
            st = st_ref[h]
            o = _dot_nt((qc * jnp.exp(bc)).astype(BF16), st.astype(BF16))
            for s in range(C):
                b_s = bc[s:s + 1, :]
                k_s = kc[s:s + 1, :]
                v_s = vc[s:s + 1, :]
                e = jnp.exp(jnp.minimum(bc - b_s, 0.0))
                col = jnp.sum(qc * k_s * e, axis=-1, keepdims=True)
                col = jnp.where(t_iota >= s, col, 0.0)
                o = o + col * v_s
            os_ref[pl.ds(r0, C), hs] = o
            b_last = bc[C - 1:C, :]
            kd = kc * jnp.exp(b_last - bc)
            st_ref[h] = st * jnp.exp(b_last) + _dot_tn(vc.astype(BF16), kd.astype(BF16))
        return carry

    lax.fori_loop(0, R // C, chunk, 0)

    for h in range(N_HEADS):
        hs = slice(h * HEAD_DIM, (h + 1) * HEAD_DIM)
        o = os_ref[:, hs]
        on = (o * _rms_inv(o)) * gn_ref[:, hs]
        gr = g_ref[:, hs]
        o_ref[:, hs] = (on * (gr * jax.nn.sigmoid(gr))).astype(o_ref.dtype)


def _hgrn(P3, lb, gnorm):
    B, Lp, _ = P3.shape
    R = SEQ_TILE
    W = MIX_WIDTH
    lb = lb.reshape(1, W).astype(F32)
    vec = pl.BlockSpec((1, W), lambda b, t: (0, 0))
    c0 = COL_A // W
    return pl.pallas_call(
        _hgrn_body,
        grid=(B, Lp // R),
        in_specs=[pl.BlockSpec((None, R, W), functools.partial(lambda b, t, c: (b, t, c), c=c0 + k))
                  for k in range(4)] + [vec, vec, vec, vec],
        out_specs=pl.BlockSpec((None, R, W), lambda b, t: (b, t, 0)),
        out_shape=jax.ShapeDtypeStruct((B, Lp, W), BF16),
        scratch_shapes=[pltpu.VMEM((N_HEADS, HEAD_DIM, HEAD_DIM), F32)]
        + [pltpu.VMEM((R, W), F32)] * 4,
        compiler_params=_cparams("parallel", "arbitrary"),
        name="hgrn",
    )(P3, P3, P3, P3, jnp.log(lb), jnp.log1p(-lb), 1.0 - lb, gnorm.reshape(1, W).astype(F32))


def _conv_body(b_ref, c_ref, u_ref, w_ref, o_ref, carry_ref):
    @pl.when(pl.program_id(1) == 0)
    def _():
        carry_ref[...] = jnp.zeros_like(carry_ref)

    zc = c_ref[...] * u_ref[...]
    R = zc.shape[0]
    row = lax.broadcasted_iota(I32, zc.shape, 0)
    last = carry_ref[7:8, :]
    last2 = carry_ref[6:7, :]
    z1 = jnp.where(row == 0, last, pltpu.roll(zc, 1, axis=0))
    z2 = jnp.where(row == 0, last2, jnp.where(row == 1, last, pltpu.roll(zc, 2, axis=0)))
    y = w_ref[0:1, :] * zc + w_ref[1:2, :] * z1 + w_ref[2:3, :] * z2
    o_ref[...] = (b_ref[...] * y).astype(o_ref.dtype)
    carry_ref[...] = zc[R - 8:, :]


def _conv(P3, conv_w):
    B, Lp, _ = P3.shape
    R = SEQ_TILE
    W = MIX_WIDTH
    c0 = COL_B // W
    w8 = jnp.zeros((8, W), F32).at[:CONV_W].set(conv_w.astype(F32))
    return pl.pallas_call(
        _conv_body,
        grid=(B, Lp // R),
        in_specs=[pl.BlockSpec((None, R, W), functools.partial(lambda b, t, c: (b, t, c), c=c0 + k))
                  for k in range(3)] + [pl.BlockSpec((8, W), lambda b, t: (0, 0))],
        out_specs=pl.BlockSpec((None, R, W), lambda b, t: (b, t, 0)),
        out_shape=jax.ShapeDtypeStruct((B, Lp, W), BF16),
        scratch_shapes=[pltpu.VMEM((8, W), F32)],
        compiler_params=_cparams("parallel", "arbitrary"),
        name="conv",
    )(P3, P3, P3, w8)


def _prep_body(cq_ref, ck_ref, cv_ref, dq_ref, dk_ref, dv_ref, kiw_ref, qi_ref,
               cqg_ref, ckg_ref, dqg_ref, dkg_ref,
               cqn_ref, ckn_ref, cvb_ref, dqn_ref, dkn_ref, dvb_ref, qib_ref, kk_ref, wq_ref):
    lane = lax.broadcasted_iota(I32, (1, HEAD_DIM), 1)
    lo = lane < HALF

    def half_norm(x, g, scale):
        sq = x * x
        ms_lo = jnp.sum(jnp.where(lo, sq, 0.0), axis=-1, keepdims=True) * (1.0 / HALF)
        ms_hi = jnp.sum(jnp.where(lo, 0.0, sq), axis=-1, keepdims=True) * (1.0 / HALF)
        inv = jnp.where(lo, lax.rsqrt(ms_lo + EPS), lax.rsqrt(ms_hi + EPS))
        return ((x * inv) * g) * scale

    for h in range(N_HEADS):
        hs = slice(h * HEAD_DIM, (h + 1) * HEAD_DIM)
        cqn_ref[:, hs] = half_norm(cq_ref[:, hs], cqg_ref[...], HALF ** -0.5).astype(BF16)
        ckn_ref[:, hs] = half_norm(ck_ref[:, hs], ckg_ref[...], 1.0).astype(BF16)
        x = dq_ref[:, hs]
        dqn_ref[:, hs] = (((x * _rms_inv(x)) * dqg_ref[...]) * (HEAD_DIM ** -0.5)).astype(BF16)
    cvb_ref[...] = cv_ref[...].astype(BF16)
    x = dk_ref[...]
    dkn_ref[...] = ((x * _rms_inv(x)) * dkg_ref[...]).astype(BF16)
    dvb_ref[...] = dv_ref[...].astype(BF16)
    qib_ref[...] = (qi_ref[...] * (IDX_DIM ** -0.5)).astype(BF16)
    kiw = kiw_ref[...]
    swapped = pltpu.roll(kiw, HALF, axis=1)
    kk_ref[...] = jnp.where(lo, kiw, swapped).astype(BF16)
    wq_ref[...] = swapped * (IDX_HEADS ** -0.5)


def _prep(P, cq_g, ck_g, dq_g, dk_g, tm):
    T = P.shape[0]
    W = MIX_WIDTH
    H = HEAD_DIM

    def col(width, start):
        return pl.BlockSpec((tm, width), functools.partial(lambda i, c: (i, c), c=start // width))

    vec = pl.BlockSpec((1, H), lambda i: (0, 0))
    out = lambda width: pl.BlockSpec((tm, width), lambda i: (i, 0))
    return pl.pallas_call(
        _prep_body,
        grid=(T // tm,),
        in_specs=[col(W, COL_C), col(W, COL_C + W), col(W, COL_C + 2 * W), col(W, COL_DQ),
                  col(H, COL_DK), col(H, COL_DV), col(H, COL_KIW), col(IDX_HEADS * IDX_DIM, COL_QI),
                  vec, vec, vec, vec],
        out_specs=[out(W), out(W), out(W), out(W), out(H), out(H), out(IDX_HEADS * IDX_DIM), out(H), out(H)],
        out_shape=[jax.ShapeDtypeStruct((T, W), BF16)] * 4
        + [jax.ShapeDtypeStruct((T, H), BF16)] * 2
        + [jax.ShapeDtypeStruct((T, IDX_HEADS * IDX_DIM), BF16),
           jax.ShapeDtypeStruct((T, H), BF16), jax.ShapeDtypeStruct((T, H), F32)],
        compiler_params=_cparams("parallel"),
        name="prep",
    )(P, P, P, P, P, P, P, P,
      jnp.tile(cq_g.astype(F32), 2).reshape(1, H), jnp.tile(ck_g.astype(F32), 2).reshape(1, H),
      dq_g.astype(F32).reshape(1, H), dk_g.astype(F32).reshape(1, H))


def _flash_update(s, v_tile, m_ref, l_ref, acc_ref):
    m_prev = m_ref[...]
    m_next = jnp.maximum(m_prev, jnp.max(s, axis=-1, keepdims=True))
    p = jnp.exp(s - m_next[:, :1])
    alpha = jnp.exp(m_prev - m_next)
    l_ref[...] = alpha * l_ref[...] + jnp.sum(p, axis=-1, keepdims=True)
    acc_ref[...] = alpha * acc_ref[...] + _dot(p.astype(BF16), v_tile)
    m_ref[...] = m_next


def _diff_body(q_ref, k_ref, v_ref, near_ref, far_ref, lam_ref, sub_ref, o_ref,
               q2_ref, m_ref, l_ref, acc_ref, *, out_scale):
    i = pl.program_id(2)
    TQ = SEQ_TILE
    lane = lax.broadcasted_iota(I32, (1, HEAD_DIM), 1)
    q = q_ref[...]
    zero = jnp.zeros_like(q)
    q2_ref[0:TQ, :] = jnp.where(lane < HALF, q, zero)
    q2_ref[TQ:, :] = jnp.where(lane < HALF, zero, q)
    m_ref[...] = jnp.full_like(m_ref, MASKED)
    l_ref[...] = jnp.zeros_like(l_ref)
    acc_ref[...] = jnp.zeros_like(acc_ref)

    def tile(j, bias):
        k0 = pl.multiple_of(j * TQ, TQ)
        s = _dot_nt(q2_ref[...], k_ref[pl.ds(k0, TQ), :])
        if bias is None:
            s = s + far_ref[...][:, :1]
        else:
            s = s + jnp.concatenate([bias, bias], axis=0)
        _flash_update(s, v_ref[pl.ds(k0, TQ), :], m_ref, l_ref, acc_ref)

    def far(j, carry):
        tile(j, None)
        return carry

    lax.fori_loop(0, jnp.maximum(i - 1, 0), far, 0)

    @pl.when(i > 0)
    def _():
        tile(i - 1, near_ref[1])

    tile(i, near_ref[0])

    o = acc_ref[...] * (1.0 / l_ref[...])[:, :1]
    od = o[0:TQ, :] - lam_ref[...][:, :1] * o[TQ:, :]
    o_ref[...] = (((od * _rms_inv(od)) * sub_ref[...]) * out_scale).astype(o_ref.dtype)


def _diff(cqn, ckn, cvb, near, far, lam, subln, out_scale):
    B, Lp, W = cqn.shape
    TQ = SEQ_TILE
    H = HEAD_DIM
    return pl.pallas_call(
        functools.partial(_diff_body, out_scale=out_scale),
        grid=(B, N_HEADS, Lp // TQ),
        in_specs=[
            pl.BlockSpec((None, TQ, H), lambda b, h, i: (b, i, h)),
            pl.BlockSpec((None, Lp, H), lambda b, h, i: (b, 0, h)),
            pl.BlockSpec((None, Lp, H), lambda b, h, i: (b, 0, h)),
            pl.BlockSpec((None, 2, TQ, TQ), lambda b, h, i: (h, 0, 0, 0)),
            pl.BlockSpec((None, 1, H), lambda b, h, i: (h, 0, 0)),
            pl.BlockSpec((1, H), lambda b, h, i: (0, 0)),
            pl.BlockSpec((1, H), lambda b, h, i: (0, 0)),
        ],
        out_specs=pl.BlockSpec((None, TQ, H), lambda b, h, i: (b, i, h)),
        out_shape=jax.ShapeDtypeStruct((B, Lp, W), BF16),
        scratch_shapes=[pltpu.VMEM((2 * TQ, H), BF16), pltpu.VMEM((2 * TQ, H), F32),
                        pltpu.VMEM((2 * TQ, H), F32), pltpu.VMEM((2 * TQ, H), F32)],
        compiler_params=_cparams("parallel", "parallel", "arbitrary"),
        name="diff",
    )(cqn, ckn, cvb, near, far, lam, subln)


def _dsa_body(qi_ref, kk_ref, wq_ref, q_ref, k_ref, v_ref, near_ref, far_ref, o_ref,
              qi2_ref, wb_ref, key_ref, q4_ref, thr_ref, m_ref, l_ref, acc_ref, *, top_k):
    i = pl.program_id(1)
    TQ = SEQ_TILE
    H = HEAD_DIM
    lane = lax.broadcasted_iota(I32, (1, H), 1)

    for p in range(IDX_HEADS // 2):
        x = qi_ref[:, p * H:(p + 1) * H]
        zero = jnp.zeros_like(x)
        qi2_ref[p, 0:TQ, :] = jnp.where(lane < HALF, x, zero)
        qi2_ref[p, TQ:, :] = jnp.where(lane < HALF, zero, x)
    wq = wq_ref[...]
    for hh in range(IDX_HEADS):
        wb_ref[hh] = jnp.broadcast_to(wq[:, hh:hh + 1], (TQ, H))
    for h in range(N_HEADS):
        q4_ref[h * TQ:(h + 1) * TQ, :] = q_ref[:, h * H:(h + 1) * H]

    rel = lax.broadcasted_iota(I32, (TQ, TQ), 1) - lax.broadcasted_iota(I32, (TQ, TQ), 0)

    def index_tile(j, carry):
        k0 = pl.multiple_of(j * TQ, TQ)
        kt = kk_ref[pl.ds(k0, TQ), :]
        acc = jnp.zeros((TQ, TQ), F32)
        for p in range(IDX_HEADS // 2):
            s = jnp.maximum(_dot_nt(qi2_ref[p], kt), 0.0)
            acc = acc + s[0:TQ, :] * wb_ref[2 * p] + s[TQ:, :] * wb_ref[2 * p + 1]
        acc = acc + 0.0
        acc = jnp.where(rel <= (i - j) * TQ, acc, -jnp.inf)
        bits = pltpu.bitcast(acc, I32)
        key_ref[j] = bits ^ ((bits >> 31) & jnp.int32(0x7FFFFFFF))
        return carry

    lax.fori_loop(0, i + 1, index_tile, 0)

    thr_ref[...] = jnp.full_like(thr_ref, INT_MIN)

    def bit_step(t, carry):
        cand = thr_ref[...] + jnp.left_shift(jnp.int32(1), 31 - t)

        def count(j, cnt):
            return cnt + jnp.where(key_ref[j] >= cand, 1, 0).astype(I32)

        cnt = lax.fori_loop(0, i + 1, count, jnp.zeros((TQ, TQ), I32))
        total = jnp.sum(cnt, axis=-1, keepdims=True)
        thr_ref[...] = jnp.where(total >= top_k, cand, thr_ref[...])
        return carry

    lax.fori_loop(0, 32, bit_step, 0)

    m_ref[...] = jnp.full_like(m_ref, MASKED)
    l_ref[...] = jnp.zeros_like(l_ref)
    acc_ref[...] = jnp.zeros_like(acc_ref)

    def tile(j, near_idx):
        k0 = pl.multiple_of(j * TQ, TQ)
        s = _dot_nt(q4_ref[...], k_ref[pl.ds(k0, TQ), :])
        sel = key_ref[j] >= thr_ref[...]
        parts = []
        for h in range(N_HEADS):
            sh = s[h * TQ:(h + 1) * TQ, :]
            if near_idx is None:
                sh = sh + far_ref[h][:, :1]
            else:
                sh = sh + near_ref[h, near_idx]
            parts.append(jnp.where(sel, sh, MASKED))
        _flash_update(jnp.concatenate(parts, axis=0), v_ref[pl.ds(k0, TQ), :], m_ref, l_ref, acc_ref)

    def far(j, carry):
        tile(j, None)
        return carry

    lax.fori_loop(0, jnp.maximum(i - 1, 0), far, 0)

    @pl.when(i > 0)
    def _():
        tile(i - 1, 1)

    tile(i, 0)

    o = acc_ref[...] * (1.0 / l_ref[...])[:, :1]
    for h in range(N_HEADS):
        o_ref[:, h * H:(h + 1) * H] = o[h * TQ:(h + 1) * TQ, :].astype(o_ref.dtype)


def _dsa(qib, kk, wq, dqn, dkn, dvb, near, far, top_k):
    B, Lp, W = dqn.shape
    TQ = SEQ_TILE
    H = HEAD_DIM
    NI = IDX_HEADS * IDX_DIM
    nt = Lp // TQ
    seq = lambda width: pl.BlockSpec((None, Lp, width), lambda b, i: (b, 0, 0))
    blk = lambda width: pl.BlockSpec((None, TQ, width), lambda b, i: (b, i, 0))
    return pl.pallas_call(
        functools.partial(_dsa_body, top_k=top_k),
        grid=(B, nt),
        in_specs=[blk(NI), seq(H), blk(H), blk(W), seq(H), seq(H),
                  pl.BlockSpec((N_HEADS, 2, TQ, TQ), lambda b, i: (0, 0, 0, 0)),
                  pl.BlockSpec((N_HEADS, 1, H), lambda b, i: (0, 0, 0))],
        out_specs=blk(W),
        out_shape=jax.ShapeDtypeStruct((B, Lp, W), BF16),
        scratch_shapes=[
            pltpu.VMEM((IDX_HEADS // 2, 2 * TQ, H), BF16),
            pltpu.VMEM((IDX_HEADS, TQ, H), F32),
            pltpu.VMEM((nt, TQ, TQ), I32),
            pltpu.VMEM((N_HEADS * TQ, H), BF16),
            pltpu.VMEM((TQ, TQ), I32),
            pltpu.VMEM((N_HEADS * TQ, H), F32),
            pltpu.VMEM((N_HEADS * TQ, H), F32),
            pltpu.VMEM((N_HEADS * TQ, H), F32),
        ],
        compiler_params=_cparams("parallel", "arbitrary"),
        name="dsa",
    )(qib, kk, wq, dqn, dkn, dvb, near, far)


def _merge_body(h_ref, xn_ref, b0_ref, b1_ref, b2_ref, b3_ref, g0_ref, g1_ref, g2_ref, g3_ref,
                wb_ref, wo_ref, o_ref):
    @pl.when(pl.program_id(1) == 0)
    def _():
        o_ref[...] = h_ref[...]

    xn = xn_ref[...]
    merged = None
    for m, (b_ref, g_ref) in enumerate(((b0_ref, g0_ref), (b1_ref, g1_ref), (b2_ref, g2_ref), (b3_ref, g3_ref))):
        term = jax.nn.sigmoid(_dot(xn, g_ref[...])) * _dot(b_ref[...], wb_ref[m])
        merged = term if merged is None else merged + term
    o_ref[...] += _dot(merged.astype(BF16), wo_ref[...])


def _merge(h, xn, branches, w_gate, w_branch, w_out, tm, tn):
    T, D = h.shape
    W = MIX_WIDTH
    nn = D // tn
    row = lambda width: pl.BlockSpec((tm, width), lambda i, n: (i, 0))
    gate = lambda m: pl.BlockSpec((D, tn), functools.partial(lambda i, n, m: (0, m * nn + n), m=m))
    return pl.pallas_call(
        _merge_body,
        grid=(T // tm, nn),
        in_specs=[row(D), row(D), row(W), row(W), row(W), row(W),
                  gate(0), gate(1), gate(2), gate(3),
                  pl.BlockSpec((N_BRANCH, W, tn), lambda i, n: (0, 0, n)),
                  pl.BlockSpec((tn, D), lambda i, n: (n, 0))],
        out_specs=row(D),
        out_shape=jax.ShapeDtypeStruct((T, D), F32),
        compiler_params=_cparams("parallel", "arbitrary"),
        name="merge",
    )(h, xn, *branches, w_gate, w_gate, w_gate, w_gate, w_branch, w_out)


def _rel_bucket(n):
    max_exact = N_BUCKETS // 2
    nf = jnp.maximum(n, 1).astype(F32)
    large = max_exact + (jnp.log(nf / max_exact) / math.log(MAX_DISTANCE / max_exact)
                         * (N_BUCKETS - max_exact)).astype(I32)
    large = jnp.minimum(large, N_BUCKETS - 1)
    return jnp.where(n < max_exact, n, large)


def _bias_tiles(table):
    TQ = SEQ_TILE
    qi = jnp.arange(TQ, dtype=I32)[:, None]
    ki = jnp.arange(TQ, dtype=I32)[None, :]
    d0 = qi - ki
    tbl = table.astype(F32).T
    diag = jnp.where(d0 >= 0, tbl[:, _rel_bucket(jnp.maximum(d0, 0))], MASKED)
    prev = tbl[:, _rel_bucket(d0 + TQ)]
    near = jnp.stack([diag, prev], axis=1)
    far = jnp.broadcast_to(tbl[:, N_BUCKETS - 1][:, None, None], (tbl.shape[0], 1, HEAD_DIM))
    return near, far


def _pack_w_in(w):
    offs = np.cumsum([0, 512, 512, 512, 512, 512, 512, 512, 512, 512, 512, 512, 128, 128,
                      IDX_HEADS * IDX_DIM, IDX_DIM, IDX_HEADS]).tolist()
    seg = lambda a, b: w[:, offs[a]:offs[b]]
    D = w.shape[0]
    used = COL_KIW + IDX_DIM + IDX_HEADS
    main = jnp.concatenate([
        seg(0, 4),
        seg(13, 14),
        seg(4, 7),
        seg(7, 10),
        seg(10, 11),
        seg(11, 12),
        seg(12, 13),
        seg(14, 16),
        jnp.zeros((D, P_COLS - used), w.dtype),
    ], axis=1).astype(BF16)
    gates = w[:, offs[16]:].astype(BF16)
    return main, gates


def _largest_tile(n, cap):
    t = cap
    while n % t:
        t //= 2
    return t


def kernel(x, meta_tokens, rel_bias, ffn1_norm, ffn1_w_gu, ffn1_w_down, mix_norm, w_in, hgrn_lb, hgrn_gnorm, conv_w, diff_q_norm, diff_k_norm, diff_lambda, diff_subln, dsa_q_norm, dsa_k_norm, w_branch, w_out, ffn2_norm, ffn2_w_gu, ffn2_w_down):
    B, S, D = x.shape
    depth = w_in.shape[0]
    L = S + N_META
    Lp = -(-L // SEQ_TILE) * SEQ_TILE
    T = B * Lp
    top_k = min(TOPK_MAX, S // 4)
    tm = _largest_tile(T, 512)
    d_ff = ffn1_w_down.shape[1]
    tf = _largest_tile(d_ff, 512)

    meta = jnp.broadcast_to(meta_tokens.astype(x.dtype)[None], (B, N_META, D))
    h = jnp.concatenate([meta, x, jnp.zeros((B, Lp - L, D), x.dtype)], axis=1).reshape(T, D)

    lbs = jnp.cumsum(jax.nn.softmax(hgrn_lb.astype(F32), axis=0), axis=0)
    lbs = lbs - lbs[0:1]
    near_c, far_c = _bias_tiles(rel_bias[:, :N_HEADS])
    near_d, far_d = _bias_tiles(rel_bias[:, N_HEADS:])

    for l in range(depth):
        h = _ffn(h, ffn1_norm[l], ffn1_w_gu[l].astype(BF16), ffn1_w_down[l].astype(BF16), tm, tf)

        w_main, w_gate = _pack_w_in(w_in[l])
        xn, P = _proj(h, mix_norm[l], w_main, tm, 1024)
        P3 = P.reshape(B, Lp, P_COLS)
        br_a = _hgrn(P3, lbs[l], hgrn_gnorm[l])
        br_b = _conv(P3, conv_w[l])
        cqn, ckn, cvb, dqn, dkn, dvb, qib, kk, wq = _prep(
            P, diff_q_norm[l], diff_k_norm[l], dsa_q_norm[l], dsa_k_norm[l], tm)
        r3 = lambda a: a.reshape(B, Lp, a.shape[-1])
        lp = diff_lambda[l].astype(F32)
        lam_init = 0.8 - 0.6 * math.exp(-0.3 * l)
        lam = jnp.exp(jnp.sum(lp[0] * lp[1])) - jnp.exp(jnp.sum(lp[2] * lp[3])) + lam_init
        br_c = _diff(r3(cqn), r3(ckn), r3(cvb), near_c, far_c,
                     jnp.broadcast_to(lam, (1, HEAD_DIM)).astype(F32),
                     diff_subln[l].astype(F32).reshape(1, HEAD_DIM), 1.0 - lam_init)
        br_d = _dsa(r3(qib), r3(kk), r3(wq), r3(dqn), r3(dkn), r3(dvb), near_d, far_d, top_k)
        branches = [a.reshape(T, MIX_WIDTH) for a in (br_a, br_b, br_c, br_d)]
        h = _merge(h, xn, branches, w_gate, w_branch[l].astype(BF16), w_out[l].astype(BF16), tm, 256)

        h = _ffn(h, ffn2_norm[l], ffn2_w_gu[l].astype(BF16), ffn2_w_down[l].astype(BF16), tm, tf)

    return h.reshape(B, Lp, D)[:, N_META:L]
```

```python
import functools
import math

import jax
import jax.numpy as jnp
import numpy as np
from jax import lax
from jax.experimental import pallas as pl
from jax.experimental.pallas import tpu as pltpu

F32 = jnp.float32
BF16 = jnp.bfloat16
I32 = jnp.int32

EPS = 1e-6
N_META = 16
HEAD_DIM = 128
HALF = 64
N_HEADS = 4
MIX_WIDTH = N_HEADS * HEAD_DIM
IDX_HEADS = 16
IDX_DIM = 64
TOPK_MAX = 256
N_BRANCH = 4
N_BUCKETS = 32
MAX_DISTANCE = 128
CONV_W = 3
SEQ_TILE = 128
HGRN_CHUNK = 16
MASKED = -1e30
INT_MIN = -(2 ** 31)
KEY_NEG_INF = -2139095041
VMEM_LIMIT_BYTES = 56 * 1024 * 1024

COL_A = 0
COL_QI = 2048
COL_B = 3072
COL_C = 4608
COL_DQ = 6144
COL_DK = 6656
COL_DV = 6784
COL_KIW = 6912
P_COLS = 7168


def _cparams(*sem):
    return pltpu.CompilerParams(dimension_semantics=sem, vmem_limit_bytes=VMEM_LIMIT_BYTES)


def _dot(a, b):
    return jnp.dot(a, b, preferred_element_type=F32)


def _dot_nt(a, b):
    return lax.dot_general(a, b, (((1,), (1,)), ((), ())), preferred_element_type=F32)


def _dot_tn(a, b):
    return lax.dot_general(a, b, (((0,), (0,)), ((), ())), preferred_element_type=F32)


def _rms_inv(x):
    return lax.rsqrt(jnp.mean(x * x, axis=-1, keepdims=True) + EPS)


def _ffn_body(h_ref, g_ref, wg_ref, wu_ref, wd_ref, o_ref, xn_ref):
    j = pl.program_id(1)

    @pl.when(j == 0)
    def _():
        x = h_ref[...]
        xn_ref[...] = ((x * _rms_inv(x)) * g_ref[...]).astype(BF16)
        o_ref[...] = x

    xn = xn_ref[...]
    g = _dot(xn, wg_ref[...])
    u = _dot(xn, wu_ref[...])
    a = (g * jax.nn.sigmoid(g) * (0.5 * u)).astype(BF16)
    o_ref[...] += _dot(a, wd_ref[...])


def _ffn(h, gain, w_gu, w_down, tm, tf):
    T, D = h.shape
    F = w_down.shape[0]
    nf = F // tf
    return pl.pallas_call(
        _ffn_body,
        grid=(T // tm, nf),
        in_specs=[
            pl.BlockSpec((tm, D), lambda i, j: (i, 0)),
            pl.BlockSpec((1, D), lambda i, j: (0, 0)),
            pl.BlockSpec((D, tf), lambda i, j: (0, j)),
            pl.BlockSpec((D, tf), lambda i, j: (0, j + nf)),
            pl.BlockSpec((tf, D), lambda i, j: (j, 0)),
        ],
        out_specs=pl.BlockSpec((tm, D), lambda i, j: (i, 0)),
        out_shape=jax.ShapeDtypeStruct((T, D), F32),
        scratch_shapes=[pltpu.VMEM((tm, D), BF16)],
        compiler_params=_cparams("parallel", "arbitrary"),
        name="ffn",
    )(h, gain.reshape(1, D), w_gu, w_gu, w_down)


def _proj_body(h_ref, g_ref, w_ref, xn_ref, p_ref):
    @pl.when(pl.program_id(1) == 0)
    def _():
        x = h_ref[...]
        xn_ref[...] = ((x * _rms_inv(x)) * g_ref[...]).astype(BF16)

    p_ref[...] = _dot(xn_ref[...], w_ref[...])


def _proj(h, gain, w, tm, tn):
    T, D = h.shape
    N = w.shape[1]
    return pl.pallas_call(
        _proj_body,
        grid=(T // tm, N // tn),
        in_specs=[
            pl.BlockSpec((tm, D), lambda i, j: (i, 0)),
            pl.BlockSpec((1, D), lambda i, j: (0, 0)),
            pl.BlockSpec((D, tn), lambda i, j: (0, j)),
        ],
        out_specs=[
            pl.BlockSpec((tm, D), lambda i, j: (i, 0)),
            pl.BlockSpec((tm, tn), lambda i, j: (i, j)),
        ],
        out_shape=[jax.ShapeDtypeStruct((T, D), BF16), jax.ShapeDtypeStruct((T, N), F32)],
        compiler_params=_cparams("parallel", "arbitrary"),
        name="proj",
    )(h, gain.reshape(1, D), w)


def _hgrn_body(q_ref, f_ref, i_ref, g_ref, loglb_ref, log1mlb_ref, omlb_ref, gn_ref, o_ref,
               st_ref, qs_ref, ks_ref, bs_ref, os_ref):
    R = q_ref.shape[0]
    C = HGRN_CHUNK

    @pl.when(pl.program_id(1) == 0)
    def _():
        st_ref[...] = jnp.zeros_like(st_ref)

    z = f_ref[...]
    log_sig = jnp.minimum(z, 0.0) - jnp.log1p(jnp.exp(-jnp.abs(z)))
    y = log1mlb_ref[...] + log_sig
    a = loglb_ref[...]
    log_f = jnp.maximum(a, y) + jnp.log1p(jnp.exp(-jnp.abs(a - y)))
    row = lax.broadcasted_iota(I32, z.shape, 0) % C
    b = log_f
    for sh in (1, 2, 4, 8):
        b = b + jnp.where(row >= sh, pltpu.roll(b, sh, axis=0), 0.0)
    bs_ref[...] = b
    ks_ref[...] = omlb_ref[...] * jax.nn.sigmoid(-z)
    qr = q_ref[...]
    qs_ref[...] = qr * jax.nn.sigmoid(qr)

    t_iota = lax.broadcasted_iota(I32, (C, 1), 0)

    def chunk(c, carry):
        r0 = pl.multiple_of(c * C, C)
        for h in range(N_HEADS):
            hs = slice(h * HEAD_DIM, (h + 1) * HEAD_DIM)
            qc = qs_ref[pl.ds(r0, C), hs]
            kc = ks_ref[pl.ds(r0, C), hs]
            bc = bs_ref[pl.ds(r0, C), hs]
            vc = i_ref[pl.ds(r0, C), hs]
            st = st_ref[h]
            o = _dot_nt((qc * jnp.exp(bc)).astype(BF16), st.astype(BF16))
            for s in range(C):
                b_s = bc[s:s + 1, :]
                k_s = kc[s:s + 1, :]
                v_s = vc[s:s + 1, :]
                e = jnp.exp(jnp.minimum(bc - b_s, 0.0))
                col = jnp.sum(qc * k_s * e, axis=-1, keepdims=True)
                col = jnp.where(t_iota >= s, col, 0.0)
                o = o + col * v_s
            os_ref[pl.ds(r0, C), hs] = o
            b_last = bc[C - 1:C, :]
            kd = kc * jnp.exp(b_last - bc)
            st_ref[h] = st * jnp.exp(b_last) + _dot_tn(vc.astype(BF16), kd.astype(BF16))
        return carry

    lax.fori_loop(0, R // C, chunk, 0)

    for h in range(N_HEADS):
        hs = slice(h * HEAD_DIM, (h + 1) * HEAD_DIM)
        o = os_ref[:, hs]
        on = (o * _rms_inv(o)) * gn_ref[:, hs]
        gr = g_ref[:, hs]
        o_ref[:, hs] = (on * (gr * jax.nn.sigmoid(gr))).astype(o_ref.dtype)


def _hgrn(P3, lb, gnorm):
    B, Lp, _ = P3.shape
    R = SEQ_TILE
    W = MIX_WIDTH
    lb = lb.reshape(1, W).astype(F32)
    vec = pl.BlockSpec((1, W), lambda b, t: (0, 0))
    c0 = COL_A // W
    return pl.pallas_call(
        _hgrn_body,
        grid=(B, Lp // R),
        in_specs=[pl.BlockSpec((None, R, W), functools.partial(lambda b, t, c: (b, t, c), c=c0 + k))
                  for k in range(4)] + [vec, vec, vec, vec],
        out_specs=pl.BlockSpec((None, R, W), lambda b, t: (b, t, 0)),
        out_shape=jax.ShapeDtypeStruct((B, Lp, W), BF16),
        scratch_shapes=[pltpu.VMEM((N_HEADS, HEAD_DIM, HEAD_DIM), F32)]
        + [pltpu.VMEM((R, W), F32)] * 4,
        compiler_params=_cparams("parallel", "arbitrary"),
        name="hgrn",
    )(P3, P3, P3, P3, jnp.log(lb), jnp.log1p(-lb), 1.0 - lb, gnorm.reshape(1, W).astype(F32))


def _conv_body(b_ref, c_ref, u_ref, w_ref, o_ref, carry_ref):
    @pl.when(pl.program_id(1) == 0)
    def _():
        carry_ref[...] = jnp.zeros_like(carry_ref)

    zc = c_ref[...] * u_ref[...]
    R = zc.shape[0]
    row = lax.broadcasted_iota(I32, zc.shape, 0)
    last = carry_ref[7:8, :]
    last2 = carry_ref[6:7, :]
    z1 = jnp.where(row == 0, last, pltpu.roll(zc, 1, axis=0))
    z2 = jnp.where(row == 0, last2, jnp.where(row == 1, last, pltpu.roll(zc, 2, axis=0)))
    y = w_ref[0:1, :] * zc + w_ref[1:2, :] * z1 + w_ref[2:3, :] * z2
    o_ref[...] = (b_ref[...] * y).astype(o_ref.dtype)
    carry_ref[...] = zc[R - 8:, :]


def _conv(P3, conv_w):
    B, Lp, _ = P3.shape
    R = SEQ_TILE
    W = MIX_WIDTH
    c0 = COL_B // W
    w8 = jnp.zeros((8, W), F32).at[:CONV_W].set(conv_w.astype(F32))
    return pl.pallas_call(
        _conv_body,
        grid=(B, Lp // R),
        in_specs=[pl.BlockSpec((None, R, W), functools.partial(lambda b, t, c: (b, t, c), c=c0 + k))
                  for k in range(3)] + [pl.BlockSpec((8, W), lambda b, t: (0, 0))],
        out_specs=pl.BlockSpec((None, R, W), lambda b, t: (b, t, 0)),
        out_shape=jax.ShapeDtypeStruct((B, Lp, W), BF16),
        scratch_shapes=[pltpu.VMEM((8, W), F32)],
        compiler_params=_cparams("parallel", "arbitrary"),
        name="conv",
    )(P3, P3, P3, w8)


def _prep_body(cq_ref, ck_ref, cv_ref, dq_ref, dk_ref, dv_ref, kiw_ref, qi_ref,
               cqg_ref, ckg_ref, dqg_ref, dkg_ref,
               cqn_ref, ckn_ref, cvb_ref, dqn_ref, dkn_ref, dvb_ref, qib_ref, kk_ref, wq_ref):
    lane = lax.broadcasted_iota(I32, (1, HEAD_DIM), 1)
    lo = lane < HALF

    def half_norm(x, g, scale):
        sq = x * x
        ms_lo = jnp.sum(jnp.where(lo, sq, 0.0), axis=-1, keepdims=True) * (1.0 / HALF)
        ms_hi = jnp.sum(jnp.where(lo, 0.0, sq), axis=-1, keepdims=True) * (1.0 / HALF)
        inv = jnp.where(lo, lax.rsqrt(ms_lo + EPS), lax.rsqrt(ms_hi + EPS))
        return ((x * inv) * g) * scale

    for h in range(N_HEADS):
        hs = slice(h * HEAD_DIM, (h + 1) * HEAD_DIM)
        cqn_ref[:, hs] = half_norm(cq_ref[:, hs], cqg_ref[...], HALF ** -0.5).astype(BF16)
        ckn_ref[:, hs] = half_norm(ck_ref[:, hs], ckg_ref[...], 1.0).astype(BF16)
        x = dq_ref[:, hs]
        dqn_ref[:, hs] = (((x * _rms_inv(x)) * dqg_ref[...]) * (HEAD_DIM ** -0.5)).astype(BF16)
    x = dk_ref[...]
    dkn_ref[...] = ((x * _rms_inv(x)) * dkg_ref[...]).astype(BF16)
    for t in range(cv_ref.shape[0] // SEQ_TILE):
        rs = slice(t * SEQ_TILE, (t + 1) * SEQ_TILE)
        for h in range(N_HEADS):
            hs = slice(h * HEAD_DIM, (h + 1) * HEAD_DIM)
            cvb_ref[t, hs, :] = cv_ref[rs, hs].T.astype(BF16)
        dvb_ref[t] = dv_ref[rs, :].T.astype(BF16)
    qib_ref[...] = (qi_ref[...] * (IDX_DIM ** -0.5)).astype(BF16)
    kiw = kiw_ref[...]
    swapped = pltpu.roll(kiw, HALF, axis=1)
    kk_ref[...] = jnp.where(lo, kiw, swapped).astype(BF16)
    wq_ref[...] = swapped * (IDX_HEADS ** -0.5)


def _prep(P, cq_g, ck_g, dq_g, dk_g, tm):
    T = P.shape[0]
    W = MIX_WIDTH
    H = HEAD_DIM

    def col(width, start):
        return pl.BlockSpec((tm, width), functools.partial(lambda i, c: (i, c), c=start // width))

    vec = pl.BlockSpec((1, H), lambda i: (0, 0))
    out = lambda width: pl.BlockSpec((tm, width), lambda i: (i, 0))
    tiles = lambda width: pl.BlockSpec((tm // SEQ_TILE, width, SEQ_TILE), lambda i: (i, 0, 0))
    return pl.pallas_call(
        _prep_body,
        grid=(T // tm,),
        in_specs=[col(W, COL_C), col(W, COL_C + W), col(W, COL_C + 2 * W), col(W, COL_DQ),
                  col(H, COL_DK), col(H, COL_DV), col(H, COL_KIW), col(IDX_HEADS * IDX_DIM, COL_QI),
                  vec, vec, vec, vec],
        out_specs=[out(W), out(W), tiles(W), out(W), out(H), tiles(H), out(IDX_HEADS * IDX_DIM), out(H), out(H)],
        out_shape=[jax.ShapeDtypeStruct((T, W), BF16)] * 2
        + [jax.ShapeDtypeStruct((T // SEQ_TILE, W, SEQ_TILE), BF16), jax.ShapeDtypeStruct((T, W), BF16),
           jax.ShapeDtypeStruct((T, H), BF16), jax.ShapeDtypeStruct((T // SEQ_TILE, H, SEQ_TILE), BF16)]
        + [jax.ShapeDtypeStruct((T, IDX_HEADS * IDX_DIM), BF16),
           jax.ShapeDtypeStruct((T, H), BF16), jax.ShapeDtypeStruct((T, H), F32)],
        compiler_params=_cparams("parallel"),
        name="prep",
    )(P, P, P, P, P, P, P, P,
      jnp.tile(cq_g.astype(F32), 2).reshape(1, H), jnp.tile(ck_g.astype(F32), 2).reshape(1, H),
      dq_g.astype(F32).reshape(1, H), dk_g.astype(F32).reshape(1, H))


SUBLANES = 8
KEY_CHUNK = 4


def _max8(x):
    return jnp.max(x.reshape(x.shape[0] // SUBLANES, SUBLANES, x.shape[1]), axis=0)


def _sum8(x):
    return jnp.sum(x.reshape(x.shape[0] // SUBLANES, SUBLANES, x.shape[1]), axis=0)


def _for_tiles(n_tiles, body, chunk=KEY_CHUNK):
    n_chunks = n_tiles // chunk

    def chunked(c, carry):
        body(c * chunk, chunk)
        return carry

    lax.fori_loop(0, n_chunks, chunked, 0)
    base = n_chunks * chunk
    size = chunk // 2
    while size >= 1:
        take = (n_tiles - base) // size

        def part(_, carry, base=base, size=size):
            body(base, size)
            return carry

        lax.fori_loop(0, take, part, 0)
        base = base + take * size
        size //= 2


def _diff_body(q_ref, k_ref, vt_ref, near_ref, far_ref, lam_ref, sub_ref, o_ref,
               q2_ref, s_ref, mx_ref, l_ref, acc_ref, *, out_scale):
    i = pl.program_id(2)
    TQ = SEQ_TILE
    lane = lax.broadcasted_iota(I32, (1, HEAD_DIM), 1)
    q = q_ref[...]
    zero = jnp.zeros_like(q)
    q2_ref[0:TQ, :] = jnp.where(lane < HALF, q, zero)
    q2_ref[TQ:, :] = jnp.where(lane < HALF, zero, q)
    mx_ref[...] = jnp.full_like(mx_ref, MASKED)

    def logits(j0, n):
        k0 = pl.multiple_of(j0 * TQ, TQ)
        return _dot_nt(k_ref[pl.ds(k0, n * TQ), :], q2_ref[...])

    def put(j0, n, s):
        for t in range(n):
            s_ref[j0 + t] = s[t * TQ:(t + 1) * TQ, :]
        mx_ref[...] = jnp.maximum(mx_ref[...], _max8(s))

    n_far = jnp.maximum(i - 1, 0)
    _for_tiles(n_far, lambda j0, n: put(j0, n, logits(j0, n) + far_ref[...]))

    def prev(j, carry):
        b = near_ref[1]
        put(j, 1, logits(j, 1) + jnp.concatenate([b, b], axis=1))
        return carry

    lax.fori_loop(n_far, i, prev, 0)
    b = near_ref[0]
    put(i, 1, logits(i, 1) + jnp.concatenate([b, b], axis=1))

    m = jnp.max(mx_ref[...], axis=0, keepdims=True)
    l_ref[...] = jnp.zeros_like(l_ref)
    acc_ref[...] = jnp.zeros_like(acc_ref)

    def pv(j0, n):
        p = jnp.exp(s_ref[pl.ds(j0, n)].reshape(n * TQ, 2 * TQ) - m)
        l_ref[...] += _sum8(p)
        pb = p.astype(BF16)
        acc = acc_ref[...]
        for t in range(n):
            acc = acc + _dot(vt_ref[j0 + t], pb[t * TQ:(t + 1) * TQ, :])
        acc_ref[...] = acc

    _for_tiles(i + 1, pv)

    ot = acc_ref[...] * (1.0 / jnp.sum(l_ref[...], axis=0, keepdims=True))
    od = ot[:, 0:TQ] - lam_ref[...] * ot[:, TQ:]
    inv = lax.rsqrt(jnp.mean(od * od, axis=0, keepdims=True) + EPS)
    o_ref[...] = (((od * inv) * sub_ref[...]) * out_scale).T.astype(o_ref.dtype)


def _diff(cqn, ckn, cvt, near_t, far, lam, subln, out_scale):
    B, Lp, W = cqn.shape
    TQ = SEQ_TILE
    H = HEAD_DIM
    nt = Lp // TQ
    return pl.pallas_call(
        functools.partial(_diff_body, out_scale=out_scale),
        grid=(B, N_HEADS, nt),
        in_specs=[
            pl.BlockSpec((None, TQ, H), lambda b, h, i: (b, i, h)),
            pl.BlockSpec((None, Lp, H), lambda b, h, i: (b, 0, h)),
            pl.BlockSpec((None, nt, H, TQ), lambda b, h, i: (b, 0, h, 0)),
            pl.BlockSpec((None, 2, TQ, TQ), lambda b, h, i: (h, 0, 0, 0)),
            pl.BlockSpec((None, 1, 2 * TQ), lambda b, h, i: (h, 0, 0)),
            pl.BlockSpec((1, TQ), lambda b, h, i: (0, 0)),
            pl.BlockSpec((H, TQ), lambda b, h, i: (0, 0)),
        ],
        out_specs=pl.BlockSpec((None, TQ, H), lambda b, h, i: (b, i, h)),
        out_shape=jax.ShapeDtypeStruct((B, Lp, W), BF16),
        scratch_shapes=[pltpu.VMEM((2 * TQ, H), BF16),
                        pltpu.VMEM((nt, TQ, 2 * TQ), F32),
                        pltpu.VMEM((SUBLANES, 2 * TQ), F32),
                        pltpu.VMEM((SUBLANES, 2 * TQ), F32),
                        pltpu.VMEM((H, 2 * TQ), F32)],
        compiler_params=_cparams("parallel", "parallel", "arbitrary"),
        name="diff",
    )(cqn, ckn, cvt, near_t, far, lam, subln)


def _dsa_body(qi_ref, kk_ref, wq_ref, q_ref, k_ref, vt_ref, near_ref, far_ref, o_ref,
              qi2_ref, w_ref, key_ref, q4_ref, thr_ref, cnt_ref, s_ref, mx_ref, l_ref, acc_ref, *, top_k):
    i = pl.program_id(1)
    TQ = SEQ_TILE
    H = HEAD_DIM
    lane = lax.broadcasted_iota(I32, (1, H), 1)

    for p in range(IDX_HEADS // 2):
        x = qi_ref[:, p * H:(p + 1) * H]
        zero = jnp.zeros_like(x)
        qi2_ref[p, 0:TQ, :] = jnp.where(lane < HALF, x, zero)
        qi2_ref[p, TQ:, :] = jnp.where(lane < HALF, zero, x)
    w_ref[...] = wq_ref[...].T
    for h in range(N_HEADS):
        q4_ref[h * TQ:(h + 1) * TQ, :] = q_ref[:, h * H:(h + 1) * H]

    def index(j0, n):
        k0 = pl.multiple_of(j0 * TQ, TQ)
        kt = kk_ref[pl.ds(k0, n * TQ), :]
        acc = jnp.zeros((n * TQ, TQ), F32)
        for p in range(IDX_HEADS // 2):
            s = jnp.maximum(_dot_nt(kt, qi2_ref[p]), 0.0)
            acc = acc + s[:, 0:TQ] * w_ref[2 * p:2 * p + 1, :] + s[:, TQ:] * w_ref[2 * p + 1:2 * p + 2, :]
        acc = acc + 0.0
        bits = pltpu.bitcast(acc, I32)
        key = bits ^ ((bits >> 31) & jnp.int32(0x7FFFFFFF))
        for t in range(n):
            key_ref[j0 + t] = key[t * TQ:(t + 1) * TQ, :]

    _for_tiles(i + 1, index)
    kq = lax.broadcasted_iota(I32, (TQ, TQ), 0) - lax.broadcasted_iota(I32, (TQ, TQ), 1)
    key_ref[i] = jnp.where(kq <= 0, key_ref[i], jnp.int32(KEY_NEG_INF))

    thr_ref[...] = jnp.full_like(thr_ref, INT_MIN)

    def bit_step(t, carry):
        cand = thr_ref[...] + jnp.left_shift(jnp.int32(1), 31 - t)
        cand1 = cand[0:1, :]

        cnt_ref[...] = jnp.zeros_like(cnt_ref)

        def count(j0, n):
            ge = key_ref[pl.ds(j0, n)].reshape(n * TQ, TQ) >= cand1
            cnt_ref[...] += _sum8(jnp.where(ge, 1, 0).astype(I32))

        _for_tiles(i + 1, count)
        total = jnp.sum(cnt_ref[...], axis=0, keepdims=True)
        thr_ref[...] = jnp.where(total >= top_k, cand, thr_ref[...])
        return carry

    lax.fori_loop(0, 32, bit_step, 0)

    mx_ref[...] = jnp.full_like(mx_ref, MASKED)

    def put(j0, n, near_idx):
        k0 = pl.multiple_of(j0 * TQ, TQ)
        s = _dot_nt(k_ref[pl.ds(k0, n * TQ), :], q4_ref[...])
        thr1 = thr_ref[0:1, :]
        mx = mx_ref[...]
        for t in range(n):
            sel = key_ref[j0 + t] >= thr1
            parts = []
            for h in range(N_HEADS):
                sh = s[t * TQ:(t + 1) * TQ, h * TQ:(h + 1) * TQ]
                sh = sh + (far_ref[h] if near_idx is None else near_ref[h, near_idx])
                parts.append(jnp.where(sel, sh, MASKED))
            row = jnp.concatenate(parts, axis=1)
            s_ref[j0 + t] = row
            mx = jnp.maximum(mx, _max8(row))
        mx_ref[...] = mx

    n_far = jnp.maximum(i - 1, 0)
    _for_tiles(n_far, lambda j0, n: put(j0, n, None))

    def prev(j, carry):
        put(j, 1, 1)
        return carry

    lax.fori_loop(n_far, i, prev, 0)
    put(i, 1, 0)

    m = jnp.max(mx_ref[...], axis=0, keepdims=True)
    l_ref[...] = jnp.zeros_like(l_ref)
    acc_ref[...] = jnp.zeros_like(acc_ref)

    def pv(j0, n):
        p = jnp.exp(s_ref[pl.ds(j0, n)].reshape(n * TQ, N_HEADS * TQ) - m)
        l_ref[...] += _sum8(p)
        pb = p.astype(BF16)
        acc = acc_ref[...]
        for t in range(n):
            acc = acc + _dot(vt_ref[j0 + t], pb[t * TQ:(t + 1) * TQ, :])
        acc_ref[...] = acc

    _for_tiles(i + 1, pv)

    ot = acc_ref[...] * (1.0 / jnp.sum(l_ref[...], axis=0, keepdims=True))
    for h in range(N_HEADS):
        o_ref[:, h * H:(h + 1) * H] = ot[:, h * TQ:(h + 1) * TQ].T.astype(o_ref.dtype)


def _dsa(qib, kk, wq, dqn, dkn, dvt, near_t, far, top_k):
    B, Lp, W = dqn.shape
    TQ = SEQ_TILE
    H = HEAD_DIM
    NI = IDX_HEADS * IDX_DIM
    nt = Lp // TQ
    seq = lambda width: pl.BlockSpec((None, Lp, width), lambda b, i: (b, 0, 0))
    blk = lambda width: pl.BlockSpec((None, TQ, width), lambda b, i: (b, i, 0))
    return pl.pallas_call(
        functools.partial(_dsa_body, top_k=top_k),
        grid=(B, nt),
        in_specs=[blk(NI), seq(H), blk(H), blk(W), seq(H),
                  pl.BlockSpec((None, nt, H, TQ), lambda b, i: (b, 0, 0, 0)),
                  pl.BlockSpec((N_HEADS, 2, TQ, TQ), lambda b, i: (0, 0, 0, 0)),
                  pl.BlockSpec((N_HEADS, 1, TQ), lambda b, i: (0, 0, 0))],
        out_specs=blk(W),
        out_shape=jax.ShapeDtypeStruct((B, Lp, W), BF16),
        scratch_shapes=[
            pltpu.VMEM((IDX_HEADS // 2, 2 * TQ, H), BF16),
            pltpu.VMEM((H, TQ), F32),
            pltpu.VMEM((nt, TQ, TQ), I32),
            pltpu.VMEM((N_HEADS * TQ, H), BF16),
            pltpu.VMEM((SUBLANES, TQ), I32),
            pltpu.VMEM((SUBLANES, TQ), I32),
            pltpu.VMEM((nt, TQ, N_HEADS * TQ), F32),
            pltpu.VMEM((SUBLANES, N_HEADS * TQ), F32),
            pltpu.VMEM((SUBLANES, N_HEADS * TQ), F32),
            pltpu.VMEM((H, N_HEADS * TQ), F32),
        ],
        compiler_params=_cparams("parallel", "arbitrary"),
        name="dsa",
    )(qib, kk, wq, dqn, dkn, dvt, near_t, far)


def _merge_body(h_ref, xn_ref, b0_ref, b1_ref, b2_ref, b3_ref, g0_ref, g1_ref, g2_ref, g3_ref,
                wb_ref, wo_ref, o_ref):
    @pl.when(pl.program_id(1) == 0)
    def _():
        o_ref[...] = h_ref[...]

    xn = xn_ref[...]
    merged = None
    for m, (b_ref, g_ref) in enumerate(((b0_ref, g0_ref), (b1_ref, g1_ref), (b2_ref, g2_ref), (b3_ref, g3_ref))):
        term = jax.nn.sigmoid(_dot(xn, g_ref[...])) * _dot(b_ref[...], wb_ref[m])
        merged = term if merged is None else merged + term
    o_ref[...] += _dot(merged.astype(BF16), wo_ref[...])


def _merge(h, xn, branches, w_gate, w_branch, w_out, tm, tn):
    T, D = h.shape
    W = MIX_WIDTH
    nn = D // tn
    row = lambda width: pl.BlockSpec((tm, width), lambda i, n: (i, 0))
    gate = lambda m: pl.BlockSpec((D, tn), functools.partial(lambda i, n, m: (0, m * nn + n), m=m))
    return pl.pallas_call(
        _merge_body,
        grid=(T // tm, nn),
        in_specs=[row(D), row(D), row(W), row(W), row(W), row(W),
                  gate(0), gate(1), gate(2), gate(3),
                  pl.BlockSpec((N_BRANCH, W, tn), lambda i, n: (0, 0, n)),
                  pl.BlockSpec((tn, D), lambda i, n: (n, 0))],
        out_specs=row(D),
        out_shape=jax.ShapeDtypeStruct((T, D), F32),
        compiler_params=_cparams("parallel", "arbitrary"),
        name="merge",
    )(h, xn, *branches, w_gate, w_gate, w_gate, w_gate, w_branch, w_out)


def _rel_bucket(n):
    max_exact = N_BUCKETS // 2
    nf = jnp.maximum(n, 1).astype(F32)
    large = max_exact + (jnp.log(nf / max_exact) / math.log(MAX_DISTANCE / max_exact)
                         * (N_BUCKETS - max_exact)).astype(I32)
    large = jnp.minimum(large, N_BUCKETS - 1)
    return jnp.where(n < max_exact, n, large)


def _bias_tiles(table):
    TQ = SEQ_TILE
    qi = jnp.arange(TQ, dtype=I32)[:, None]
    ki = jnp.arange(TQ, dtype=I32)[None, :]
    d0 = qi - ki
    tbl = table.astype(F32).T
    diag = jnp.where(d0 >= 0, tbl[:, _rel_bucket(jnp.maximum(d0, 0))], MASKED)
    prev = tbl[:, _rel_bucket(d0 + TQ)]
    near_t = jnp.stack([diag, prev], axis=1).transpose(0, 1, 3, 2)
    far = jnp.broadcast_to(tbl[:, N_BUCKETS - 1][:, None, None], (tbl.shape[0], 1, TQ))
    return near_t, far


def _pack_w_in(w):
    offs = np.cumsum([0, 512, 512, 512, 512, 512, 512, 512, 512, 512, 512, 512, 128, 128,
                      IDX_HEADS * IDX_DIM, IDX_DIM, IDX_HEADS]).tolist()
    seg = lambda a, b: w[:, offs[a]:offs[b]]
    D = w.shape[0]
    used = COL_KIW + IDX_DIM + IDX_HEADS
    main = jnp.concatenate([
        seg(0, 4),
        seg(13, 14),
        seg(4, 7),
        seg(7, 10),
        seg(10, 11),
        seg(11, 12),
        seg(12, 13),
        seg(14, 16),
        jnp.zeros((D, P_COLS - used), w.dtype),
    ], axis=1).astype(BF16)
    gates = w[:, offs[16]:].astype(BF16)
    return main, gates


def _largest_tile(n, cap):
    t = cap
    while n % t:
        t //= 2
    return t


def kernel(x, meta_tokens, rel_bias, ffn1_norm, ffn1_w_gu, ffn1_w_down, mix_norm, w_in, hgrn_lb, hgrn_gnorm, conv_w, diff_q_norm, diff_k_norm, diff_lambda, diff_subln, dsa_q_norm, dsa_k_norm, w_branch, w_out, ffn2_norm, ffn2_w_gu, ffn2_w_down):
    B, S, D = x.shape
    depth = w_in.shape[0]
    L = S + N_META
    Lp = -(-L // SEQ_TILE) * SEQ_TILE
    T = B * Lp
    top_k = min(TOPK_MAX, S // 4)
    tm = _largest_tile(T, 512)
    d_ff = ffn1_w_down.shape[1]
    tf = _largest_tile(d_ff, 512)

    meta = jnp.broadcast_to(meta_tokens.astype(x.dtype)[None], (B, N_META, D))
    h = jnp.concatenate([meta, x, jnp.zeros((B, Lp - L, D), x.dtype)], axis=1).reshape(T, D)

    lbs = jnp.cumsum(jax.nn.softmax(hgrn_lb.astype(F32), axis=0), axis=0)
    lbs = lbs - lbs[0:1]
    near_c, far_c = _bias_tiles(rel_bias[:, :N_HEADS])
    near_d, far_d = _bias_tiles(rel_bias[:, N_HEADS:])

    for l in range(depth):
        h = _ffn(h, ffn1_norm[l], ffn1_w_gu[l].astype(BF16), ffn1_w_down[l].astype(BF16), tm, tf)

        w_main, w_gate = _pack_w_in(w_in[l])
        xn, P = _proj(h, mix_norm[l], w_main, tm, 1024)
        P3 = P.reshape(B, Lp, P_COLS)
        br_a = _hgrn(P3, lbs[l], hgrn_gnorm[l])
        br_b = _conv(P3, conv_w[l])
        cqn, ckn, cvt, dqn, dkn, dvt, qib, kk, wq = _prep(
            P, diff_q_norm[l], diff_k_norm[l], dsa_q_norm[l], dsa_k_norm[l], tm)
        r3 = lambda a: a.reshape(B, Lp, a.shape[-1])
        lp = diff_lambda[l].astype(F32)
        lam_init = 0.8 - 0.6 * math.exp(-0.3 * l)
        lam = jnp.exp(jnp.sum(lp[0] * lp[1])) - jnp.exp(jnp.sum(lp[2] * lp[3])) + lam_init
        tiles = lambda a: a.reshape(B, Lp // SEQ_TILE, a.shape[-2], SEQ_TILE)
        br_c = _diff(r3(cqn), r3(ckn), tiles(cvt), near_c, jnp.tile(far_c, (1, 1, 2)),
                     jnp.broadcast_to(lam, (1, SEQ_TILE)).astype(F32),
                     jnp.broadcast_to(diff_subln[l].astype(F32)[:, None], (HEAD_DIM, SEQ_TILE)),
                     1.0 - lam_init)
        br_d = _dsa(r3(qib), r3(kk), r3(wq), r3(dqn), r3(dkn), tiles(dvt), near_d, far_d, top_k)
        branches = [a.reshape(T, MIX_WIDTH) for a in (br_a, br_b, br_c, br_d)]
        h = _merge(h, xn, branches, w_gate, w_branch[l].astype(BF16), w_out[l].astype(BF16), tm, 256)

        h = _ffn(h, ffn2_norm[l], ffn2_w_gu[l].astype(BF16), ffn2_w_down[l].astype(BF16), tm, tf)

    return h.reshape(B, Lp, D)[:, N_META:L]
```

```python
import functools
import math

import jax
import jax.numpy as jnp
import numpy as np
from jax import lax
from jax.experimental import pallas as pl
from jax.experimental.pallas import tpu as pltpu

F32 = jnp.float32
BF16 = jnp.bfloat16
I32 = jnp.int32

EPS = 1e-6
N_META = 16
HEAD_DIM = 128
HALF = 64
N_HEADS = 4
MIX_WIDTH = N_HEADS * HEAD_DIM
IDX_HEADS = 16
IDX_DIM = 64
TOPK_MAX = 256
N_BRANCH = 4
N_BUCKETS = 32
MAX_DISTANCE = 128
CONV_W = 3
SEQ_TILE = 128
HGRN_CHUNK = 16
MASKED = -1e30
INT_MIN = -(2 ** 31)
KEY_NEG_INF = -2139095041
VMEM_LIMIT_BYTES = 56 * 1024 * 1024

COL_A = 0
COL_QI = 2048
COL_B = 3072
COL_C = 4608
COL_DQ = 6144
COL_DK = 6656
COL_DV = 6784
COL_KIW = 6912
P_COLS = 7168


def _cparams(*sem):
    return pltpu.CompilerParams(dimension_semantics=sem, vmem_limit_bytes=VMEM_LIMIT_BYTES)


def _dot(a, b):
    return jnp.dot(a, b, preferred_element_type=F32)


def _dot_nt(a, b):
    return lax.dot_general(a, b, (((1,), (1,)), ((), ())), preferred_element_type=F32)


def _dot_tn(a, b):
    return lax.dot_general(a, b, (((0,), (0,)), ((), ())), preferred_element_type=F32)


def _rms_inv(x):
    return lax.rsqrt(jnp.mean(x * x, axis=-1, keepdims=True) + EPS)


def _ffn_body(h_ref, g_ref, wg_ref, wu_ref, wd_ref, o_ref, xn_ref):
    j = pl.program_id(1)

    @pl.when(j == 0)
    def _():
        x = h_ref[...]
        xn_ref[...] = ((x * _rms_inv(x)) * g_ref[...]).astype(BF16)
        o_ref[...] = x

    xn = xn_ref[...]
    g = _dot(xn, wg_ref[...])
    u = _dot(xn, wu_ref[...])
    a = (g * jax.nn.sigmoid(g) * (0.5 * u)).astype(BF16)
    o_ref[...] += _dot(a, wd_ref[...])


def _ffn(h, gain, w_gu, w_down, tm, tf):
    T, D = h.shape
    F = w_down.shape[0]
    nf = F // tf
    return pl.pallas_call(
        _ffn_body,
        grid=(T // tm, nf),
        in_specs=[
            pl.BlockSpec((tm, D), lambda i, j: (i, 0)),
            pl.BlockSpec((1, D), lambda i, j: (0, 0)),
            pl.BlockSpec((D, tf), lambda i, j: (0, j)),
            pl.BlockSpec((D, tf), lambda i, j: (0, j + nf)),
            pl.BlockSpec((tf, D), lambda i, j: (j, 0)),
        ],
        out_specs=pl.BlockSpec((tm, D), lambda i, j: (i, 0)),
        out_shape=jax.ShapeDtypeStruct((T, D), F32),
        scratch_shapes=[pltpu.VMEM((tm, D), BF16)],
        compiler_params=_cparams("parallel", "arbitrary"),
        name="ffn",
    )(h, gain.reshape(1, D), w_gu, w_gu, w_down)


def _proj_body(h_ref, g_ref, w_ref, xn_ref, p_ref):
    @pl.when(pl.program_id(1) == 0)
    def _():
        x = h_ref[...]
        xn_ref[...] = ((x * _rms_inv(x)) * g_ref[...]).astype(BF16)

    p_ref[...] = _dot(xn_ref[...], w_ref[...])


def _proj(h, gain, w, tm, tn):
    T, D = h.shape
    N = w.shape[1]
    return pl.pallas_call(
        _proj_body,
        grid=(T // tm, N // tn),
        in_specs=[
            pl.BlockSpec((tm, D), lambda i, j: (i, 0)),
            pl.BlockSpec((1, D), lambda i, j: (0, 0)),
            pl.BlockSpec((D, tn), lambda i, j: (0, j)),
        ],
        out_specs=[
            pl.BlockSpec((tm, D), lambda i, j: (i, 0)),
            pl.BlockSpec((tm, tn), lambda i, j: (i, j)),
        ],
        out_shape=[jax.ShapeDtypeStruct((T, D), BF16), jax.ShapeDtypeStruct((T, N), F32)],
        compiler_params=_cparams("parallel", "arbitrary"),
        name="proj",
    )(h, gain.reshape(1, D), w)


def _hgrn_body(q_ref, f_ref, i_ref, g_ref, loglb_ref, log1mlb_ref, omlb_ref, gn_ref, o_ref,
               st_ref, qs_ref, ks_ref, bs_ref, os_ref):
    R = q_ref.shape[0]
    C = HGRN_CHUNK

    @pl.when(pl.program_id(1) == 0)
    def _():
        st_ref[...] = jnp.zeros_like(st_ref)

    z = f_ref[...]
    log_sig = jnp.minimum(z, 0.0) - jnp.log1p(jnp.exp(-jnp.abs(z)))
    y = log1mlb_ref[...] + log_sig
    a = loglb_ref[...]
    log_f = jnp.maximum(a, y) + jnp.log1p(jnp.exp(-jnp.abs(a - y)))
    row = lax.broadcasted_iota(I32, z.shape, 0) % C
    b = log_f
    for sh in (1, 2, 4, 8):
        b = b + jnp.where(row >= sh, pltpu.roll(b, sh, axis=0), 0.0)
    bs_ref[...] = b
    ks_ref[...] = omlb_ref[...] * jax.nn.sigmoid(-z)
    qr = q_ref[...]
    qs_ref[...] = qr * jax.nn.sigmoid(qr)

    t_iota = lax.broadcasted_iota(I32, (C, 1), 0)

    def chunk(c, carry):
        r0 = pl.multiple_of(c * C, C)
        for h in range(N_HEADS):
            hs = slice(h * HEAD_DIM, (h + 1) * HEAD_DIM)
            qc = qs_ref[pl.ds(r0, C), hs]
            kc = ks_ref[pl.ds(r0, C), hs]
            bc = bs_ref[pl.ds(r0, C), hs]
            vc = i_ref[pl.ds(r0, C), hs]
            st = st_ref[h]
            o = _dot_nt((qc * jnp.exp(bc)).astype(BF16), st.astype(BF16))
            for s in range(C):
                b_s = bc[s:s + 1, :]
                k_s = kc[s:s + 1, :]
                v_s = vc[s:s + 1, :]
                e = jnp.exp(jnp.minimum(bc - b_s, 0.0))
                col = jnp.sum(qc * k_s * e, axis=-1, keepdims=True)
                col = jnp.where(t_iota >= s, col, 0.0)
                o = o + col * v_s
            os_ref[pl.ds(r0, C), hs] = o
            b_last = bc[C - 1:C, :]
            kd = kc * jnp.exp(b_last - bc)
            st_ref[h] = st * jnp.exp(b_last) + _dot_tn(vc.astype(BF16), kd.astype(BF16))
        return carry

    lax.fori_loop(0, R // C, chunk, 0)

    for h in range(N_HEADS):
        hs = slice(h * HEAD_DIM, (h + 1) * HEAD_DIM)
        o = os_ref[:, hs]
        on = (o * _rms_inv(o)) * gn_ref[:, hs]
        gr = g_ref[:, hs]
        o_ref[:, hs] = (on * (gr * jax.nn.sigmoid(gr))).astype(o_ref.dtype)


def _hgrn(P3, lb, gnorm):
    B, Lp, _ = P3.shape
    R = SEQ_TILE
    W = MIX_WIDTH
    lb = lb.reshape(1, W).astype(F32)
    vec = pl.BlockSpec((1, W), lambda b, t: (0, 0))
    c0 = COL_A // W
    return pl.pallas_call(
        _hgrn_body,
        grid=(B, Lp // R),
        in_specs=[pl.BlockSpec((None, R, W), functools.partial(lambda b, t, c: (b, t, c), c=c0 + k))
                  for k in range(4)] + [vec, vec, vec, vec],
        out_specs=pl.BlockSpec((None, R, W), lambda b, t: (b, t, 0)),
        out_shape=jax.ShapeDtypeStruct((B, Lp, W), BF16),
        scratch_shapes=[pltpu.VMEM((N_HEADS, HEAD_DIM, HEAD_DIM), F32)]
        + [pltpu.VMEM((R, W), F32)] * 4,
        compiler_params=_cparams("parallel", "arbitrary"),
        name="hgrn",
    )(P3, P3, P3, P3, jnp.log(lb), jnp.log1p(-lb), 1.0 - lb, gnorm.reshape(1, W).astype(F32))


def _conv_body(b_ref, c_ref, u_ref, w_ref, o_ref, carry_ref):
    @pl.when(pl.program_id(1) == 0)
    def _():
        carry_ref[...] = jnp.zeros_like(carry_ref)

    zc = c_ref[...] * u_ref[...]
    R = zc.shape[0]
    row = lax.broadcasted_iota(I32, zc.shape, 0)
    last = carry_ref[7:8, :]
    last2 = carry_ref[6:7, :]
    z1 = jnp.where(row == 0, last, pltpu.roll(zc, 1, axis=0))
    z2 = jnp.where(row == 0, last2, jnp.where(row == 1, last, pltpu.roll(zc, 2, axis=0)))
    y = w_ref[0:1, :] * zc + w_ref[1:2, :] * z1 + w_ref[2:3, :] * z2
    o_ref[...] = (b_ref[...] * y).astype(o_ref.dtype)
    carry_ref[...] = zc[R - 8:, :]


def _conv(P3, conv_w):
    B, Lp, _ = P3.shape
    R = SEQ_TILE
    W = MIX_WIDTH
    c0 = COL_B // W
    w8 = jnp.zeros((8, W), F32).at[:CONV_W].set(conv_w.astype(F32))
    return pl.pallas_call(
        _conv_body,
        grid=(B, Lp // R),
        in_specs=[pl.BlockSpec((None, R, W), functools.partial(lambda b, t, c: (b, t, c), c=c0 + k))
                  for k in range(3)] + [pl.BlockSpec((8, W), lambda b, t: (0, 0))],
        out_specs=pl.BlockSpec((None, R, W), lambda b, t: (b, t, 0)),
        out_shape=jax.ShapeDtypeStruct((B, Lp, W), BF16),
        scratch_shapes=[pltpu.VMEM((8, W), F32)],
        compiler_params=_cparams("parallel", "arbitrary"),
        name="conv",
    )(P3, P3, P3, w8)


def _prep_body(cq_ref, ck_ref, cv_ref, dq_ref, dk_ref, dv_ref, kiw_ref, qi_ref,
               cqg_ref, ckg_ref, dqg_ref, dkg_ref,
               cqn_ref, ckn_ref, cvb_ref, dqn_ref, dkn_ref, dvb_ref, qib_ref, kk_ref, wq_ref):
    lane = lax.broadcasted_iota(I32, (1, HEAD_DIM), 1)
    lo = lane < HALF

    def half_norm(x, g, scale):
        sq = x * x
        ms_lo = jnp.sum(jnp.where(lo, sq, 0.0), axis=-1, keepdims=True) * (1.0 / HALF)
        ms_hi = jnp.sum(jnp.where(lo, 0.0, sq), axis=-1, keepdims=True) * (1.0 / HALF)
        inv = jnp.where(lo, lax.rsqrt(ms_lo + EPS), lax.rsqrt(ms_hi + EPS))
        return ((x * inv) * g) * scale

    for h in range(N_HEADS):
        hs = slice(h * HEAD_DIM, (h + 1) * HEAD_DIM)
        cqn_ref[:, hs] = half_norm(cq_ref[:, hs], cqg_ref[...], HALF ** -0.5).astype(BF16)
        ckn_ref[:, hs] = half_norm(ck_ref[:, hs], ckg_ref[...], 1.0).astype(BF16)
        x = dq_ref[:, hs]
        dqn_ref[:, hs] = (((x * _rms_inv(x)) * dqg_ref[...]) * (HEAD_DIM ** -0.5)).astype(BF16)
    x = dk_ref[...]
    dkn_ref[...] = ((x * _rms_inv(x)) * dkg_ref[...]).astype(BF16)
    for t in range(cv_ref.shape[0] // SEQ_TILE):
        rs = slice(t * SEQ_TILE, (t + 1) * SEQ_TILE)
        for h in range(N_HEADS):
            hs = slice(h * HEAD_DIM, (h + 1) * HEAD_DIM)
            cvb_ref[t, hs, :] = cv_ref[rs, hs].T.astype(BF16)
        dvb_ref[t] = dv_ref[rs, :].T.astype(BF16)
    qib_ref[...] = (qi_ref[...] * (IDX_DIM ** -0.5)).astype(BF16)
    kiw = kiw_ref[...]
    swapped = pltpu.roll(kiw, HALF, axis=1)
    kk_ref[...] = jnp.where(lo, kiw, swapped).astype(BF16)
    wq_ref[...] = swapped * (IDX_HEADS ** -0.5)


def _prep(P, cq_g, ck_g, dq_g, dk_g, tm):
    T = P.shape[0]
    W = MIX_WIDTH
    H = HEAD_DIM

    def col(width, start):
        return pl.BlockSpec((tm, width), functools.partial(lambda i, c: (i, c), c=start // width))

    vec = pl.BlockSpec((1, H), lambda i: (0, 0))
    out = lambda width: pl.BlockSpec((tm, width), lambda i: (i, 0))
    tiles = lambda width: pl.BlockSpec((tm // SEQ_TILE, width, SEQ_TILE), lambda i: (i, 0, 0))
    return pl.pallas_call(
        _prep_body,
        grid=(T // tm,),
        in_specs=[col(W, COL_C), col(W, COL_C + W), col(W, COL_C + 2 * W), col(W, COL_DQ),
                  col(H, COL_DK), col(H, COL_DV), col(H, COL_KIW), col(IDX_HEADS * IDX_DIM, COL_QI),
                  vec, vec, vec, vec],
        out_specs=[out(W), out(W), tiles(W), out(W), out(H), tiles(H), out(IDX_HEADS * IDX_DIM), out(H), out(H)],
        out_shape=[jax.ShapeDtypeStruct((T, W), BF16)] * 2
        + [jax.ShapeDtypeStruct((T // SEQ_TILE, W, SEQ_TILE), BF16), jax.ShapeDtypeStruct((T, W), BF16),
           jax.ShapeDtypeStruct((T, H), BF16), jax.ShapeDtypeStruct((T // SEQ_TILE, H, SEQ_TILE), BF16)]
        + [jax.ShapeDtypeStruct((T, IDX_HEADS * IDX_DIM), BF16),
           jax.ShapeDtypeStruct((T, H), BF16), jax.ShapeDtypeStruct((T, H), F32)],
        compiler_params=_cparams("parallel"),
        name="prep",
    )(P, P, P, P, P, P, P, P,
      jnp.tile(cq_g.astype(F32), 2).reshape(1, H), jnp.tile(ck_g.astype(F32), 2).reshape(1, H),
      dq_g.astype(F32).reshape(1, H), dk_g.astype(F32).reshape(1, H))


SUBLANES = 8
KEY_CHUNK = 4


def _max8(x):
    return jnp.max(x.reshape(x.shape[0] // SUBLANES, SUBLANES, x.shape[1]), axis=0)


def _sum8(x):
    return jnp.sum(x.reshape(x.shape[0] // SUBLANES, SUBLANES, x.shape[1]), axis=0)


def _for_tiles(n_tiles, body, chunk=KEY_CHUNK):
    n_chunks = n_tiles // chunk

    def chunked(c, carry):
        body(c * chunk, chunk)
        return carry

    lax.fori_loop(0, n_chunks, chunked, 0)
    base = n_chunks * chunk
    size = chunk // 2
    while size >= 1:
        take = (n_tiles - base) // size

        def part(_, carry, base=base, size=size):
            body(base, size)
            return carry

        lax.fori_loop(0, take, part, 0)
        base = base + take * size
        size //= 2


def _diff_body(q_ref, k_ref, vt_ref, near_ref, far_ref, lam_ref, sub_ref, o_ref,
               q2_ref, s_ref, mx_ref, l_ref, acc_ref, *, out_scale):
    i = pl.program_id(1)
    TQ = SEQ_TILE
    H = HEAD_DIM
    W2 = 2 * TQ
    lane = lax.broadcasted_iota(I32, (1, H), 1)
    for h in range(N_HEADS):
        q = q_ref[:, h * H:(h + 1) * H]
        zero = jnp.zeros_like(q)
        q2_ref[h, 0:TQ, :] = jnp.where(lane < HALF, q, zero)
        q2_ref[h, TQ:, :] = jnp.where(lane < HALF, zero, q)
    mx_ref[...] = jnp.full_like(mx_ref, MASKED)

    def put(j0, n, near_idx):
        k0 = pl.multiple_of(j0 * TQ, TQ)
        parts = []
        for h in range(N_HEADS):
            s = _dot_nt(k_ref[pl.ds(k0, n * TQ), h * H:(h + 1) * H], q2_ref[h])
            if near_idx is None:
                parts.append(s + far_ref[h])
            else:
                b = near_ref[h, near_idx]
                parts.append(s + jnp.concatenate([b, b], axis=1))
        s = jnp.concatenate(parts, axis=1)
        for t in range(n):
            s_ref[j0 + t] = s[t * TQ:(t + 1) * TQ, :]
        mx_ref[...] = jnp.maximum(mx_ref[...], _max8(s))

    n_far = jnp.maximum(i - 1, 0)
    _for_tiles(n_far, lambda j0, n: put(j0, n, None))

    def prev(j, carry):
        put(j, 1, 1)
        return carry

    lax.fori_loop(n_far, i, prev, 0)
    put(i, 1, 0)

    m = jnp.max(mx_ref[...], axis=0, keepdims=True)
    l_ref[...] = jnp.zeros_like(l_ref)
    acc_ref[...] = jnp.zeros_like(acc_ref)

    def pv(j0, n):
        p = jnp.exp(s_ref[pl.ds(j0, n)].reshape(n * TQ, N_HEADS * W2) - m)
        l_ref[...] += _sum8(p)
        pb = p.astype(BF16)
        for h in range(N_HEADS):
            acc = acc_ref[:, h * W2:(h + 1) * W2]
            for t in range(n):
                acc = acc + _dot(vt_ref[j0 + t, h * H:(h + 1) * H, :],
                                 pb[t * TQ:(t + 1) * TQ, h * W2:(h + 1) * W2])
            acc_ref[:, h * W2:(h + 1) * W2] = acc

    _for_tiles(i + 1, pv)

    ot = acc_ref[...] * (1.0 / jnp.sum(l_ref[...], axis=0, keepdims=True))
    for h in range(N_HEADS):
        od = ot[:, h * W2:h * W2 + TQ] - lam_ref[...] * ot[:, h * W2 + TQ:(h + 1) * W2]
        inv = lax.rsqrt(jnp.mean(od * od, axis=0, keepdims=True) + EPS)
        o_ref[:, h * H:(h + 1) * H] = (((od * inv) * sub_ref[...]) * out_scale).T.astype(o_ref.dtype)


def _diff(cqn, ckn, cvt, near_t, far, lam, subln, out_scale):
    B, Lp, W = cqn.shape
    TQ = SEQ_TILE
    H = HEAD_DIM
    nt = Lp // TQ
    return pl.pallas_call(
        functools.partial(_diff_body, out_scale=out_scale),
        grid=(B, nt),
        in_specs=[
            pl.BlockSpec((None, TQ, W), lambda b, i: (b, i, 0)),
            pl.BlockSpec((None, Lp, W), lambda b, i: (b, 0, 0)),
            pl.BlockSpec((None, nt, W, TQ), lambda b, i: (b, 0, 0, 0)),
            pl.BlockSpec((N_HEADS, 2, TQ, TQ), lambda b, i: (0, 0, 0, 0)),
            pl.BlockSpec((N_HEADS, 1, 2 * TQ), lambda b, i: (0, 0, 0)),
            pl.BlockSpec((1, TQ), lambda b, i: (0, 0)),
            pl.BlockSpec((H, TQ), lambda b, i: (0, 0)),
        ],
        out_specs=pl.BlockSpec((None, TQ, W), lambda b, i: (b, i, 0)),
        out_shape=jax.ShapeDtypeStruct((B, Lp, W), BF16),
        scratch_shapes=[pltpu.VMEM((N_HEADS, 2 * TQ, H), BF16),
                        pltpu.VMEM((nt, TQ, N_HEADS * 2 * TQ), F32),
                        pltpu.VMEM((SUBLANES, N_HEADS * 2 * TQ), F32),
                        pltpu.VMEM((SUBLANES, N_HEADS * 2 * TQ), F32),
                        pltpu.VMEM((H, N_HEADS * 2 * TQ), F32)],
        compiler_params=_cparams("parallel", "arbitrary"),
        name="diff",
    )(cqn, ckn, cvt, near_t, far, lam, subln)


def _dsa_body(qi_ref, kk_ref, wq_ref, q_ref, k_ref, vt_ref, near_ref, far_ref, o_ref,
              qi2_ref, w_ref, key_ref, q4_ref, thr_ref, cnt_ref, s_ref, mx_ref, l_ref, acc_ref, *, top_k):
    i = pl.program_id(1)
    TQ = SEQ_TILE
    H = HEAD_DIM
    lane = lax.broadcasted_iota(I32, (1, H), 1)

    for p in range(IDX_HEADS // 2):
        x = qi_ref[:, p * H:(p + 1) * H]
        zero = jnp.zeros_like(x)
        qi2_ref[p, 0:TQ, :] = jnp.where(lane < HALF, x, zero)
        qi2_ref[p, TQ:, :] = jnp.where(lane < HALF, zero, x)
    w_ref[...] = wq_ref[...].T
    for h in range(N_HEADS):
        q4_ref[h * TQ:(h + 1) * TQ, :] = q_ref[:, h * H:(h + 1) * H]

    def index(j0, n):
        k0 = pl.multiple_of(j0 * TQ, TQ)
        kt = kk_ref[pl.ds(k0, n * TQ), :]
        acc = jnp.zeros((n * TQ, TQ), F32)
        for p in range(IDX_HEADS // 2):
            s = jnp.maximum(_dot_nt(kt, qi2_ref[p]), 0.0)
            acc = acc + s[:, 0:TQ] * w_ref[2 * p:2 * p + 1, :] + s[:, TQ:] * w_ref[2 * p + 1:2 * p + 2, :]
        acc = acc + 0.0
        bits = pltpu.bitcast(acc, I32)
        key = bits ^ ((bits >> 31) & jnp.int32(0x7FFFFFFF))
        for t in range(n):
            key_ref[j0 + t] = key[t * TQ:(t + 1) * TQ, :]

    _for_tiles(i + 1, index)
    kq = lax.broadcasted_iota(I32, (TQ, TQ), 0) - lax.broadcasted_iota(I32, (TQ, TQ), 1)
    key_ref[i] = jnp.where(kq <= 0, key_ref[i], jnp.int32(KEY_NEG_INF))

    thr_ref[...] = jnp.full_like(thr_ref, INT_MIN)

    def bit_step(t, carry):
        cand = thr_ref[...] + jnp.left_shift(jnp.int32(1), 31 - t)
        cand1 = cand[0:1, :]

        cnt_ref[...] = jnp.zeros_like(cnt_ref)

        def count(j0, n):
            ge = key_ref[pl.ds(j0, n)].reshape(n * TQ, TQ) >= cand1
            cnt_ref[...] += _sum8(jnp.where(ge, 1, 0).astype(I32))

        _for_tiles(i + 1, count)
        total = jnp.sum(cnt_ref[...], axis=0, keepdims=True)
        thr_ref[...] = jnp.where(total >= top_k, cand, thr_ref[...])
        return carry

    lax.fori_loop(0, 32, bit_step, 0)

    mx_ref[...] = jnp.full_like(mx_ref, MASKED)

    def put(j0, n, near_idx):
        k0 = pl.multiple_of(j0 * TQ, TQ)
        s = _dot_nt(k_ref[pl.ds(k0, n * TQ), :], q4_ref[...])
        thr1 = thr_ref[0:1, :]
        mx = mx_ref[...]
        for t in range(n):
            sel = key_ref[j0 + t] >= thr1
            parts = []
            for h in range(N_HEADS):
                sh = s[t * TQ:(t + 1) * TQ, h * TQ:(h + 1) * TQ]
                sh = sh + (far_ref[h] if near_idx is None else near_ref[h, near_idx])
                parts.append(jnp.where(sel, sh, MASKED))
            row = jnp.concatenate(parts, axis=1)
            s_ref[j0 + t] = row
            mx = jnp.maximum(mx, _max8(row))
        mx_ref[...] = mx

    n_far = jnp.maximum(i - 1, 0)
    _for_tiles(n_far, lambda j0, n: put(j0, n, None))

    def prev(j, carry):
        put(j, 1, 1)
        return carry

    lax.fori_loop(n_far, i, prev, 0)
    put(i, 1, 0)

    m = jnp.max(mx_ref[...], axis=0, keepdims=True)
    l_ref[...] = jnp.zeros_like(l_ref)
    acc_ref[...] = jnp.zeros_like(acc_ref)

    def pv(j0, n):
        p = jnp.exp(s_ref[pl.ds(j0, n)].reshape(n * TQ, N_HEADS * TQ) - m)
        l_ref[...] += _sum8(p)
        pb = p.astype(BF16)
        acc = acc_ref[...]
        for t in range(n):
            acc = acc + _dot(vt_ref[j0 + t], pb[t * TQ:(t + 1) * TQ, :])
        acc_ref[...] = acc

    _for_tiles(i + 1, pv)

    ot = acc_ref[...] * (1.0 / jnp.sum(l_ref[...], axis=0, keepdims=True))
    for h in range(N_HEADS):
        o_ref[:, h * H:(h + 1) * H] = ot[:, h * TQ:(h + 1) * TQ].T.astype(o_ref.dtype)


def _dsa(qib, kk, wq, dqn, dkn, dvt, near_t, far, top_k):
    B, Lp, W = dqn.shape
    TQ = SEQ_TILE
    H = HEAD_DIM
    NI = IDX_HEADS * IDX_DIM
    nt = Lp // TQ
    seq = lambda width: pl.BlockSpec((None, Lp, width), lambda b, i: (b, 0, 0))
    blk = lambda width: pl.BlockSpec((None, TQ, width), lambda b, i: (b, i, 0))
    return pl.pallas_call(
        functools.partial(_dsa_body, top_k=top_k),
        grid=(B, nt),
        in_specs=[blk(NI), seq(H), blk(H), blk(W), seq(H),
                  pl.BlockSpec((None, nt, H, TQ), lambda b, i: (b, 0, 0, 0)),
                  pl.BlockSpec((N_HEADS, 2, TQ, TQ), lambda b, i: (0, 0, 0, 0)),
                  pl.BlockSpec((N_HEADS, 1, TQ), lambda b, i: (0, 0, 0))],
        out_specs=blk(W),
        out_shape=jax.ShapeDtypeStruct((B, Lp, W), BF16),
        scratch_shapes=[
            pltpu.VMEM((IDX_HEADS // 2, 2 * TQ, H), BF16),
            pltpu.VMEM((H, TQ), F32),
            pltpu.VMEM((nt, TQ, TQ), I32),
            pltpu.VMEM((N_HEADS * TQ, H), BF16),
            pltpu.VMEM((SUBLANES, TQ), I32),
            pltpu.VMEM((SUBLANES, TQ), I32),
            pltpu.VMEM((nt, TQ, N_HEADS * TQ), F32),
            pltpu.VMEM((SUBLANES, N_HEADS * TQ), F32),
            pltpu.VMEM((SUBLANES, N_HEADS * TQ), F32),
            pltpu.VMEM((H, N_HEADS * TQ), F32),
        ],
        compiler_params=_cparams("parallel", "arbitrary"),
        name="dsa",
    )(qib, kk, wq, dqn, dkn, dvt, near_t, far)


def _merge_body(h_ref, xn_ref, b0_ref, b1_ref, b2_ref, b3_ref, g0_ref, g1_ref, g2_ref, g3_ref,
                wb_ref, wo_ref, o_ref):
    @pl.when(pl.program_id(1) == 0)
    def _():
        o_ref[...] = h_ref[...]

    xn = xn_ref[...]
    merged = None
    for m, (b_ref, g_ref) in enumerate(((b0_ref, g0_ref), (b1_ref, g1_ref), (b2_ref, g2_ref), (b3_ref, g3_ref))):
        term = jax.nn.sigmoid(_dot(xn, g_ref[...])) * _dot(b_ref[...], wb_ref[m])
        merged = term if merged is None else merged + term
    o_ref[...] += _dot(merged.astype(BF16), wo_ref[...])


def _merge(h, xn, branches, w_gate, w_branch, w_out, tm, tn):
    T, D = h.shape
    W = MIX_WIDTH
    nn = D // tn
    row = lambda width: pl.BlockSpec((tm, width), lambda i, n: (i, 0))
    gate = lambda m: pl.BlockSpec((D, tn), functools.partial(lambda i, n, m: (0, m * nn + n), m=m))
    return pl.pallas_call(
        _merge_body,
        grid=(T // tm, nn),
        in_specs=[row(D), row(D), row(W), row(W), row(W), row(W),
                  gate(0), gate(1), gate(2), gate(3),
                  pl.BlockSpec((N_BRANCH, W, tn), lambda i, n: (0, 0, n)),
                  pl.BlockSpec((tn, D), lambda i, n: (n, 0))],
        out_specs=row(D),
        out_shape=jax.ShapeDtypeStruct((T, D), F32),
        compiler_params=_cparams("parallel", "arbitrary"),
        name="merge",
    )(h, xn, *branches, w_gate, w_gate, w_gate, w_gate, w_branch, w_out)


def _rel_bucket(n):
    max_exact = N_BUCKETS // 2
    nf = jnp.maximum(n, 1).astype(F32)
    large = max_exact + (jnp.log(nf / max_exact) / math.log(MAX_DISTANCE / max_exact)
                         * (N_BUCKETS - max_exact)).astype(I32)
    large = jnp.minimum(large, N_BUCKETS - 1)
    return jnp.where(n < max_exact, n, large)


def _bias_tiles(table):
    TQ = SEQ_TILE
    qi = jnp.arange(TQ, dtype=I32)[:, None]
    ki = jnp.arange(TQ, dtype=I32)[None, :]
    d0 = qi - ki
    tbl = table.astype(F32).T
    diag = jnp.where(d0 >= 0, tbl[:, _rel_bucket(jnp.maximum(d0, 0))], MASKED)
    prev = tbl[:, _rel_bucket(d0 + TQ)]
    near_t = jnp.stack([diag, prev], axis=1).transpose(0, 1, 3, 2)
    far = jnp.broadcast_to(tbl[:, N_BUCKETS - 1][:, None, None], (tbl.shape[0], 1, TQ))
    return near_t, far


def _pack_w_in(w):
    offs = np.cumsum([0, 512, 512, 512, 512, 512, 512, 512, 512, 512, 512, 512, 128, 128,
                      IDX_HEADS * IDX_DIM, IDX_DIM, IDX_HEADS]).tolist()
    seg = lambda a, b: w[:, offs[a]:offs[b]]
    D = w.shape[0]
    used = COL_KIW + IDX_DIM + IDX_HEADS
    main = jnp.concatenate([
        seg(0, 4),
        seg(13, 14),
        seg(4, 7),
        seg(7, 10),
        seg(10, 11),
        seg(11, 12),
        seg(12, 13),
        seg(14, 16),
        jnp.zeros((D, P_COLS - used), w.dtype),
    ], axis=1).astype(BF16)
    gates = w[:, offs[16]:].astype(BF16)
    return main, gates


def _largest_tile(n, cap):
    t = cap
    while n % t:
        t //= 2
    return t


def kernel(x, meta_tokens, rel_bias, ffn1_norm, ffn1_w_gu, ffn1_w_down, mix_norm, w_in, hgrn_lb, hgrn_gnorm, conv_w, diff_q_norm, diff_k_norm, diff_lambda, diff_subln, dsa_q_norm, dsa_k_norm, w_branch, w_out, ffn2_norm, ffn2_w_gu, ffn2_w_down):
    B, S, D = x.shape
    depth = w_in.shape[0]
    L = S + N_META
    Lp = -(-L // SEQ_TILE) * SEQ_TILE
    T = B * Lp
    top_k = min(TOPK_MAX, S // 4)
    tm = _largest_tile(T, 512)
    d_ff = ffn1_w_down.shape[1]
    tf = _largest_tile(d_ff, 512)

    meta = jnp.broadcast_to(meta_tokens.astype(x.dtype)[None], (B, N_META, D))
    h = jnp.concatenate([meta, x, jnp.zeros((B, Lp - L, D), x.dtype)], axis=1).reshape(T, D)

    lbs = jnp.cumsum(jax.nn.softmax(hgrn_lb.astype(F32), axis=0), axis=0)
    lbs = lbs - lbs[0:1]
    near_c, far_c = _bias_tiles(rel_bias[:, :N_HEADS])
    near_d, far_d = _bias_tiles(rel_bias[:, N_HEADS:])

    for l in range(depth):
        h = _ffn(h, ffn1_norm[l], ffn1_w_gu[l].astype(BF16), ffn1_w_down[l].astype(BF16), tm, tf)

        w_main, w_gate = _pack_w_in(w_in[l])
        xn, P = _proj(h, mix_norm[l], w_main, tm, 1024)
        P3 = P.reshape(B, Lp, P_COLS)
        br_a = _hgrn(P3, lbs[l], hgrn_gnorm[l])
        br_b = _conv(P3, conv_w[l])
        cqn, ckn, cvt, dqn, dkn, dvt, qib, kk, wq = _prep(
            P, diff_q_norm[l], diff_k_norm[l], dsa_q_norm[l], dsa_k_norm[l], tm)
        r3 = lambda a: a.reshape(B, Lp, a.shape[-1])
        lp = diff_lambda[l].astype(F32)
        lam_init = 0.8 - 0.6 * math.exp(-0.3 * l)
        lam = jnp.exp(jnp.sum(lp[0] * lp[1])) - jnp.exp(jnp.sum(lp[2] * lp[3])) + lam_init
        tiles = lambda a: a.reshape(B, Lp // SEQ_TILE, a.shape[-2], SEQ_TILE)
        br_c = _diff(r3(cqn), r3(ckn), tiles(cvt), near_c, jnp.tile(far_c, (1, 1, 2)),
                     jnp.broadcast_to(lam, (1, SEQ_TILE)).astype(F32),
                     jnp.broadcast_to(diff_subln[l].astype(F32)[:, None], (HEAD_DIM, SEQ_TILE)),
                     1.0 - lam_init)
        br_d = _dsa(r3(qib), r3(kk), r3(wq), r3(dqn), r3(dkn), tiles(dvt), near_d, far_d, top_k)
        branches = [a.reshape(T, MIX_WIDTH) for a in (br_a, br_b, br_c, br_d)]
        h = _merge(h, xn, branches, w_gate, w_branch[l].astype(BF16), w_out[l].astype(BF16), tm, 256)

        h = _ffn(h, ffn2_norm[l], ffn2_w_gu[l].astype(BF16), ffn2_w_down[l].astype(BF16), tm, tf)

    return h.reshape(B, Lp, D)[:, N_META:L]
```

```python
import functools
import math

import jax
import jax.numpy as jnp
import numpy as np
from jax import lax
from jax.experimental import pallas as pl
from jax.experimental.pallas import tpu as pltpu

F32 = jnp.float32
BF16 = jnp.bfloat16
I32 = jnp.int32

EPS = 1e-6
N_META = 16
HEAD_DIM = 128
HALF = 64
N_HEADS = 4
MIX_WIDTH = N_HEADS * HEAD_DIM
IDX_HEADS = 16
IDX_DIM = 64
TOPK_MAX = 256
N_BRANCH = 4
N_BUCKETS = 32
MAX_DISTANCE = 128
CONV_W = 3
SEQ_TILE = 128
HGRN_CHUNK = 16
MASKED = -1e30
INT_MIN = -(2 ** 31)
KEY_NEG_INF = -2139095041
VMEM_LIMIT_BYTES = 56 * 1024 * 1024

COL_A = 0
COL_QI = 2048
COL_B = 3072
COL_C = 4608
COL_DQ = 6144
COL_DK = 6656
COL_DV = 6784
COL_KIW = 6912
P_COLS = 7168


def _cparams(*sem):
    return pltpu.CompilerParams(dimension_semantics=sem, vmem_limit_bytes=VMEM_LIMIT_BYTES)


def _dot(a, b):
    return jnp.dot(a, b, preferred_element_type=F32)


def _dot_nt(a, b):
    return lax.dot_general(a, b, (((1,), (1,)), ((), ())), preferred_element_type=F32)


def _dot_tn(a, b):
    return lax.dot_general(a, b, (((0,), (0,)), ((), ())), preferred_element_type=F32)


def _rms_inv(x):
    return lax.rsqrt(jnp.mean(x * x, axis=-1, keepdims=True) + EPS)


def _ffn_body(h_ref, g_ref, wg_ref, wu_ref, wd_ref, o_ref, xn_ref):
    j = pl.program_id(1)

    @pl.when(j == 0)
    def _():
        x = h_ref[...]
        xn_ref[...] = ((x * _rms_inv(x)) * g_ref[...]).astype(BF16)
        o_ref[...] = x

    xn = xn_ref[...]
    g = _dot(xn, wg_ref[...])
    u = _dot(xn, wu_ref[...])
    a = (g * jax.nn.sigmoid(g) * (0.5 * u)).astype(BF16)
    o_ref[...] += _dot(a, wd_ref[...])


def _ffn(h, gain, w_gu, w_down, tm, tf):
    T, D = h.shape
    F = w_down.shape[0]
    nf = F // tf
    return pl.pallas_call(
        _ffn_body,
        grid=(T // tm, nf),
        in_specs=[
            pl.BlockSpec((tm, D), lambda i, j: (i, 0)),
            pl.BlockSpec((1, D), lambda i, j: (0, 0)),
            pl.BlockSpec((D, tf), lambda i, j: (0, j)),
            pl.BlockSpec((D, tf), lambda i, j: (0, j + nf)),
            pl.BlockSpec((tf, D), lambda i, j: (j, 0)),
        ],
        out_specs=pl.BlockSpec((tm, D), lambda i, j: (i, 0)),
        out_shape=jax.ShapeDtypeStruct((T, D), F32),
        scratch_shapes=[pltpu.VMEM((tm, D), BF16)],
        compiler_params=_cparams("parallel", "arbitrary"),
        name="ffn",
    )(h, gain.reshape(1, D), w_gu, w_gu, w_down)


def _proj_body(h_ref, g_ref, w_ref, xn_ref, p_ref):
    @pl.when(pl.program_id(1) == 0)
    def _():
        x = h_ref[...]
        xn_ref[...] = ((x * _rms_inv(x)) * g_ref[...]).astype(BF16)

    p_ref[...] = _dot(xn_ref[...], w_ref[...])


def _proj(h, gain, w, tm, tn):
    T, D = h.shape
    N = w.shape[1]
    return pl.pallas_call(
        _proj_body,
        grid=(T // tm, N // tn),
        in_specs=[
            pl.BlockSpec((tm, D), lambda i, j: (i, 0)),
            pl.BlockSpec((1, D), lambda i, j: (0, 0)),
            pl.BlockSpec((D, tn), lambda i, j: (0, j)),
        ],
        out_specs=[
            pl.BlockSpec((tm, D), lambda i, j: (i, 0)),
            pl.BlockSpec((tm, tn), lambda i, j: (i, j)),
        ],
        out_shape=[jax.ShapeDtypeStruct((T, D), BF16), jax.ShapeDtypeStruct((T, N), F32)],
        compiler_params=_cparams("parallel", "arbitrary"),
        name="proj",
    )(h, gain.reshape(1, D), w)


def _hgrn_body(q_ref, f_ref, i_ref, g_ref, loglb_ref, log1mlb_ref, omlb_ref, gn_ref, o_ref,
               st_ref, qs_ref, ks_ref, bs_ref, os_ref):
    R = q_ref.shape[0]
    C = HGRN_CHUNK

    @pl.when(pl.program_id(1) == 0)
    def _():
        st_ref[...] = jnp.zeros_like(st_ref)

    z = f_ref[...]
    log_sig = jnp.minimum(z, 0.0) - jnp.log1p(jnp.exp(-jnp.abs(z)))
    y = log1mlb_ref[...] + log_sig
    a = loglb_ref[...]
    log_f = jnp.maximum(a, y) + jnp.log1p(jnp.exp(-jnp.abs(a - y)))
    row = lax.broadcasted_iota(I32, z.shape, 0) % C
    b = log_f
    for sh in (1, 2, 4, 8):
        b = b + jnp.where(row >= sh, pltpu.roll(b, sh, axis=0), 0.0)
    bs_ref[...] = b
    ks_ref[...] = omlb_ref[...] * jax.nn.sigmoid(-z)
    qr = q_ref[...]
    qs_ref[...] = qr * jax.nn.sigmoid(qr)

    t_iota = lax.broadcasted_iota(I32, (C, 1), 0)

    def chunk(c, carry):
        r0 = pl.multiple_of(c * C, C)
        for h in range(N_HEADS):
            hs = slice(h * HEAD_DIM, (h + 1) * HEAD_DIM)
            qc = qs_ref[pl.ds(r0, C), hs]
            kc = ks_ref[pl.ds(r0, C), hs]
            bc = bs_ref[pl.ds(r0, C), hs]
            vc = i_ref[pl.ds(r0, C), hs]
            st = st_ref[h]
            o = _dot_nt((qc * jnp.exp(bc)).astype(BF16), st.astype(BF16))
            for s in range(C):
                b_s = bc[s:s + 1, :]
                k_s = kc[s:s + 1, :]
                v_s = vc[s:s + 1, :]
                e = jnp.exp(jnp.minimum(bc - b_s, 0.0))
                col = jnp.sum(qc * k_s * e, axis=-1, keepdims=True)
                col = jnp.where(t_iota >= s, col, 0.0)
                o = o + col * v_s
            os_ref[pl.ds(r0, C), hs] = o
            b_last = bc[C - 1:C, :]
            kd = kc * jnp.exp(b_last - bc)
            st_ref[h] = st * jnp.exp(b_last) + _dot_tn(vc.astype(BF16), kd.astype(BF16))
        return carry

    lax.fori_loop(0, R // C, chunk, 0)

    for h in range(N_HEADS):
        hs = slice(h * HEAD_DIM, (h + 1) * HEAD_DIM)
        o = os_ref[:, hs]
        on = (o * _rms_inv(o)) * gn_ref[:, hs]
        gr = g_ref[:, hs]
        o_ref[:, hs] = (on * (gr * jax.nn.sigmoid(gr))).astype(o_ref.dtype)


def _hgrn(P3, lb, gnorm):
    B, Lp, _ = P3.shape
    R = SEQ_TILE
    W = MIX_WIDTH
    lb = lb.reshape(1, W).astype(F32)
    vec = pl.BlockSpec((1, W), lambda b, t: (0, 0))
    c0 = COL_A // W
    return pl.pallas_call(
        _hgrn_body,
        grid=(B, Lp // R),
        in_specs=[pl.BlockSpec((None, R, W), functools.partial(lambda b, t, c: (b, t, c), c=c0 + k))
                  for k in range(4)] + [vec, vec, vec, vec],
        out_specs=pl.BlockSpec((None, R, W), lambda b, t: (b, t, 0)),
        out_shape=jax.ShapeDtypeStruct((B, Lp, W), BF16),
        scratch_shapes=[pltpu.VMEM((N_HEADS, HEAD_DIM, HEAD_DIM), F32)]
        + [pltpu.VMEM((R, W), F32)] * 4,
        compiler_params=_cparams("parallel", "arbitrary"),
        name="hgrn",
    )(P3, P3, P3, P3, jnp.log(lb), jnp.log1p(-lb), 1.0 - lb, gnorm.reshape(1, W).astype(F32))


def _conv_body(b_ref, c_ref, u_ref, w_ref, o_ref, carry_ref):
    @pl.when(pl.program_id(1) == 0)
    def _():
        carry_ref[...] = jnp.zeros_like(carry_ref)

    zc = c_ref[...] * u_ref[...]
    R = zc.shape[0]
    row = lax.broadcasted_iota(I32, zc.shape, 0)
    last = carry_ref[7:8, :]
    last2 = carry_ref[6:7, :]
    z1 = jnp.where(row == 0, last, pltpu.roll(zc, 1, axis=0))
    z2 = jnp.where(row == 0, last2, jnp.where(row == 1, last, pltpu.roll(zc, 2, axis=0)))
    y = w_ref[0:1, :] * zc + w_ref[1:2, :] * z1 + w_ref[2:3, :] * z2
    o_ref[...] = (b_ref[...] * y).astype(o_ref.dtype)
    carry_ref[...] = zc[R - 8:, :]


def _conv(P3, conv_w):
    B, Lp, _ = P3.shape
    R = SEQ_TILE
    W = MIX_WIDTH
    c0 = COL_B // W
    w8 = jnp.zeros((8, W), F32).at[:CONV_W].set(conv_w.astype(F32))
    return pl.pallas_call(
        _conv_body,
        grid=(B, Lp // R),
        in_specs=[pl.BlockSpec((None, R, W), functools.partial(lambda b, t, c: (b, t, c), c=c0 + k))
                  for k in range(3)] + [pl.BlockSpec((8, W), lambda b, t: (0, 0))],
        out_specs=pl.BlockSpec((None, R, W), lambda b, t: (b, t, 0)),
        out_shape=jax.ShapeDtypeStruct((B, Lp, W), BF16),
        scratch_shapes=[pltpu.VMEM((8, W), F32)],
        compiler_params=_cparams("parallel", "arbitrary"),
        name="conv",
    )(P3, P3, P3, w8)


def _prep_body(cq_ref, ck_ref, cv_ref, dq_ref, dk_ref, dv_ref, kiw_ref, qi_ref,
               cqg_ref, ckg_ref, dqg_ref, dkg_ref,
               cqn_ref, ckn_ref, cvb_ref, dqn_ref, dkn_ref, dvb_ref, qib_ref, kk_ref, wq_ref):
    lane = lax.broadcasted_iota(I32, (1, HEAD_DIM), 1)
    lo = lane < HALF

    def half_norm(x, g, scale):
        sq = x * x
        ms_lo = jnp.sum(jnp.where(lo, sq, 0.0), axis=-1, keepdims=True) * (1.0 / HALF)
        ms_hi = jnp.sum(jnp.where(lo, 0.0, sq), axis=-1, keepdims=True) * (1.0 / HALF)
        inv = jnp.where(lo, lax.rsqrt(ms_lo + EPS), lax.rsqrt(ms_hi + EPS))
        return ((x * inv) * g) * scale

    for h in range(N_HEADS):
        hs = slice(h * HEAD_DIM, (h + 1) * HEAD_DIM)
        cqn_ref[:, hs] = half_norm(cq_ref[:, hs], cqg_ref[...], HALF ** -0.5).astype(BF16)
        ckn_ref[:, hs] = half_norm(ck_ref[:, hs], ckg_ref[...], 1.0).astype(BF16)
        x = dq_ref[:, hs]
        dqn_ref[:, hs] = (((x * _rms_inv(x)) * dqg_ref[...]) * (HEAD_DIM ** -0.5)).astype(BF16)
    x = dk_ref[...]
    dkn_ref[...] = ((x * _rms_inv(x)) * dkg_ref[...]).astype(BF16)
    for t in range(cv_ref.shape[0] // SEQ_TILE):
        rs = slice(t * SEQ_TILE, (t + 1) * SEQ_TILE)
        for h in range(N_HEADS):
            hs = slice(h * HEAD_DIM, (h + 1) * HEAD_DIM)
            cvb_ref[t, hs, :] = cv_ref[rs, hs].T.astype(BF16)
        dvb_ref[t] = dv_ref[rs, :].T.astype(BF16)
    qib_ref[...] = (qi_ref[...] * (IDX_DIM ** -0.5)).astype(BF16)
    kiw = kiw_ref[...]
    swapped = pltpu.roll(kiw, HALF, axis=1)
    kk_ref[...] = jnp.where(lo, kiw, swapped).astype(BF16)
    wq_ref[...] = swapped * (IDX_HEADS ** -0.5)


def _prep(P, cq_g, ck_g, dq_g, dk_g, tm):
    T = P.shape[0]
    W = MIX_WIDTH
    H = HEAD_DIM

    def col(width, start):
        return pl.BlockSpec((tm, width), functools.partial(lambda i, c: (i, c), c=start // width))

    vec = pl.BlockSpec((1, H), lambda i: (0, 0))
    out = lambda width: pl.BlockSpec((tm, width), lambda i: (i, 0))
    tiles = lambda width: pl.BlockSpec((tm // SEQ_TILE, width, SEQ_TILE), lambda i: (i, 0, 0))
    return pl.pallas_call(
        _prep_body,
        grid=(T // tm,),
        in_specs=[col(W, COL_C), col(W, COL_C + W), col(W, COL_C + 2 * W), col(W, COL_DQ),
                  col(H, COL_DK), col(H, COL_DV), col(H, COL_KIW), col(IDX_HEADS * IDX_DIM, COL_QI),
                  vec, vec, vec, vec],
        out_specs=[out(W), out(W), tiles(W), out(W), out(H), tiles(H), out(IDX_HEADS * IDX_DIM), out(H), out(H)],
        out_shape=[jax.ShapeDtypeStruct((T, W), BF16)] * 2
        + [jax.ShapeDtypeStruct((T // SEQ_TILE, W, SEQ_TILE), BF16), jax.ShapeDtypeStruct((T, W), BF16),
           jax.ShapeDtypeStruct((T, H), BF16), jax.ShapeDtypeStruct((T // SEQ_TILE, H, SEQ_TILE), BF16)]
        + [jax.ShapeDtypeStruct((T, IDX_HEADS * IDX_DIM), BF16),
           jax.ShapeDtypeStruct((T, H), BF16), jax.ShapeDtypeStruct((T, H), F32)],
        compiler_params=_cparams("parallel"),
        name="prep",
    )(P, P, P, P, P, P, P, P,
      jnp.tile(cq_g.astype(F32), 2).reshape(1, H), jnp.tile(ck_g.astype(F32), 2).reshape(1, H),
      dq_g.astype(F32).reshape(1, H), dk_g.astype(F32).reshape(1, H))


SUBLANES = 8
KEY_CHUNK = 4


def _max8(x):
    return jnp.max(x.reshape(x.shape[0] // SUBLANES, SUBLANES, x.shape[1]), axis=0)


def _sum8(x):
    return jnp.sum(x.reshape(x.shape[0] // SUBLANES, SUBLANES, x.shape[1]), axis=0)


def _bit_transpose32(rows):
    a = list(rows)
    j = 16
    m = 0x0000FFFF
    while j:
        k = 0
        while k < 32:
            t = (a[k] ^ lax.shift_right_logical(a[k + j], jnp.int32(j))) & jnp.int32(m)
            a[k] = a[k] ^ t
            a[k + j] = a[k + j] ^ (t << j)
            k = (k + j + 1) & ~j
        j >>= 1
        m ^= (m << j) & 0xFFFFFFFF
    return a


def _for_tiles(n_tiles, body, chunk=KEY_CHUNK):
    n_chunks = n_tiles // chunk

    def chunked(c, carry):
        body(c * chunk, chunk)
        return carry

    lax.fori_loop(0, n_chunks, chunked, 0)
    base = n_chunks * chunk
    size = chunk // 2
    while size >= 1:
        take = (n_tiles - base) // size

        def part(_, carry, base=base, size=size):
            body(base, size)
            return carry

        lax.fori_loop(0, take, part, 0)
        base = base + take * size
        size //= 2


def _diff_body(q_ref, k_ref, vt_ref, near_ref, far_ref, lam_ref, sub_ref, o_ref,
               q2_ref, s_ref, mx_ref, l_ref, acc_ref, *, out_scale):
    i = pl.program_id(1)
    TQ = SEQ_TILE
    H = HEAD_DIM
    W2 = 2 * TQ
    lane = lax.broadcasted_iota(I32, (1, H), 1)
    for h in range(N_HEADS):
        q = q_ref[:, h * H:(h + 1) * H]
        zero = jnp.zeros_like(q)
        q2_ref[h, 0:TQ, :] = jnp.where(lane < HALF, q, zero)
        q2_ref[h, TQ:, :] = jnp.where(lane < HALF, zero, q)
    mx_ref[...] = jnp.full_like(mx_ref, MASKED)

    def put(j0, n, near_idx):
        k0 = pl.multiple_of(j0 * TQ, TQ)
        parts = []
        for h in range(N_HEADS):
            s = _dot_nt(k_ref[pl.ds(k0, n * TQ), h * H:(h + 1) * H], q2_ref[h])
            if near_idx is None:
                parts.append(s + far_ref[h])
            else:
                b = near_ref[h, near_idx]
                parts.append(s + jnp.concatenate([b, b], axis=1))
        s = jnp.concatenate(parts, axis=1)
        for t in range(n):
            s_ref[j0 + t] = s[t * TQ:(t + 1) * TQ, :]
        mx_ref[...] = jnp.maximum(mx_ref[...], _max8(s))

    n_far = jnp.maximum(i - 1, 0)
    _for_tiles(n_far, lambda j0, n: put(j0, n, None))

    def prev(j, carry):
        put(j, 1, 1)
        return carry

    lax.fori_loop(n_far, i, prev, 0)
    put(i, 1, 0)

    m = jnp.max(mx_ref[...], axis=0, keepdims=True)
    l_ref[...] = jnp.zeros_like(l_ref)
    acc_ref[...] = jnp.zeros_like(acc_ref)

    def pv(j0, n):
        p = jnp.exp(s_ref[pl.ds(j0, n)].reshape(n * TQ, N_HEADS * W2) - m)
        l_ref[...] += _sum8(p)
        pb = p.astype(BF16)
        for h in range(N_HEADS):
            acc = acc_ref[:, h * W2:(h + 1) * W2]
            for t in range(n):
                acc = acc + _dot(vt_ref[j0 + t, h * H:(h + 1) * H, :],
                                 pb[t * TQ:(t + 1) * TQ, h * W2:(h + 1) * W2])
            acc_ref[:, h * W2:(h + 1) * W2] = acc

    _for_tiles(i + 1, pv)

    ot = acc_ref[...] * (1.0 / jnp.sum(l_ref[...], axis=0, keepdims=True))
    for h in range(N_HEADS):
        od = ot[:, h * W2:h * W2 + TQ] - lam_ref[...] * ot[:, h * W2 + TQ:(h + 1) * W2]
        inv = lax.rsqrt(jnp.mean(od * od, axis=0, keepdims=True) + EPS)
        o_ref[:, h * H:(h + 1) * H] = (((od * inv) * sub_ref[...]) * out_scale).T.astype(o_ref.dtype)


def _diff(cqn, ckn, cvt, near_t, far, lam, subln, out_scale):
    B, Lp, W = cqn.shape
    TQ = SEQ_TILE
    H = HEAD_DIM
    nt = Lp // TQ
    return pl.pallas_call(
        functools.partial(_diff_body, out_scale=out_scale),
        grid=(B, nt),
        in_specs=[
            pl.BlockSpec((None, TQ, W), lambda b, i: (b, i, 0)),
            pl.BlockSpec((None, Lp, W), lambda b, i: (b, 0, 0)),
            pl.BlockSpec((None, nt, W, TQ), lambda b, i: (b, 0, 0, 0)),
            pl.BlockSpec((N_HEADS, 2, TQ, TQ), lambda b, i: (0, 0, 0, 0)),
            pl.BlockSpec((N_HEADS, 1, 2 * TQ), lambda b, i: (0, 0, 0)),
            pl.BlockSpec((1, TQ), lambda b, i: (0, 0)),
            pl.BlockSpec((H, TQ), lambda b, i: (0, 0)),
        ],
        out_specs=pl.BlockSpec((None, TQ, W), lambda b, i: (b, i, 0)),
        out_shape=jax.ShapeDtypeStruct((B, Lp, W), BF16),
        scratch_shapes=[pltpu.VMEM((N_HEADS, 2 * TQ, H), BF16),
                        pltpu.VMEM((nt, TQ, N_HEADS * 2 * TQ), F32),
                        pltpu.VMEM((SUBLANES, N_HEADS * 2 * TQ), F32),
                        pltpu.VMEM((SUBLANES, N_HEADS * 2 * TQ), F32),
                        pltpu.VMEM((H, N_HEADS * 2 * TQ), F32)],
        compiler_params=_cparams("parallel", "arbitrary"),
        name="diff",
    )(cqn, ckn, cvt, near_t, far, lam, subln)


def _dsa_body(qi_ref, kk_ref, wq_ref, q_ref, k_ref, vt_ref, near_ref, far_ref, o_ref,
              qi2_ref, w_ref, key_ref, kq_ref, plane_ref, alive_ref, q4_ref, thr_ref, s_ref, mx_ref, l_ref, acc_ref, *, top_k):
    i = pl.program_id(1)
    TQ = SEQ_TILE
    H = HEAD_DIM
    lane = lax.broadcasted_iota(I32, (1, H), 1)

    for p in range(IDX_HEADS // 2):
        x = qi_ref[:, p * H:(p + 1) * H]
        zero = jnp.zeros_like(x)
        qi2_ref[p, 0:TQ, :] = jnp.where(lane < HALF, x, zero)
        qi2_ref[p, TQ:, :] = jnp.where(lane < HALF, zero, x)
    w_ref[...] = wq_ref[...].T
    for h in range(N_HEADS):
        q4_ref[h * TQ:(h + 1) * TQ, :] = q_ref[:, h * H:(h + 1) * H]

    def index(j0, n):
        k0 = pl.multiple_of(j0 * TQ, TQ)
        kt = kk_ref[pl.ds(k0, n * TQ), :]
        acc = jnp.zeros((n * TQ, TQ), F32)
        for p in range(IDX_HEADS // 2):
            s = jnp.maximum(_dot_nt(kt, qi2_ref[p]), 0.0)
            acc = acc + s[:, 0:TQ] * w_ref[2 * p:2 * p + 1, :] + s[:, TQ:] * w_ref[2 * p + 1:2 * p + 2, :]
        acc = acc + 0.0
        bits = pltpu.bitcast(acc, I32)
        key = bits ^ ((bits >> 31) & jnp.int32(0x7FFFFFFF))
        tiles = []
        for t in range(n):
            kt_ = key[t * TQ:(t + 1) * TQ, :]
            if t == n - 1:
                kt_ = jnp.where(kq_ref[...] <= (i - (j0 + t)) * TQ, kt_, jnp.int32(KEY_NEG_INF))
            key_ref[j0 + t] = kt_
            tiles.append(kt_)
        if n == 1:
            tiles.append(None)
        for u in range(len(tiles) // 2):
            rows = []
            for tile in tiles[2 * u:2 * u + 2]:
                for r in range(TQ // SUBLANES):
                    if tile is None:
                        rows.append(jnp.zeros((SUBLANES, TQ), I32))
                    else:
                        rows.append(tile[r * SUBLANES:(r + 1) * SUBLANES, :] ^ jnp.int32(INT_MIN))
            planes = _bit_transpose32(rows)
            pair = j0 // 2 + u
            for b in range(32):
                plane_ref[pair, b] = planes[b]

    @pl.when(i == 0)
    def _():
        plane_ref[...] = jnp.zeros_like(plane_ref)

    kq_ref[...] = lax.broadcasted_iota(I32, (TQ, TQ), 0) - lax.broadcasted_iota(I32, (TQ, TQ), 1)
    _for_tiles(i + 1, index)

    n_pairs = plane_ref.shape[0]
    for tp in range(n_pairs):
        alive_ref[tp] = jnp.full((SUBLANES, TQ), -1, I32)

    def bit_step(t, carry):
        above, thr = carry
        ones = [alive_ref[tp] & plane_ref[tp, t] for tp in range(n_pairs)]
        cnt = lax.population_count(ones[0])
        for tp in range(1, n_pairs):
            cnt = cnt + lax.population_count(ones[tp])
        hit = above + jnp.sum(cnt, axis=0, keepdims=True)
        take = hit >= top_k
        for tp in range(n_pairs):
            alive_ref[tp] = jnp.where(take, ones[tp], alive_ref[tp] ^ ones[tp])
        thr = thr | jnp.where(take, jnp.left_shift(jnp.int32(1), 31 - t), 0)
        return jnp.where(take, above, hit), thr

    zero8 = jnp.zeros((SUBLANES, TQ), I32)
    _, thr_u = lax.fori_loop(0, 32, bit_step, (zero8, zero8))
    thr_ref[...] = thr_u ^ jnp.int32(INT_MIN)

    mx_ref[...] = jnp.full_like(mx_ref, MASKED)

    def put(j0, n, near_idx):
        k0 = pl.multiple_of(j0 * TQ, TQ)
        s = _dot_nt(k_ref[pl.ds(k0, n * TQ), :], q4_ref[...])
        thr1 = thr_ref[0:1, :]
        mx = mx_ref[...]
        for t in range(n):
            sel = key_ref[j0 + t] >= thr1
            parts = []
            for h in range(N_HEADS):
                sh = s[t * TQ:(t + 1) * TQ, h * TQ:(h + 1) * TQ]
                sh = sh + (far_ref[h] if near_idx is None else near_ref[h, near_idx])
                parts.append(jnp.where(sel, sh, MASKED))
            row = jnp.concatenate(parts, axis=1)
            s_ref[j0 + t] = row
            mx = jnp.maximum(mx, _max8(row))
        mx_ref[...] = mx

    n_far = jnp.maximum(i - 1, 0)
    _for_tiles(n_far, lambda j0, n: put(j0, n, None))

    def prev(j, carry):
        put(j, 1, 1)
        return carry

    lax.fori_loop(n_far, i, prev, 0)
    put(i, 1, 0)

    m = jnp.max(mx_ref[...], axis=0, keepdims=True)
    l_ref[...] = jnp.zeros_like(l_ref)
    acc_ref[...] = jnp.zeros_like(acc_ref)

    def pv(j0, n):
        p = jnp.exp(s_ref[pl.ds(j0, n)].reshape(n * TQ, N_HEADS * TQ) - m)
        l_ref[...] += _sum8(p)
        pb = p.astype(BF16)
        acc = acc_ref[...]
        for t in range(n):
            acc = acc + _dot(vt_ref[j0 + t], pb[t * TQ:(t + 1) * TQ, :])
        acc_ref[...] = acc

    _for_tiles(i + 1, pv)

    ot = acc_ref[...] * (1.0 / jnp.sum(l_ref[...], axis=0, keepdims=True))
    for h in range(N_HEADS):
        o_ref[:, h * H:(h + 1) * H] = ot[:, h * TQ:(h + 1) * TQ].T.astype(o_ref.dtype)


def _dsa(qib, kk, wq, dqn, dkn, dvt, near_t, far, top_k):
    B, Lp, W = dqn.shape
    TQ = SEQ_TILE
    H = HEAD_DIM
    NI = IDX_HEADS * IDX_DIM
    nt = Lp // TQ
    seq = lambda width: pl.BlockSpec((None, Lp, width), lambda b, i: (b, 0, 0))
    blk = lambda width: pl.BlockSpec((None, TQ, width), lambda b, i: (b, i, 0))
    return pl.pallas_call(
        functools.partial(_dsa_body, top_k=top_k),
        grid=(B, nt),
        in_specs=[blk(NI), seq(H), blk(H), blk(W), seq(H),
                  pl.BlockSpec((None, nt, H, TQ), lambda b, i: (b, 0, 0, 0)),
                  pl.BlockSpec((N_HEADS, 2, TQ, TQ), lambda b, i: (0, 0, 0, 0)),
                  pl.BlockSpec((N_HEADS, 1, TQ), lambda b, i: (0, 0, 0))],
        out_specs=blk(W),
        out_shape=jax.ShapeDtypeStruct((B, Lp, W), BF16),
        scratch_shapes=[
            pltpu.VMEM((IDX_HEADS // 2, 2 * TQ, H), BF16),
            pltpu.VMEM((H, TQ), F32),
            pltpu.VMEM((nt, TQ, TQ), I32),
            pltpu.VMEM((TQ, TQ), I32),
            pltpu.VMEM(((nt + 1) // 2, 32, SUBLANES, TQ), I32),
            pltpu.VMEM(((nt + 1) // 2, SUBLANES, TQ), I32),
            pltpu.VMEM((N_HEADS * TQ, H), BF16),
            pltpu.VMEM((SUBLANES, TQ), I32),
            pltpu.VMEM((nt, TQ, N_HEADS * TQ), F32),
            pltpu.VMEM((SUBLANES, N_HEADS * TQ), F32),
            pltpu.VMEM((SUBLANES, N_HEADS * TQ), F32),
            pltpu.VMEM((H, N_HEADS * TQ), F32),
        ],
        compiler_params=_cparams("parallel", "arbitrary"),
        name="dsa",
    )(qib, kk, wq, dqn, dkn, dvt, near_t, far)


def _merge_body(h_ref, xn_ref, b0_ref, b1_ref, b2_ref, b3_ref, g0_ref, g1_ref, g2_ref, g3_ref,
                wb_ref, wo_ref, o_ref):
    @pl.when(pl.program_id(1) == 0)
    def _():
        o_ref[...] = h_ref[...]

    xn = xn_ref[...]
    merged = None
    for m, (b_ref, g_ref) in enumerate(((b0_ref, g0_ref), (b1_ref, g1_ref), (b2_ref, g2_ref), (b3_ref, g3_ref))):
        term = jax.nn.sigmoid(_dot(xn, g_ref[...])) * _dot(b_ref[...], wb_ref[m])
        merged = term if merged is None else merged + term
    o_ref[...] += _dot(merged.astype(BF16), wo_ref[...])


def _merge(h, xn, branches, w_gate, w_branch, w_out, tm, tn):
    T, D = h.shape
    W = MIX_WIDTH
    nn = D // tn
    row = lambda width: pl.BlockSpec((tm, width), lambda i, n: (i, 0))
    gate = lambda m: pl.BlockSpec((D, tn), functools.partial(lambda i, n, m: (0, m * nn + n), m=m))
    return pl.pallas_call(
        _merge_body,
        grid=(T // tm, nn),
        in_specs=[row(D), row(D), row(W), row(W), row(W), row(W),
                  gate(0), gate(1), gate(2), gate(3),
                  pl.BlockSpec((N_BRANCH, W, tn), lambda i, n: (0, 0, n)),
                  pl.BlockSpec((tn, D), lambda i, n: (n, 0))],
        out_specs=row(D),
        out_shape=jax.ShapeDtypeStruct((T, D), F32),
        compiler_params=_cparams("parallel", "arbitrary"),
        name="merge",
    )(h, xn, *branches, w_gate, w_gate, w_gate, w_gate, w_branch, w_out)


def _rel_bucket(n):
    max_exact = N_BUCKETS // 2
    nf = jnp.maximum(n, 1).astype(F32)
    large = max_exact + (jnp.log(nf / max_exact) / math.log(MAX_DISTANCE / max_exact)
                         * (N_BUCKETS - max_exact)).astype(I32)
    large = jnp.minimum(large, N_BUCKETS - 1)
    return jnp.where(n < max_exact, n, large)


def _bias_tiles(table):
    TQ = SEQ_TILE
    qi = jnp.arange(TQ, dtype=I32)[:, None]
    ki = jnp.arange(TQ, dtype=I32)[None, :]
    d0 = qi - ki
    tbl = table.astype(F32).T

    def lookup(bucket):
        out = jnp.zeros((tbl.shape[0],) + bucket.shape, F32)
        for b in range(N_BUCKETS):
            out = jnp.where(bucket[None] == b, tbl[:, b][:, None, None], out)
        return out

    diag = jnp.where(d0 >= 0, lookup(_rel_bucket(jnp.maximum(d0, 0))), MASKED)
    prev = lookup(_rel_bucket(d0 + TQ))
    near_t = jnp.stack([diag, prev], axis=1).transpose(0, 1, 3, 2)
    far = jnp.broadcast_to(tbl[:, N_BUCKETS - 1][:, None, None], (tbl.shape[0], 1, TQ))
    return near_t, far


def _pack_w_in(w):
    offs = np.cumsum([0, 512, 512, 512, 512, 512, 512, 512, 512, 512, 512, 512, 128, 128,
                      IDX_HEADS * IDX_DIM, IDX_DIM, IDX_HEADS]).tolist()
    seg = lambda a, b: w[:, offs[a]:offs[b]]
    D = w.shape[0]
    used = COL_KIW + IDX_DIM + IDX_HEADS
    main = jnp.concatenate([
        seg(0, 4),
        seg(13, 14),
        seg(4, 7),
        seg(7, 10),
        seg(10, 11),
        seg(11, 12),
        seg(12, 13),
        seg(14, 16),
        jnp.zeros((D, P_COLS - used), w.dtype),
    ], axis=1).astype(BF16)
    gates = w[:, offs[16]:].astype(BF16)
    return main, gates


def _largest_tile(n, cap):
    t = cap
    while n % t:
        t //= 2
    return t


def kernel(x, meta_tokens, rel_bias, ffn1_norm, ffn1_w_gu, ffn1_w_down, mix_norm, w_in, hgrn_lb, hgrn_gnorm, conv_w, diff_q_norm, diff_k_norm, diff_lambda, diff_subln, dsa_q_norm, dsa_k_norm, w_branch, w_out, ffn2_norm, ffn2_w_gu, ffn2_w_down):
    B, S, D = x.shape
    depth = w_in.shape[0]
    L = S + N_META
    Lp = -(-L // SEQ_TILE) * SEQ_TILE
    T = B * Lp
    top_k = min(TOPK_MAX, S // 4)
    tm = _largest_tile(T, 512)
    d_ff = ffn1_w_down.shape[1]
    tf = _largest_tile(d_ff, 512)

    meta = jnp.broadcast_to(meta_tokens.astype(x.dtype)[None], (B, N_META, D))
    h = jnp.concatenate([meta, x, jnp.zeros((B, Lp - L, D), x.dtype)], axis=1).reshape(T, D)

    lbs = jnp.cumsum(jax.nn.softmax(hgrn_lb.astype(F32), axis=0), axis=0)
    lbs = lbs - lbs[0:1]
    near_c, far_c = _bias_tiles(rel_bias[:, :N_HEADS])
    near_d, far_d = _bias_tiles(rel_bias[:, N_HEADS:])

    for l in range(depth):
        h = _ffn(h, ffn1_norm[l], ffn1_w_gu[l].astype(BF16), ffn1_w_down[l].astype(BF16), tm, tf)

        w_main, w_gate = _pack_w_in(w_in[l])
        xn, P = _proj(h, mix_norm[l], w_main, tm, 1024)
        P3 = P.reshape(B, Lp, P_COLS)
        br_a = _hgrn(P3, lbs[l], hgrn_gnorm[l])
        br_b = _conv(P3, conv_w[l])
        cqn, ckn, cvt, dqn, dkn, dvt, qib, kk, wq = _prep(
            P, diff_q_norm[l], diff_k_norm[l], dsa_q_norm[l], dsa_k_norm[l], tm)
        r3 = lambda a: a.reshape(B, Lp, a.shape[-1])
        lp = diff_lambda[l].astype(F32)
        lam_init = 0.8 - 0.6 * math.exp(-0.3 * l)
        lam = jnp.exp(jnp.sum(lp[0] * lp[1])) - jnp.exp(jnp.sum(lp[2] * lp[3])) + lam_init
        tiles = lambda a: a.reshape(B, Lp // SEQ_TILE, a.shape[-2], SEQ_TILE)
        br_c = _diff(r3(cqn), r3(ckn), tiles(cvt), near_c, jnp.tile(far_c, (1, 1, 2)),
                     jnp.broadcast_to(lam, (1, SEQ_TILE)).astype(F32),
                     jnp.broadcast_to(diff_subln[l].astype(F32)[:, None], (HEAD_DIM, SEQ_TILE)),
                     1.0 - lam_init)
        br_d = _dsa(r3(qib), r3(kk), r3(wq), r3(dqn), r3(dkn), tiles(dvt), near_d, far_d, top_k)
        branches = [a.reshape(T, MIX_WIDTH) for a in (br_a, br_b, br_c, br_d)]
        h = _merge(h, xn, branches, w_gate, w_branch[l].astype(BF16), w_out[l].astype(BF16), tm, 256)

        h = _ffn(h, ffn2_norm[l], ffn2_w_gu[l].astype(BF16), ffn2_w_down[l].astype(BF16), tm, tf)

    return h.reshape(B, Lp, D)[:, N_META:L]
```

```python
import functools
import math

import jax
import jax.numpy as jnp
import numpy as np
from jax import lax
from jax.experimental import pallas as pl
from jax.experimental.pallas import tpu as pltpu

F32 = jnp.float32
BF16 = jnp.bfloat16
I32 = jnp.int32

EPS = 1e-6
N_META = 16
HEAD_DIM = 128
HALF = 64
N_HEADS = 4
MIX_WIDTH = N_HEADS * HEAD_DIM
IDX_HEADS = 16
IDX_DIM = 64
TOPK_MAX = 256
N_BRANCH = 4
N_BUCKETS = 32
MAX_DISTANCE = 128
CONV_W = 3
SEQ_TILE = 128
HGRN_CHUNK = 16
MASKED = -1e30
INT_MIN = -(2 ** 31)
KEY_NEG_INF = -2139095041
VMEM_LIMIT_BYTES = 56 * 1024 * 1024

COL_A = 0
COL_QI = 2048
COL_B = 3072
COL_C = 4608
COL_DQ = 6144
COL_DK = 6656
COL_DV = 6784
COL_KIW = 6912
P_COLS = 7168


def _cparams(*sem):
    return pltpu.CompilerParams(dimension_semantics=sem, vmem_limit_bytes=VMEM_LIMIT_BYTES)


def _dot(a, b):
    return jnp.dot(a, b, preferred_element_type=F32)


def _dot_nt(a, b):
    return lax.dot_general(a, b, (((1,), (1,)), ((), ())), preferred_element_type=F32)


def _dot_tn(a, b):
    return lax.dot_general(a, b, (((0,), (0,)), ((), ())), preferred_element_type=F32)


def _rms_inv(x):
    return lax.rsqrt(jnp.mean(x * x, axis=-1, keepdims=True) + EPS)


def _ffn_body(*refs, has_delta):
    h_ref = refs[0]
    d_ref = refs[1] if has_delta else None
    g_ref, wg_ref, wu_ref, wd_ref, o_ref, xn_ref = refs[1 + has_delta:]
    j = pl.program_id(1)

    @pl.when(j == 0)
    def _():
        x = h_ref[...]
        if has_delta:
            x = x + d_ref[...]
        xn_ref[...] = ((x * _rms_inv(x)) * g_ref[...]).astype(BF16)
        o_ref[...] = x

    xn = xn_ref[...]
    g = _dot(xn, wg_ref[...])
    u = _dot(xn, wu_ref[...])
    a = (g * jax.nn.sigmoid(g) * (0.5 * u)).astype(BF16)
    o_ref[...] += _dot(a, wd_ref[...])


def _ffn(h, gain, w_gu, w_down, tm, tf, delta=None, keep=None):
    T, D = h.shape
    F = w_down.shape[0]
    nf = F // tf
    rows = [h] if delta is None else [h, delta]
    if keep is None:
        n_row_tiles = T // tm
        row_in = pl.BlockSpec((tm, D), lambda i, j: (i, 0))
        t_out = T
    else:
        B, Lp, first, n_rows = keep
        per_seq = n_rows // tm
        n_row_tiles = B * per_seq
        row_in = pl.BlockSpec((pl.Element(tm), pl.Element(D)),
                              lambda i, j: (pl.multiple_of(
                                  (i // per_seq) * Lp + first + (i % per_seq) * tm, SUBLANES), 0))
        t_out = B * n_rows
    return pl.pallas_call(
        functools.partial(_ffn_body, has_delta=delta is not None),
        grid=(n_row_tiles, nf),
        in_specs=[row_in for _ in rows] + [
            pl.BlockSpec((1, D), lambda i, j: (0, 0)),
            pl.BlockSpec((D, tf), lambda i, j: (0, j)),
            pl.BlockSpec((D, tf), lambda i, j: (0, j + nf)),
            pl.BlockSpec((tf, D), lambda i, j: (j, 0)),
        ],
        out_specs=pl.BlockSpec((tm, D), lambda i, j: (i, 0)),
        out_shape=jax.ShapeDtypeStruct((t_out, D), F32),
        scratch_shapes=[pltpu.VMEM((tm, D), BF16)],
        compiler_params=_cparams("parallel", "arbitrary"),
        name="ffn",
    )(*rows, gain.reshape(1, D), w_gu, w_gu, w_down)


def _proj_body(h_ref, g_ref, w_ref, xn_ref, p_ref):
    @pl.when(pl.program_id(1) == 0)
    def _():
        x = h_ref[...]
        xn_ref[...] = ((x * _rms_inv(x)) * g_ref[...]).astype(BF16)

    p_ref[...] = _dot(xn_ref[...], w_ref[...])


def _proj(h, gain, w, tm, tn):
    T, D = h.shape
    N = w.shape[1]
    return pl.pallas_call(
        _proj_body,
        grid=(T // tm, N // tn),
        in_specs=[
            pl.BlockSpec((tm, D), lambda i, j: (i, 0)),
            pl.BlockSpec((1, D), lambda i, j: (0, 0)),
            pl.BlockSpec((D, tn), lambda i, j: (0, j)),
        ],
        out_specs=[
            pl.BlockSpec((tm, D), lambda i, j: (i, 0)),
            pl.BlockSpec((tm, tn), lambda i, j: (i, j)),
        ],
        out_shape=[jax.ShapeDtypeStruct((T, D), BF16), jax.ShapeDtypeStruct((T, N), F32)],
        compiler_params=_cparams("parallel", "arbitrary"),
        name="proj",
    )(h, gain.reshape(1, D), w)


def _hgrn_body(q_ref, f_ref, i_ref, g_ref, loglb_ref, log1mlb_ref, omlb_ref, gn_ref, o_ref,
               st_ref, qs_ref, ks_ref, bs_ref, os_ref):
    R = q_ref.shape[0]
    C = HGRN_CHUNK

    @pl.when(pl.program_id(1) == 0)
    def _():
        st_ref[...] = jnp.zeros_like(st_ref)

    z = f_ref[...]
    log_sig = jnp.minimum(z, 0.0) - jnp.log1p(jnp.exp(-jnp.abs(z)))
    y = log1mlb_ref[...] + log_sig
    a = loglb_ref[...]
    log_f = jnp.maximum(a, y) + jnp.log1p(jnp.exp(-jnp.abs(a - y)))
    row = lax.broadcasted_iota(I32, z.shape, 0) % C
    b = log_f
    for sh in (1, 2, 4, 8):
        b = b + jnp.where(row >= sh, pltpu.roll(b, sh, axis=0), 0.0)
    bs_ref[...] = b
    ks_ref[...] = omlb_ref[...] * jax.nn.sigmoid(-z)
    qr = q_ref[...]
    qs_ref[...] = qr * jax.nn.sigmoid(qr)

    t_iota = lax.broadcasted_iota(I32, (C, 1), 0)

    def chunk(c, carry):
        r0 = pl.multiple_of(c * C, C)
        for h in range(N_HEADS):
            hs = slice(h * HEAD_DIM, (h + 1) * HEAD_DIM)
            qc = qs_ref[pl.ds(r0, C), hs]
            kc = ks_ref[pl.ds(r0, C), hs]
            bc = bs_ref[pl.ds(r0, C), hs]
            vc = i_ref[pl.ds(r0, C), hs]
            st = st_ref[h]
            o = _dot_nt((qc * jnp.exp(bc)).astype(BF16), st.astype(BF16))
            for s in range(C):
                b_s = bc[s:s + 1, :]
                k_s = kc[s:s + 1, :]
                v_s = vc[s:s + 1, :]
                e = jnp.exp(bc - b_s)
                col = jnp.sum(qc * k_s * e, axis=-1, keepdims=True)
                col = jnp.where(t_iota >= s, col, 0.0)
                o = o + col * v_s
            os_ref[pl.ds(r0, C), hs] = o
            b_last = bc[C - 1:C, :]
            kd = kc * jnp.exp(b_last - bc)
            st_ref[h] = st * jnp.exp(b_last) + _dot_tn(vc.astype(BF16), kd.astype(BF16))
        return carry

    lax.fori_loop(0, R // C, chunk, 0)

    for h in range(N_HEADS):
        hs = slice(h * HEAD_DIM, (h + 1) * HEAD_DIM)
        o = os_ref[:, hs]
        on = (o * _rms_inv(o)) * gn_ref[:, hs]
        gr = g_ref[:, hs]
        o_ref[:, hs] = (on * (gr * jax.nn.sigmoid(gr))).astype(o_ref.dtype)


def _hgrn(P3, lb, gnorm):
    B, Lp, _ = P3.shape
    R = SEQ_TILE
    W = MIX_WIDTH
    lb = lb.reshape(1, W).astype(F32)
    vec = pl.BlockSpec((1, W), lambda b, t: (0, 0))
    c0 = COL_A // W
    return pl.pallas_call(
        _hgrn_body,
        grid=(B, Lp // R),
        in_specs=[pl.BlockSpec((None, R, W), functools.partial(lambda b, t, c: (b, t, c), c=c0 + k))
                  for k in range(4)] + [vec, vec, vec, vec],
        out_specs=pl.BlockSpec((None, R, W), lambda b, t: (b, t, 0)),
        out_shape=jax.ShapeDtypeStruct((B, Lp, W), BF16),
        scratch_shapes=[pltpu.VMEM((N_HEADS, HEAD_DIM, HEAD_DIM), F32)]
        + [pltpu.VMEM((R, W), F32)] * 4,
        compiler_params=_cparams("parallel", "arbitrary"),
        name="hgrn",
    )(P3, P3, P3, P3, jnp.log(lb), jnp.log1p(-lb), 1.0 - lb, gnorm.reshape(1, W).astype(F32))


def _conv_body(b_ref, c_ref, u_ref, w_ref, o_ref, carry_ref):
    @pl.when(pl.program_id(1) == 0)
    def _():
        carry_ref[...] = jnp.zeros_like(carry_ref)

    zc = c_ref[...] * u_ref[...]
    R = zc.shape[0]
    row = lax.broadcasted_iota(I32, zc.shape, 0)
    last = carry_ref[7:8, :]
    last2 = carry_ref[6:7, :]
    z1 = jnp.where(row == 0, last, pltpu.roll(zc, 1, axis=0))
    z2 = jnp.where(row == 0, last2, jnp.where(row == 1, last, pltpu.roll(zc, 2, axis=0)))
    y = w_ref[0:1, :] * zc + w_ref[1:2, :] * z1 + w_ref[2:3, :] * z2
    o_ref[...] = (b_ref[...] * y).astype(o_ref.dtype)
    carry_ref[...] = zc[R - 8:, :]


def _conv(P3, conv_w):
    B, Lp, _ = P3.shape
    R = SEQ_TILE
    W = MIX_WIDTH
    c0 = COL_B // W
    w8 = jnp.zeros((8, W), F32).at[:CONV_W].set(conv_w.astype(F32))
    return pl.pallas_call(
        _conv_body,
        grid=(B, Lp // R),
        in_specs=[pl.BlockSpec((None, R, W), functools.partial(lambda b, t, c: (b, t, c), c=c0 + k))
                  for k in range(3)] + [pl.BlockSpec((8, W), lambda b, t: (0, 0))],
        out_specs=pl.BlockSpec((None, R, W), lambda b, t: (b, t, 0)),
        out_shape=jax.ShapeDtypeStruct((B, Lp, W), BF16),
        scratch_shapes=[pltpu.VMEM((8, W), F32)],
        compiler_params=_cparams("parallel", "arbitrary"),
        name="conv",
    )(P3, P3, P3, w8)


def _prep_body(cq_ref, ck_ref, cv_ref, dq_ref, dk_ref, dv_ref, kiw_ref, qi_ref,
               cqg_ref, ckg_ref, dqg_ref, dkg_ref,
               cqn_ref, ckn_ref, cvb_ref, dqn_ref, dkn_ref, dvb_ref, qib_ref, kk_ref, wq_ref):
    lane = lax.broadcasted_iota(I32, (1, HEAD_DIM), 1)
    lo = lane < HALF

    def half_norm(x, g, scale):
        sq = x * x
        ms_lo = jnp.sum(jnp.where(lo, sq, 0.0), axis=-1, keepdims=True) * (1.0 / HALF)
        ms_hi = jnp.sum(jnp.where(lo, 0.0, sq), axis=-1, keepdims=True) * (1.0 / HALF)
        inv = jnp.where(lo, lax.rsqrt(ms_lo + EPS), lax.rsqrt(ms_hi + EPS))
        return ((x * inv) * g) * scale

    for h in range(N_HEADS):
        hs = slice(h * HEAD_DIM, (h + 1) * HEAD_DIM)
        cqn_ref[:, hs] = half_norm(cq_ref[:, hs], cqg_ref[...], HALF ** -0.5).astype(BF16)
        ckn_ref[:, hs] = half_norm(ck_ref[:, hs], ckg_ref[...], 1.0).astype(BF16)
        x = dq_ref[:, hs]
        dqn_ref[:, hs] = (((x * _rms_inv(x)) * dqg_ref[...]) * (HEAD_DIM ** -0.5)).astype(BF16)
    x = dk_ref[...]
    dkn_ref[...] = ((x * _rms_inv(x)) * dkg_ref[...]).astype(BF16)
    for t in range(cv_ref.shape[0] // SEQ_TILE):
        rs = slice(t * SEQ_TILE, (t + 1) * SEQ_TILE)
        for h in range(N_HEADS):
            hs = slice(h * HEAD_DIM, (h + 1) * HEAD_DIM)
            cvb_ref[t, hs, :] = cv_ref[rs, hs].T.astype(BF16)
        dvb_ref[t] = dv_ref[rs, :].T.astype(BF16)
    qib_ref[...] = (qi_ref[...] * (IDX_DIM ** -0.5)).astype(BF16)
    kiw = kiw_ref[...]
    swapped = pltpu.roll(kiw, HALF, axis=1)
    kk_ref[...] = jnp.where(lo, kiw, swapped).astype(BF16)
    wq_ref[...] = swapped * (IDX_HEADS ** -0.5)


def _prep(P, cq_g, ck_g, dq_g, dk_g, tm):
    T = P.shape[0]
    W = MIX_WIDTH
    H = HEAD_DIM

    def col(width, start):
        return pl.BlockSpec((tm, width), functools.partial(lambda i, c: (i, c), c=start // width))

    vec = pl.BlockSpec((1, H), lambda i: (0, 0))
    out = lambda width: pl.BlockSpec((tm, width), lambda i: (i, 0))
    tiles = lambda width: pl.BlockSpec((tm // SEQ_TILE, width, SEQ_TILE), lambda i: (i, 0, 0))
    return pl.pallas_call(
        _prep_body,
        grid=(T // tm,),
        in_specs=[col(W, COL_C), col(W, COL_C + W), col(W, COL_C + 2 * W), col(W, COL_DQ),
                  col(H, COL_DK), col(H, COL_DV), col(H, COL_KIW), col(IDX_HEADS * IDX_DIM, COL_QI),
                  vec, vec, vec, vec],
        out_specs=[out(W), out(W), tiles(W), out(W), out(H), tiles(H), out(IDX_HEADS * IDX_DIM), out(H), out(H)],
        out_shape=[jax.ShapeDtypeStruct((T, W), BF16)] * 2
        + [jax.ShapeDtypeStruct((T // SEQ_TILE, W, SEQ_TILE), BF16), jax.ShapeDtypeStruct((T, W), BF16),
           jax.ShapeDtypeStruct((T, H), BF16), jax.ShapeDtypeStruct((T // SEQ_TILE, H, SEQ_TILE), BF16)]
        + [jax.ShapeDtypeStruct((T, IDX_HEADS * IDX_DIM), BF16),
           jax.ShapeDtypeStruct((T, H), BF16), jax.ShapeDtypeStruct((T, H), F32)],
        compiler_params=_cparams("parallel"),
        name="prep",
    )(P, P, P, P, P, P, P, P,
      jnp.tile(cq_g.astype(F32), 2).reshape(1, H), jnp.tile(ck_g.astype(F32), 2).reshape(1, H),
      dq_g.astype(F32).reshape(1, H), dk_g.astype(F32).reshape(1, H))


SUBLANES = 8
KEY_CHUNK = 4


def _max8(x):
    return jnp.max(x.reshape(x.shape[0] // SUBLANES, SUBLANES, x.shape[1]), axis=0)


def _sum8(x):
    return jnp.sum(x.reshape(x.shape[0] // SUBLANES, SUBLANES, x.shape[1]), axis=0)


def _bit_transpose32(rows):
    a = list(rows)
    j = 16
    m = 0x0000FFFF
    while j:
        k = 0
        while k < 32:
            t = (a[k] ^ lax.shift_right_logical(a[k + j], jnp.int32(j))) & jnp.int32(m)
            a[k] = a[k] ^ t
            a[k + j] = a[k + j] ^ (t << j)
            k = (k + j + 1) & ~j
        j >>= 1
        m ^= (m << j) & 0xFFFFFFFF
    return a


def _for_tiles(n_tiles, body, chunk=KEY_CHUNK):
    n_chunks = n_tiles // chunk

    def chunked(c, carry):
        body(c * chunk, chunk)
        return carry

    lax.fori_loop(0, n_chunks, chunked, 0)
    base = n_chunks * chunk
    size = chunk // 2
    while size >= 1:
        take = (n_tiles - base) // size

        def part(_, carry, base=base, size=size):
            body(base, size)
            return carry

        lax.fori_loop(0, take, part, 0)
        base = base + take * size
        size //= 2


def _diff_body(q_ref, k_ref, vt_ref, near_ref, lam_ref, sub_ref, o_ref,
               q2_ref, s_ref, mx_ref, l_ref, acc_ref, *, out_scale):
    i = pl.program_id(1)
    TQ = SEQ_TILE
    H = HEAD_DIM
    W2 = 2 * TQ
    lane = lax.broadcasted_iota(I32, (1, H), 1)
    for h in range(N_HEADS):
        q = q_ref[:, h * H:(h + 1) * H]
        zero = jnp.zeros_like(q)
        q2_ref[h, 0:TQ, :] = jnp.where(lane < HALF, q, zero)
        q2_ref[h, TQ:, :] = jnp.where(lane < HALF, zero, q)
    mx_ref[...] = jnp.full_like(mx_ref, MASKED)

    def put(j0, n, near_idx):
        k0 = pl.multiple_of(j0 * TQ, TQ)
        parts = []
        for h in range(N_HEADS):
            s = _dot_nt(k_ref[pl.ds(k0, n * TQ), h * H:(h + 1) * H], q2_ref[h])
            if near_idx is None:
                parts.append(s)
            else:
                b = near_ref[h, near_idx]
                parts.append(s + jnp.concatenate([b, b], axis=1))
        s = jnp.concatenate(parts, axis=1)
        for t in range(n):
            s_ref[j0 + t] = s[t * TQ:(t + 1) * TQ, :]
        mx_ref[...] = jnp.maximum(mx_ref[...], _max8(s))

    n_far = jnp.maximum(i - 1, 0)
    _for_tiles(n_far, lambda j0, n: put(j0, n, None))

    def prev(j, carry):
        put(j, 1, 1)
        return carry

    lax.fori_loop(n_far, i, prev, 0)
    put(i, 1, 0)

    m = jnp.max(mx_ref[...], axis=0, keepdims=True)
    l_ref[...] = jnp.zeros_like(l_ref)
    acc_ref[...] = jnp.zeros_like(acc_ref)

    def pv(j0, n):
        p = jnp.exp(s_ref[pl.ds(j0, n)].reshape(n * TQ, N_HEADS * W2) - m)
        l_ref[...] += _sum8(p)
        pb = p.astype(BF16)
        for h in range(N_HEADS):
            acc = acc_ref[:, h * W2:(h + 1) * W2]
            for t in range(n):
                acc = acc + _dot(vt_ref[j0 + t, h * H:(h + 1) * H, :],
                                 pb[t * TQ:(t + 1) * TQ, h * W2:(h + 1) * W2])
            acc_ref[:, h * W2:(h + 1) * W2] = acc

    _for_tiles(i + 1, pv)

    ot = acc_ref[...] * (1.0 / jnp.sum(l_ref[...], axis=0, keepdims=True))
    for h in range(N_HEADS):
        od = ot[:, h * W2:h * W2 + TQ] - lam_ref[...] * ot[:, h * W2 + TQ:(h + 1) * W2]
        inv = lax.rsqrt(jnp.mean(od * od, axis=0, keepdims=True) + EPS)
        o_ref[:, h * H:(h + 1) * H] = (((od * inv) * sub_ref[...]) * out_scale).T.astype(o_ref.dtype)


def _diff(cqn, ckn, cvt, near_t, lam, subln, out_scale):
    B, Lp, W = cqn.shape
    TQ = SEQ_TILE
    H = HEAD_DIM
    nt = Lp // TQ
    return pl.pallas_call(
        functools.partial(_diff_body, out_scale=out_scale),
        grid=(B, nt),
        in_specs=[
            pl.BlockSpec((None, TQ, W), lambda b, i: (b, i, 0)),
            pl.BlockSpec((None, Lp, W), lambda b, i: (b, 0, 0)),
            pl.BlockSpec((None, nt, W, TQ), lambda b, i: (b, 0, 0, 0)),
            pl.BlockSpec((N_HEADS, 2, TQ, TQ), lambda b, i: (0, 0, 0, 0)),
            pl.BlockSpec((1, TQ), lambda b, i: (0, 0)),
            pl.BlockSpec((H, TQ), lambda b, i: (0, 0)),
        ],
        out_specs=pl.BlockSpec((None, TQ, W), lambda b, i: (b, i, 0)),
        out_shape=jax.ShapeDtypeStruct((B, Lp, W), BF16),
        scratch_shapes=[pltpu.VMEM((N_HEADS, 2 * TQ, H), BF16),
                        pltpu.VMEM((nt, TQ, N_HEADS * 2 * TQ), F32),
                        pltpu.VMEM((SUBLANES, N_HEADS * 2 * TQ), F32),
                        pltpu.VMEM((SUBLANES, N_HEADS * 2 * TQ), F32),
                        pltpu.VMEM((H, N_HEADS * 2 * TQ), F32)],
        compiler_params=_cparams("parallel", "arbitrary"),
        name="diff",
    )(cqn, ckn, cvt, near_t, lam, subln)


def _dsa_body(qi_ref, kk_ref, wq_ref, q_ref, k_ref, vt_ref, near_ref, o_ref,
              qi2_ref, w_ref, key_ref, kq_ref, plane_ref, alive_ref, q4_ref, thr_ref, s_ref, mx_ref, l_ref, acc_ref, *, top_k):
    i = pl.program_id(1)
    TQ = SEQ_TILE
    H = HEAD_DIM
    lane = lax.broadcasted_iota(I32, (1, H), 1)

    for p in range(IDX_HEADS // 2):
        x = qi_ref[:, p * H:(p + 1) * H]
        zero = jnp.zeros_like(x)
        qi2_ref[p, 0:TQ, :] = jnp.where(lane < HALF, x, zero)
        qi2_ref[p, TQ:, :] = jnp.where(lane < HALF, zero, x)
    w_ref[...] = wq_ref[...].T
    for h in range(N_HEADS):
        q4_ref[h * TQ:(h + 1) * TQ, :] = q_ref[:, h * H:(h + 1) * H]

    def index(j0, n):
        k0 = pl.multiple_of(j0 * TQ, TQ)
        kt = kk_ref[pl.ds(k0, n * TQ), :]
        acc = jnp.zeros((n * TQ, TQ), F32)
        for p in range(IDX_HEADS // 2):
            s = jnp.maximum(_dot_nt(kt, qi2_ref[p]), 0.0)
            acc = acc + s[:, 0:TQ] * w_ref[2 * p:2 * p + 1, :] + s[:, TQ:] * w_ref[2 * p + 1:2 * p + 2, :]
        acc = acc + 0.0
        bits = pltpu.bitcast(acc, I32)
        key = bits ^ ((bits >> 31) & jnp.int32(0x7FFFFFFF))
        tiles = []
        for t in range(n):
            kt_ = key[t * TQ:(t + 1) * TQ, :]
            if t == n - 1:
                kt_ = jnp.where(kq_ref[...] <= (i - (j0 + t)) * TQ, kt_, jnp.int32(KEY_NEG_INF))
            key_ref[j0 + t] = kt_
            tiles.append(kt_)
        if n == 1:
            tiles.append(None)
        for u in range(len(tiles) // 2):
            rows = []
            for tile in tiles[2 * u:2 * u + 2]:
                for r in range(TQ // SUBLANES):
                    if tile is None:
                        rows.append(jnp.zeros((SUBLANES, TQ), I32))
                    else:
                        rows.append(tile[r * SUBLANES:(r + 1) * SUBLANES, :] ^ jnp.int32(INT_MIN))
            planes = _bit_transpose32(rows)
            pair = j0 // 2 + u
            for b in range(32):
                plane_ref[pair, b] = planes[b]

    @pl.when(i == 0)
    def _():
        plane_ref[...] = jnp.zeros_like(plane_ref)

    kq_ref[...] = lax.broadcasted_iota(I32, (TQ, TQ), 0) - lax.broadcasted_iota(I32, (TQ, TQ), 1)
    _for_tiles(i + 1, index)

    n_pairs = plane_ref.shape[0]
    for tp in range(n_pairs):
        alive_ref[tp] = jnp.full((SUBLANES, TQ), -1, I32)

    def bit_step(t, carry):
        above, thr = carry
        ones = [alive_ref[tp] & plane_ref[tp, t] for tp in range(n_pairs)]
        cnt = lax.population_count(ones[0])
        for tp in range(1, n_pairs):
            cnt = cnt + lax.population_count(ones[tp])
        hit = above + jnp.sum(cnt, axis=0, keepdims=True)
        take = hit >= top_k
        for tp in range(n_pairs):
            alive_ref[tp] = jnp.where(take, ones[tp], alive_ref[tp] ^ ones[tp])
        thr = thr | jnp.where(take, jnp.left_shift(jnp.int32(1), 31 - t), 0)
        return jnp.where(take, above, hit), thr

    zero8 = jnp.zeros((SUBLANES, TQ), I32)
    _, thr_u = lax.fori_loop(0, 32, bit_step, (zero8, zero8))
    thr_ref[...] = thr_u ^ jnp.int32(INT_MIN)

    mx_ref[...] = jnp.full_like(mx_ref, MASKED)

    def put(j0, n, near_idx):
        k0 = pl.multiple_of(j0 * TQ, TQ)
        s = _dot_nt(k_ref[pl.ds(k0, n * TQ), :], q4_ref[...])
        thr1 = thr_ref[0:1, :]
        mx = mx_ref[...]
        for t in range(n):
            sel = key_ref[j0 + t] >= thr1
            parts = []
            for h in range(N_HEADS):
                sh = s[t * TQ:(t + 1) * TQ, h * TQ:(h + 1) * TQ]
                if near_idx is not None:
                    sh = sh + near_ref[h, near_idx]
                parts.append(jnp.where(sel, sh, MASKED))
            row = jnp.concatenate(parts, axis=1)
            s_ref[j0 + t] = row
            mx = jnp.maximum(mx, _max8(row))
        mx_ref[...] = mx

    n_far = jnp.maximum(i - 1, 0)
    _for_tiles(n_far, lambda j0, n: put(j0, n, None))

    def prev(j, carry):
        put(j, 1, 1)
        return carry

    lax.fori_loop(n_far, i, prev, 0)
    put(i, 1, 0)

    m = jnp.max(mx_ref[...], axis=0, keepdims=True)
    l_ref[...] = jnp.zeros_like(l_ref)
    acc_ref[...] = jnp.zeros_like(acc_ref)

    def pv(j0, n):
        p = jnp.exp(s_ref[pl.ds(j0, n)].reshape(n * TQ, N_HEADS * TQ) - m)
        l_ref[...] += _sum8(p)
        pb = p.astype(BF16)
        acc = acc_ref[...]
        for t in range(n):
            acc = acc + _dot(vt_ref[j0 + t], pb[t * TQ:(t + 1) * TQ, :])
        acc_ref[...] = acc

    _for_tiles(i + 1, pv)

    ot = acc_ref[...] * (1.0 / jnp.sum(l_ref[...], axis=0, keepdims=True))
    for h in range(N_HEADS):
        o_ref[:, h * H:(h + 1) * H] = ot[:, h * TQ:(h + 1) * TQ].T.astype(o_ref.dtype)


def _dsa(qib, kk, wq, dqn, dkn, dvt, near_t, top_k):
    B, Lp, W = dqn.shape
    TQ = SEQ_TILE
    H = HEAD_DIM
    NI = IDX_HEADS * IDX_DIM
    nt = Lp // TQ
    seq = lambda width: pl.BlockSpec((None, Lp, width), lambda b, i: (b, 0, 0))
    blk = lambda width: pl.BlockSpec((None, TQ, width), lambda b, i: (b, i, 0))
    return pl.pallas_call(
        functools.partial(_dsa_body, top_k=top_k),
        grid=(B, nt),
        in_specs=[blk(NI), seq(H), blk(H), blk(W), seq(H),
                  pl.BlockSpec((None, nt, H, TQ), lambda b, i: (b, 0, 0, 0)),
                  pl.BlockSpec((N_HEADS, 2, TQ, TQ), lambda b, i: (0, 0, 0, 0))],
        out_specs=blk(W),
        out_shape=jax.ShapeDtypeStruct((B, Lp, W), BF16),
        scratch_shapes=[
            pltpu.VMEM((IDX_HEADS // 2, 2 * TQ, H), BF16),
            pltpu.VMEM((H, TQ), F32),
            pltpu.VMEM((nt, TQ, TQ), I32),
            pltpu.VMEM((TQ, TQ), I32),
            pltpu.VMEM(((nt + 1) // 2, 32, SUBLANES, TQ), I32),
            pltpu.VMEM(((nt + 1) // 2, SUBLANES, TQ), I32),
            pltpu.VMEM((N_HEADS * TQ, H), BF16),
            pltpu.VMEM((SUBLANES, TQ), I32),
            pltpu.VMEM((nt, TQ, N_HEADS * TQ), F32),
            pltpu.VMEM((SUBLANES, N_HEADS * TQ), F32),
            pltpu.VMEM((SUBLANES, N_HEADS * TQ), F32),
            pltpu.VMEM((H, N_HEADS * TQ), F32),
        ],
        compiler_params=_cparams("parallel", "arbitrary"),
        name="dsa",
    )(qib, kk, wq, dqn, dkn, dvt, near_t)


def _merge_body(xn_ref, b0_ref, b1_ref, b2_ref, b3_ref, g0_ref, g1_ref, g2_ref, g3_ref,
                wb_ref, wo_ref, o_ref):
    xn = xn_ref[...]
    merged = None
    for m, (b_ref, g_ref) in enumerate(((b0_ref, g0_ref), (b1_ref, g1_ref), (b2_ref, g2_ref), (b3_ref, g3_ref))):
        term = jax.nn.sigmoid(_dot(xn, g_ref[...])) * _dot(b_ref[...], wb_ref[m])
        merged = term if merged is None else merged + term
    out = _dot(merged.astype(BF16), wo_ref[...])

    @pl.when(pl.program_id(1) == 0)
    def _():
        o_ref[...] = out

    @pl.when(pl.program_id(1) > 0)
    def _():
        o_ref[...] += out


def _merge(xn, branches, w_gate, w_branch, w_out, tm, tn):
    T, D = xn.shape
    W = MIX_WIDTH
    nn = D // tn
    row = lambda width: pl.BlockSpec((tm, width), lambda i, n: (i, 0))
    gate = lambda m: pl.BlockSpec((D, tn), functools.partial(lambda i, n, m: (0, m * nn + n), m=m))
    return pl.pallas_call(
        _merge_body,
        grid=(T // tm, nn),
        in_specs=[row(D), row(W), row(W), row(W), row(W),
                  gate(0), gate(1), gate(2), gate(3),
                  pl.BlockSpec((N_BRANCH, W, tn), lambda i, n: (0, 0, n)),
                  pl.BlockSpec((tn, D), lambda i, n: (n, 0))],
        out_specs=row(D),
        out_shape=jax.ShapeDtypeStruct((T, D), F32),
        compiler_params=_cparams("parallel", "arbitrary"),
        name="merge",
    )(xn, *branches, w_gate, w_gate, w_gate, w_gate, w_branch, w_out)


def _rel_bucket(n):
    max_exact = N_BUCKETS // 2
    nf = jnp.maximum(n, 1).astype(F32)
    large = max_exact + (jnp.log(nf / max_exact) / math.log(MAX_DISTANCE / max_exact)
                         * (N_BUCKETS - max_exact)).astype(I32)
    large = jnp.minimum(large, N_BUCKETS - 1)
    return jnp.where(n < max_exact, n, large)


def _bias_tiles(table):
    TQ = SEQ_TILE
    qi = jnp.arange(TQ, dtype=I32)[:, None]
    ki = jnp.arange(TQ, dtype=I32)[None, :]
    d0 = qi - ki
    tbl = table.astype(F32).T

    def lookup(bucket):
        out = jnp.zeros((tbl.shape[0],) + bucket.shape, F32)
        for b in range(N_BUCKETS):
            out = jnp.where(bucket[None] == b, tbl[:, b][:, None, None], out)
        return out

    far = tbl[:, N_BUCKETS - 1][:, None, None]
    diag = jnp.where(d0 >= 0, lookup(_rel_bucket(jnp.maximum(d0, 0))) - far, MASKED)
    prev = lookup(_rel_bucket(d0 + TQ)) - far
    return jnp.stack([diag, prev], axis=1).transpose(0, 1, 3, 2)


def _pack_w_in(w):
    offs = np.cumsum([0, 512, 512, 512, 512, 512, 512, 512, 512, 512, 512, 512, 128, 128,
                      IDX_HEADS * IDX_DIM, IDX_DIM, IDX_HEADS]).tolist()
    seg = lambda a, b: w[:, offs[a]:offs[b]]
    D = w.shape[0]
    used = COL_KIW + IDX_DIM + IDX_HEADS
    main = jnp.concatenate([
        seg(0, 4),
        seg(13, 14),
        seg(4, 7),
        seg(7, 10),
        seg(10, 11),
        seg(11, 12),
        seg(12, 13),
        seg(14, 16),
        jnp.zeros((D, P_COLS - used), w.dtype),
    ], axis=1).astype(BF16)
    gates = w[:, offs[16]:].astype(BF16)
    return main, gates


def _largest_tile(n, cap):
    t = cap
    while n % t:
        t //= 2
    return t


def kernel(x, meta_tokens, rel_bias, ffn1_norm, ffn1_w_gu, ffn1_w_down, mix_norm, w_in, hgrn_lb, hgrn_gnorm, conv_w, diff_q_norm, diff_k_norm, diff_lambda, diff_subln, dsa_q_norm, dsa_k_norm, w_branch, w_out, ffn2_norm, ffn2_w_gu, ffn2_w_down):
    B, S, D = x.shape
    depth = w_in.shape[0]
    L = S + N_META
    Lp = -(-L // SEQ_TILE) * SEQ_TILE
    T = B * Lp
    top_k = min(TOPK_MAX, S // 4)
    tm = _largest_tile(T, 512)
    tm_wide = _largest_tile(T, 1024)
    d_ff = ffn1_w_down.shape[1]
    tf = _largest_tile(d_ff, 512)

    meta = jnp.broadcast_to(meta_tokens.astype(x.dtype)[None], (B, N_META, D))
    h = jnp.concatenate([meta, x, jnp.zeros((B, Lp - L, D), x.dtype)], axis=1).reshape(T, D)

    lbs = jnp.cumsum(jax.nn.softmax(hgrn_lb.astype(F32), axis=0), axis=0)
    lbs = lbs - lbs[0:1]
    near_c = _bias_tiles(rel_bias[:, :N_HEADS])
    near_d = _bias_tiles(rel_bias[:, N_HEADS:])

    for l in range(depth):
        h = _ffn(h, ffn1_norm[l], ffn1_w_gu[l].astype(BF16), ffn1_w_down[l].astype(BF16), tm, tf)

        w_main, w_gate = _pack_w_in(w_in[l])
        xn, P = _proj(h, mix_norm[l], w_main, tm_wide, 1024)
        P3 = P.reshape(B, Lp, P_COLS)
        br_a = _hgrn(P3, lbs[l], hgrn_gnorm[l])
        br_b = _conv(P3, conv_w[l])
        cqn, ckn, cvt, dqn, dkn, dvt, qib, kk, wq = _prep(
            P, diff_q_norm[l], diff_k_norm[l], dsa_q_norm[l], dsa_k_norm[l], tm)
        r3 = lambda a: a.reshape(B, Lp, a.shape[-1])
        lp = diff_lambda[l].astype(F32)
        lam_init = 0.8 - 0.6 * math.exp(-0.3 * l)
        lam = jnp.exp(jnp.sum(lp[0] * lp[1])) - jnp.exp(jnp.sum(lp[2] * lp[3])) + lam_init
        tiles = lambda a: a.reshape(B, Lp // SEQ_TILE, a.shape[-2], SEQ_TILE)
        br_c = _diff(r3(cqn), r3(ckn), tiles(cvt), near_c,
                     jnp.broadcast_to(lam, (1, SEQ_TILE)).astype(F32),
                     jnp.broadcast_to(diff_subln[l].astype(F32)[:, None], (HEAD_DIM, SEQ_TILE)),
                     1.0 - lam_init)
        br_d = _dsa(r3(qib), r3(kk), r3(wq), r3(dqn), r3(dkn), tiles(dvt), near_d, top_k)
        branches = [a.reshape(T, MIX_WIDTH) for a in (br_a, br_b, br_c, br_d)]
        mixed = _merge(xn, branches, w_gate, w_branch[l].astype(BF16), w_out[l].astype(BF16), tm, 256)

        last = l == depth - 1
        h = _ffn(h, ffn2_norm[l], ffn2_w_gu[l].astype(BF16), ffn2_w_down[l].astype(BF16),
                 _largest_tile(S, 512) if last else tm, tf, delta=mixed,
                 keep=(B, Lp, N_META, S) if last else None)

    return h.reshape(B, S, D)
```

```python
import functools
import math

import jax
import jax.numpy as jnp
import numpy as np
from jax import lax
from jax.experimental import pallas as pl
from jax.experimental.pallas import tpu as pltpu

F32 = jnp.float32
BF16 = jnp.bfloat16
I32 = jnp.int32

EPS = 1e-6
N_META = 16
HEAD_DIM = 128
HALF = 64
N_HEADS = 4
MIX_WIDTH = N_HEADS * HEAD_DIM
IDX_HEADS = 16
IDX_DIM = 64
TOPK_MAX = 256
N_BRANCH = 4
N_BUCKETS = 32
MAX_DISTANCE = 128
CONV_W = 3
SEQ_TILE = 128
HGRN_CHUNK = 8
HGRN_GROUP = 4
MASKED = -1e30
INT_MIN = -(2 ** 31)
KEY_NEG_INF = -2139095041
VMEM_LIMIT_BYTES = 56 * 1024 * 1024

COL_A = 0
COL_QI = 2048
COL_B = 3072
COL_C = 4608
COL_DQ = 6144
COL_DK = 6656
COL_DV = 6784
COL_KIW = 6912
P_COLS = 7168


def _cparams(*sem):
    return pltpu.CompilerParams(dimension_semantics=sem, vmem_limit_bytes=VMEM_LIMIT_BYTES)


def _dot(a, b):
    return jnp.dot(a, b, preferred_element_type=F32)


def _dot_nt(a, b):
    return lax.dot_general(a, b, (((1,), (1,)), ((), ())), preferred_element_type=F32)


def _dot_tn(a, b):
    return lax.dot_general(a, b, (((0,), (0,)), ((), ())), preferred_element_type=F32)


def _rms_inv(x):
    return lax.rsqrt(jnp.mean(x * x, axis=-1, keepdims=True) + EPS)


def _ffn_body(h_ref, g_ref, wg_ref, wu_ref, wd_ref, o_ref, xn_ref):
    j = pl.program_id(1)

    @pl.when(j == 0)
    def _():
        x = h_ref[...]
        xn_ref[...] = ((x * _rms_inv(x)) * g_ref[...]).astype(BF16)
        o_ref[...] = x

    xn = xn_ref[...]
    g = _dot(xn, wg_ref[...])
    u = _dot(xn, wu_ref[...])
    a = (g * jax.nn.sigmoid(g) * (0.5 * u)).astype(BF16)
    o_ref[...] += _dot(a, wd_ref[...])


def _ffn(h, gain, w_gu, w_down, tm, tf, keep=None):
    T, D = h.shape
    F = w_down.shape[0]
    nf = F // tf
    if keep is None:
        n_row_tiles = T // tm
        row_in = pl.BlockSpec((tm, D), lambda i, j: (i, 0))
        t_out = T
    else:
        B, Lp, first, n_rows = keep
        per_seq = n_rows // tm
        n_row_tiles = B * per_seq
        row_in = pl.BlockSpec((pl.Element(tm), pl.Element(D)),
                              lambda i, j: (pl.multiple_of(
                                  (i // per_seq) * Lp + first + (i % per_seq) * tm, SUBLANES), 0))
        t_out = B * n_rows
    return pl.pallas_call(
        _ffn_body,
        grid=(n_row_tiles, nf),
        in_specs=[
            row_in,
            pl.BlockSpec((1, D), lambda i, j: (0, 0)),
            pl.BlockSpec((D, tf), lambda i, j: (0, j)),
            pl.BlockSpec((D, tf), lambda i, j: (0, j + nf)),
            pl.BlockSpec((tf, D), lambda i, j: (j, 0)),
        ],
        out_specs=pl.BlockSpec((tm, D), lambda i, j: (i, 0)),
        out_shape=jax.ShapeDtypeStruct((t_out, D), F32),
        scratch_shapes=[pltpu.VMEM((tm, D), BF16)],
        compiler_params=_cparams("parallel", "arbitrary"),
        name="ffn",
    )(h, gain.reshape(1, D), w_gu, w_gu, w_down)


def _proj_body(h_ref, g_ref, w_ref, xn_ref, p_ref):
    @pl.when(pl.program_id(1) == 0)
    def _():
        x = h_ref[...]
        xn_ref[...] = ((x * _rms_inv(x)) * g_ref[...]).astype(BF16)

    p_ref[...] = _dot(xn_ref[...], w_ref[...])


def _proj(h, gain, w, tm, tn):
    T, D = h.shape
    N = w.shape[1]
    return pl.pallas_call(
        _proj_body,
        grid=(T // tm, N // tn),
        in_specs=[
            pl.BlockSpec((tm, D), lambda i, j: (i, 0)),
            pl.BlockSpec((1, D), lambda i, j: (0, 0)),
            pl.BlockSpec((D, tn), lambda i, j: (0, j)),
        ],
        out_specs=[
            pl.BlockSpec((tm, D), lambda i, j: (i, 0)),
            pl.BlockSpec((tm, tn), lambda i, j: (i, j)),
        ],
        out_shape=[jax.ShapeDtypeStruct((T, D), BF16), jax.ShapeDtypeStruct((T, N), F32)],
        compiler_params=_cparams("parallel", "arbitrary"),
        name="proj",
    )(h, gain.reshape(1, D), w)


def _hgrn_body(q_ref, f_ref, i_ref, g_ref, loglb_ref, log1mlb_ref, omlb_ref, gn_ref, o_ref,
               st_ref, qs_ref, ks_ref, bs_ref, os_ref):
    G, R, _ = q_ref.shape
    C = HGRN_CHUNK

    @pl.when(pl.program_id(1) == 0)
    def _():
        st_ref[...] = jnp.zeros_like(st_ref)

    row = lax.broadcasted_iota(I32, (R, MIX_WIDTH), 0) % C
    for g in range(G):
        z = f_ref[g]
        log_sig = jnp.minimum(z, 0.0) - jnp.log(1.0 + jnp.exp(-jnp.abs(z)))
        y = log1mlb_ref[...] + log_sig
        a = loglb_ref[...]
        log_f = jnp.maximum(a, y) + jnp.log(1.0 + jnp.exp(-jnp.abs(a - y)))
        b = log_f
        for sh in [1 << e for e in range(C.bit_length() - 1)]:
            b = b + jnp.where(row >= sh, pltpu.roll(b, sh, axis=0), 0.0)
        bs_ref[g] = b
        ks_ref[g] = omlb_ref[...] * jax.nn.sigmoid(-z)
        qr = q_ref[g]
        qs_ref[g] = qr * jax.nn.sigmoid(qr)

    t_iota = lax.broadcasted_iota(I32, (C, 1), 0)

    def chunk(c, carry):
        r0 = pl.multiple_of(c * C, C)
        for g in range(G):
            for h in range(N_HEADS):
                hs = slice(h * HEAD_DIM, (h + 1) * HEAD_DIM)
                qc = qs_ref[g, pl.ds(r0, C), hs]
                kc = ks_ref[g, pl.ds(r0, C), hs]
                bc = bs_ref[g, pl.ds(r0, C), hs]
                vc = i_ref[g, pl.ds(r0, C), hs]
                st = st_ref[g, h]
                o = _dot_nt((qc * jnp.exp(bc)).astype(BF16), st.astype(BF16))
                for s in range(C):
                    b_s = bc[s:s + 1, :]
                    k_s = kc[s:s + 1, :]
                    v_s = vc[s:s + 1, :]
                    e = jnp.exp(bc - b_s)
                    col = jnp.sum(qc * k_s * e, axis=-1, keepdims=True)
                    col = jnp.where(t_iota >= s, col, 0.0)
                    o = o + col * v_s
                os_ref[g, pl.ds(r0, C), hs] = o
                b_last = bc[C - 1:C, :]
                kd = kc * jnp.exp(b_last - bc)
                st_ref[g, h] = st * jnp.exp(b_last) + _dot_tn(vc.astype(BF16), kd.astype(BF16))
        return carry

    lax.fori_loop(0, R // C, chunk, 0)

    for g in range(G):
        for h in range(N_HEADS):
            hs = slice(h * HEAD_DIM, (h + 1) * HEAD_DIM)
            o = os_ref[g, :, hs]
            on = (o * _rms_inv(o)) * gn_ref[:, hs]
            gr = g_ref[g, :, hs]
            o_ref[g, :, hs] = (on * (gr * jax.nn.sigmoid(gr))).astype(o_ref.dtype)


def _hgrn(P3, lb, gnorm):
    B, Lp, _ = P3.shape
    R = SEQ_TILE
    W = MIX_WIDTH
    G = _largest_tile(B, HGRN_GROUP)
    lb = lb.reshape(1, W).astype(F32)
    vec = pl.BlockSpec((1, W), lambda b, t: (0, 0))
    c0 = COL_A // W
    return pl.pallas_call(
        _hgrn_body,
        grid=(B // G, Lp // R),
        in_specs=[pl.BlockSpec((G, R, W), functools.partial(lambda b, t, c: (b, t, c), c=c0 + k))
                  for k in range(4)] + [vec, vec, vec, vec],
        out_specs=pl.BlockSpec((G, R, W), lambda b, t: (b, t, 0)),
        out_shape=jax.ShapeDtypeStruct((B, Lp, W), BF16),
        scratch_shapes=[pltpu.VMEM((G, N_HEADS, HEAD_DIM, HEAD_DIM), F32)]
        + [pltpu.VMEM((G, R, W), F32)] * 4,
        compiler_params=_cparams("parallel", "arbitrary"),
        name="hgrn",
    )(P3, P3, P3, P3, jnp.log(lb), jnp.log1p(-lb), 1.0 - lb, gnorm.reshape(1, W).astype(F32))


def _conv_body(b_ref, c_ref, u_ref, w_ref, o_ref, carry_ref):
    @pl.when(pl.program_id(1) == 0)
    def _():
        carry_ref[...] = jnp.zeros_like(carry_ref)

    zc = c_ref[...] * u_ref[...]
    R = zc.shape[0]
    row = lax.broadcasted_iota(I32, zc.shape, 0)
    last = carry_ref[7:8, :]
    last2 = carry_ref[6:7, :]
    z1 = jnp.where(row == 0, last, pltpu.roll(zc, 1, axis=0))
    z2 = jnp.where(row == 0, last2, jnp.where(row == 1, last, pltpu.roll(zc, 2, axis=0)))
    y = w_ref[0:1, :] * zc + w_ref[1:2, :] * z1 + w_ref[2:3, :] * z2
    o_ref[...] = (b_ref[...] * y).astype(o_ref.dtype)
    carry_ref[...] = zc[R - 8:, :]


def _conv(P3, conv_w):
    B, Lp, _ = P3.shape
    R = SEQ_TILE
    W = MIX_WIDTH
    c0 = COL_B // W
    w8 = jnp.zeros((8, W), F32).at[:CONV_W].set(conv_w.astype(F32))
    return pl.pallas_call(
        _conv_body,
        grid=(B, Lp // R),
        in_specs=[pl.BlockSpec((None, R, W), functools.partial(lambda b, t, c: (b, t, c), c=c0 + k))
                  for k in range(3)] + [pl.BlockSpec((8, W), lambda b, t: (0, 0))],
        out_specs=pl.BlockSpec((None, R, W), lambda b, t: (b, t, 0)),
        out_shape=jax.ShapeDtypeStruct((B, Lp, W), BF16),
        scratch_shapes=[pltpu.VMEM((8, W), F32)],
        compiler_params=_cparams("parallel", "arbitrary"),
        name="conv",
    )(P3, P3, P3, w8)


def _prep_body(cq_ref, ck_ref, cv_ref, dq_ref, dk_ref, dv_ref, kiw_ref, qi_ref,
               cqg_ref, ckg_ref, dqg_ref, dkg_ref,
               cqn_ref, ckn_ref, cvb_ref, dqn_ref, dkn_ref, dvb_ref, qib_ref, kk_ref, wq_ref):
    lane = lax.broadcasted_iota(I32, (1, HEAD_DIM), 1)
    lo = lane < HALF

    def half_norm(x, g, scale):
        sq = x * x
        ms_lo = jnp.sum(jnp.where(lo, sq, 0.0), axis=-1, keepdims=True) * (1.0 / HALF)
        ms_hi = jnp.sum(jnp.where(lo, 0.0, sq), axis=-1, keepdims=True) * (1.0 / HALF)
        inv = jnp.where(lo, lax.rsqrt(ms_lo + EPS), lax.rsqrt(ms_hi + EPS))
        return ((x * inv) * g) * scale

    for h in range(N_HEADS):
        hs = slice(h * HEAD_DIM, (h + 1) * HEAD_DIM)
        cqn_ref[:, hs] = half_norm(cq_ref[:, hs], cqg_ref[...], HALF ** -0.5).astype(BF16)
        ckn_ref[:, hs] = half_norm(ck_ref[:, hs], ckg_ref[...], 1.0).astype(BF16)
        x = dq_ref[:, hs]
        dqn_ref[:, hs] = (((x * _rms_inv(x)) * dqg_ref[...]) * (HEAD_DIM ** -0.5)).astype(BF16)
    x = dk_ref[...]
    dkn_ref[...] = ((x * _rms_inv(x)) * dkg_ref[...]).astype(BF16)
    for t in range(cv_ref.shape[0] // SEQ_TILE):
        rs = slice(t * SEQ_TILE, (t + 1) * SEQ_TILE)
        for h in range(N_HEADS):
            hs = slice(h * HEAD_DIM, (h + 1) * HEAD_DIM)
            cvb_ref[t, hs, :] = cv_ref[rs, hs].T.astype(BF16)
        dvb_ref[t] = dv_ref[rs, :].T.astype(BF16)
    qib_ref[...] = (qi_ref[...] * (IDX_DIM ** -0.5)).astype(BF16)
    kiw = kiw_ref[...]
    swapped = pltpu.roll(kiw, HALF, axis=1)
    kk_ref[...] = jnp.where(lo, kiw, swapped).astype(BF16)
    wq_ref[...] = swapped * (IDX_HEADS ** -0.5)


def _prep(P, cq_g, ck_g, dq_g, dk_g, tm):
    T = P.shape[0]
    W = MIX_WIDTH
    H = HEAD_DIM

    def col(width, start):
        return pl.BlockSpec((tm, width), functools.partial(lambda i, c: (i, c), c=start // width))

    vec = pl.BlockSpec((1, H), lambda i: (0, 0))
    out = lambda width: pl.BlockSpec((tm, width), lambda i: (i, 0))
    tiles = lambda width: pl.BlockSpec((tm // SEQ_TILE, width, SEQ_TILE), lambda i: (i, 0, 0))
    return pl.pallas_call(
        _prep_body,
        grid=(T // tm,),
        in_specs=[col(W, COL_C), col(W, COL_C + W), col(W, COL_C + 2 * W), col(W, COL_DQ),
                  col(H, COL_DK), col(H, COL_DV), col(H, COL_KIW), col(IDX_HEADS * IDX_DIM, COL_QI),
                  vec, vec, vec, vec],
        out_specs=[out(W), out(W), tiles(W), out(W), out(H), tiles(H), out(IDX_HEADS * IDX_DIM), out(H), out(H)],
        out_shape=[jax.ShapeDtypeStruct((T, W), BF16)] * 2
        + [jax.ShapeDtypeStruct((T // SEQ_TILE, W, SEQ_TILE), BF16), jax.ShapeDtypeStruct((T, W), BF16),
           jax.ShapeDtypeStruct((T, H), BF16), jax.ShapeDtypeStruct((T // SEQ_TILE, H, SEQ_TILE), BF16)]
        + [jax.ShapeDtypeStruct((T, IDX_HEADS * IDX_DIM), BF16),
           jax.ShapeDtypeStruct((T, H), BF16), jax.ShapeDtypeStruct((T, H), F32)],
        compiler_params=_cparams("parallel"),
        name="prep",
    )(P, P, P, P, P, P, P, P,
      jnp.tile(cq_g.astype(F32), 2).reshape(1, H), jnp.tile(ck_g.astype(F32), 2).reshape(1, H),
      dq_g.astype(F32).reshape(1, H), dk_g.astype(F32).reshape(1, H))


SUBLANES = 8
KEY_CHUNK = 4


def _max8(x):
    return jnp.max(x.reshape(x.shape[0] // SUBLANES, SUBLANES, x.shape[1]), axis=0)


def _sum8(x):
    return jnp.sum(x.reshape(x.shape[0] // SUBLANES, SUBLANES, x.shape[1]), axis=0)


def _bit_transpose32(rows):
    a = list(rows)
    j = 16
    m = 0x0000FFFF
    while j:
        k = 0
        while k < 32:
            t = (a[k] ^ lax.shift_right_logical(a[k + j], jnp.int32(j))) & jnp.int32(m)
            a[k] = a[k] ^ t
            a[k + j] = a[k + j] ^ (t << j)
            k = (k + j + 1) & ~j
        j >>= 1
        m ^= (m << j) & 0xFFFFFFFF
    return a


def _for_tiles(n_tiles, body, chunk=KEY_CHUNK):
    n_chunks = n_tiles // chunk

    def chunked(c, carry):
        body(c * chunk, chunk)
        return carry

    lax.fori_loop(0, n_chunks, chunked, 0)
    base = n_chunks * chunk
    size = chunk // 2
    while size >= 1:
        take = (n_tiles - base) // size

        def part(_, carry, base=base, size=size):
            body(base, size)
            return carry

        lax.fori_loop(0, take, part, 0)
        base = base + take * size
        size //= 2


def _diff_body(q_ref, k_ref, vt_ref, near_ref, lam_ref, sub_ref, o_ref,
               q2_ref, s_ref, mx_ref, l_ref, acc_ref, *, out_scale):
    i = pl.program_id(1)
    TQ = SEQ_TILE
    H = HEAD_DIM
    W2 = 2 * TQ
    lane = lax.broadcasted_iota(I32, (1, H), 1)
    for h in range(N_HEADS):
        q = q_ref[:, h * H:(h + 1) * H]
        zero = jnp.zeros_like(q)
        q2_ref[h, 0:TQ, :] = jnp.where(lane < HALF, q, zero)
        q2_ref[h, TQ:, :] = jnp.where(lane < HALF, zero, q)
    mx_ref[...] = jnp.full_like(mx_ref, MASKED)

    def put(j0, n, near_idx):
        k0 = pl.multiple_of(j0 * TQ, TQ)
        parts = []
        for h in range(N_HEADS):
            s = _dot_nt(k_ref[pl.ds(k0, n * TQ), h * H:(h + 1) * H], q2_ref[h])
            if near_idx is None:
                parts.append(s)
            else:
                b = near_ref[h, near_idx]
                parts.append(s + jnp.concatenate([b, b], axis=1))
        s = jnp.concatenate(parts, axis=1)
        for t in range(n):
            s_ref[j0 + t] = s[t * TQ:(t + 1) * TQ, :]
        mx_ref[...] = jnp.maximum(mx_ref[...], _max8(s))

    n_far = jnp.maximum(i - 1, 0)
    _for_tiles(n_far, lambda j0, n: put(j0, n, None))

    def prev(j, carry):
        put(j, 1, 1)
        return carry

    lax.fori_loop(n_far, i, prev, 0)
    put(i, 1, 0)

    m = jnp.max(mx_ref[...], axis=0, keepdims=True)
    l_ref[...] = jnp.zeros_like(l_ref)
    acc_ref[...] = jnp.zeros_like(acc_ref)

    def pv(j0, n):
        p = jnp.exp(s_ref[pl.ds(j0, n)].reshape(n * TQ, N_HEADS * W2) - m)
        l_ref[...] += _sum8(p)
        pb = p.astype(BF16)
        for h in range(N_HEADS):
            acc = acc_ref[:, h * W2:(h + 1) * W2]
            for t in range(n):
                acc = acc + _dot(vt_ref[j0 + t, h * H:(h + 1) * H, :],
                                 pb[t * TQ:(t + 1) * TQ, h * W2:(h + 1) * W2])
            acc_ref[:, h * W2:(h + 1) * W2] = acc

    _for_tiles(i + 1, pv)

    ot = acc_ref[...] * (1.0 / jnp.sum(l_ref[...], axis=0, keepdims=True))
    for h in range(N_HEADS):
        od = ot[:, h * W2:h * W2 + TQ] - lam_ref[...] * ot[:, h * W2 + TQ:(h + 1) * W2]
        inv = lax.rsqrt(jnp.mean(od * od, axis=0, keepdims=True) + EPS)
        o_ref[:, h * H:(h + 1) * H] = (((od * inv) * sub_ref[...]) * out_scale).T.astype(o_ref.dtype)


def _diff(cqn, ckn, cvt, near_t, lam, subln, out_scale):
    B, Lp, W = cqn.shape
    TQ = SEQ_TILE
    H = HEAD_DIM
    nt = Lp // TQ
    return pl.pallas_call(
        functools.partial(_diff_body, out_scale=out_scale),
        grid=(B, nt),
        in_specs=[
            pl.BlockSpec((None, TQ, W), lambda b, i: (b, i, 0)),
            pl.BlockSpec((None, Lp, W), lambda b, i: (b, 0, 0)),
            pl.BlockSpec((None, nt, W, TQ), lambda b, i: (b, 0, 0, 0)),
            pl.BlockSpec((N_HEADS, 2, TQ, TQ), lambda b, i: (0, 0, 0, 0)),
            pl.BlockSpec((1, TQ), lambda b, i: (0, 0)),
            pl.BlockSpec((H, TQ), lambda b, i: (0, 0)),
        ],
        out_specs=pl.BlockSpec((None, TQ, W), lambda b, i: (b, i, 0)),
        out_shape=jax.ShapeDtypeStruct((B, Lp, W), BF16),
        scratch_shapes=[pltpu.VMEM((N_HEADS, 2 * TQ, H), BF16),
                        pltpu.VMEM((nt, TQ, N_HEADS * 2 * TQ), F32),
                        pltpu.VMEM((SUBLANES, N_HEADS * 2 * TQ), F32),
                        pltpu.VMEM((SUBLANES, N_HEADS * 2 * TQ), F32),
                        pltpu.VMEM((H, N_HEADS * 2 * TQ), F32)],
        compiler_params=_cparams("parallel", "arbitrary"),
        name="diff",
    )(cqn, ckn, cvt, near_t, lam, subln)


def _dsa_body(qi_ref, kk_ref, wq_ref, q_ref, k_ref, vt_ref, near_ref, o_ref,
              qi2_ref, w_ref, key_ref, kq_ref, plane_ref, alive_ref, q4_ref, thr_ref, s_ref, mx_ref, l_ref, acc_ref, *, top_k):
    i = pl.program_id(1)
    TQ = SEQ_TILE
    H = HEAD_DIM
    lane = lax.broadcasted_iota(I32, (1, H), 1)

    for p in range(IDX_HEADS // 2):
        x = qi_ref[:, p * H:(p + 1) * H]
        zero = jnp.zeros_like(x)
        qi2_ref[p, 0:TQ, :] = jnp.where(lane < HALF, x, zero)
        qi2_ref[p, TQ:, :] = jnp.where(lane < HALF, zero, x)
    w_ref[...] = wq_ref[...].T
    for h in range(N_HEADS):
        q4_ref[h * TQ:(h + 1) * TQ, :] = q_ref[:, h * H:(h + 1) * H]

    def index(j0, n):
        k0 = pl.multiple_of(j0 * TQ, TQ)
        kt = kk_ref[pl.ds(k0, n * TQ), :]
        acc = jnp.zeros((n * TQ, TQ), F32)
        for p in range(IDX_HEADS // 2):
            s = jnp.maximum(_dot_nt(kt, qi2_ref[p]), 0.0)
            acc = acc + s[:, 0:TQ] * w_ref[2 * p:2 * p + 1, :] + s[:, TQ:] * w_ref[2 * p + 1:2 * p + 2, :]
        acc = acc + 0.0
        bits = pltpu.bitcast(acc, I32)
        key = bits ^ ((bits >> 31) & jnp.int32(0x7FFFFFFF))
        tiles = []
        for t in range(n):
            kt_ = key[t * TQ:(t + 1) * TQ, :]
            if t == n - 1:
                kt_ = jnp.where(kq_ref[...] <= (i - (j0 + t)) * TQ, kt_, jnp.int32(KEY_NEG_INF))
            key_ref[j0 + t] = kt_
            tiles.append(kt_)
        if n == 1:
            tiles.append(None)
        for u in range(len(tiles) // 2):
            rows = []
            for tile in tiles[2 * u:2 * u + 2]:
                for r in range(TQ // SUBLANES):
                    if tile is None:
                        rows.append(jnp.zeros((SUBLANES, TQ), I32))
                    else:
                        rows.append(tile[r * SUBLANES:(r + 1) * SUBLANES, :] ^ jnp.int32(INT_MIN))
            planes = _bit_transpose32(rows)
            pair = j0 // 2 + u
            for b in range(32):
                plane_ref[pair, b] = planes[b]

    @pl.when(i == 0)
    def _():
        plane_ref[...] = jnp.zeros_like(plane_ref)

    kq_ref[...] = lax.broadcasted_iota(I32, (TQ, TQ), 0) - lax.broadcasted_iota(I32, (TQ, TQ), 1)
    _for_tiles(i + 1, index)

    n_pairs = plane_ref.shape[0]
    for tp in range(n_pairs):
        alive_ref[tp] = jnp.full((SUBLANES, TQ), -1, I32)

    def bit_step(t, carry):
        above, thr = carry
        ones = [alive_ref[tp] & plane_ref[tp, t] for tp in range(n_pairs)]
        cnt = lax.population_count(ones[0])
        for tp in range(1, n_pairs):
            cnt = cnt + lax.population_count(ones[tp])
        hit = above + jnp.sum(cnt, axis=0, keepdims=True)
        take = hit >= top_k
        for tp in range(n_pairs):
            alive_ref[tp] = jnp.where(take, ones[tp], alive_ref[tp] ^ ones[tp])
        thr = thr | jnp.where(take, jnp.left_shift(jnp.int32(1), 31 - t), 0)
        return jnp.where(take, above, hit), thr

    zero8 = jnp.zeros((SUBLANES, TQ), I32)
    _, thr_u = lax.fori_loop(0, 32, bit_step, (zero8, zero8))
    thr_ref[...] = thr_u ^ jnp.int32(INT_MIN)

    mx_ref[...] = jnp.full_like(mx_ref, MASKED)

    def put(j0, n, near_idx):
        k0 = pl.multiple_of(j0 * TQ, TQ)
        s = _dot_nt(k_ref[pl.ds(k0, n * TQ), :], q4_ref[...])
        thr1 = thr_ref[0:1, :]
        mx = mx_ref[...]
        for t in range(n):
            sel = key_ref[j0 + t] >= thr1
            parts = []
            for h in range(N_HEADS):
                sh = s[t * TQ:(t + 1) * TQ, h * TQ:(h + 1) * TQ]
                if near_idx is not None:
                    sh = sh + near_ref[h, near_idx]
                parts.append(jnp.where(sel, sh, MASKED))
            row = jnp.concatenate(parts, axis=1)
            s_ref[j0 + t] = row
            mx = jnp.maximum(mx, _max8(row))
        mx_ref[...] = mx

    n_far = jnp.maximum(i - 1, 0)
    _for_tiles(n_far, lambda j0, n: put(j0, n, None))

    def prev(j, carry):
        put(j, 1, 1)
        return carry

    lax.fori_loop(n_far, i, prev, 0)
    put(i, 1, 0)

    m = jnp.max(mx_ref[...], axis=0, keepdims=True)
    l_ref[...] = jnp.zeros_like(l_ref)
    acc_ref[...] = jnp.zeros_like(acc_ref)

    def pv(j0, n):
        p = jnp.exp(s_ref[pl.ds(j0, n)].reshape(n * TQ, N_HEADS * TQ) - m)
        l_ref[...] += _sum8(p)
        pb = p.astype(BF16)
        acc = acc_ref[...]
        for t in range(n):
            acc = acc + _dot(vt_ref[j0 + t], pb[t * TQ:(t + 1) * TQ, :])
        acc_ref[...] = acc

    _for_tiles(i + 1, pv)

    ot = acc_ref[...] * (1.0 / jnp.sum(l_ref[...], axis=0, keepdims=True))
    for h in range(N_HEADS):
        o_ref[:, h * H:(h + 1) * H] = ot[:, h * TQ:(h + 1) * TQ].T.astype(o_ref.dtype)


def _dsa(qib, kk, wq, dqn, dkn, dvt, near_t, top_k):
    B, Lp, W = dqn.shape
    TQ = SEQ_TILE
    H = HEAD_DIM
    NI = IDX_HEADS * IDX_DIM
    nt = Lp // TQ
    seq = lambda width: pl.BlockSpec((None, Lp, width), lambda b, i: (b, 0, 0))
    blk = lambda width: pl.BlockSpec((None, TQ, width), lambda b, i: (b, i, 0))
    return pl.pallas_call(
        functools.partial(_dsa_body, top_k=top_k),
        grid=(B, nt),
        in_specs=[blk(NI), seq(H), blk(H), blk(W), seq(H),
                  pl.BlockSpec((None, nt, H, TQ), lambda b, i: (b, 0, 0, 0)),
                  pl.BlockSpec((N_HEADS, 2, TQ, TQ), lambda b, i: (0, 0, 0, 0))],
        out_specs=blk(W),
        out_shape=jax.ShapeDtypeStruct((B, Lp, W), BF16),
        scratch_shapes=[
            pltpu.VMEM((IDX_HEADS // 2, 2 * TQ, H), BF16),
            pltpu.VMEM((H, TQ), F32),
            pltpu.VMEM((nt, TQ, TQ), I32),
            pltpu.VMEM((TQ, TQ), I32),
            pltpu.VMEM(((nt + 1) // 2, 32, SUBLANES, TQ), I32),
            pltpu.VMEM(((nt + 1) // 2, SUBLANES, TQ), I32),
            pltpu.VMEM((N_HEADS * TQ, H), BF16),
            pltpu.VMEM((SUBLANES, TQ), I32),
            pltpu.VMEM((nt, TQ, N_HEADS * TQ), F32),
            pltpu.VMEM((SUBLANES, N_HEADS * TQ), F32),
            pltpu.VMEM((SUBLANES, N_HEADS * TQ), F32),
            pltpu.VMEM((H, N_HEADS * TQ), F32),
        ],
        compiler_params=_cparams("parallel", "arbitrary"),
        name="dsa",
    )(qib, kk, wq, dqn, dkn, dvt, near_t)


def _merge_body(h_ref, xn_ref, b0_ref, b1_ref, b2_ref, b3_ref, g0_ref, g1_ref, g2_ref, g3_ref,
                wb_ref, wo_ref, o_ref):
    @pl.when(pl.program_id(1) == 0)
    def _():
        o_ref[...] = h_ref[...]

    xn = xn_ref[...]
    merged = None
    for m, (b_ref, g_ref) in enumerate(((b0_ref, g0_ref), (b1_ref, g1_ref), (b2_ref, g2_ref), (b3_ref, g3_ref))):
        term = jax.nn.sigmoid(_dot(xn, g_ref[...])) * _dot(b_ref[...], wb_ref[m])
        merged = term if merged is None else merged + term
    o_ref[...] += _dot(merged.astype(BF16), wo_ref[...])


def _merge(h, xn, branches, w_gate, w_branch, w_out, tm, tn):
    T, D = h.shape
    W = MIX_WIDTH
    nn = D // tn
    row = lambda width: pl.BlockSpec((tm, width), lambda i, n: (i, 0))
    gate = lambda m: pl.BlockSpec((D, tn), functools.partial(lambda i, n, m: (0, m * nn + n), m=m))
    return pl.pallas_call(
        _merge_body,
        grid=(T // tm, nn),
        in_specs=[row(D), row(D), row(W), row(W), row(W), row(W),
                  gate(0), gate(1), gate(2), gate(3),
                  pl.BlockSpec((N_BRANCH, W, tn), lambda i, n: (0, 0, n)),
                  pl.BlockSpec((tn, D), lambda i, n: (n, 0))],
        out_specs=row(D),
        out_shape=jax.ShapeDtypeStruct((T, D), F32),
        compiler_params=_cparams("parallel", "arbitrary"),
        name="merge",
    )(h, xn, *branches, w_gate, w_gate, w_gate, w_gate, w_branch, w_out)


def _rel_bucket(n):
    max_exact = N_BUCKETS // 2
    nf = jnp.maximum(n, 1).astype(F32)
    large = max_exact + (jnp.log(nf / max_exact) / math.log(MAX_DISTANCE / max_exact)
                         * (N_BUCKETS - max_exact)).astype(I32)
    large = jnp.minimum(large, N_BUCKETS - 1)
    return jnp.where(n < max_exact, n, large)


def _bias_tiles(table):
    TQ = SEQ_TILE
    qi = jnp.arange(TQ, dtype=I32)[:, None]
    ki = jnp.arange(TQ, dtype=I32)[None, :]
    d0 = qi - ki
    tbl = table.astype(F32).T

    def lookup(bucket):
        out = jnp.zeros((tbl.shape[0],) + bucket.shape, F32)
        for b in range(N_BUCKETS):
            out = jnp.where(bucket[None] == b, tbl[:, b][:, None, None], out)
        return out

    far = tbl[:, N_BUCKETS - 1][:, None, None]
    diag = jnp.where(d0 >= 0, lookup(_rel_bucket(jnp.maximum(d0, 0))) - far, MASKED)
    prev = lookup(_rel_bucket(d0 + TQ)) - far
    return jnp.stack([diag, prev], axis=1).transpose(0, 1, 3, 2)


def _pack_w_in(w):
    offs = np.cumsum([0, 512, 512, 512, 512, 512, 512, 512, 512, 512, 512, 512, 128, 128,
                      IDX_HEADS * IDX_DIM, IDX_DIM, IDX_HEADS]).tolist()
    seg = lambda a, b: w[:, offs[a]:offs[b]]
    D = w.shape[0]
    used = COL_KIW + IDX_DIM + IDX_HEADS
    main = jnp.concatenate([
        seg(0, 4),
        seg(13, 14),
        seg(4, 7),
        seg(7, 10),
        seg(10, 11),
        seg(11, 12),
        seg(12, 13),
        seg(14, 16),
        jnp.zeros((D, P_COLS - used), w.dtype),
    ], axis=1).astype(BF16)
    gates = w[:, offs[16]:].astype(BF16)
    return main, gates


def _largest_tile(n, cap):
    t = cap
    while n % t:
        t //= 2
    return t


def kernel(x, meta_tokens, rel_bias, ffn1_norm, ffn1_w_gu, ffn1_w_down, mix_norm, w_in, hgrn_lb, hgrn_gnorm, conv_w, diff_q_norm, diff_k_norm, diff_lambda, diff_subln, dsa_q_norm, dsa_k_norm, w_branch, w_out, ffn2_norm, ffn2_w_gu, ffn2_w_down):
    B, S, D = x.shape
    depth = w_in.shape[0]
    L = S + N_META
    Lp = -(-L // SEQ_TILE) * SEQ_TILE
    T = B * Lp
    top_k = min(TOPK_MAX, S // 4)
    tm = _largest_tile(T, 512)
    tm_wide = _largest_tile(T, 1024)
    d_ff = ffn1_w_down.shape[1]
    tf = _largest_tile(d_ff, 512)

    meta = jnp.broadcast_to(meta_tokens.astype(x.dtype)[None], (B, N_META, D))
    h = jnp.concatenate([meta, x, jnp.zeros((B, Lp - L, D), x.dtype)], axis=1).reshape(T, D)

    lbs = jnp.cumsum(jax.nn.softmax(hgrn_lb.astype(F32), axis=0), axis=0)
    lbs = lbs - lbs[0:1]
    near_c = _bias_tiles(rel_bias[:, :N_HEADS])
    near_d = _bias_tiles(rel_bias[:, N_HEADS:])

    for l in range(depth):
        h = _ffn(h, ffn1_norm[l], ffn1_w_gu[l].astype(BF16), ffn1_w_down[l].astype(BF16), tm, tf)

        w_main, w_gate = _pack_w_in(w_in[l])
        xn, P = _proj(h, mix_norm[l], w_main, tm_wide, 1024)
        P3 = P.reshape(B, Lp, P_COLS)
        br_a = _hgrn(P3, lbs[l], hgrn_gnorm[l])
        br_b = _conv(P3, conv_w[l])
        cqn, ckn, cvt, dqn, dkn, dvt, qib, kk, wq = _prep(
            P, diff_q_norm[l], diff_k_norm[l], dsa_q_norm[l], dsa_k_norm[l], tm)
        r3 = lambda a: a.reshape(B, Lp, a.shape[-1])
        lp = diff_lambda[l].astype(F32)
        lam_init = 0.8 - 0.6 * math.exp(-0.3 * l)
        lam = jnp.exp(jnp.sum(lp[0] * lp[1])) - jnp.exp(jnp.sum(lp[2] * lp[3])) + lam_init
        tiles = lambda a: a.reshape(B, Lp // SEQ_TILE, a.shape[-2], SEQ_TILE)
        br_c = _diff(r3(cqn), r3(ckn), tiles(cvt), near_c,
                     jnp.broadcast_to(lam, (1, SEQ_TILE)).astype(F32),
                     jnp.broadcast_to(diff_subln[l].astype(F32)[:, None], (HEAD_DIM, SEQ_TILE)),
                     1.0 - lam_init)
        br_d = _dsa(r3(qib), r3(kk), r3(wq), r3(dqn), r3(dkn), tiles(dvt), near_d, top_k)
        branches = [a.reshape(T, MIX_WIDTH) for a in (br_a, br_b, br_c, br_d)]
        h = _merge(h, xn, branches, w_gate, w_branch[l].astype(BF16), w_out[l].astype(BF16), tm, 256)

        last = l == depth - 1
        h = _ffn(h, ffn2_norm[l], ffn2_w_gu[l].astype(BF16), ffn2_w_down[l].astype(BF16),
                 _largest_tile(S, 512) if last else tm, tf,
                 keep=(B, Lp, N_META, S) if last else None)

    return h.reshape(B, S, D)
```

```python
import functools
import math

import jax
import jax.numpy as jnp
import numpy as np
from jax import lax
from jax.experimental import pallas as pl
from jax.experimental.pallas import tpu as pltpu

F32 = jnp.float32
BF16 = jnp.bfloat16
I32 = jnp.int32

EPS = 1e-6
N_META = 16
HEAD_DIM = 128
HALF = 64
N_HEADS = 4
MIX_WIDTH = N_HEADS * HEAD_DIM
IDX_HEADS = 16
IDX_DIM = 64
TOPK_MAX = 256
N_BRANCH = 4
N_BUCKETS = 32
MAX_DISTANCE = 128
CONV_W = 3
SEQ_TILE = 128
HGRN_CHUNK = 8
HGRN_GROUP = 4
MASKED = -1e30
INT_MIN = -(2 ** 31)
KEY_NEG_INF = -2139095041
VMEM_LIMIT_BYTES = 56 * 1024 * 1024

COL_A = 0
COL_QI = 2048
COL_B = 3072
COL_C = 4608
COL_DQ = 6144
COL_DK = 6656
COL_DV = 6784
COL_KIW = 6912
P_COLS = 7168


def _cparams(*sem):
    return pltpu.CompilerParams(dimension_semantics=sem, vmem_limit_bytes=VMEM_LIMIT_BYTES)


def _dot(a, b):
    return jnp.dot(a, b, preferred_element_type=F32)


def _dot_nt(a, b):
    return lax.dot_general(a, b, (((1,), (1,)), ((), ())), preferred_element_type=F32)


def _dot_tn(a, b):
    return lax.dot_general(a, b, (((0,), (0,)), ((), ())), preferred_element_type=F32)


def _rms_inv(x):
    return lax.rsqrt(jnp.mean(x * x, axis=-1, keepdims=True) + EPS)


def _ffn_body(h_ref, g_ref, wg_ref, wu_ref, wd_ref, o_ref, xn_ref):
    j = pl.program_id(1)

    @pl.when(j == 0)
    def _():
        x = h_ref[...]
        xn_ref[...] = ((x * _rms_inv(x)) * g_ref[...]).astype(BF16)
        o_ref[...] = x

    xn = xn_ref[...]
    g = _dot(xn, wg_ref[...])
    u = _dot(xn, wu_ref[...])
    a = (g * jax.nn.sigmoid(g) * (0.5 * u)).astype(BF16)
    o_ref[...] += _dot(a, wd_ref[...])


def _ffn(h, gain, w_gu, w_down, tm, tf, keep=None):
    T, D = h.shape
    F = w_down.shape[0]
    nf = F // tf
    if keep is None:
        n_row_tiles = T // tm
        row_in = pl.BlockSpec((tm, D), lambda i, j: (i, 0))
        t_out = T
    else:
        B, Lp, first, n_rows = keep
        per_seq = n_rows // tm
        n_row_tiles = B * per_seq
        row_in = pl.BlockSpec((pl.Element(tm), pl.Element(D)),
                              lambda i, j: (pl.multiple_of(
                                  (i // per_seq) * Lp + first + (i % per_seq) * tm, SUBLANES), 0))
        t_out = B * n_rows
    return pl.pallas_call(
        _ffn_body,
        grid=(n_row_tiles, nf),
        in_specs=[
            row_in,
            pl.BlockSpec((1, D), lambda i, j: (0, 0)),
            pl.BlockSpec((D, tf), lambda i, j: (0, j)),
            pl.BlockSpec((D, tf), lambda i, j: (0, j + nf)),
            pl.BlockSpec((tf, D), lambda i, j: (j, 0)),
        ],
        out_specs=pl.BlockSpec((tm, D), lambda i, j: (i, 0)),
        out_shape=jax.ShapeDtypeStruct((t_out, D), F32),
        scratch_shapes=[pltpu.VMEM((tm, D), BF16)],
        compiler_params=_cparams("parallel", "arbitrary"),
        name="ffn",
    )(h, gain.reshape(1, D), w_gu, w_gu, w_down)


def _proj_body(h_ref, g_ref, w_ref, xn_ref, p_ref):
    @pl.when(pl.program_id(1) == 0)
    def _():
        x = h_ref[...]
        xn_ref[...] = ((x * _rms_inv(x)) * g_ref[...]).astype(BF16)

    p_ref[...] = _dot(xn_ref[...], w_ref[...])


def _proj(h, gain, w, tm, tn):
    T, D = h.shape
    N = w.shape[1]
    return pl.pallas_call(
        _proj_body,
        grid=(T // tm, N // tn),
        in_specs=[
            pl.BlockSpec((tm, D), lambda i, j: (i, 0)),
            pl.BlockSpec((1, D), lambda i, j: (0, 0)),
            pl.BlockSpec((D, tn), lambda i, j: (0, j)),
        ],
        out_specs=[
            pl.BlockSpec((tm, D), lambda i, j: (i, 0)),
            pl.BlockSpec((tm, tn), lambda i, j: (i, j)),
        ],
        out_shape=[jax.ShapeDtypeStruct((T, D), BF16), jax.ShapeDtypeStruct((T, N), F32)],
        compiler_params=_cparams("parallel", "arbitrary"),
        name="proj",
    )(h, gain.reshape(1, D), w)


def _hgrn_body(q_ref, f_ref, i_ref, g_ref, loglb_ref, log1mlb_ref, omlb_ref, gn_ref, o_ref,
               st_ref, qs_ref, ks_ref, bs_ref, os_ref):
    G, R, _ = q_ref.shape
    C = HGRN_CHUNK

    @pl.when(pl.program_id(1) == 0)
    def _():
        st_ref[...] = jnp.zeros_like(st_ref)

    row = lax.broadcasted_iota(I32, (R, MIX_WIDTH), 0) % C
    for g in range(G):
        z = f_ref[g]
        log_sig = jnp.minimum(z, 0.0) - jnp.log(1.0 + jnp.exp(-jnp.abs(z)))
        y = log1mlb_ref[...] + log_sig
        a = loglb_ref[...]
        log_f = jnp.maximum(a, y) + jnp.log(1.0 + jnp.exp(-jnp.abs(a - y)))
        b = log_f
        for sh in [1 << e for e in range(C.bit_length() - 1)]:
            b = b + jnp.where(row >= sh, pltpu.roll(b, sh, axis=0), 0.0)
        bs_ref[g] = b
        ks_ref[g] = omlb_ref[...] * jax.nn.sigmoid(-z)
        qr = q_ref[g]
        qs_ref[g] = qr * jax.nn.sigmoid(qr)

    t_iota = lax.broadcasted_iota(I32, (C, 1), 0)

    def chunk(c, carry):
        r0 = pl.multiple_of(c * C, C)
        for g in range(G):
            for h in range(N_HEADS):
                hs = slice(h * HEAD_DIM, (h + 1) * HEAD_DIM)
                qc = qs_ref[g, pl.ds(r0, C), hs]
                kc = ks_ref[g, pl.ds(r0, C), hs]
                bc = bs_ref[g, pl.ds(r0, C), hs]
                vc = i_ref[g, pl.ds(r0, C), hs]
                st = st_ref[g, h]
                o = _dot_nt((qc * jnp.exp(bc)).astype(BF16), st.astype(BF16))
                for s in range(C):
                    b_s = bc[s:s + 1, :]
                    k_s = kc[s:s + 1, :]
                    v_s = vc[s:s + 1, :]
                    e = jnp.exp(bc - b_s)
                    col = jnp.sum(qc * k_s * e, axis=-1, keepdims=True)
                    col = jnp.where(t_iota >= s, col, 0.0)
                    o = o + col * v_s
                os_ref[g, pl.ds(r0, C), hs] = o
                b_last = bc[C - 1:C, :]
                kd = kc * jnp.exp(b_last - bc)
                st_ref[g, h] = st * jnp.exp(b_last) + _dot_tn(vc.astype(BF16), kd.astype(BF16))
        return carry

    lax.fori_loop(0, R // C, chunk, 0)

    for g in range(G):
        for h in range(N_HEADS):
            hs = slice(h * HEAD_DIM, (h + 1) * HEAD_DIM)
            o = os_ref[g, :, hs]
            on = (o * _rms_inv(o)) * gn_ref[:, hs]
            gr = g_ref[g, :, hs]
            o_ref[g, :, hs] = (on * (gr * jax.nn.sigmoid(gr))).astype(o_ref.dtype)


def _hgrn(P3, lb, gnorm):
    B, Lp, _ = P3.shape
    R = SEQ_TILE
    W = MIX_WIDTH
    G = _largest_tile(B, HGRN_GROUP)
    lb = lb.reshape(1, W).astype(F32)
    vec = pl.BlockSpec((1, W), lambda b, t: (0, 0))
    c0 = COL_A // W
    return pl.pallas_call(
        _hgrn_body,
        grid=(B // G, Lp // R),
        in_specs=[pl.BlockSpec((G, R, W), functools.partial(lambda b, t, c: (b, t, c), c=c0 + k))
                  for k in range(4)] + [vec, vec, vec, vec],
        out_specs=pl.BlockSpec((G, R, W), lambda b, t: (b, t, 0)),
        out_shape=jax.ShapeDtypeStruct((B, Lp, W), BF16),
        scratch_shapes=[pltpu.VMEM((G, N_HEADS, HEAD_DIM, HEAD_DIM), F32)]
        + [pltpu.VMEM((G, R, W), F32)] * 4,
        compiler_params=_cparams("parallel", "arbitrary"),
        name="hgrn",
    )(P3, P3, P3, P3, jnp.log(lb), jnp.log1p(-lb), 1.0 - lb, gnorm.reshape(1, W).astype(F32))


def _conv_body(b_ref, c_ref, u_ref, w_ref, o_ref, carry_ref):
    @pl.when(pl.program_id(1) == 0)
    def _():
        carry_ref[...] = jnp.zeros_like(carry_ref)

    zc = c_ref[...] * u_ref[...]
    R = zc.shape[0]
    row = lax.broadcasted_iota(I32, zc.shape, 0)
    last = carry_ref[7:8, :]
    last2 = carry_ref[6:7, :]
    z1 = jnp.where(row == 0, last, pltpu.roll(zc, 1, axis=0))
    z2 = jnp.where(row == 0, last2, jnp.where(row == 1, last, pltpu.roll(zc, 2, axis=0)))
    y = w_ref[0:1, :] * zc + w_ref[1:2, :] * z1 + w_ref[2:3, :] * z2
    o_ref[...] = (b_ref[...] * y).astype(o_ref.dtype)
    carry_ref[...] = zc[R - 8:, :]


def _conv(P3, conv_w):
    B, Lp, _ = P3.shape
    R = SEQ_TILE
    W = MIX_WIDTH
    c0 = COL_B // W
    w8 = jnp.zeros((8, W), F32).at[:CONV_W].set(conv_w.astype(F32))
    return pl.pallas_call(
        _conv_body,
        grid=(B, Lp // R),
        in_specs=[pl.BlockSpec((None, R, W), functools.partial(lambda b, t, c: (b, t, c), c=c0 + k))
                  for k in range(3)] + [pl.BlockSpec((8, W), lambda b, t: (0, 0))],
        out_specs=pl.BlockSpec((None, R, W), lambda b, t: (b, t, 0)),
        out_shape=jax.ShapeDtypeStruct((B, Lp, W), BF16),
        scratch_shapes=[pltpu.VMEM((8, W), F32)],
        compiler_params=_cparams("parallel", "arbitrary"),
        name="conv",
    )(P3, P3, P3, w8)


def _prep_body(cq_ref, ck_ref, cv_ref, dq_ref, dk_ref, dv_ref, kiw_ref, qi_ref,
               cqg_ref, ckg_ref, dqg_ref, dkg_ref,
               cqn_ref, ckn_ref, cvb_ref, dqn_ref, dkn_ref, dvb_ref, qib_ref, kk_ref, wq_ref):
    lane = lax.broadcasted_iota(I32, (1, HEAD_DIM), 1)
    lo = lane < HALF

    def half_norm(x, g, scale):
        sq = x * x
        ms_lo = jnp.sum(jnp.where(lo, sq, 0.0), axis=-1, keepdims=True) * (1.0 / HALF)
        ms_hi = jnp.sum(jnp.where(lo, 0.0, sq), axis=-1, keepdims=True) * (1.0 / HALF)
        inv = jnp.where(lo, lax.rsqrt(ms_lo + EPS), lax.rsqrt(ms_hi + EPS))
        return ((x * inv) * g) * scale

    for h in range(N_HEADS):
        hs = slice(h * HEAD_DIM, (h + 1) * HEAD_DIM)
        cqn_ref[:, hs] = half_norm(cq_ref[:, hs], cqg_ref[...], HALF ** -0.5).astype(BF16)
        ckn_ref[:, hs] = half_norm(ck_ref[:, hs], ckg_ref[...], 1.0).astype(BF16)
        x = dq_ref[:, hs]
        dqn_ref[:, hs] = (((x * _rms_inv(x)) * dqg_ref[...]) * (HEAD_DIM ** -0.5)).astype(BF16)
    x = dk_ref[...]
    dkn_ref[...] = ((x * _rms_inv(x)) * dkg_ref[...]).astype(BF16)
    for t in range(cv_ref.shape[0] // SEQ_TILE):
        rs = slice(t * SEQ_TILE, (t + 1) * SEQ_TILE)
        for h in range(N_HEADS):
            hs = slice(h * HEAD_DIM, (h + 1) * HEAD_DIM)
            cvb_ref[t, hs, :] = cv_ref[rs, hs].T.astype(BF16)
        dvb_ref[t] = dv_ref[rs, :].T.astype(BF16)
    qib_ref[...] = (qi_ref[...] * (IDX_DIM ** -0.5)).astype(BF16)
    kiw = kiw_ref[...]
    swapped = pltpu.roll(kiw, HALF, axis=1)
    kk_ref[...] = jnp.where(lo, kiw, swapped).astype(BF16)
    wq_ref[...] = swapped * (IDX_HEADS ** -0.5)


def _prep(P, cq_g, ck_g, dq_g, dk_g, tm):
    T = P.shape[0]
    W = MIX_WIDTH
    H = HEAD_DIM

    def col(width, start):
        return pl.BlockSpec((tm, width), functools.partial(lambda i, c: (i, c), c=start // width))

    vec = pl.BlockSpec((1, H), lambda i: (0, 0))
    out = lambda width: pl.BlockSpec((tm, width), lambda i: (i, 0))
    tiles = lambda width: pl.BlockSpec((tm // SEQ_TILE, width, SEQ_TILE), lambda i: (i, 0, 0))
    return pl.pallas_call(
        _prep_body,
        grid=(T // tm,),
        in_specs=[col(W, COL_C), col(W, COL_C + W), col(W, COL_C + 2 * W), col(W, COL_DQ),
                  col(H, COL_DK), col(H, COL_DV), col(H, COL_KIW), col(IDX_HEADS * IDX_DIM, COL_QI),
                  vec, vec, vec, vec],
        out_specs=[out(W), out(W), tiles(W), out(W), out(H), tiles(H), out(IDX_HEADS * IDX_DIM), out(H), out(H)],
        out_shape=[jax.ShapeDtypeStruct((T, W), BF16)] * 2
        + [jax.ShapeDtypeStruct((T // SEQ_TILE, W, SEQ_TILE), BF16), jax.ShapeDtypeStruct((T, W), BF16),
           jax.ShapeDtypeStruct((T, H), BF16), jax.ShapeDtypeStruct((T // SEQ_TILE, H, SEQ_TILE), BF16)]
        + [jax.ShapeDtypeStruct((T, IDX_HEADS * IDX_DIM), BF16),
           jax.ShapeDtypeStruct((T, H), BF16), jax.ShapeDtypeStruct((T, H), F32)],
        compiler_params=_cparams("parallel"),
        name="prep",
    )(P, P, P, P, P, P, P, P,
      jnp.tile(cq_g.astype(F32), 2).reshape(1, H), jnp.tile(ck_g.astype(F32), 2).reshape(1, H),
      dq_g.astype(F32).reshape(1, H), dk_g.astype(F32).reshape(1, H))


SUBLANES = 8
KEY_CHUNK = 8


def _max8(x):
    return jnp.max(x.reshape(x.shape[0] // SUBLANES, SUBLANES, x.shape[1]), axis=0)


def _sum8(x):
    return jnp.sum(x.reshape(x.shape[0] // SUBLANES, SUBLANES, x.shape[1]), axis=0)


def _bit_transpose32(rows):
    a = list(rows)
    j = 16
    m = 0x0000FFFF
    while j:
        k = 0
        while k < 32:
            t = (a[k] ^ lax.shift_right_logical(a[k + j], jnp.int32(j))) & jnp.int32(m)
            a[k] = a[k] ^ t
            a[k + j] = a[k + j] ^ (t << j)
            k = (k + j + 1) & ~j
        j >>= 1
        m ^= (m << j) & 0xFFFFFFFF
    return a


def _for_tiles(n_tiles, body, chunk=KEY_CHUNK):
    n_chunks = n_tiles // chunk

    def chunked(c, carry):
        body(c * chunk, chunk)
        return carry

    lax.fori_loop(0, n_chunks, chunked, 0)
    base = n_chunks * chunk
    size = chunk // 2
    while size >= 1:
        take = (n_tiles - base) // size

        def part(_, carry, base=base, size=size):
            body(base, size)
            return carry

        lax.fori_loop(0, take, part, 0)
        base = base + take * size
        size //= 2


def _diff_body(q_ref, k_ref, vt_ref, near_ref, lam_ref, sub_ref, o_ref,
               q2_ref, s_ref, mx_ref, l_ref, acc_ref, *, out_scale):
    i = pl.program_id(1)
    TQ = SEQ_TILE
    H = HEAD_DIM
    W2 = 2 * TQ
    lane = lax.broadcasted_iota(I32, (1, H), 1)
    for h in range(N_HEADS):
        q = q_ref[:, h * H:(h + 1) * H]
        zero = jnp.zeros_like(q)
        q2_ref[h, 0:TQ, :] = jnp.where(lane < HALF, q, zero)
        q2_ref[h, TQ:, :] = jnp.where(lane < HALF, zero, q)
    mx_ref[...] = jnp.full_like(mx_ref, MASKED)

    def put(j0, n, near_idx):
        k0 = pl.multiple_of(j0 * TQ, TQ)
        parts = []
        for h in range(N_HEADS):
            s = _dot_nt(k_ref[pl.ds(k0, n * TQ), h * H:(h + 1) * H], q2_ref[h])
            if near_idx is None:
                parts.append(s)
            else:
                b = near_ref[h, near_idx]
                parts.append(s + jnp.concatenate([b, b], axis=1))
        s = jnp.concatenate(parts, axis=1)
        for t in range(n):
            s_ref[j0 + t] = s[t * TQ:(t + 1) * TQ, :]
        mx_ref[...] = jnp.maximum(mx_ref[...], _max8(s))

    n_far = jnp.maximum(i - 1, 0)
    _for_tiles(n_far, lambda j0, n: put(j0, n, None))

    def prev(j, carry):
        put(j, 1, 1)
        return carry

    lax.fori_loop(n_far, i, prev, 0)
    put(i, 1, 0)

    m = jnp.max(mx_ref[...], axis=0, keepdims=True)
    l_ref[...] = jnp.zeros_like(l_ref)
    acc_ref[...] = jnp.zeros_like(acc_ref)

    def pv(j0, n):
        p = jnp.exp(s_ref[pl.ds(j0, n)].reshape(n * TQ, N_HEADS * W2) - m)
        l_ref[...] += _sum8(p)
        pb = p.astype(BF16)
        for h in range(N_HEADS):
            acc = acc_ref[:, h * W2:(h + 1) * W2]
            for t in range(n):
                acc = acc + _dot(vt_ref[j0 + t, h * H:(h + 1) * H, :],
                                 pb[t * TQ:(t + 1) * TQ, h * W2:(h + 1) * W2])
            acc_ref[:, h * W2:(h + 1) * W2] = acc

    _for_tiles(i + 1, pv)

    ot = acc_ref[...] * (1.0 / jnp.sum(l_ref[...], axis=0, keepdims=True))
    for h in range(N_HEADS):
        od = ot[:, h * W2:h * W2 + TQ] - lam_ref[...] * ot[:, h * W2 + TQ:(h + 1) * W2]
        inv = lax.rsqrt(jnp.mean(od * od, axis=0, keepdims=True) + EPS)
        o_ref[:, h * H:(h + 1) * H] = (((od * inv) * sub_ref[...]) * out_scale).T.astype(o_ref.dtype)


def _diff(cqn, ckn, cvt, near_t, lam, subln, out_scale):
    B, Lp, W = cqn.shape
    TQ = SEQ_TILE
    H = HEAD_DIM
    nt = Lp // TQ
    return pl.pallas_call(
        functools.partial(_diff_body, out_scale=out_scale),
        grid=(B, nt),
        in_specs=[
            pl.BlockSpec((None, TQ, W), lambda b, i: (b, i, 0)),
            pl.BlockSpec((None, Lp, W), lambda b, i: (b, 0, 0)),
            pl.BlockSpec((None, nt, W, TQ), lambda b, i: (b, 0, 0, 0)),
            pl.BlockSpec((N_HEADS, 2, TQ, TQ), lambda b, i: (0, 0, 0, 0)),
            pl.BlockSpec((1, TQ), lambda b, i: (0, 0)),
            pl.BlockSpec((H, TQ), lambda b, i: (0, 0)),
        ],
        out_specs=pl.BlockSpec((None, TQ, W), lambda b, i: (b, i, 0)),
        out_shape=jax.ShapeDtypeStruct((B, Lp, W), BF16),
        scratch_shapes=[pltpu.VMEM((N_HEADS, 2 * TQ, H), BF16),
                        pltpu.VMEM((nt, TQ, N_HEADS * 2 * TQ), F32),
                        pltpu.VMEM((SUBLANES, N_HEADS * 2 * TQ), F32),
                        pltpu.VMEM((SUBLANES, N_HEADS * 2 * TQ), F32),
                        pltpu.VMEM((H, N_HEADS * 2 * TQ), F32)],
        compiler_params=_cparams("parallel", "arbitrary"),
        name="diff",
    )(cqn, ckn, cvt, near_t, lam, subln)


def _dsa_body(qi_ref, kk_ref, wq_ref, q_ref, k_ref, vt_ref, near_ref, o_ref,
              qi2_ref, w_ref, key_ref, kq_ref, plane_ref, alive_ref, q4_ref, thr_ref, s_ref, mx_ref, l_ref, acc_ref, *, top_k):
    i = pl.program_id(1)
    TQ = SEQ_TILE
    H = HEAD_DIM
    lane = lax.broadcasted_iota(I32, (1, H), 1)

    for p in range(IDX_HEADS // 2):
        x = qi_ref[:, p * H:(p + 1) * H]
        zero = jnp.zeros_like(x)
        qi2_ref[p, 0:TQ, :] = jnp.where(lane < HALF, x, zero)
        qi2_ref[p, TQ:, :] = jnp.where(lane < HALF, zero, x)
    w_ref[...] = wq_ref[...].T
    for h in range(N_HEADS):
        q4_ref[h * TQ:(h + 1) * TQ, :] = q_ref[:, h * H:(h + 1) * H]

    def index(j0, n):
        k0 = pl.multiple_of(j0 * TQ, TQ)
        kt = kk_ref[pl.ds(k0, n * TQ), :]
        acc = jnp.zeros((n * TQ, TQ), F32)
        for p in range(IDX_HEADS // 2):
            s = jnp.maximum(_dot_nt(kt, qi2_ref[p]), 0.0)
            acc = acc + s[:, 0:TQ] * w_ref[2 * p:2 * p + 1, :] + s[:, TQ:] * w_ref[2 * p + 1:2 * p + 2, :]
        acc = acc + 0.0
        bits = pltpu.bitcast(acc, I32)
        key = bits ^ ((bits >> 31) & jnp.int32(0x7FFFFFFF))
        tiles = []
        for t in range(n):
            kt_ = key[t * TQ:(t + 1) * TQ, :]
            if t == n - 1:
                kt_ = jnp.where(kq_ref[...] <= (i - (j0 + t)) * TQ, kt_, jnp.int32(KEY_NEG_INF))
            key_ref[j0 + t] = kt_
            tiles.append(kt_)
        if n == 1:
            tiles.append(None)
        for u in range(len(tiles) // 2):
            rows = []
            for tile in tiles[2 * u:2 * u + 2]:
                for r in range(TQ // SUBLANES):
                    if tile is None:
                        rows.append(jnp.zeros((SUBLANES, TQ), I32))
                    else:
                        rows.append(tile[r * SUBLANES:(r + 1) * SUBLANES, :] ^ jnp.int32(INT_MIN))
            planes = _bit_transpose32(rows)
            pair = j0 // 2 + u
            for b in range(32):
                plane_ref[pair, b] = planes[b]

    @pl.when(i == 0)
    def _():
        plane_ref[...] = jnp.zeros_like(plane_ref)

    kq_ref[...] = lax.broadcasted_iota(I32, (TQ, TQ), 0) - lax.broadcasted_iota(I32, (TQ, TQ), 1)
    _for_tiles(i + 1, index)

    n_pairs = plane_ref.shape[0]
    for tp in range(n_pairs):
        alive_ref[tp] = jnp.full((SUBLANES, TQ), -1, I32)

    def bit_step(t, carry):
        above, thr = carry
        ones = [alive_ref[tp] & plane_ref[tp, t] for tp in range(n_pairs)]
        cnt = lax.population_count(ones[0])
        for tp in range(1, n_pairs):
            cnt = cnt + lax.population_count(ones[tp])
        hit = above + jnp.sum(cnt, axis=0, keepdims=True)
        take = hit >= top_k
        for tp in range(n_pairs):
            alive_ref[tp] = jnp.where(take, ones[tp], alive_ref[tp] ^ ones[tp])
        thr = thr | jnp.where(take, jnp.left_shift(jnp.int32(1), 31 - t), 0)
        return jnp.where(take, above, hit), thr

    zero8 = jnp.zeros((SUBLANES, TQ), I32)
    _, thr_u = lax.fori_loop(0, 32, bit_step, (zero8, zero8))
    thr_ref[...] = thr_u ^ jnp.int32(INT_MIN)

    mx_ref[...] = jnp.full_like(mx_ref, MASKED)

    def put(j0, n, near_idx):
        k0 = pl.multiple_of(j0 * TQ, TQ)
        s = _dot_nt(k_ref[pl.ds(k0, n * TQ), :], q4_ref[...])
        thr1 = thr_ref[0:1, :]
        mx = mx_ref[...]
        for t in range(n):
            sel = key_ref[j0 + t] >= thr1
            parts = []
            for h in range(N_HEADS):
                sh = s[t * TQ:(t + 1) * TQ, h * TQ:(h + 1) * TQ]
                if near_idx is not None:
                    sh = sh + near_ref[h, near_idx]
                parts.append(jnp.where(sel, sh, MASKED))
            row = jnp.concatenate(parts, axis=1)
            s_ref[j0 + t] = row
            mx = jnp.maximum(mx, _max8(row))
        mx_ref[...] = mx

    n_far = jnp.maximum(i - 1, 0)
    _for_tiles(n_far, lambda j0, n: put(j0, n, None))

    def prev(j, carry):
        put(j, 1, 1)
        return carry

    lax.fori_loop(n_far, i, prev, 0)
    put(i, 1, 0)

    m = jnp.max(mx_ref[...], axis=0, keepdims=True)
    l_ref[...] = jnp.zeros_like(l_ref)
    acc_ref[...] = jnp.zeros_like(acc_ref)

    def pv(j0, n):
        p = jnp.exp(s_ref[pl.ds(j0, n)].reshape(n * TQ, N_HEADS * TQ) - m)
        l_ref[...] += _sum8(p)
        pb = p.astype(BF16)
        acc = acc_ref[...]
        for t in range(n):
            acc = acc + _dot(vt_ref[j0 + t], pb[t * TQ:(t + 1) * TQ, :])
        acc_ref[...] = acc

    _for_tiles(i + 1, pv)

    ot = acc_ref[...] * (1.0 / jnp.sum(l_ref[...], axis=0, keepdims=True))
    for h in range(N_HEADS):
        o_ref[:, h * H:(h + 1) * H] = ot[:, h * TQ:(h + 1) * TQ].T.astype(o_ref.dtype)


def _dsa(qib, kk, wq, dqn, dkn, dvt, near_t, top_k):
    B, Lp, W = dqn.shape
    TQ = SEQ_TILE
    H = HEAD_DIM
    NI = IDX_HEADS * IDX_DIM
    nt = Lp // TQ
    seq = lambda width: pl.BlockSpec((None, Lp, width), lambda b, i: (b, 0, 0))
    blk = lambda width: pl.BlockSpec((None, TQ, width), lambda b, i: (b, i, 0))
    return pl.pallas_call(
        functools.partial(_dsa_body, top_k=top_k),
        grid=(B, nt),
        in_specs=[blk(NI), seq(H), blk(H), blk(W), seq(H),
                  pl.BlockSpec((None, nt, H, TQ), lambda b, i: (b, 0, 0, 0)),
                  pl.BlockSpec((N_HEADS, 2, TQ, TQ), lambda b, i: (0, 0, 0, 0))],
        out_specs=blk(W),
        out_shape=jax.ShapeDtypeStruct((B, Lp, W), BF16),
        scratch_shapes=[
            pltpu.VMEM((IDX_HEADS // 2, 2 * TQ, H), BF16),
            pltpu.VMEM((H, TQ), F32),
            pltpu.VMEM((nt, TQ, TQ), I32),
            pltpu.VMEM((TQ, TQ), I32),
            pltpu.VMEM(((nt + 1) // 2, 32, SUBLANES, TQ), I32),
            pltpu.VMEM(((nt + 1) // 2, SUBLANES, TQ), I32),
            pltpu.VMEM((N_HEADS * TQ, H), BF16),
            pltpu.VMEM((SUBLANES, TQ), I32),
            pltpu.VMEM((nt, TQ, N_HEADS * TQ), F32),
            pltpu.VMEM((SUBLANES, N_HEADS * TQ), F32),
            pltpu.VMEM((SUBLANES, N_HEADS * TQ), F32),
            pltpu.VMEM((H, N_HEADS * TQ), F32),
        ],
        compiler_params=_cparams("parallel", "arbitrary"),
        name="dsa",
    )(qib, kk, wq, dqn, dkn, dvt, near_t)


def _merge_body(h_ref, xn_ref, b0_ref, b1_ref, b2_ref, b3_ref, g0_ref, g1_ref, g2_ref, g3_ref,
                wb_ref, wo_ref, o_ref):
    @pl.when(pl.program_id(1) == 0)
    def _():
        o_ref[...] = h_ref[...]

    xn = xn_ref[...]
    merged = None
    for m, (b_ref, g_ref) in enumerate(((b0_ref, g0_ref), (b1_ref, g1_ref), (b2_ref, g2_ref), (b3_ref, g3_ref))):
        term = jax.nn.sigmoid(_dot(xn, g_ref[...])) * _dot(b_ref[...], wb_ref[m])
        merged = term if merged is None else merged + term
    o_ref[...] += _dot(merged.astype(BF16), wo_ref[...])


def _merge(h, xn, branches, w_gate, w_branch, w_out, tm, tn):
    T, D = h.shape
    W = MIX_WIDTH
    nn = D // tn
    row = lambda width: pl.BlockSpec((tm, width), lambda i, n: (i, 0))
    gate = lambda m: pl.BlockSpec((D, tn), functools.partial(lambda i, n, m: (0, m * nn + n), m=m))
    return pl.pallas_call(
        _merge_body,
        grid=(T // tm, nn),
        in_specs=[row(D), row(D), row(W), row(W), row(W), row(W),
                  gate(0), gate(1), gate(2), gate(3),
                  pl.BlockSpec((N_BRANCH, W, tn), lambda i, n: (0, 0, n)),
                  pl.BlockSpec((tn, D), lambda i, n: (n, 0))],
        out_specs=row(D),
        out_shape=jax.ShapeDtypeStruct((T, D), F32),
        compiler_params=_cparams("parallel", "arbitrary"),
        name="merge",
    )(h, xn, *branches, w_gate, w_gate, w_gate, w_gate, w_branch, w_out)


def _rel_bucket(n):
    max_exact = N_BUCKETS // 2
    nf = jnp.maximum(n, 1).astype(F32)
    large = max_exact + (jnp.log(nf / max_exact) / math.log(MAX_DISTANCE / max_exact)
                         * (N_BUCKETS - max_exact)).astype(I32)
    large = jnp.minimum(large, N_BUCKETS - 1)
    return jnp.where(n < max_exact, n, large)


def _bias_tiles(table):
    TQ = SEQ_TILE
    qi = jnp.arange(TQ, dtype=I32)[:, None]
    ki = jnp.arange(TQ, dtype=I32)[None, :]
    d0 = qi - ki
    tbl = table.astype(F32).T

    def lookup(bucket):
        out = jnp.zeros((tbl.shape[0],) + bucket.shape, F32)
        for b in range(N_BUCKETS):
            out = jnp.where(bucket[None] == b, tbl[:, b][:, None, None], out)
        return out

    far = tbl[:, N_BUCKETS - 1][:, None, None]
    diag = jnp.where(d0 >= 0, lookup(_rel_bucket(jnp.maximum(d0, 0))) - far, MASKED)
    prev = lookup(_rel_bucket(d0 + TQ)) - far
    return jnp.stack([diag, prev], axis=1).transpose(0, 1, 3, 2)


def _pack_w_in(w):
    offs = np.cumsum([0, 512, 512, 512, 512, 512, 512, 512, 512, 512, 512, 512, 128, 128,
                      IDX_HEADS * IDX_DIM, IDX_DIM, IDX_HEADS]).tolist()
    seg = lambda a, b: w[:, offs[a]:offs[b]]
    D = w.shape[0]
    used = COL_KIW + IDX_DIM + IDX_HEADS
    main = jnp.concatenate([
        seg(0, 4),
        seg(13, 14),
        seg(4, 7),
        seg(7, 10),
        seg(10, 11),
        seg(11, 12),
        seg(12, 13),
        seg(14, 16),
        jnp.zeros((D, P_COLS - used), w.dtype),
    ], axis=1).astype(BF16)
    gates = w[:, offs[16]:].astype(BF16)
    return main, gates


def _largest_tile(n, cap):
    t = cap
    while n % t:
        t //= 2
    return t


def kernel(x, meta_tokens, rel_bias, ffn1_norm, ffn1_w_gu, ffn1_w_down, mix_norm, w_in, hgrn_lb, hgrn_gnorm, conv_w, diff_q_norm, diff_k_norm, diff_lambda, diff_subln, dsa_q_norm, dsa_k_norm, w_branch, w_out, ffn2_norm, ffn2_w_gu, ffn2_w_down):
    B, S, D = x.shape
    depth = w_in.shape[0]
    L = S + N_META
    Lp = -(-L // SEQ_TILE) * SEQ_TILE
    T = B * Lp
    top_k = min(TOPK_MAX, S // 4)
    tm = _largest_tile(T, 512)
    tm_wide = _largest_tile(T, 1024)
    d_ff = ffn1_w_down.shape[1]
    tf = _largest_tile(d_ff, 512)

    meta = jnp.broadcast_to(meta_tokens.astype(x.dtype)[None], (B, N_META, D))
    h = jnp.concatenate([meta, x, jnp.zeros((B, Lp - L, D), x.dtype)], axis=1).reshape(T, D)

    lbs = jnp.cumsum(jax.nn.softmax(hgrn_lb.astype(F32), axis=0), axis=0)
    lbs = lbs - lbs[0:1]
    near_c = _bias_tiles(rel_bias[:, :N_HEADS])
    near_d = _bias_tiles(rel_bias[:, N_HEADS:])

    for l in range(depth):
        h = _ffn(h, ffn1_norm[l], ffn1_w_gu[l].astype(BF16), ffn1_w_down[l].astype(BF16), tm, tf)

        w_main, w_gate = _pack_w_in(w_in[l])
        xn, P = _proj(h, mix_norm[l], w_main, tm_wide, 1024)
        P3 = P.reshape(B, Lp, P_COLS)
        br_a = _hgrn(P3, lbs[l], hgrn_gnorm[l])
        br_b = _conv(P3, conv_w[l])
        cqn, ckn, cvt, dqn, dkn, dvt, qib, kk, wq = _prep(
            P, diff_q_norm[l], diff_k_norm[l], dsa_q_norm[l], dsa_k_norm[l], tm)
        r3 = lambda a: a.reshape(B, Lp, a.shape[-1])
        lp = diff_lambda[l].astype(F32)
        lam_init = 0.8 - 0.6 * math.exp(-0.3 * l)
        lam = jnp.exp(jnp.sum(lp[0] * lp[1])) - jnp.exp(jnp.sum(lp[2] * lp[3])) + lam_init
        tiles = lambda a: a.reshape(B, Lp // SEQ_TILE, a.shape[-2], SEQ_TILE)
        br_c = _diff(r3(cqn), r3(ckn), tiles(cvt), near_c,
                     jnp.broadcast_to(lam, (1, SEQ_TILE)).astype(F32),
                     jnp.broadcast_to(diff_subln[l].astype(F32)[:, None], (HEAD_DIM, SEQ_TILE)),
                     1.0 - lam_init)
        br_d = _dsa(r3(qib), r3(kk), r3(wq), r3(dqn), r3(dkn), tiles(dvt), near_d, top_k)
        branches = [a.reshape(T, MIX_WIDTH) for a in (br_a, br_b, br_c, br_d)]
        h = _merge(h, xn, branches, w_gate, w_branch[l].astype(BF16), w_out[l].astype(BF16), tm, 512)

        last = l == depth - 1
        h = _ffn(h, ffn2_norm[l], ffn2_w_gu[l].astype(BF16), ffn2_w_down[l].astype(BF16),
                 _largest_tile(S, 512) if last else tm, tf,
                 keep=(B, Lp, N_META, S) if last else None)

    return h.reshape(B, S, D)
```

```python
import functools
import math

import jax
import jax.numpy as jnp
import numpy as np
from jax import lax
from jax.experimental import pallas as pl
from jax.experimental.pallas import tpu as pltpu

F32 = jnp.float32
BF16 = jnp.bfloat16
I32 = jnp.int32

EPS = 1e-6
N_META = 16
HEAD_DIM = 128
HALF = 64
N_HEADS = 4
MIX_WIDTH = N_HEADS * HEAD_DIM
IDX_HEADS = 16
IDX_DIM = 64
TOPK_MAX = 256
N_BRANCH = 4
N_BUCKETS = 32
MAX_DISTANCE = 128
CONV_W = 3
SEQ_TILE = 128
HGRN_CHUNK = 8
HGRN_GROUP = 4
CONV_ROWS_MAX = 1408
FFN_ROWS_MAX = 768
MASKED = -1e30
INT_MIN = -(2 ** 31)
KEY_NEG_INF = -2139095041
VMEM_LIMIT_BYTES = 56 * 1024 * 1024

COL_A = 0
COL_QI = 2048
COL_B = 3072
COL_C = 4608
COL_DQ = 6144
COL_DK = 6656
COL_DV = 6784
COL_KIW = 6912
P_COLS = 7168


def _cparams(*sem):
    return pltpu.CompilerParams(dimension_semantics=sem, vmem_limit_bytes=VMEM_LIMIT_BYTES)


def _dot(a, b):
    return jnp.dot(a, b, preferred_element_type=F32)


def _dot_nt(a, b):
    return lax.dot_general(a, b, (((1,), (1,)), ((), ())), preferred_element_type=F32)


def _dot_tn(a, b):
    return lax.dot_general(a, b, (((0,), (0,)), ((), ())), preferred_element_type=F32)


def _rms_inv(x):
    return lax.rsqrt(jnp.mean(x * x, axis=-1, keepdims=True) + EPS)


def _ffn_body(h_ref, g_ref, wg_ref, wu_ref, wd_ref, o_ref, xn_ref):
    j = pl.program_id(1)

    @pl.when(j == 0)
    def _():
        x = h_ref[...]
        xn_ref[...] = ((x * _rms_inv(x)) * g_ref[...]).astype(BF16)
        o_ref[...] = x

    xn = xn_ref[...]
    g = _dot(xn, wg_ref[...])
    u = _dot(xn, wu_ref[...])
    a = (g * jax.nn.sigmoid(g) * (0.5 * u)).astype(BF16)
    o_ref[...] += _dot(a, wd_ref[...])


def _ffn(h, gain, w_gu, w_down, tm, tf, keep=None):
    T, D = h.shape
    F = w_down.shape[0]
    nf = F // tf
    if keep is None:
        n_row_tiles = T // tm
        row_in = pl.BlockSpec((tm, D), lambda i, j: (i, 0))
        t_out = T
    else:
        B, Lp, first, n_rows = keep
        per_seq = n_rows // tm
        n_row_tiles = B * per_seq
        row_in = pl.BlockSpec((pl.Element(tm), pl.Element(D)),
                              lambda i, j: (pl.multiple_of(
                                  (i // per_seq) * Lp + first + (i % per_seq) * tm, SUBLANES), 0))
        t_out = B * n_rows
    return pl.pallas_call(
        _ffn_body,
        grid=(n_row_tiles, nf),
        in_specs=[
            row_in,
            pl.BlockSpec((1, D), lambda i, j: (0, 0)),
            pl.BlockSpec((D, tf), lambda i, j: (0, j)),
            pl.BlockSpec((D, tf), lambda i, j: (0, j + nf)),
            pl.BlockSpec((tf, D), lambda i, j: (j, 0)),
        ],
        out_specs=pl.BlockSpec((tm, D), lambda i, j: (i, 0)),
        out_shape=jax.ShapeDtypeStruct((t_out, D), F32),
        scratch_shapes=[pltpu.VMEM((tm, D), BF16)],
        compiler_params=_cparams("parallel", "arbitrary"),
        name="ffn",
    )(h, gain.reshape(1, D), w_gu, w_gu, w_down)


def _proj_body(h_ref, g_ref, w_ref, xn_ref, p_ref):
    @pl.when(pl.program_id(1) == 0)
    def _():
        x = h_ref[...]
        xn_ref[...] = ((x * _rms_inv(x)) * g_ref[...]).astype(BF16)

    p_ref[...] = _dot(xn_ref[...], w_ref[...])


def _proj(h, gain, w, tm, tn):
    T, D = h.shape
    N = w.shape[1]
    return pl.pallas_call(
        _proj_body,
        grid=(T // tm, N // tn),
        in_specs=[
            pl.BlockSpec((tm, D), lambda i, j: (i, 0)),
            pl.BlockSpec((1, D), lambda i, j: (0, 0)),
            pl.BlockSpec((D, tn), lambda i, j: (0, j)),
        ],
        out_specs=[
            pl.BlockSpec((tm, D), lambda i, j: (i, 0)),
            pl.BlockSpec((tm, tn), lambda i, j: (i, j)),
        ],
        out_shape=[jax.ShapeDtypeStruct((T, D), BF16), jax.ShapeDtypeStruct((T, N), F32)],
        compiler_params=_cparams("parallel", "arbitrary"),
        name="proj",
    )(h, gain.reshape(1, D), w)


def _hgrn_body(q_ref, f_ref, i_ref, g_ref, loglb_ref, log1mlb_ref, omlb_ref, gn_ref, o_ref,
               st_ref, qs_ref, ks_ref, bs_ref, os_ref):
    G, R, _ = q_ref.shape
    C = HGRN_CHUNK

    @pl.when(pl.program_id(1) == 0)
    def _():
        st_ref[...] = jnp.zeros_like(st_ref)

    row = lax.broadcasted_iota(I32, (R, MIX_WIDTH), 0) % C
    for g in range(G):
        z = f_ref[g]
        log_sig = jnp.minimum(z, 0.0) - jnp.log(1.0 + jnp.exp(-jnp.abs(z)))
        y = log1mlb_ref[...] + log_sig
        a = loglb_ref[...]
        log_f = jnp.maximum(a, y) + jnp.log(1.0 + jnp.exp(-jnp.abs(a - y)))
        b = log_f
        for sh in [1 << e for e in range(C.bit_length() - 1)]:
            b = b + jnp.where(row >= sh, pltpu.roll(b, sh, axis=0), 0.0)
        bs_ref[g] = b
        ks_ref[g] = omlb_ref[...] * jax.nn.sigmoid(-z)
        qr = q_ref[g]
        qs_ref[g] = qr * jax.nn.sigmoid(qr)

    t_iota = lax.broadcasted_iota(I32, (C, 1), 0)

    def chunk(c, carry):
        r0 = pl.multiple_of(c * C, C)
        for g in range(G):
            for h in range(N_HEADS):
                hs = slice(h * HEAD_DIM, (h + 1) * HEAD_DIM)
                qc = qs_ref[g, pl.ds(r0, C), hs]
                kc = ks_ref[g, pl.ds(r0, C), hs]
                bc = bs_ref[g, pl.ds(r0, C), hs]
                vc = i_ref[g, pl.ds(r0, C), hs]
                st = st_ref[g, h]
                o = _dot_nt((qc * jnp.exp(bc)).astype(BF16), st.astype(BF16))
                for s in range(C):
                    b_s = bc[s:s + 1, :]
                    k_s = kc[s:s + 1, :]
                    v_s = vc[s:s + 1, :]
                    e = jnp.exp(bc - b_s)
                    col = jnp.sum(qc * k_s * e, axis=-1, keepdims=True)
                    col = jnp.where(t_iota >= s, col, 0.0)
                    o = o + col * v_s
                os_ref[g, pl.ds(r0, C), hs] = o
                b_last = bc[C - 1:C, :]
                kd = kc * jnp.exp(b_last - bc)
                st_ref[g, h] = st * jnp.exp(b_last) + _dot_tn(vc.astype(BF16), kd.astype(BF16))
        return carry

    lax.fori_loop(0, R // C, chunk, 0)

    for g in range(G):
        for h in range(N_HEADS):
            hs = slice(h * HEAD_DIM, (h + 1) * HEAD_DIM)
            o = os_ref[g, :, hs]
            on = (o * _rms_inv(o)) * gn_ref[:, hs]
            gr = g_ref[g, :, hs]
            o_ref[g, :, hs] = (on * (gr * jax.nn.sigmoid(gr))).astype(o_ref.dtype)


def _hgrn(P3, lb, gnorm):
    B, Lp, _ = P3.shape
    R = SEQ_TILE
    W = MIX_WIDTH
    G = _largest_tile(B, HGRN_GROUP)
    lb = lb.reshape(1, W).astype(F32)
    vec = pl.BlockSpec((1, W), lambda b, t: (0, 0))
    c0 = COL_A // W
    return pl.pallas_call(
        _hgrn_body,
        grid=(B // G, Lp // R),
        in_specs=[pl.BlockSpec((G, R, W), functools.partial(lambda b, t, c: (b, t, c), c=c0 + k))
                  for k in range(4)] + [vec, vec, vec, vec],
        out_specs=pl.BlockSpec((G, R, W), lambda b, t: (b, t, 0)),
        out_shape=jax.ShapeDtypeStruct((B, Lp, W), BF16),
        scratch_shapes=[pltpu.VMEM((G, N_HEADS, HEAD_DIM, HEAD_DIM), F32)]
        + [pltpu.VMEM((G, R, W), F32)] * 4,
        compiler_params=_cparams("parallel", "arbitrary"),
        name="hgrn",
    )(P3, P3, P3, P3, jnp.log(lb), jnp.log1p(-lb), 1.0 - lb, gnorm.reshape(1, W).astype(F32))


def _conv_body(b_ref, c_ref, u_ref, w_ref, o_ref, carry_ref):
    @pl.when(pl.program_id(1) == 0)
    def _():
        carry_ref[...] = jnp.zeros_like(carry_ref)

    zc = c_ref[...] * u_ref[...]
    R = zc.shape[0]
    row = lax.broadcasted_iota(I32, zc.shape, 0)
    last = carry_ref[7:8, :]
    last2 = carry_ref[6:7, :]
    z1 = jnp.where(row == 0, last, pltpu.roll(zc, 1, axis=0))
    z2 = jnp.where(row == 0, last2, jnp.where(row == 1, last, pltpu.roll(zc, 2, axis=0)))
    y = w_ref[0:1, :] * zc + w_ref[1:2, :] * z1 + w_ref[2:3, :] * z2
    o_ref[...] = (b_ref[...] * y).astype(o_ref.dtype)
    carry_ref[...] = zc[R - 8:, :]


def _conv(P3, conv_w):
    B, Lp, _ = P3.shape
    R = _largest_multiple(Lp, SEQ_TILE, CONV_ROWS_MAX)
    W = MIX_WIDTH
    c0 = COL_B // W
    w8 = jnp.zeros((8, W), F32).at[:CONV_W].set(conv_w.astype(F32))
    return pl.pallas_call(
        _conv_body,
        grid=(B, Lp // R),
        in_specs=[pl.BlockSpec((None, R, W), functools.partial(lambda b, t, c: (b, t, c), c=c0 + k))
                  for k in range(3)] + [pl.BlockSpec((8, W), lambda b, t: (0, 0))],
        out_specs=pl.BlockSpec((None, R, W), lambda b, t: (b, t, 0)),
        out_shape=jax.ShapeDtypeStruct((B, Lp, W), BF16),
        scratch_shapes=[pltpu.VMEM((8, W), F32)],
        compiler_params=_cparams("parallel", "arbitrary"),
        name="conv",
    )(P3, P3, P3, w8)


def _prep_body(cq_ref, ck_ref, cv_ref, dq_ref, dk_ref, dv_ref, kiw_ref, qi_ref,
               cqg_ref, ckg_ref, dqg_ref, dkg_ref,
               cqn_ref, ckn_ref, cvb_ref, dqn_ref, dkn_ref, dvb_ref, qib_ref, kk_ref, wq_ref):
    lane = lax.broadcasted_iota(I32, (1, HEAD_DIM), 1)
    lo = lane < HALF

    def half_norm(x, g, scale):
        sq = x * x
        ms_lo = jnp.sum(jnp.where(lo, sq, 0.0), axis=-1, keepdims=True) * (1.0 / HALF)
        ms_hi = jnp.sum(jnp.where(lo, 0.0, sq), axis=-1, keepdims=True) * (1.0 / HALF)
        inv = jnp.where(lo, lax.rsqrt(ms_lo + EPS), lax.rsqrt(ms_hi + EPS))
        return ((x * inv) * g) * scale

    for h in range(N_HEADS):
        hs = slice(h * HEAD_DIM, (h + 1) * HEAD_DIM)
        cqn_ref[:, hs] = half_norm(cq_ref[:, hs], cqg_ref[...], HALF ** -0.5).astype(BF16)
        ckn_ref[:, hs] = half_norm(ck_ref[:, hs], ckg_ref[...], 1.0).astype(BF16)
        x = dq_ref[:, hs]
        dqn_ref[:, hs] = (((x * _rms_inv(x)) * dqg_ref[...]) * (HEAD_DIM ** -0.5)).astype(BF16)
    x = dk_ref[...]
    dkn_ref[...] = ((x * _rms_inv(x)) * dkg_ref[...]).astype(BF16)
    for t in range(cv_ref.shape[0] // SEQ_TILE):
        rs = slice(t * SEQ_TILE, (t + 1) * SEQ_TILE)
        for h in range(N_HEADS):
            hs = slice(h * HEAD_DIM, (h + 1) * HEAD_DIM)
            cvb_ref[t, hs, :] = cv_ref[rs, hs].T.astype(BF16)
        dvb_ref[t] = dv_ref[rs, :].T.astype(BF16)
    qib_ref[...] = (qi_ref[...] * (IDX_DIM ** -0.5)).astype(BF16)
    kiw = kiw_ref[...]
    swapped = pltpu.roll(kiw, HALF, axis=1)
    kk_ref[...] = jnp.where(lo, kiw, swapped).astype(BF16)
    wq_ref[...] = swapped * (IDX_HEADS ** -0.5)


def _prep(P, cq_g, ck_g, dq_g, dk_g, tm):
    T = P.shape[0]
    W = MIX_WIDTH
    H = HEAD_DIM

    def col(width, start):
        return pl.BlockSpec((tm, width), functools.partial(lambda i, c: (i, c), c=start // width))

    vec = pl.BlockSpec((1, H), lambda i: (0, 0))
    out = lambda width: pl.BlockSpec((tm, width), lambda i: (i, 0))
    tiles = lambda width: pl.BlockSpec((tm // SEQ_TILE, width, SEQ_TILE), lambda i: (i, 0, 0))
    return pl.pallas_call(
        _prep_body,
        grid=(T // tm,),
        in_specs=[col(W, COL_C), col(W, COL_C + W), col(W, COL_C + 2 * W), col(W, COL_DQ),
                  col(H, COL_DK), col(H, COL_DV), col(H, COL_KIW), col(IDX_HEADS * IDX_DIM, COL_QI),
                  vec, vec, vec, vec],
        out_specs=[out(W), out(W), tiles(W), out(W), out(H), tiles(H), out(IDX_HEADS * IDX_DIM), out(H), out(H)],
        out_shape=[jax.ShapeDtypeStruct((T, W), BF16)] * 2
        + [jax.ShapeDtypeStruct((T // SEQ_TILE, W, SEQ_TILE), BF16), jax.ShapeDtypeStruct((T, W), BF16),
           jax.ShapeDtypeStruct((T, H), BF16), jax.ShapeDtypeStruct((T // SEQ_TILE, H, SEQ_TILE), BF16)]
        + [jax.ShapeDtypeStruct((T, IDX_HEADS * IDX_DIM), BF16),
           jax.ShapeDtypeStruct((T, H), BF16), jax.ShapeDtypeStruct((T, H), F32)],
        compiler_params=_cparams("parallel"),
        name="prep",
    )(P, P, P, P, P, P, P, P,
      jnp.tile(cq_g.astype(F32), 2).reshape(1, H), jnp.tile(ck_g.astype(F32), 2).reshape(1, H),
      dq_g.astype(F32).reshape(1, H), dk_g.astype(F32).reshape(1, H))


SUBLANES = 8
KEY_CHUNK = 8


def _max8(x):
    return jnp.max(x.reshape(x.shape[0] // SUBLANES, SUBLANES, x.shape[1]), axis=0)


def _sum8(x):
    return jnp.sum(x.reshape(x.shape[0] // SUBLANES, SUBLANES, x.shape[1]), axis=0)


def _bit_transpose32(rows):
    a = list(rows)
    j = 16
    m = 0x0000FFFF
    while j:
        k = 0
        while k < 32:
            t = (a[k] ^ lax.shift_right_logical(a[k + j], jnp.int32(j))) & jnp.int32(m)
            a[k] = a[k] ^ t
            a[k + j] = a[k + j] ^ (t << j)
            k = (k + j + 1) & ~j
        j >>= 1
        m ^= (m << j) & 0xFFFFFFFF
    return a


def _for_tiles(n_tiles, body, chunk=KEY_CHUNK):
    n_chunks = n_tiles // chunk

    def chunked(c, carry):
        body(c * chunk, chunk)
        return carry

    lax.fori_loop(0, n_chunks, chunked, 0)
    base = n_chunks * chunk
    size = chunk // 2
    while size >= 1:
        take = (n_tiles - base) // size

        def part(_, carry, base=base, size=size):
            body(base, size)
            return carry

        lax.fori_loop(0, take, part, 0)
        base = base + take * size
        size //= 2


def _diff_body(q_ref, k_ref, vt_ref, near_ref, lam_ref, sub_ref, o_ref,
               q2_ref, s_ref, mx_ref, l_ref, acc_ref, *, out_scale):
    i = pl.program_id(1)
    TQ = SEQ_TILE
    H = HEAD_DIM
    W2 = 2 * TQ
    lane = lax.broadcasted_iota(I32, (1, H), 1)
    for h in range(N_HEADS):
        q = q_ref[:, h * H:(h + 1) * H]
        zero = jnp.zeros_like(q)
        q2_ref[h, 0:TQ, :] = jnp.where(lane < HALF, q, zero)
        q2_ref[h, TQ:, :] = jnp.where(lane < HALF, zero, q)
    mx_ref[...] = jnp.full_like(mx_ref, MASKED)

    def put(j0, n, near_idx):
        k0 = pl.multiple_of(j0 * TQ, TQ)
        parts = []
        for h in range(N_HEADS):
            s = _dot_nt(k_ref[pl.ds(k0, n * TQ), h * H:(h + 1) * H], q2_ref[h])
            if near_idx is None:
                parts.append(s)
            else:
                b = near_ref[h, near_idx]
                parts.append(s + jnp.concatenate([b, b], axis=1))
        s = jnp.concatenate(parts, axis=1)
        for t in range(n):
            s_ref[j0 + t] = s[t * TQ:(t + 1) * TQ, :]
        mx_ref[...] = jnp.maximum(mx_ref[...], _max8(s))

    n_far = jnp.maximum(i - 1, 0)
    _for_tiles(n_far, lambda j0, n: put(j0, n, None))

    def prev(j, carry):
        put(j, 1, 1)
        return carry

    lax.fori_loop(n_far, i, prev, 0)
    put(i, 1, 0)

    m = jnp.max(mx_ref[...], axis=0, keepdims=True)
    l_ref[...] = jnp.zeros_like(l_ref)
    acc_ref[...] = jnp.zeros_like(acc_ref)

    def pv(j0, n):
        p = jnp.exp(s_ref[pl.ds(j0, n)].reshape(n * TQ, N_HEADS * W2) - m)
        l_ref[...] += _sum8(p)
        pb = p.astype(BF16)
        for h in range(N_HEADS):
            acc = acc_ref[:, h * W2:(h + 1) * W2]
            for t in range(n):
                acc = acc + _dot(vt_ref[j0 + t, h * H:(h + 1) * H, :],
                                 pb[t * TQ:(t + 1) * TQ, h * W2:(h + 1) * W2])
            acc_ref[:, h * W2:(h + 1) * W2] = acc

    _for_tiles(i + 1, pv)

    ot = acc_ref[...] * (1.0 / jnp.sum(l_ref[...], axis=0, keepdims=True))
    for h in range(N_HEADS):
        od = ot[:, h * W2:h * W2 + TQ] - lam_ref[...] * ot[:, h * W2 + TQ:(h + 1) * W2]
        inv = lax.rsqrt(jnp.mean(od * od, axis=0, keepdims=True) + EPS)
        o_ref[:, h * H:(h + 1) * H] = (((od * inv) * sub_ref[...]) * out_scale).T.astype(o_ref.dtype)


def _diff(cqn, ckn, cvt, near_t, lam, subln, out_scale):
    B, Lp, W = cqn.shape
    TQ = SEQ_TILE
    H = HEAD_DIM
    nt = Lp // TQ
    return pl.pallas_call(
        functools.partial(_diff_body, out_scale=out_scale),
        grid=(B, nt),
        in_specs=[
            pl.BlockSpec((None, TQ, W), lambda b, i: (b, i, 0)),
            pl.BlockSpec((None, Lp, W), lambda b, i: (b, 0, 0)),
            pl.BlockSpec((None, nt, W, TQ), lambda b, i: (b, 0, 0, 0)),
            pl.BlockSpec((N_HEADS, 2, TQ, TQ), lambda b, i: (0, 0, 0, 0)),
            pl.BlockSpec((1, TQ), lambda b, i: (0, 0)),
            pl.BlockSpec((H, TQ), lambda b, i: (0, 0)),
        ],
        out_specs=pl.BlockSpec((None, TQ, W), lambda b, i: (b, i, 0)),
        out_shape=jax.ShapeDtypeStruct((B, Lp, W), BF16),
        scratch_shapes=[pltpu.VMEM((N_HEADS, 2 * TQ, H), BF16),
                        pltpu.VMEM((nt, TQ, N_HEADS * 2 * TQ), F32),
                        pltpu.VMEM((SUBLANES, N_HEADS * 2 * TQ), F32),
                        pltpu.VMEM((SUBLANES, N_HEADS * 2 * TQ), F32),
                        pltpu.VMEM((H, N_HEADS * 2 * TQ), F32)],
        compiler_params=_cparams("parallel", "arbitrary"),
        name="diff",
    )(cqn, ckn, cvt, near_t, lam, subln)


def _dsa_body(qi_ref, kk_ref, wq_ref, q_ref, k_ref, vt_ref, near_ref, o_ref,
              qi2_ref, w_ref, key_ref, kq_ref, plane_ref, alive_ref, q4_ref, thr_ref, s_ref, mx_ref, l_ref, acc_ref, *, top_k):
    i = pl.program_id(1)
    TQ = SEQ_TILE
    H = HEAD_DIM
    lane = lax.broadcasted_iota(I32, (1, H), 1)

    for p in range(IDX_HEADS // 2):
        x = qi_ref[:, p * H:(p + 1) * H]
        zero = jnp.zeros_like(x)
        qi2_ref[p, 0:TQ, :] = jnp.where(lane < HALF, x, zero)
        qi2_ref[p, TQ:, :] = jnp.where(lane < HALF, zero, x)
    w_ref[...] = wq_ref[...].T
    for h in range(N_HEADS):
        q4_ref[h * TQ:(h + 1) * TQ, :] = q_ref[:, h * H:(h + 1) * H]

    def index(j0, n):
        k0 = pl.multiple_of(j0 * TQ, TQ)
        kt = kk_ref[pl.ds(k0, n * TQ), :]
        acc = jnp.zeros((n * TQ, TQ), F32)
        for p in range(IDX_HEADS // 2):
            s = jnp.maximum(_dot_nt(kt, qi2_ref[p]), 0.0)
            acc = acc + s[:, 0:TQ] * w_ref[2 * p:2 * p + 1, :] + s[:, TQ:] * w_ref[2 * p + 1:2 * p + 2, :]
        acc = acc + 0.0
        bits = pltpu.bitcast(acc, I32)
        key = bits ^ ((bits >> 31) & jnp.int32(0x7FFFFFFF))
        tiles = []
        for t in range(n):
            kt_ = key[t * TQ:(t + 1) * TQ, :]
            if t == n - 1:
                kt_ = jnp.where(kq_ref[...] <= (i - (j0 + t)) * TQ, kt_, jnp.int32(KEY_NEG_INF))
            key_ref[j0 + t] = kt_
            tiles.append(kt_)
        if n == 1:
            tiles.append(None)
        for u in range(len(tiles) // 2):
            rows = []
            for tile in tiles[2 * u:2 * u + 2]:
                for r in range(TQ // SUBLANES):
                    if tile is None:
                        rows.append(jnp.zeros((SUBLANES, TQ), I32))
                    else:
                        rows.append(tile[r * SUBLANES:(r + 1) * SUBLANES, :] ^ jnp.int32(INT_MIN))
            planes = _bit_transpose32(rows)
            pair = j0 // 2 + u
            for b in range(32):
                plane_ref[pair, b] = planes[b]

    @pl.when(i == 0)
    def _():
        plane_ref[...] = jnp.zeros_like(plane_ref)

    kq_ref[...] = lax.broadcasted_iota(I32, (TQ, TQ), 0) - lax.broadcasted_iota(I32, (TQ, TQ), 1)
    _for_tiles(i + 1, index)

    n_pairs = plane_ref.shape[0]
    for tp in range(n_pairs):
        alive_ref[tp] = jnp.full((SUBLANES, TQ), -1, I32)

    def bit_step(t, carry):
        above, thr = carry
        ones = [alive_ref[tp] & plane_ref[tp, t] for tp in range(n_pairs)]
        cnt = lax.population_count(ones[0])
        for tp in range(1, n_pairs):
            cnt = cnt + lax.population_count(ones[tp])
        hit = above + jnp.sum(cnt, axis=0, keepdims=True)
        take = hit >= top_k
        for tp in range(n_pairs):
            alive_ref[tp] = jnp.where(take, ones[tp], alive_ref[tp] ^ ones[tp])
        thr = thr | jnp.where(take, jnp.left_shift(jnp.int32(1), 31 - t), 0)
        return jnp.where(take, above, hit), thr

    zero8 = jnp.zeros((SUBLANES, TQ), I32)
    _, thr_u = lax.fori_loop(0, 32, bit_step, (zero8, zero8))
    thr_ref[...] = thr_u ^ jnp.int32(INT_MIN)

    mx_ref[...] = jnp.full_like(mx_ref, MASKED)

    def put(j0, n, near_idx):
        k0 = pl.multiple_of(j0 * TQ, TQ)
        s = _dot_nt(k_ref[pl.ds(k0, n * TQ), :], q4_ref[...])
        thr1 = thr_ref[0:1, :]
        mx = mx_ref[...]
        for t in range(n):
            sel = key_ref[j0 + t] >= thr1
            parts = []
            for h in range(N_HEADS):
                sh = s[t * TQ:(t + 1) * TQ, h * TQ:(h + 1) * TQ]
                if near_idx is not None:
                    sh = sh + near_ref[h, near_idx]
                parts.append(jnp.where(sel, sh, MASKED))
            row = jnp.concatenate(parts, axis=1)
            s_ref[j0 + t] = row
            mx = jnp.maximum(mx, _max8(row))
        mx_ref[...] = mx

    n_far = jnp.maximum(i - 1, 0)
    _for_tiles(n_far, lambda j0, n: put(j0, n, None))

    def prev(j, carry):
        put(j, 1, 1)
        return carry

    lax.fori_loop(n_far, i, prev, 0)
    put(i, 1, 0)

    m = jnp.max(mx_ref[...], axis=0, keepdims=True)
    l_ref[...] = jnp.zeros_like(l_ref)
    acc_ref[...] = jnp.zeros_like(acc_ref)

    def pv(j0, n):
        p = jnp.exp(s_ref[pl.ds(j0, n)].reshape(n * TQ, N_HEADS * TQ) - m)
        l_ref[...] += _sum8(p)
        pb = p.astype(BF16)
        acc = acc_ref[...]
        for t in range(n):
            acc = acc + _dot(vt_ref[j0 + t], pb[t * TQ:(t + 1) * TQ, :])
        acc_ref[...] = acc

    _for_tiles(i + 1, pv)

    ot = acc_ref[...] * (1.0 / jnp.sum(l_ref[...], axis=0, keepdims=True))
    for h in range(N_HEADS):
        o_ref[:, h * H:(h + 1) * H] = ot[:, h * TQ:(h + 1) * TQ].T.astype(o_ref.dtype)


def _dsa(qib, kk, wq, dqn, dkn, dvt, near_t, top_k):
    B, Lp, W = dqn.shape
    TQ = SEQ_TILE
    H = HEAD_DIM
    NI = IDX_HEADS * IDX_DIM
    nt = Lp // TQ
    seq = lambda width: pl.BlockSpec((None, Lp, width), lambda b, i: (b, 0, 0))
    blk = lambda width: pl.BlockSpec((None, TQ, width), lambda b, i: (b, i, 0))
    return pl.pallas_call(
        functools.partial(_dsa_body, top_k=top_k),
        grid=(B, nt),
        in_specs=[blk(NI), seq(H), blk(H), blk(W), seq(H),
                  pl.BlockSpec((None, nt, H, TQ), lambda b, i: (b, 0, 0, 0)),
                  pl.BlockSpec((N_HEADS, 2, TQ, TQ), lambda b, i: (0, 0, 0, 0))],
        out_specs=blk(W),
        out_shape=jax.ShapeDtypeStruct((B, Lp, W), BF16),
        scratch_shapes=[
            pltpu.VMEM((IDX_HEADS // 2, 2 * TQ, H), BF16),
            pltpu.VMEM((H, TQ), F32),
            pltpu.VMEM((nt, TQ, TQ), I32),
            pltpu.VMEM((TQ, TQ), I32),
            pltpu.VMEM(((nt + 1) // 2, 32, SUBLANES, TQ), I32),
            pltpu.VMEM(((nt + 1) // 2, SUBLANES, TQ), I32),
            pltpu.VMEM((N_HEADS * TQ, H), BF16),
            pltpu.VMEM((SUBLANES, TQ), I32),
            pltpu.VMEM((nt, TQ, N_HEADS * TQ), F32),
            pltpu.VMEM((SUBLANES, N_HEADS * TQ), F32),
            pltpu.VMEM((SUBLANES, N_HEADS * TQ), F32),
            pltpu.VMEM((H, N_HEADS * TQ), F32),
        ],
        compiler_params=_cparams("parallel", "arbitrary"),
        name="dsa",
    )(qib, kk, wq, dqn, dkn, dvt, near_t)


def _merge_body(h_ref, xn_ref, b0_ref, b1_ref, b2_ref, b3_ref, g0_ref, g1_ref, g2_ref, g3_ref,
                wb_ref, wo_ref, o_ref):
    @pl.when(pl.program_id(1) == 0)
    def _():
        o_ref[...] = h_ref[...]

    xn = xn_ref[...]
    merged = None
    for m, (b_ref, g_ref) in enumerate(((b0_ref, g0_ref), (b1_ref, g1_ref), (b2_ref, g2_ref), (b3_ref, g3_ref))):
        term = jax.nn.sigmoid(_dot(xn, g_ref[...])) * _dot(b_ref[...], wb_ref[m])
        merged = term if merged is None else merged + term
    o_ref[...] += _dot(merged.astype(BF16), wo_ref[...])


def _merge(h, xn, branches, w_gate, w_branch, w_out, tm, tn):
    T, D = h.shape
    W = MIX_WIDTH
    nn = D // tn
    row = lambda width: pl.BlockSpec((tm, width), lambda i, n: (i, 0))
    gate = lambda m: pl.BlockSpec((D, tn), functools.partial(lambda i, n, m: (0, m * nn + n), m=m))
    return pl.pallas_call(
        _merge_body,
        grid=(T // tm, nn),
        in_specs=[row(D), row(D), row(W), row(W), row(W), row(W),
                  gate(0), gate(1), gate(2), gate(3),
                  pl.BlockSpec((N_BRANCH, W, tn), lambda i, n: (0, 0, n)),
                  pl.BlockSpec((tn, D), lambda i, n: (n, 0))],
        out_specs=row(D),
        out_shape=jax.ShapeDtypeStruct((T, D), F32),
        compiler_params=_cparams("parallel", "arbitrary"),
        name="merge",
    )(h, xn, *branches, w_gate, w_gate, w_gate, w_gate, w_branch, w_out)


def _rel_bucket(n):
    max_exact = N_BUCKETS // 2
    nf = jnp.maximum(n, 1).astype(F32)
    large = max_exact + (jnp.log(nf / max_exact) / math.log(MAX_DISTANCE / max_exact)
                         * (N_BUCKETS - max_exact)).astype(I32)
    large = jnp.minimum(large, N_BUCKETS - 1)
    return jnp.where(n < max_exact, n, large)


def _bias_tiles(table):
    TQ = SEQ_TILE
    qi = jnp.arange(TQ, dtype=I32)[:, None]
    ki = jnp.arange(TQ, dtype=I32)[None, :]
    d0 = qi - ki
    tbl = table.astype(F32).T

    def lookup(bucket):
        out = jnp.zeros((tbl.shape[0],) + bucket.shape, F32)
        for b in range(N_BUCKETS):
            out = jnp.where(bucket[None] == b, tbl[:, b][:, None, None], out)
        return out

    far = tbl[:, N_BUCKETS - 1][:, None, None]
    diag = jnp.where(d0 >= 0, lookup(_rel_bucket(jnp.maximum(d0, 0))) - far, MASKED)
    prev = lookup(_rel_bucket(d0 + TQ)) - far
    return jnp.stack([diag, prev], axis=1).transpose(0, 1, 3, 2)


def _pack_w_in(w):
    offs = np.cumsum([0, 512, 512, 512, 512, 512, 512, 512, 512, 512, 512, 512, 128, 128,
                      IDX_HEADS * IDX_DIM, IDX_DIM, IDX_HEADS]).tolist()
    seg = lambda a, b: w[:, offs[a]:offs[b]]
    D = w.shape[0]
    used = COL_KIW + IDX_DIM + IDX_HEADS
    main = jnp.concatenate([
        seg(0, 4),
        seg(13, 14),
        seg(4, 7),
        seg(7, 10),
        seg(10, 11),
        seg(11, 12),
        seg(12, 13),
        seg(14, 16),
        jnp.zeros((D, P_COLS - used), w.dtype),
    ], axis=1).astype(BF16)
    gates = w[:, offs[16]:].astype(BF16)
    return main, gates


def _largest_tile(n, cap):
    t = cap
    while n % t:
        t //= 2
    return t


def _largest_multiple(n, unit, cap):
    return max(t for t in range(unit, cap + 1, unit) if n % t == 0)


def kernel(x, meta_tokens, rel_bias, ffn1_norm, ffn1_w_gu, ffn1_w_down, mix_norm, w_in, hgrn_lb, hgrn_gnorm, conv_w, diff_q_norm, diff_k_norm, diff_lambda, diff_subln, dsa_q_norm, dsa_k_norm, w_branch, w_out, ffn2_norm, ffn2_w_gu, ffn2_w_down):
    B, S, D = x.shape
    depth = w_in.shape[0]
    L = S + N_META
    Lp = -(-L // SEQ_TILE) * SEQ_TILE
    T = B * Lp
    top_k = min(TOPK_MAX, S // 4)
    tm = _largest_tile(T, 512)
    tm_wide = _largest_tile(T, 1024)
    tm_ffn = _largest_multiple(T, SEQ_TILE, FFN_ROWS_MAX)
    d_ff = ffn1_w_down.shape[1]
    tf = _largest_tile(d_ff, 512)

    meta = jnp.broadcast_to(meta_tokens.astype(x.dtype)[None], (B, N_META, D))
    h = jnp.concatenate([meta, x, jnp.zeros((B, Lp - L, D), x.dtype)], axis=1).reshape(T, D)

    lbs = jnp.cumsum(jax.nn.softmax(hgrn_lb.astype(F32), axis=0), axis=0)
    lbs = lbs - lbs[0:1]
    near_c = _bias_tiles(rel_bias[:, :N_HEADS])
    near_d = _bias_tiles(rel_bias[:, N_HEADS:])

    for l in range(depth):
        h = _ffn(h, ffn1_norm[l], ffn1_w_gu[l].astype(BF16), ffn1_w_down[l].astype(BF16), tm_ffn, tf)

        w_main, w_gate = _pack_w_in(w_in[l])
        xn, P = _proj(h, mix_norm[l], w_main, tm_wide, 1024)
        P3 = P.reshape(B, Lp, P_COLS)
        br_a = _hgrn(P3, lbs[l], hgrn_gnorm[l])
        br_b = _conv(P3, conv_w[l])
        cqn, ckn, cvt, dqn, dkn, dvt, qib, kk, wq = _prep(
            P, diff_q_norm[l], diff_k_norm[l], dsa_q_norm[l], dsa_k_norm[l], tm)
        r3 = lambda a: a.reshape(B, Lp, a.shape[-1])
        lp = diff_lambda[l].astype(F32)
        lam_init = 0.8 - 0.6 * math.exp(-0.3 * l)
        lam = jnp.exp(jnp.sum(lp[0] * lp[1])) - jnp.exp(jnp.sum(lp[2] * lp[3])) + lam_init
        tiles = lambda a: a.reshape(B, Lp // SEQ_TILE, a.shape[-2], SEQ_TILE)
        br_c = _diff(r3(cqn), r3(ckn), tiles(cvt), near_c,
                     jnp.broadcast_to(lam, (1, SEQ_TILE)).astype(F32),
                     jnp.broadcast_to(diff_subln[l].astype(F32)[:, None], (HEAD_DIM, SEQ_TILE)),
                     1.0 - lam_init)
        br_d = _dsa(r3(qib), r3(kk), r3(wq), r3(dqn), r3(dkn), tiles(dvt), near_d, top_k)
        branches = [a.reshape(T, MIX_WIDTH) for a in (br_a, br_b, br_c, br_d)]
        h = _merge(h, xn, branches, w_gate, w_branch[l].astype(BF16), w_out[l].astype(BF16), tm, 512)

        last = l == depth - 1
        h = _ffn(h, ffn2_norm[l], ffn2_w_gu[l].astype(BF16), ffn2_w_down[l].astype(BF16),
                 _largest_multiple(S, SUBLANES, FFN_ROWS_MAX) if last else tm_ffn, tf,
                 keep=(B, Lp, N_META, S) if last else None)

    return h.reshape(B, S, D)
```

```python
import functools
import math

import jax
import jax.numpy as jnp
import numpy as np
from jax import lax
from jax.experimental import pallas as pl
from jax.experimental.pallas import tpu as pltpu

F32 = jnp.float32
BF16 = jnp.bfloat16
I32 = jnp.int32

EPS = 1e-6
N_META = 16
HEAD_DIM = 128
HALF = 64
N_HEADS = 4
MIX_WIDTH = N_HEADS * HEAD_DIM
IDX_HEADS = 16
IDX_DIM = 64
TOPK_MAX = 256
N_BRANCH = 4
N_BUCKETS = 32
MAX_DISTANCE = 128
CONV_W = 3
SEQ_TILE = 128
HGRN_CHUNK = 8
HGRN_GROUP = 8
CONV_ROWS_MAX = 1408
FFN_ROWS_MAX = 768
MASKED = -1e30
LOG2E = math.log2(math.e)
INT_MIN = -(2 ** 31)
KEY_NEG_INF = -2139095041
VMEM_LIMIT_BYTES = 56 * 1024 * 1024

COL_A = 0
COL_QI = 2048
COL_B = 3072
COL_C = 4608
COL_DQ = 6144
COL_DK = 6656
COL_DV = 6784
COL_KIW = 6912
P_COLS = 7168


def _cparams(*sem):
    return pltpu.CompilerParams(dimension_semantics=sem, vmem_limit_bytes=VMEM_LIMIT_BYTES)


def _dot(a, b):
    return jnp.dot(a, b, preferred_element_type=F32)


def _dot_nt(a, b):
    return lax.dot_general(a, b, (((1,), (1,)), ((), ())), preferred_element_type=F32)


def _dot_tn(a, b):
    return lax.dot_general(a, b, (((0,), (0,)), ((), ())), preferred_element_type=F32)


def _rms_inv(x):
    return lax.rsqrt(jnp.mean(x * x, axis=-1, keepdims=True) + EPS)


def _ffn_body(h_ref, g_ref, wg_ref, wu_ref, wd_ref, o_ref, xn_ref):
    j = pl.program_id(1)

    @pl.when(j == 0)
    def _():
        x = h_ref[...]
        xn_ref[...] = ((x * _rms_inv(x)) * g_ref[...]).astype(BF16)
        o_ref[...] = x

    xn = xn_ref[...]
    g = _dot(xn, wg_ref[...])
    u = _dot(xn, wu_ref[...])
    a = (g * jax.nn.sigmoid(g) * (0.5 * u)).astype(BF16)
    o_ref[...] += _dot(a, wd_ref[...])


def _ffn(h, gain, w_gu, w_down, tm, tf, keep=None):
    T, D = h.shape
    F = w_down.shape[0]
    nf = F // tf
    if keep is None:
        n_row_tiles = T // tm
        row_in = pl.BlockSpec((tm, D), lambda i, j: (i, 0))
        t_out = T
    else:
        B, Lp, first, n_rows = keep
        per_seq = n_rows // tm
        n_row_tiles = B * per_seq
        row_in = pl.BlockSpec((pl.Element(tm), pl.Element(D)),
                              lambda i, j: (pl.multiple_of(
                                  (i // per_seq) * Lp + first + (i % per_seq) * tm, SUBLANES), 0))
        t_out = B * n_rows
    return pl.pallas_call(
        _ffn_body,
        grid=(n_row_tiles, nf),
        in_specs=[
            row_in,
            pl.BlockSpec((1, D), lambda i, j: (0, 0)),
            pl.BlockSpec((D, tf), lambda i, j: (0, j)),
            pl.BlockSpec((D, tf), lambda i, j: (0, j + nf)),
            pl.BlockSpec((tf, D), lambda i, j: (j, 0)),
        ],
        out_specs=pl.BlockSpec((tm, D), lambda i, j: (i, 0)),
        out_shape=jax.ShapeDtypeStruct((t_out, D), F32),
        scratch_shapes=[pltpu.VMEM((tm, D), BF16)],
        compiler_params=_cparams("parallel", "arbitrary"),
        name="ffn",
    )(h, gain.reshape(1, D), w_gu, w_gu, w_down)


def _proj_body(h_ref, g_ref, w_ref, xn_ref, p_ref):
    @pl.when(pl.program_id(1) == 0)
    def _():
        x = h_ref[...]
        xn_ref[...] = ((x * _rms_inv(x)) * g_ref[...]).astype(BF16)

    p_ref[...] = _dot(xn_ref[...], w_ref[...])


def _proj(h, gain, w, tm, tn):
    T, D = h.shape
    N = w.shape[1]
    return pl.pallas_call(
        _proj_body,
        grid=(T // tm, N // tn),
        in_specs=[
            pl.BlockSpec((tm, D), lambda i, j: (i, 0)),
            pl.BlockSpec((1, D), lambda i, j: (0, 0)),
            pl.BlockSpec((D, tn), lambda i, j: (0, j)),
        ],
        out_specs=[
            pl.BlockSpec((tm, D), lambda i, j: (i, 0)),
            pl.BlockSpec((tm, tn), lambda i, j: (i, j)),
        ],
        out_shape=[jax.ShapeDtypeStruct((T, D), BF16), jax.ShapeDtypeStruct((T, N), F32)],
        compiler_params=_cparams("parallel", "arbitrary"),
        name="proj",
    )(h, gain.reshape(1, D), w)


def _hgrn_body(q_ref, f_ref, i_ref, g_ref, loglb_ref, log1mlb_ref, omlb_ref, gn_ref, o_ref,
               st_ref, qs_ref, ks_ref, bs_ref, os_ref):
    G, R, _ = q_ref.shape
    C = HGRN_CHUNK

    @pl.when(pl.program_id(1) == 0)
    def _():
        st_ref[...] = jnp.zeros_like(st_ref)

    row = lax.broadcasted_iota(I32, (R, MIX_WIDTH), 0) % C
    for g in range(G):
        z = f_ref[g]
        log_sig = jnp.minimum(z, 0.0) - jnp.log(1.0 + jnp.exp(-jnp.abs(z)))
        y = log1mlb_ref[...] + log_sig
        a = loglb_ref[...]
        log_f = jnp.maximum(a, y) + jnp.log(1.0 + jnp.exp(-jnp.abs(a - y)))
        b = log_f
        for sh in [1 << e for e in range(C.bit_length() - 1)]:
            b = b + jnp.where(row >= sh, pltpu.roll(b, sh, axis=0), 0.0)
        bs_ref[g] = b
        ks_ref[g] = omlb_ref[...] * jax.nn.sigmoid(-z)
        qr = q_ref[g]
        qs_ref[g] = qr * jax.nn.sigmoid(qr)

    t_iota = lax.broadcasted_iota(I32, (C, 1), 0)

    def chunk(c, carry):
        r0 = pl.multiple_of(c * C, C)
        for g in range(G):
            for h in range(N_HEADS):
                hs = slice(h * HEAD_DIM, (h + 1) * HEAD_DIM)
                qc = qs_ref[g, pl.ds(r0, C), hs]
                kc = ks_ref[g, pl.ds(r0, C), hs]
                bc = bs_ref[g, pl.ds(r0, C), hs]
                vc = i_ref[g, pl.ds(r0, C), hs]
                st = st_ref[g, h]
                o = _dot_nt((qc * jnp.exp(bc)).astype(BF16), st.astype(BF16))
                for s in range(C):
                    b_s = bc[s:s + 1, :]
                    k_s = kc[s:s + 1, :]
                    v_s = vc[s:s + 1, :]
                    e = jnp.exp(bc - b_s)
                    col = jnp.sum(qc * k_s * e, axis=-1, keepdims=True)
                    col = jnp.where(t_iota >= s, col, 0.0)
                    o = o + col * v_s
                os_ref[g, pl.ds(r0, C), hs] = o
                b_last = bc[C - 1:C, :]
                kd = kc * jnp.exp(b_last - bc)
                st_ref[g, h] = st * jnp.exp(b_last) + _dot_tn(vc.astype(BF16), kd.astype(BF16))
        return carry

    lax.fori_loop(0, R // C, chunk, 0)

    for g in range(G):
        for h in range(N_HEADS):
            hs = slice(h * HEAD_DIM, (h + 1) * HEAD_DIM)
            o = os_ref[g, :, hs]
            on = (o * _rms_inv(o)) * gn_ref[:, hs]
            gr = g_ref[g, :, hs]
            o_ref[g, :, hs] = (on * (gr * jax.nn.sigmoid(gr))).astype(o_ref.dtype)


def _hgrn(P3, lb, gnorm):
    B, Lp, _ = P3.shape
    R = SEQ_TILE
    W = MIX_WIDTH
    G = _largest_tile(B, HGRN_GROUP)
    lb = lb.reshape(1, W).astype(F32)
    vec = pl.BlockSpec((1, W), lambda b, t: (0, 0))
    c0 = COL_A // W
    return pl.pallas_call(
        _hgrn_body,
        grid=(B // G, Lp // R),
        in_specs=[pl.BlockSpec((G, R, W), functools.partial(lambda b, t, c: (b, t, c), c=c0 + k))
                  for k in range(4)] + [vec, vec, vec, vec],
        out_specs=pl.BlockSpec((G, R, W), lambda b, t: (b, t, 0)),
        out_shape=jax.ShapeDtypeStruct((B, Lp, W), BF16),
        scratch_shapes=[pltpu.VMEM((G, N_HEADS, HEAD_DIM, HEAD_DIM), F32)]
        + [pltpu.VMEM((G, R, W), F32)] * 4,
        compiler_params=_cparams("parallel", "arbitrary"),
        name="hgrn",
    )(P3, P3, P3, P3, jnp.log(lb), jnp.log1p(-lb), 1.0 - lb, gnorm.reshape(1, W).astype(F32))


def _conv_body(b_ref, c_ref, u_ref, w_ref, o_ref, carry_ref):
    @pl.when(pl.program_id(1) == 0)
    def _():
        carry_ref[...] = jnp.zeros_like(carry_ref)

    zc = c_ref[...] * u_ref[...]
    R = zc.shape[0]
    row = lax.broadcasted_iota(I32, zc.shape, 0)
    last = carry_ref[7:8, :]
    last2 = carry_ref[6:7, :]
    z1 = jnp.where(row == 0, last, pltpu.roll(zc, 1, axis=0))
    z2 = jnp.where(row == 0, last2, jnp.where(row == 1, last, pltpu.roll(zc, 2, axis=0)))
    y = w_ref[0:1, :] * zc + w_ref[1:2, :] * z1 + w_ref[2:3, :] * z2
    o_ref[...] = (b_ref[...] * y).astype(o_ref.dtype)
    carry_ref[...] = zc[R - 8:, :]


def _conv(P3, conv_w):
    B, Lp, _ = P3.shape
    R = _largest_multiple(Lp, SEQ_TILE, CONV_ROWS_MAX)
    W = MIX_WIDTH
    c0 = COL_B // W
    w8 = jnp.zeros((8, W), F32).at[:CONV_W].set(conv_w.astype(F32))
    return pl.pallas_call(
        _conv_body,
        grid=(B, Lp // R),
        in_specs=[pl.BlockSpec((None, R, W), functools.partial(lambda b, t, c: (b, t, c), c=c0 + k))
                  for k in range(3)] + [pl.BlockSpec((8, W), lambda b, t: (0, 0))],
        out_specs=pl.BlockSpec((None, R, W), lambda b, t: (b, t, 0)),
        out_shape=jax.ShapeDtypeStruct((B, Lp, W), BF16),
        scratch_shapes=[pltpu.VMEM((8, W), F32)],
        compiler_params=_cparams("parallel", "arbitrary"),
        name="conv",
    )(P3, P3, P3, w8)


def _prep_body(cq_ref, ck_ref, cv_ref, dq_ref, dk_ref, dv_ref, kiw_ref, qi_ref,
               cqg_ref, ckg_ref, dqg_ref, dkg_ref,
               cqn_ref, ckn_ref, cvb_ref, dqn_ref, dkn_ref, dvb_ref, qib_ref, kk_ref, wq_ref):
    lane = lax.broadcasted_iota(I32, (1, HEAD_DIM), 1)
    lo = lane < HALF

    def half_norm(x, g, scale):
        sq = x * x
        ms_lo = jnp.sum(jnp.where(lo, sq, 0.0), axis=-1, keepdims=True) * (1.0 / HALF)
        ms_hi = jnp.sum(jnp.where(lo, 0.0, sq), axis=-1, keepdims=True) * (1.0 / HALF)
        inv = jnp.where(lo, lax.rsqrt(ms_lo + EPS), lax.rsqrt(ms_hi + EPS))
        return ((x * inv) * g) * scale

    for h in range(N_HEADS):
        hs = slice(h * HEAD_DIM, (h + 1) * HEAD_DIM)
        cqn_ref[:, hs] = half_norm(cq_ref[:, hs], cqg_ref[...], HALF ** -0.5 * LOG2E).astype(BF16)
        ckn_ref[:, hs] = half_norm(ck_ref[:, hs], ckg_ref[...], 1.0).astype(BF16)
        x = dq_ref[:, hs]
        dqn_ref[:, hs] = (((x * _rms_inv(x)) * dqg_ref[...]) * (HEAD_DIM ** -0.5 * LOG2E)).astype(BF16)
    x = dk_ref[...]
    dkn_ref[...] = ((x * _rms_inv(x)) * dkg_ref[...]).astype(BF16)
    for t in range(cv_ref.shape[0] // SEQ_TILE):
        rs = slice(t * SEQ_TILE, (t + 1) * SEQ_TILE)
        for h in range(N_HEADS):
            hs = slice(h * HEAD_DIM, (h + 1) * HEAD_DIM)
            cvb_ref[t, hs, :] = cv_ref[rs, hs].T.astype(BF16)
        dvb_ref[t] = dv_ref[rs, :].T.astype(BF16)
    qib_ref[...] = (qi_ref[...] * (IDX_DIM ** -0.5)).astype(BF16)
    kiw = kiw_ref[...]
    swapped = pltpu.roll(kiw, HALF, axis=1)
    kk_ref[...] = jnp.where(lo, kiw, swapped).astype(BF16)
    wq_ref[...] = swapped * (IDX_HEADS ** -0.5)


def _prep(P, cq_g, ck_g, dq_g, dk_g, tm):
    T = P.shape[0]
    W = MIX_WIDTH
    H = HEAD_DIM

    def col(width, start):
        return pl.BlockSpec((tm, width), functools.partial(lambda i, c: (i, c), c=start // width))

    vec = pl.BlockSpec((1, H), lambda i: (0, 0))
    out = lambda width: pl.BlockSpec((tm, width), lambda i: (i, 0))
    tiles = lambda width: pl.BlockSpec((tm // SEQ_TILE, width, SEQ_TILE), lambda i: (i, 0, 0))
    return pl.pallas_call(
        _prep_body,
        grid=(T // tm,),
        in_specs=[col(W, COL_C), col(W, COL_C + W), col(W, COL_C + 2 * W), col(W, COL_DQ),
                  col(H, COL_DK), col(H, COL_DV), col(H, COL_KIW), col(IDX_HEADS * IDX_DIM, COL_QI),
                  vec, vec, vec, vec],
        out_specs=[out(W), out(W), tiles(W), out(W), out(H), tiles(H), out(IDX_HEADS * IDX_DIM), out(H), out(H)],
        out_shape=[jax.ShapeDtypeStruct((T, W), BF16)] * 2
        + [jax.ShapeDtypeStruct((T // SEQ_TILE, W, SEQ_TILE), BF16), jax.ShapeDtypeStruct((T, W), BF16),
           jax.ShapeDtypeStruct((T, H), BF16), jax.ShapeDtypeStruct((T // SEQ_TILE, H, SEQ_TILE), BF16)]
        + [jax.ShapeDtypeStruct((T, IDX_HEADS * IDX_DIM), BF16),
           jax.ShapeDtypeStruct((T, H), BF16), jax.ShapeDtypeStruct((T, H), F32)],
        compiler_params=_cparams("parallel"),
        name="prep",
    )(P, P, P, P, P, P, P, P,
      jnp.tile(cq_g.astype(F32), 2).reshape(1, H), jnp.tile(ck_g.astype(F32), 2).reshape(1, H),
      dq_g.astype(F32).reshape(1, H), dk_g.astype(F32).reshape(1, H))


SUBLANES = 8
KEY_CHUNK = 8


def _max8(x):
    return jnp.max(x.reshape(x.shape[0] // SUBLANES, SUBLANES, x.shape[1]), axis=0)


def _sum8(x):
    return jnp.sum(x.reshape(x.shape[0] // SUBLANES, SUBLANES, x.shape[1]), axis=0)


def _bit_transpose32(rows):
    a = list(rows)
    j = 16
    m = 0x0000FFFF
    while j:
        k = 0
        while k < 32:
            t = (a[k] ^ lax.shift_right_logical(a[k + j], jnp.int32(j))) & jnp.int32(m)
            a[k] = a[k] ^ t
            a[k + j] = a[k + j] ^ (t << j)
            k = (k + j + 1) & ~j
        j >>= 1
        m ^= (m << j) & 0xFFFFFFFF
    return a


def _for_tiles(n_tiles, body, chunk=KEY_CHUNK):
    n_chunks = n_tiles // chunk

    def chunked(c, carry):
        body(c * chunk, chunk)
        return carry

    lax.fori_loop(0, n_chunks, chunked, 0)
    base = n_chunks * chunk
    size = chunk // 2
    while size >= 1:
        take = (n_tiles - base) // size

        def part(_, carry, base=base, size=size):
            body(base, size)
            return carry

        lax.fori_loop(0, take, part, 0)
        base = base + take * size
        size //= 2


def _diff_body(q_ref, k_ref, vt_ref, near_ref, lam_ref, sub_ref, o_ref,
               q2_ref, s_ref, mx_ref, l_ref, acc_ref, *, out_scale):
    i = pl.program_id(1)
    TQ = SEQ_TILE
    H = HEAD_DIM
    W2 = 2 * TQ
    lane = lax.broadcasted_iota(I32, (1, H), 1)
    for h in range(N_HEADS):
        q = q_ref[:, h * H:(h + 1) * H]
        zero = jnp.zeros_like(q)
        q2_ref[h, 0:TQ, :] = jnp.where(lane < HALF, q, zero)
        q2_ref[h, TQ:, :] = jnp.where(lane < HALF, zero, q)
    mx_ref[...] = jnp.full_like(mx_ref, MASKED)

    def put(j0, n, near_idx):
        k0 = pl.multiple_of(j0 * TQ, TQ)
        parts = []
        for h in range(N_HEADS):
            s = _dot_nt(k_ref[pl.ds(k0, n * TQ), h * H:(h + 1) * H], q2_ref[h])
            if near_idx is None:
                parts.append(s)
            else:
                b = near_ref[h, near_idx]
                parts.append(s + jnp.concatenate([b, b], axis=1))
        s = jnp.concatenate(parts, axis=1)
        for t in range(n):
            s_ref[j0 + t] = s[t * TQ:(t + 1) * TQ, :]
        mx_ref[...] = jnp.maximum(mx_ref[...], _max8(s))

    n_far = jnp.maximum(i - 1, 0)
    _for_tiles(n_far, lambda j0, n: put(j0, n, None))

    def prev(j, carry):
        put(j, 1, 1)
        return carry

    lax.fori_loop(n_far, i, prev, 0)
    put(i, 1, 0)

    m = jnp.max(mx_ref[...], axis=0, keepdims=True)
    l_ref[...] = jnp.zeros_like(l_ref)
    acc_ref[...] = jnp.zeros_like(acc_ref)

    def pv(j0, n):
        p = jnp.exp2(s_ref[pl.ds(j0, n)].reshape(n * TQ, N_HEADS * W2) - m)
        l_ref[...] += _sum8(p)
        pb = p.astype(BF16)
        for h in range(N_HEADS):
            acc = acc_ref[:, h * W2:(h + 1) * W2]
            for t in range(n):
                acc = acc + _dot(vt_ref[j0 + t, h * H:(h + 1) * H, :],
                                 pb[t * TQ:(t + 1) * TQ, h * W2:(h + 1) * W2])
            acc_ref[:, h * W2:(h + 1) * W2] = acc

    _for_tiles(i + 1, pv)

    ot = acc_ref[...] * (1.0 / jnp.sum(l_ref[...], axis=0, keepdims=True))
    for h in range(N_HEADS):
        od = ot[:, h * W2:h * W2 + TQ] - lam_ref[...] * ot[:, h * W2 + TQ:(h + 1) * W2]
        inv = lax.rsqrt(jnp.mean(od * od, axis=0, keepdims=True) + EPS)
        o_ref[:, h * H:(h + 1) * H] = (((od * inv) * sub_ref[...]) * out_scale).T.astype(o_ref.dtype)


def _diff(cqn, ckn, cvt, near_t, lam, subln, out_scale):
    B, Lp, W = cqn.shape
    TQ = SEQ_TILE
    H = HEAD_DIM
    nt = Lp // TQ
    return pl.pallas_call(
        functools.partial(_diff_body, out_scale=out_scale),
        grid=(B, nt),
        in_specs=[
            pl.BlockSpec((None, TQ, W), lambda b, i: (b, i, 0)),
            pl.BlockSpec((None, Lp, W), lambda b, i: (b, 0, 0)),
            pl.BlockSpec((None, nt, W, TQ), lambda b, i: (b, 0, 0, 0)),
            pl.BlockSpec((N_HEADS, 2, TQ, TQ), lambda b, i: (0, 0, 0, 0)),
            pl.BlockSpec((1, TQ), lambda b, i: (0, 0)),
            pl.BlockSpec((H, TQ), lambda b, i: (0, 0)),
        ],
        out_specs=pl.BlockSpec((None, TQ, W), lambda b, i: (b, i, 0)),
        out_shape=jax.ShapeDtypeStruct((B, Lp, W), BF16),
        scratch_shapes=[pltpu.VMEM((N_HEADS, 2 * TQ, H), BF16),
                        pltpu.VMEM((nt, TQ, N_HEADS * 2 * TQ), F32),
                        pltpu.VMEM((SUBLANES, N_HEADS * 2 * TQ), F32),
                        pltpu.VMEM((SUBLANES, N_HEADS * 2 * TQ), F32),
                        pltpu.VMEM((H, N_HEADS * 2 * TQ), F32)],
        compiler_params=_cparams("parallel", "arbitrary"),
        name="diff",
    )(cqn, ckn, cvt, near_t, lam, subln)


def _dsa_body(qi_ref, kk_ref, wq_ref, q_ref, k_ref, vt_ref, near_ref, o_ref,
              qi2_ref, w_ref, key_ref, kq_ref, plane_ref, alive_ref, q4_ref, thr_ref, s_ref, mx_ref, l_ref, acc_ref, *, top_k):
    i = pl.program_id(1)
    TQ = SEQ_TILE
    H = HEAD_DIM
    lane = lax.broadcasted_iota(I32, (1, H), 1)

    for p in range(IDX_HEADS // 2):
        x = qi_ref[:, p * H:(p + 1) * H]
        zero = jnp.zeros_like(x)
        qi2_ref[p, 0:TQ, :] = jnp.where(lane < HALF, x, zero)
        qi2_ref[p, TQ:, :] = jnp.where(lane < HALF, zero, x)
    w_ref[...] = wq_ref[...].T
    for h in range(N_HEADS):
        q4_ref[h * TQ:(h + 1) * TQ, :] = q_ref[:, h * H:(h + 1) * H]

    def index(j0, n):
        k0 = pl.multiple_of(j0 * TQ, TQ)
        kt = kk_ref[pl.ds(k0, n * TQ), :]
        acc = jnp.zeros((n * TQ, TQ), F32)
        for p in range(IDX_HEADS // 2):
            s = jnp.maximum(_dot_nt(kt, qi2_ref[p]), 0.0)
            acc = acc + s[:, 0:TQ] * w_ref[2 * p:2 * p + 1, :] + s[:, TQ:] * w_ref[2 * p + 1:2 * p + 2, :]
        bits = pltpu.bitcast(acc, I32)
        key = bits ^ ((bits >> 31) & jnp.int32(0x7FFFFFFF))
        tiles = []
        for t in range(n):
            kt_ = key[t * TQ:(t + 1) * TQ, :]
            if t == n - 1:
                kt_ = jnp.where(kq_ref[...] <= (i - (j0 + t)) * TQ, kt_, jnp.int32(KEY_NEG_INF))
            key_ref[j0 + t] = kt_
            tiles.append(kt_)
        if n == 1:
            tiles.append(None)
        for u in range(len(tiles) // 2):
            rows = []
            for tile in tiles[2 * u:2 * u + 2]:
                for r in range(TQ // SUBLANES):
                    if tile is None:
                        rows.append(jnp.full((SUBLANES, TQ), INT_MIN, I32))
                    else:
                        rows.append(tile[r * SUBLANES:(r + 1) * SUBLANES, :])
            planes = _bit_transpose32(rows)
            planes[0] = ~planes[0]
            pair = j0 // 2 + u
            for b in range(32):
                plane_ref[pair, b] = planes[b]

    @pl.when(i == 0)
    def _():
        plane_ref[...] = jnp.zeros_like(plane_ref)

    kq_ref[...] = lax.broadcasted_iota(I32, (TQ, TQ), 0) - lax.broadcasted_iota(I32, (TQ, TQ), 1)
    _for_tiles(i + 1, index)

    n_pairs = plane_ref.shape[0]
    for tp in range(n_pairs):
        alive_ref[tp] = jnp.full((SUBLANES, TQ), -1, I32)

    def bit_step(t, carry):
        above, thr = carry
        ones = [alive_ref[tp] & plane_ref[tp, t] for tp in range(n_pairs)]
        cnt = lax.population_count(ones[0])
        for tp in range(1, n_pairs):
            cnt = cnt + lax.population_count(ones[tp])
        hit = above + jnp.sum(cnt, axis=0, keepdims=True)
        take = hit >= top_k
        for tp in range(n_pairs):
            alive_ref[tp] = jnp.where(take, ones[tp], alive_ref[tp] ^ ones[tp])
        thr = thr | jnp.where(take, jnp.left_shift(jnp.int32(1), 31 - t), 0)
        return jnp.where(take, above, hit), thr

    zero8 = jnp.zeros((SUBLANES, TQ), I32)
    _, thr_u = lax.fori_loop(0, 32, bit_step, (zero8, zero8))
    thr_ref[...] = thr_u ^ jnp.int32(INT_MIN)

    mx_ref[...] = jnp.full_like(mx_ref, MASKED)

    def put(j0, n, near_idx):
        k0 = pl.multiple_of(j0 * TQ, TQ)
        s = _dot_nt(k_ref[pl.ds(k0, n * TQ), :], q4_ref[...])
        thr1 = thr_ref[0:1, :]
        mx = mx_ref[...]
        for t in range(n):
            sel = key_ref[j0 + t] >= thr1
            parts = []
            for h in range(N_HEADS):
                sh = s[t * TQ:(t + 1) * TQ, h * TQ:(h + 1) * TQ]
                if near_idx is not None:
                    sh = sh + near_ref[h, near_idx]
                parts.append(jnp.where(sel, sh, MASKED))
            row = jnp.concatenate(parts, axis=1)
            s_ref[j0 + t] = row
            mx = jnp.maximum(mx, _max8(row))
        mx_ref[...] = mx

    n_far = jnp.maximum(i - 1, 0)
    _for_tiles(n_far, lambda j0, n: put(j0, n, None))

    def prev(j, carry):
        put(j, 1, 1)
        return carry

    lax.fori_loop(n_far, i, prev, 0)
    put(i, 1, 0)

    m = jnp.max(mx_ref[...], axis=0, keepdims=True)
    l_ref[...] = jnp.zeros_like(l_ref)
    acc_ref[...] = jnp.zeros_like(acc_ref)

    def pv(j0, n):
        p = jnp.exp2(s_ref[pl.ds(j0, n)].reshape(n * TQ, N_HEADS * TQ) - m)
        l_ref[...] += _sum8(p)
        pb = p.astype(BF16)
        acc = acc_ref[...]
        for t in range(n):
            acc = acc + _dot(vt_ref[j0 + t], pb[t * TQ:(t + 1) * TQ, :])
        acc_ref[...] = acc

    _for_tiles(i + 1, pv)

    ot = acc_ref[...] * (1.0 / jnp.sum(l_ref[...], axis=0, keepdims=True))
    for h in range(N_HEADS):
        o_ref[:, h * H:(h + 1) * H] = ot[:, h * TQ:(h + 1) * TQ].T.astype(o_ref.dtype)


def _dsa(qib, kk, wq, dqn, dkn, dvt, near_t, top_k):
    B, Lp, W = dqn.shape
    TQ = SEQ_TILE
    H = HEAD_DIM
    NI = IDX_HEADS * IDX_DIM
    nt = Lp // TQ
    seq = lambda width: pl.BlockSpec((None, Lp, width), lambda b, i: (b, 0, 0))
    blk = lambda width: pl.BlockSpec((None, TQ, width), lambda b, i: (b, i, 0))
    return pl.pallas_call(
        functools.partial(_dsa_body, top_k=top_k),
        grid=(B, nt),
        in_specs=[blk(NI), seq(H), blk(H), blk(W), seq(H),
                  pl.BlockSpec((None, nt, H, TQ), lambda b, i: (b, 0, 0, 0)),
                  pl.BlockSpec((N_HEADS, 2, TQ, TQ), lambda b, i: (0, 0, 0, 0))],
        out_specs=blk(W),
        out_shape=jax.ShapeDtypeStruct((B, Lp, W), BF16),
        scratch_shapes=[
            pltpu.VMEM((IDX_HEADS // 2, 2 * TQ, H), BF16),
            pltpu.VMEM((H, TQ), F32),
            pltpu.VMEM((nt, TQ, TQ), I32),
            pltpu.VMEM((TQ, TQ), I32),
            pltpu.VMEM(((nt + 1) // 2, 32, SUBLANES, TQ), I32),
            pltpu.VMEM(((nt + 1) // 2, SUBLANES, TQ), I32),
            pltpu.VMEM((N_HEADS * TQ, H), BF16),
            pltpu.VMEM((SUBLANES, TQ), I32),
            pltpu.VMEM((nt, TQ, N_HEADS * TQ), F32),
            pltpu.VMEM((SUBLANES, N_HEADS * TQ), F32),
            pltpu.VMEM((SUBLANES, N_HEADS * TQ), F32),
            pltpu.VMEM((H, N_HEADS * TQ), F32),
        ],
        compiler_params=_cparams("parallel", "arbitrary"),
        name="dsa",
    )(qib, kk, wq, dqn, dkn, dvt, near_t)


def _merge_body(h_ref, xn_ref, b0_ref, b1_ref, b2_ref, b3_ref, g0_ref, g1_ref, g2_ref, g3_ref,
                wb_ref, wo_ref, o_ref):
    @pl.when(pl.program_id(1) == 0)
    def _():
        o_ref[...] = h_ref[...]

    xn = xn_ref[...]
    merged = None
    for m, (b_ref, g_ref) in enumerate(((b0_ref, g0_ref), (b1_ref, g1_ref), (b2_ref, g2_ref), (b3_ref, g3_ref))):
        term = jax.nn.sigmoid(_dot(xn, g_ref[...])) * _dot(b_ref[...], wb_ref[m])
        merged = term if merged is None else merged + term
    o_ref[...] += _dot(merged.astype(BF16), wo_ref[...])


def _merge(h, xn, branches, w_gate, w_branch, w_out, tm, tn):
    T, D = h.shape
    W = MIX_WIDTH
    nn = D // tn
    row = lambda width: pl.BlockSpec((tm, width), lambda i, n: (i, 0))
    gate = lambda m: pl.BlockSpec((D, tn), functools.partial(lambda i, n, m: (0, m * nn + n), m=m))
    return pl.pallas_call(
        _merge_body,
        grid=(T // tm, nn),
        in_specs=[row(D), row(D), row(W), row(W), row(W), row(W),
                  gate(0), gate(1), gate(2), gate(3),
                  pl.BlockSpec((N_BRANCH, W, tn), lambda i, n: (0, 0, n)),
                  pl.BlockSpec((tn, D), lambda i, n: (n, 0))],
        out_specs=row(D),
        out_shape=jax.ShapeDtypeStruct((T, D), F32),
        compiler_params=_cparams("parallel", "arbitrary"),
        name="merge",
    )(h, xn, *branches, w_gate, w_gate, w_gate, w_gate, w_branch, w_out)


def _rel_bucket(n):
    max_exact = N_BUCKETS // 2
    nf = jnp.maximum(n, 1).astype(F32)
    large = max_exact + (jnp.log(nf / max_exact) / math.log(MAX_DISTANCE / max_exact)
                         * (N_BUCKETS - max_exact)).astype(I32)
    large = jnp.minimum(large, N_BUCKETS - 1)
    return jnp.where(n < max_exact, n, large)


def _bias_tiles(table):
    TQ = SEQ_TILE
    qi = jnp.arange(TQ, dtype=I32)[:, None]
    ki = jnp.arange(TQ, dtype=I32)[None, :]
    d0 = qi - ki
    tbl = table.astype(F32).T

    def lookup(bucket):
        out = jnp.zeros((tbl.shape[0],) + bucket.shape, F32)
        for b in range(N_BUCKETS):
            out = jnp.where(bucket[None] == b, tbl[:, b][:, None, None], out)
        return out

    far = tbl[:, N_BUCKETS - 1][:, None, None]
    diag = jnp.where(d0 >= 0, (lookup(_rel_bucket(jnp.maximum(d0, 0))) - far) * LOG2E, MASKED)
    prev = (lookup(_rel_bucket(d0 + TQ)) - far) * LOG2E
    return jnp.stack([diag, prev], axis=1).transpose(0, 1, 3, 2)


def _pack_w_in(w):
    offs = np.cumsum([0, 512, 512, 512, 512, 512, 512, 512, 512, 512, 512, 512, 128, 128,
                      IDX_HEADS * IDX_DIM, IDX_DIM, IDX_HEADS]).tolist()
    seg = lambda a, b: w[:, offs[a]:offs[b]]
    D = w.shape[0]
    used = COL_KIW + IDX_DIM + IDX_HEADS
    main = jnp.concatenate([
        seg(0, 4),
        seg(13, 14),
        seg(4, 7),
        seg(7, 10),
        seg(10, 11),
        seg(11, 12),
        seg(12, 13),
        seg(14, 16),
        jnp.zeros((D, P_COLS - used), w.dtype),
    ], axis=1).astype(BF16)
    gates = w[:, offs[16]:].astype(BF16)
    return main, gates


def _largest_tile(n, cap):
    t = cap
    while n % t:
        t //= 2
    return t


def _largest_multiple(n, unit, cap):
    return max(t for t in range(unit, cap + 1, unit) if n % t == 0)


def kernel(x, meta_tokens, rel_bias, ffn1_norm, ffn1_w_gu, ffn1_w_down, mix_norm, w_in, hgrn_lb, hgrn_gnorm, conv_w, diff_q_norm, diff_k_norm, diff_lambda, diff_subln, dsa_q_norm, dsa_k_norm, w_branch, w_out, ffn2_norm, ffn2_w_gu, ffn2_w_down):
    B, S, D = x.shape
    depth = w_in.shape[0]
    L = S + N_META
    Lp = -(-L // SEQ_TILE) * SEQ_TILE
    T = B * Lp
    top_k = min(TOPK_MAX, S // 4)
    tm = _largest_tile(T, 512)
    tm_wide = _largest_tile(T, 1024)
    tm_ffn = _largest_multiple(T, SEQ_TILE, FFN_ROWS_MAX)
    d_ff = ffn1_w_down.shape[1]
    tf = _largest_tile(d_ff, 512)

    meta = jnp.broadcast_to(meta_tokens.astype(x.dtype)[None], (B, N_META, D))
    h = jnp.concatenate([meta, x, jnp.zeros((B, Lp - L, D), x.dtype)], axis=1).reshape(T, D)

    lbs = jnp.cumsum(jax.nn.softmax(hgrn_lb.astype(F32), axis=0), axis=0)
    lbs = lbs - lbs[0:1]
    near_c = _bias_tiles(rel_bias[:, :N_HEADS])
    near_d = _bias_tiles(rel_bias[:, N_HEADS:])

    for l in range(depth):
        h = _ffn(h, ffn1_norm[l], ffn1_w_gu[l].astype(BF16), ffn1_w_down[l].astype(BF16), tm_ffn, tf)

        w_main, w_gate = _pack_w_in(w_in[l])
        xn, P = _proj(h, mix_norm[l], w_main, tm_wide, 1024)
        P3 = P.reshape(B, Lp, P_COLS)
        br_a = _hgrn(P3, lbs[l], hgrn_gnorm[l])
        br_b = _conv(P3, conv_w[l])
        cqn, ckn, cvt, dqn, dkn, dvt, qib, kk, wq = _prep(
            P, diff_q_norm[l], diff_k_norm[l], dsa_q_norm[l], dsa_k_norm[l], tm)
        r3 = lambda a: a.reshape(B, Lp, a.shape[-1])
        lp = diff_lambda[l].astype(F32)
        lam_init = 0.8 - 0.6 * math.exp(-0.3 * l)
        lam = jnp.exp(jnp.sum(lp[0] * lp[1])) - jnp.exp(jnp.sum(lp[2] * lp[3])) + lam_init
        tiles = lambda a: a.reshape(B, Lp // SEQ_TILE, a.shape[-2], SEQ_TILE)
        br_c = _diff(r3(cqn), r3(ckn), tiles(cvt), near_c,
                     jnp.broadcast_to(lam, (1, SEQ_TILE)).astype(F32),
                     jnp.broadcast_to(diff_subln[l].astype(F32)[:, None], (HEAD_DIM, SEQ_TILE)),
                     1.0 - lam_init)
        br_d = _dsa(r3(qib), r3(kk), r3(wq), r3(dqn), r3(dkn), tiles(dvt), near_d, top_k)
        branches = [a.reshape(T, MIX_WIDTH) for a in (br_a, br_b, br_c, br_d)]
        h = _merge(h, xn, branches, w_gate, w_branch[l].astype(BF16), w_out[l].astype(BF16), tm, 512)

        last = l == depth - 1
        h = _ffn(h, ffn2_norm[l], ffn2_w_gu[l].astype(BF16), ffn2_w_down[l].astype(BF16),
                 _largest_multiple(S, SUBLANES, FFN_ROWS_MAX) if last else tm_ffn, tf,
                 keep=(B, Lp, N_META, S) if last else None)

    return h.reshape(B, S, D)
```

```python
import functools
import math

import jax
import jax.numpy as jnp
import numpy as np
from jax import lax
from jax.experimental import pallas as pl
from jax.experimental.pallas import tpu as pltpu

F32 = jnp.float32
BF16 = jnp.bfloat16
I32 = jnp.int32

EPS = 1e-6
N_META = 16
HEAD_DIM = 128
HALF = 64
N_HEADS = 4
MIX_WIDTH = N_HEADS * HEAD_DIM
IDX_HEADS = 16
IDX_DIM = 64
TOPK_MAX = 256
N_BRANCH = 4
N_BUCKETS = 32
MAX_DISTANCE = 128
CONV_W = 3
SEQ_TILE = 128
HGRN_CHUNK = 8
HGRN_GROUP = 8
CONV_ROWS_MAX = 1408
FFN_ROWS_MAX = 1024
MASKED = -1e30
LOG2E = math.log2(math.e)
INT_MIN = -(2 ** 31)
KEY_NEG_INF = -2139095041
VMEM_LIMIT_BYTES = 56 * 1024 * 1024

COL_A = 0
COL_QI = 2048
COL_B = 3072
COL_C = 4608
COL_DQ = 6144
COL_DK = 6656
COL_DV = 6784
COL_KIW = 6912
P_COLS = 7168


def _cparams(*sem):
    return pltpu.CompilerParams(dimension_semantics=sem, vmem_limit_bytes=VMEM_LIMIT_BYTES)


def _dot(a, b):
    return jnp.dot(a, b, preferred_element_type=F32)


def _dot_nt(a, b):
    return lax.dot_general(a, b, (((1,), (1,)), ((), ())), preferred_element_type=F32)


def _dot_tn(a, b):
    return lax.dot_general(a, b, (((0,), (0,)), ((), ())), preferred_element_type=F32)


def _rms_inv(x):
    return lax.rsqrt(jnp.mean(x * x, axis=-1, keepdims=True) + EPS)


def _ffn_body(h_ref, g_ref, wg_ref, wu_ref, wd_ref, o_ref, xn_ref):
    j = pl.program_id(1)

    @pl.when(j == 0)
    def _():
        x = h_ref[...]
        xn_ref[...] = ((x * _rms_inv(x)) * g_ref[...]).astype(BF16)
        o_ref[...] = x

    xn = xn_ref[...]
    g = _dot(xn, wg_ref[...])
    u = _dot(xn, wu_ref[...])
    a = (g * jax.nn.sigmoid(g) * (0.5 * u)).astype(BF16)
    o_ref[...] += _dot(a, wd_ref[...])


def _ffn(h, gain, w_gu, w_down, tm, tf, keep=None):
    T, D = h.shape
    F = w_down.shape[0]
    nf = F // tf
    if keep is None:
        n_row_tiles = T // tm
        row_in = pl.BlockSpec((tm, D), lambda i, j: (i, 0))
        t_out = T
    else:
        B, Lp, first, n_rows = keep
        per_seq = n_rows // tm
        n_row_tiles = B * per_seq
        row_in = pl.BlockSpec((pl.Element(tm), pl.Element(D)),
                              lambda i, j: (pl.multiple_of(
                                  (i // per_seq) * Lp + first + (i % per_seq) * tm, SUBLANES), 0))
        t_out = B * n_rows
    return pl.pallas_call(
        _ffn_body,
        grid=(n_row_tiles, nf),
        in_specs=[
            row_in,
            pl.BlockSpec((1, D), lambda i, j: (0, 0)),
            pl.BlockSpec((D, tf), lambda i, j: (0, j)),
            pl.BlockSpec((D, tf), lambda i, j: (0, j + nf)),
            pl.BlockSpec((tf, D), lambda i, j: (j, 0)),
        ],
        out_specs=pl.BlockSpec((tm, D), lambda i, j: (i, 0)),
        out_shape=jax.ShapeDtypeStruct((t_out, D), F32),
        scratch_shapes=[pltpu.VMEM((tm, D), BF16)],
        compiler_params=_cparams("parallel", "arbitrary"),
        name="ffn",
    )(h, gain.reshape(1, D), w_gu, w_gu, w_down)


def _proj_body(h_ref, g_ref, w_ref, xn_ref, p_ref):
    @pl.when(pl.program_id(1) == 0)
    def _():
        x = h_ref[...]
        xn_ref[...] = ((x * _rms_inv(x)) * g_ref[...]).astype(BF16)

    p_ref[...] = _dot(xn_ref[...], w_ref[...])


def _proj(h, gain, w, tm, tn):
    T, D = h.shape
    N = w.shape[1]
    return pl.pallas_call(
        _proj_body,
        grid=(T // tm, N // tn),
        in_specs=[
            pl.BlockSpec((tm, D), lambda i, j: (i, 0)),
            pl.BlockSpec((1, D), lambda i, j: (0, 0)),
            pl.BlockSpec((D, tn), lambda i, j: (0, j)),
        ],
        out_specs=[
            pl.BlockSpec((tm, D), lambda i, j: (i, 0)),
            pl.BlockSpec((tm, tn), lambda i, j: (i, j)),
        ],
        out_shape=[jax.ShapeDtypeStruct((T, D), BF16), jax.ShapeDtypeStruct((T, N), F32)],
        compiler_params=_cparams("parallel", "arbitrary"),
        name="proj",
    )(h, gain.reshape(1, D), w)


def _hgrn_body(q_ref, f_ref, i_ref, g_ref, loglb_ref, log1mlb_ref, omlb_ref, gn_ref, o_ref,
               st_ref, qs_ref, ks_ref, bs_ref, os_ref):
    G, R, _ = q_ref.shape
    C = HGRN_CHUNK

    @pl.when(pl.program_id(1) == 0)
    def _():
        st_ref[...] = jnp.zeros_like(st_ref)

    row = lax.broadcasted_iota(I32, (R, MIX_WIDTH), 0) % C
    for g in range(G):
        z = f_ref[g]
        log_sig = jnp.minimum(z, 0.0) - jnp.log(1.0 + jnp.exp(-jnp.abs(z)))
        y = log1mlb_ref[...] + log_sig
        a = loglb_ref[...]
        log_f = jnp.maximum(a, y) + jnp.log(1.0 + jnp.exp(-jnp.abs(a - y)))
        b = log_f
        for sh in [1 << e for e in range(C.bit_length() - 1)]:
            b = b + jnp.where(row >= sh, pltpu.roll(b, sh, axis=0), 0.0)
        bs_ref[g] = b
        ks_ref[g] = omlb_ref[...] * jax.nn.sigmoid(-z)
        qr = q_ref[g]
        qs_ref[g] = qr * jax.nn.sigmoid(qr)

    t_iota = lax.broadcasted_iota(I32, (C, 1), 0)

    def chunk(c, carry):
        r0 = pl.multiple_of(c * C, C)
        for g in range(G):
            for h in range(N_HEADS):
                hs = slice(h * HEAD_DIM, (h + 1) * HEAD_DIM)
                qc = qs_ref[g, pl.ds(r0, C), hs]
                kc = ks_ref[g, pl.ds(r0, C), hs]
                bc = bs_ref[g, pl.ds(r0, C), hs]
                vc = i_ref[g, pl.ds(r0, C), hs]
                st = st_ref[g, h]
                o = _dot_nt((qc * jnp.exp(bc)).astype(BF16), st.astype(BF16))
                for s in range(C):
                    b_s = bc[s:s + 1, :]
                    k_s = kc[s:s + 1, :]
                    v_s = vc[s:s + 1, :]
                    e = jnp.exp(bc - b_s)
                    col = jnp.sum(qc * k_s * e, axis=-1, keepdims=True)
                    col = jnp.where(t_iota >= s, col, 0.0)
                    o = o + col * v_s
                os_ref[g, pl.ds(r0, C), hs] = o
                b_last = bc[C - 1:C, :]
                kd = kc * jnp.exp(b_last - bc)
                st_ref[g, h] = st * jnp.exp(b_last) + _dot_tn(vc.astype(BF16), kd.astype(BF16))
        return carry

    lax.fori_loop(0, R // C, chunk, 0)

    for g in range(G):
        for h in range(N_HEADS):
            hs = slice(h * HEAD_DIM, (h + 1) * HEAD_DIM)
            o = os_ref[g, :, hs]
            on = (o * _rms_inv(o)) * gn_ref[:, hs]
            gr = g_ref[g, :, hs]
            o_ref[g, :, hs] = (on * (gr * jax.nn.sigmoid(gr))).astype(o_ref.dtype)


def _hgrn(P3, lb, gnorm):
    B, Lp, _ = P3.shape
    R = SEQ_TILE
    W = MIX_WIDTH
    G = _largest_tile(B, HGRN_GROUP)
    lb = lb.reshape(1, W).astype(F32)
    vec = pl.BlockSpec((1, W), lambda b, t: (0, 0))
    c0 = COL_A // W
    return pl.pallas_call(
        _hgrn_body,
        grid=(B // G, Lp // R),
        in_specs=[pl.BlockSpec((G, R, W), functools.partial(lambda b, t, c: (b, t, c), c=c0 + k))
                  for k in range(4)] + [vec, vec, vec, vec],
        out_specs=pl.BlockSpec((G, R, W), lambda b, t: (b, t, 0)),
        out_shape=jax.ShapeDtypeStruct((B, Lp, W), BF16),
        scratch_shapes=[pltpu.VMEM((G, N_HEADS, HEAD_DIM, HEAD_DIM), F32)]
        + [pltpu.VMEM((G, R, W), F32)] * 4,
        compiler_params=_cparams("parallel", "arbitrary"),
        name="hgrn",
    )(P3, P3, P3, P3, jnp.log(lb), jnp.log1p(-lb), 1.0 - lb, gnorm.reshape(1, W).astype(F32))


def _conv_body(b_ref, c_ref, u_ref, w_ref, o_ref, carry_ref):
    @pl.when(pl.program_id(1) == 0)
    def _():
        carry_ref[...] = jnp.zeros_like(carry_ref)

    zc = c_ref[...] * u_ref[...]
    R = zc.shape[0]
    row = lax.broadcasted_iota(I32, zc.shape, 0)
    last = carry_ref[7:8, :]
    last2 = carry_ref[6:7, :]
    z1 = jnp.where(row == 0, last, pltpu.roll(zc, 1, axis=0))
    z2 = jnp.where(row == 0, last2, jnp.where(row == 1, last, pltpu.roll(zc, 2, axis=0)))
    y = w_ref[0:1, :] * zc + w_ref[1:2, :] * z1 + w_ref[2:3, :] * z2
    o_ref[...] = (b_ref[...] * y).astype(o_ref.dtype)
    carry_ref[...] = zc[R - 8:, :]


def _conv(P3, conv_w):
    B, Lp, _ = P3.shape
    R = _largest_multiple(Lp, SEQ_TILE, CONV_ROWS_MAX)
    W = MIX_WIDTH
    c0 = COL_B // W
    w8 = jnp.zeros((8, W), F32).at[:CONV_W].set(conv_w.astype(F32))
    return pl.pallas_call(
        _conv_body,
        grid=(B, Lp // R),
        in_specs=[pl.BlockSpec((None, R, W), functools.partial(lambda b, t, c: (b, t, c), c=c0 + k))
                  for k in range(3)] + [pl.BlockSpec((8, W), lambda b, t: (0, 0))],
        out_specs=pl.BlockSpec((None, R, W), lambda b, t: (b, t, 0)),
        out_shape=jax.ShapeDtypeStruct((B, Lp, W), BF16),
        scratch_shapes=[pltpu.VMEM((8, W), F32)],
        compiler_params=_cparams("parallel", "arbitrary"),
        name="conv",
    )(P3, P3, P3, w8)


def _prep_body(cq_ref, ck_ref, cv_ref, dq_ref, dk_ref, dv_ref, kiw_ref, qi_ref,
               cqg_ref, ckg_ref, dqg_ref, dkg_ref,
               cqn_ref, ckn_ref, cvb_ref, dqn_ref, dkn_ref, dvb_ref, qib_ref, kk_ref, wq_ref):
    lane = lax.broadcasted_iota(I32, (1, HEAD_DIM), 1)
    lo = lane < HALF

    def half_norm(x, g, scale):
        sq = x * x
        ms_lo = jnp.sum(jnp.where(lo, sq, 0.0), axis=-1, keepdims=True) * (1.0 / HALF)
        ms_hi = jnp.sum(jnp.where(lo, 0.0, sq), axis=-1, keepdims=True) * (1.0 / HALF)
        inv = jnp.where(lo, lax.rsqrt(ms_lo + EPS), lax.rsqrt(ms_hi + EPS))
        return ((x * inv) * g) * scale

    for h in range(N_HEADS):
        hs = slice(h * HEAD_DIM, (h + 1) * HEAD_DIM)
        cqn_ref[:, hs] = half_norm(cq_ref[:, hs], cqg_ref[...], HALF ** -0.5 * LOG2E).astype(BF16)
        ckn_ref[:, hs] = half_norm(ck_ref[:, hs], ckg_ref[...], 1.0).astype(BF16)
        x = dq_ref[:, hs]
        dqn_ref[:, hs] = (((x * _rms_inv(x)) * dqg_ref[...]) * (HEAD_DIM ** -0.5 * LOG2E)).astype(BF16)
    x = dk_ref[...]
    dkn_ref[...] = ((x * _rms_inv(x)) * dkg_ref[...]).astype(BF16)
    for t in range(cv_ref.shape[0] // SEQ_TILE):
        rs = slice(t * SEQ_TILE, (t + 1) * SEQ_TILE)
        for h in range(N_HEADS):
            hs = slice(h * HEAD_DIM, (h + 1) * HEAD_DIM)
            cvb_ref[t, hs, :] = cv_ref[rs, hs].T.astype(BF16)
        dvb_ref[t] = dv_ref[rs, :].T.astype(BF16)
    qib_ref[...] = (qi_ref[...] * (IDX_DIM ** -0.5)).astype(BF16)
    kiw = kiw_ref[...]
    swapped = pltpu.roll(kiw, HALF, axis=1)
    kk_ref[...] = jnp.where(lo, kiw, swapped).astype(BF16)
    wq_ref[...] = swapped * (IDX_HEADS ** -0.5)


def _prep(P, cq_g, ck_g, dq_g, dk_g, tm):
    T = P.shape[0]
    W = MIX_WIDTH
    H = HEAD_DIM

    def col(width, start):
        return pl.BlockSpec((tm, width), functools.partial(lambda i, c: (i, c), c=start // width))

    vec = pl.BlockSpec((1, H), lambda i: (0, 0))
    out = lambda width: pl.BlockSpec((tm, width), lambda i: (i, 0))
    tiles = lambda width: pl.BlockSpec((tm // SEQ_TILE, width, SEQ_TILE), lambda i: (i, 0, 0))
    return pl.pallas_call(
        _prep_body,
        grid=(T // tm,),
        in_specs=[col(W, COL_C), col(W, COL_C + W), col(W, COL_C + 2 * W), col(W, COL_DQ),
                  col(H, COL_DK), col(H, COL_DV), col(H, COL_KIW), col(IDX_HEADS * IDX_DIM, COL_QI),
                  vec, vec, vec, vec],
        out_specs=[out(W), out(W), tiles(W), out(W), out(H), tiles(H), out(IDX_HEADS * IDX_DIM), out(H), out(H)],
        out_shape=[jax.ShapeDtypeStruct((T, W), BF16)] * 2
        + [jax.ShapeDtypeStruct((T // SEQ_TILE, W, SEQ_TILE), BF16), jax.ShapeDtypeStruct((T, W), BF16),
           jax.ShapeDtypeStruct((T, H), BF16), jax.ShapeDtypeStruct((T // SEQ_TILE, H, SEQ_TILE), BF16)]
        + [jax.ShapeDtypeStruct((T, IDX_HEADS * IDX_DIM), BF16),
           jax.ShapeDtypeStruct((T, H), BF16), jax.ShapeDtypeStruct((T, H), F32)],
        compiler_params=_cparams("parallel"),
        name="prep",
    )(P, P, P, P, P, P, P, P,
      jnp.tile(cq_g.astype(F32), 2).reshape(1, H), jnp.tile(ck_g.astype(F32), 2).reshape(1, H),
      dq_g.astype(F32).reshape(1, H), dk_g.astype(F32).reshape(1, H))


SUBLANES = 8
KEY_CHUNK = 8


def _max8(x):
    return jnp.max(x.reshape(x.shape[0] // SUBLANES, SUBLANES, x.shape[1]), axis=0)


def _sum8(x):
    return jnp.sum(x.reshape(x.shape[0] // SUBLANES, SUBLANES, x.shape[1]), axis=0)


def _bit_transpose32(rows):
    a = list(rows)
    j = 16
    m = 0x0000FFFF
    while j:
        k = 0
        while k < 32:
            t = (a[k] ^ lax.shift_right_logical(a[k + j], jnp.int32(j))) & jnp.int32(m)
            a[k] = a[k] ^ t
            a[k + j] = a[k + j] ^ (t << j)
            k = (k + j + 1) & ~j
        j >>= 1
        m ^= (m << j) & 0xFFFFFFFF
    return a


def _for_tiles(n_tiles, body, chunk=KEY_CHUNK):
    n_chunks = n_tiles // chunk

    def chunked(c, carry):
        body(c * chunk, chunk)
        return carry

    lax.fori_loop(0, n_chunks, chunked, 0)
    base = n_chunks * chunk
    size = chunk // 2
    while size >= 1:
        take = (n_tiles - base) // size

        def part(_, carry, base=base, size=size):
            body(base, size)
            return carry

        lax.fori_loop(0, take, part, 0)
        base = base + take * size
        size //= 2


def _diff_body(q_ref, k_ref, vt_ref, near_ref, lam_ref, sub_ref, o_ref,
               q2_ref, s_ref, mx_ref, l_ref, acc_ref, *, out_scale):
    i = pl.program_id(1)
    TQ = SEQ_TILE
    H = HEAD_DIM
    W2 = 2 * TQ
    lane = lax.broadcasted_iota(I32, (1, H), 1)
    for h in range(N_HEADS):
        q = q_ref[:, h * H:(h + 1) * H]
        zero = jnp.zeros_like(q)
        q2_ref[h, 0:TQ, :] = jnp.where(lane < HALF, q, zero)
        q2_ref[h, TQ:, :] = jnp.where(lane < HALF, zero, q)
    mx_ref[...] = jnp.full_like(mx_ref, MASKED)

    def put(j0, n, near_idx):
        k0 = pl.multiple_of(j0 * TQ, TQ)
        parts = []
        for h in range(N_HEADS):
            s = _dot_nt(k_ref[pl.ds(k0, n * TQ), h * H:(h + 1) * H], q2_ref[h])
            if near_idx is None:
                parts.append(s)
            else:
                b = near_ref[h, near_idx]
                parts.append(s + jnp.concatenate([b, b], axis=1))
        s = jnp.concatenate(parts, axis=1)
        for t in range(n):
            s_ref[j0 + t] = s[t * TQ:(t + 1) * TQ, :]
        mx_ref[...] = jnp.maximum(mx_ref[...], _max8(s))

    n_far = jnp.maximum(i - 1, 0)
    _for_tiles(n_far, lambda j0, n: put(j0, n, None))

    def prev(j, carry):
        put(j, 1, 1)
        return carry

    lax.fori_loop(n_far, i, prev, 0)
    put(i, 1, 0)

    m = jnp.max(mx_ref[...], axis=0, keepdims=True)
    l_ref[...] = jnp.zeros_like(l_ref)
    acc_ref[...] = jnp.zeros_like(acc_ref)

    def pv(j0, n):
        p = jnp.exp2(s_ref[pl.ds(j0, n)].reshape(n * TQ, N_HEADS * W2) - m)
        l_ref[...] += _sum8(p)
        pb = p.astype(BF16)
        for h in range(N_HEADS):
            acc = acc_ref[:, h * W2:(h + 1) * W2]
            for t in range(n):
                acc = acc + _dot(vt_ref[j0 + t, h * H:(h + 1) * H, :],
                                 pb[t * TQ:(t + 1) * TQ, h * W2:(h + 1) * W2])
            acc_ref[:, h * W2:(h + 1) * W2] = acc

    _for_tiles(i + 1, pv)

    ot = acc_ref[...] * (1.0 / jnp.sum(l_ref[...], axis=0, keepdims=True))
    for h in range(N_HEADS):
        od = ot[:, h * W2:h * W2 + TQ] - lam_ref[...] * ot[:, h * W2 + TQ:(h + 1) * W2]
        inv = lax.rsqrt(jnp.mean(od * od, axis=0, keepdims=True) + EPS)
        o_ref[:, h * H:(h + 1) * H] = (((od * inv) * sub_ref[...]) * out_scale).T.astype(o_ref.dtype)


def _diff(cqn, ckn, cvt, near_t, lam, subln, out_scale):
    B, Lp, W = cqn.shape
    TQ = SEQ_TILE
    H = HEAD_DIM
    nt = Lp // TQ
    return pl.pallas_call(
        functools.partial(_diff_body, out_scale=out_scale),
        grid=(B, nt),
        in_specs=[
            pl.BlockSpec((None, TQ, W), lambda b, i: (b, i, 0)),
            pl.BlockSpec((None, Lp, W), lambda b, i: (b, 0, 0)),
            pl.BlockSpec((None, nt, W, TQ), lambda b, i: (b, 0, 0, 0)),
            pl.BlockSpec((N_HEADS, 2, TQ, TQ), lambda b, i: (0, 0, 0, 0)),
            pl.BlockSpec((1, TQ), lambda b, i: (0, 0)),
            pl.BlockSpec((H, TQ), lambda b, i: (0, 0)),
        ],
        out_specs=pl.BlockSpec((None, TQ, W), lambda b, i: (b, i, 0)),
        out_shape=jax.ShapeDtypeStruct((B, Lp, W), BF16),
        scratch_shapes=[pltpu.VMEM((N_HEADS, 2 * TQ, H), BF16),
                        pltpu.VMEM((nt, TQ, N_HEADS * 2 * TQ), F32),
                        pltpu.VMEM((SUBLANES, N_HEADS * 2 * TQ), F32),
                        pltpu.VMEM((SUBLANES, N_HEADS * 2 * TQ), F32),
                        pltpu.VMEM((H, N_HEADS * 2 * TQ), F32)],
        compiler_params=_cparams("parallel", "arbitrary"),
        name="diff",
    )(cqn, ckn, cvt, near_t, lam, subln)


def _dsa_body(qi_ref, kk_ref, wq_ref, q_ref, k_ref, vt_ref, near_ref, o_ref,
              qi2_ref, w_ref, key_ref, kq_ref, plane_ref, alive_ref, q4_ref, thr_ref, s_ref, mx_ref, l_ref, acc_ref, *, top_k):
    i = pl.program_id(1)
    TQ = SEQ_TILE
    H = HEAD_DIM
    lane = lax.broadcasted_iota(I32, (1, H), 1)

    for p in range(IDX_HEADS // 2):
        x = qi_ref[:, p * H:(p + 1) * H]
        zero = jnp.zeros_like(x)
        qi2_ref[p, 0:TQ, :] = jnp.where(lane < HALF, x, zero)
        qi2_ref[p, TQ:, :] = jnp.where(lane < HALF, zero, x)
    w_ref[...] = wq_ref[...].T
    for h in range(N_HEADS):
        q4_ref[h * TQ:(h + 1) * TQ, :] = q_ref[:, h * H:(h + 1) * H]

    def index(j0, n):
        k0 = pl.multiple_of(j0 * TQ, TQ)
        kt = kk_ref[pl.ds(k0, n * TQ), :]
        acc = jnp.zeros((n * TQ, TQ), F32)
        for p in range(IDX_HEADS // 2):
            s = jnp.maximum(_dot_nt(kt, qi2_ref[p]), 0.0)
            acc = acc + s[:, 0:TQ] * w_ref[2 * p:2 * p + 1, :] + s[:, TQ:] * w_ref[2 * p + 1:2 * p + 2, :]
        bits = pltpu.bitcast(acc, I32)
        key = bits ^ ((bits >> 31) & jnp.int32(0x7FFFFFFF))
        tiles = []
        for t in range(n):
            kt_ = key[t * TQ:(t + 1) * TQ, :]
            if t == n - 1:
                kt_ = jnp.where(kq_ref[...] <= (i - (j0 + t)) * TQ, kt_, jnp.int32(KEY_NEG_INF))
            key_ref[j0 + t] = kt_
            tiles.append(kt_)
        if n == 1:
            tiles.append(None)
        for u in range(len(tiles) // 2):
            rows = []
            for tile in tiles[2 * u:2 * u + 2]:
                for r in range(TQ // SUBLANES):
                    if tile is None:
                        rows.append(jnp.full((SUBLANES, TQ), INT_MIN, I32))
                    else:
                        rows.append(tile[r * SUBLANES:(r + 1) * SUBLANES, :])
            planes = _bit_transpose32(rows)
            planes[0] = ~planes[0]
            pair = j0 // 2 + u
            for b in range(32):
                plane_ref[pair, b] = planes[b]

    @pl.when(i == 0)
    def _():
        plane_ref[...] = jnp.zeros_like(plane_ref)

    kq_ref[...] = lax.broadcasted_iota(I32, (TQ, TQ), 0) - lax.broadcasted_iota(I32, (TQ, TQ), 1)
    _for_tiles(i + 1, index)

    n_pairs = plane_ref.shape[0]
    for tp in range(n_pairs):
        alive_ref[tp] = jnp.full((SUBLANES, TQ), -1, I32)

    def bit_step(t, carry):
        above, thr = carry
        ones = [alive_ref[tp] & plane_ref[tp, t] for tp in range(n_pairs)]
        cnt = lax.population_count(ones[0])
        for tp in range(1, n_pairs):
            cnt = cnt + lax.population_count(ones[tp])
        hit = above + jnp.sum(cnt, axis=0, keepdims=True)
        take = hit >= top_k
        for tp in range(n_pairs):
            alive_ref[tp] = jnp.where(take, ones[tp], alive_ref[tp] ^ ones[tp])
        thr = thr | jnp.where(take, jnp.left_shift(jnp.int32(1), 31 - t), 0)
        return jnp.where(take, above, hit), thr

    zero8 = jnp.zeros((SUBLANES, TQ), I32)
    _, thr_u = lax.fori_loop(0, 32, bit_step, (zero8, zero8))
    thr_ref[...] = thr_u ^ jnp.int32(INT_MIN)

    mx_ref[...] = jnp.full_like(mx_ref, MASKED)

    def put(j0, n, near_idx):
        k0 = pl.multiple_of(j0 * TQ, TQ)
        s = _dot_nt(k_ref[pl.ds(k0, n * TQ), :], q4_ref[...])
        thr1 = thr_ref[0:1, :]
        mx = mx_ref[...]
        for t in range(n):
            sel = key_ref[j0 + t] >= thr1
            parts = []
            for h in range(N_HEADS):
                sh = s[t * TQ:(t + 1) * TQ, h * TQ:(h + 1) * TQ]
                if near_idx is not None:
                    sh = sh + near_ref[h, near_idx]
                parts.append(jnp.where(sel, sh, MASKED))
            row = jnp.concatenate(parts, axis=1)
            s_ref[j0 + t] = row
            mx = jnp.maximum(mx, _max8(row))
        mx_ref[...] = mx

    n_far = jnp.maximum(i - 1, 0)
    _for_tiles(n_far, lambda j0, n: put(j0, n, None))

    def prev(j, carry):
        put(j, 1, 1)
        return carry

    lax.fori_loop(n_far, i, prev, 0)
    put(i, 1, 0)

    m = jnp.max(mx_ref[...], axis=0, keepdims=True)
    l_ref[...] = jnp.zeros_like(l_ref)
    acc_ref[...] = jnp.zeros_like(acc_ref)

    def pv(j0, n):
        p = jnp.exp2(s_ref[pl.ds(j0, n)].reshape(n * TQ, N_HEADS * TQ) - m)
        l_ref[...] += _sum8(p)
        pb = p.astype(BF16)
        acc = acc_ref[...]
        for t in range(n):
            acc = acc + _dot(vt_ref[j0 + t], pb[t * TQ:(t + 1) * TQ, :])
        acc_ref[...] = acc

    _for_tiles(i + 1, pv)

    ot = acc_ref[...] * (1.0 / jnp.sum(l_ref[...], axis=0, keepdims=True))
    for h in range(N_HEADS):
        o_ref[:, h * H:(h + 1) * H] = ot[:, h * TQ:(h + 1) * TQ].T.astype(o_ref.dtype)


def _dsa(qib, kk, wq, dqn, dkn, dvt, near_t, top_k):
    B, Lp, W = dqn.shape
    TQ = SEQ_TILE
    H = HEAD_DIM
    NI = IDX_HEADS * IDX_DIM
    nt = Lp // TQ
    seq = lambda width: pl.BlockSpec((None, Lp, width), lambda b, i: (b, 0, 0))
    blk = lambda width: pl.BlockSpec((None, TQ, width), lambda b, i: (b, i, 0))
    return pl.pallas_call(
        functools.partial(_dsa_body, top_k=top_k),
        grid=(B, nt),
        in_specs=[blk(NI), seq(H), blk(H), blk(W), seq(H),
                  pl.BlockSpec((None, nt, H, TQ), lambda b, i: (b, 0, 0, 0)),
                  pl.BlockSpec((N_HEADS, 2, TQ, TQ), lambda b, i: (0, 0, 0, 0))],
        out_specs=blk(W),
        out_shape=jax.ShapeDtypeStruct((B, Lp, W), BF16),
        scratch_shapes=[
            pltpu.VMEM((IDX_HEADS // 2, 2 * TQ, H), BF16),
            pltpu.VMEM((H, TQ), F32),
            pltpu.VMEM((nt, TQ, TQ), I32),
            pltpu.VMEM((TQ, TQ), I32),
            pltpu.VMEM(((nt + 1) // 2, 32, SUBLANES, TQ), I32),
            pltpu.VMEM(((nt + 1) // 2, SUBLANES, TQ), I32),
            pltpu.VMEM((N_HEADS * TQ, H), BF16),
            pltpu.VMEM((SUBLANES, TQ), I32),
            pltpu.VMEM((nt, TQ, N_HEADS * TQ), F32),
            pltpu.VMEM((SUBLANES, N_HEADS * TQ), F32),
            pltpu.VMEM((SUBLANES, N_HEADS * TQ), F32),
            pltpu.VMEM((H, N_HEADS * TQ), F32),
        ],
        compiler_params=_cparams("parallel", "arbitrary"),
        name="dsa",
    )(qib, kk, wq, dqn, dkn, dvt, near_t)


def _merge_body(h_ref, xn_ref, b0_ref, b1_ref, b2_ref, b3_ref, g0_ref, g1_ref, g2_ref, g3_ref,
                wb_ref, wo_ref, o_ref):
    @pl.when(pl.program_id(1) == 0)
    def _():
        o_ref[...] = h_ref[...]

    xn = xn_ref[...]
    merged = None
    for m, (b_ref, g_ref) in enumerate(((b0_ref, g0_ref), (b1_ref, g1_ref), (b2_ref, g2_ref), (b3_ref, g3_ref))):
        term = jax.nn.sigmoid(_dot(xn, g_ref[...])) * _dot(b_ref[...], wb_ref[m])
        merged = term if merged is None else merged + term
    o_ref[...] += _dot(merged.astype(BF16), wo_ref[...])


def _merge(h, xn, branches, w_gate, w_branch, w_out, tm, tn):
    T, D = h.shape
    W = MIX_WIDTH
    nn = D // tn
    row = lambda width: pl.BlockSpec((tm, width), lambda i, n: (i, 0))
    gate = lambda m: pl.BlockSpec((D, tn), functools.partial(lambda i, n, m: (0, m * nn + n), m=m))
    return pl.pallas_call(
        _merge_body,
        grid=(T // tm, nn),
        in_specs=[row(D), row(D), row(W), row(W), row(W), row(W),
                  gate(0), gate(1), gate(2), gate(3),
                  pl.BlockSpec((N_BRANCH, W, tn), lambda i, n: (0, 0, n)),
                  pl.BlockSpec((tn, D), lambda i, n: (n, 0))],
        out_specs=row(D),
        out_shape=jax.ShapeDtypeStruct((T, D), F32),
        compiler_params=_cparams("parallel", "arbitrary"),
        name="merge",
    )(h, xn, *branches, w_gate, w_gate, w_gate, w_gate, w_branch, w_out)


def _rel_bucket(n):
    max_exact = N_BUCKETS // 2
    nf = jnp.maximum(n, 1).astype(F32)
    large = max_exact + (jnp.log(nf / max_exact) / math.log(MAX_DISTANCE / max_exact)
                         * (N_BUCKETS - max_exact)).astype(I32)
    large = jnp.minimum(large, N_BUCKETS - 1)
    return jnp.where(n < max_exact, n, large)


def _bias_tiles(table):
    TQ = SEQ_TILE
    qi = jnp.arange(TQ, dtype=I32)[:, None]
    ki = jnp.arange(TQ, dtype=I32)[None, :]
    d0 = qi - ki
    tbl = table.astype(F32).T

    def lookup(bucket):
        out = jnp.zeros((tbl.shape[0],) + bucket.shape, F32)
        for b in range(N_BUCKETS):
            out = jnp.where(bucket[None] == b, tbl[:, b][:, None, None], out)
        return out

    far = tbl[:, N_BUCKETS - 1][:, None, None]
    diag = jnp.where(d0 >= 0, (lookup(_rel_bucket(jnp.maximum(d0, 0))) - far) * LOG2E, MASKED)
    prev = (lookup(_rel_bucket(d0 + TQ)) - far) * LOG2E
    return jnp.stack([diag, prev], axis=1).transpose(0, 1, 3, 2)


def _pack_w_in(w):
    offs = np.cumsum([0, 512, 512, 512, 512, 512, 512, 512, 512, 512, 512, 512, 128, 128,
                      IDX_HEADS * IDX_DIM, IDX_DIM, IDX_HEADS]).tolist()
    seg = lambda a, b: w[:, offs[a]:offs[b]]
    D = w.shape[0]
    used = COL_KIW + IDX_DIM + IDX_HEADS
    main = jnp.concatenate([
        seg(0, 4),
        seg(13, 14),
        seg(4, 7),
        seg(7, 10),
        seg(10, 11),
        seg(11, 12),
        seg(12, 13),
        seg(14, 16),
        jnp.zeros((D, P_COLS - used), w.dtype),
    ], axis=1).astype(BF16)
    gates = w[:, offs[16]:].astype(BF16)
    return main, gates


def _largest_tile(n, cap):
    t = cap
    while n % t:
        t //= 2
    return t


def _largest_multiple(n, unit, cap):
    return max(t for t in range(unit, cap + 1, unit) if n % t == 0)


def kernel(x, meta_tokens, rel_bias, ffn1_norm, ffn1_w_gu, ffn1_w_down, mix_norm, w_in, hgrn_lb, hgrn_gnorm, conv_w, diff_q_norm, diff_k_norm, diff_lambda, diff_subln, dsa_q_norm, dsa_k_norm, w_branch, w_out, ffn2_norm, ffn2_w_gu, ffn2_w_down):
    B, S, D = x.shape
    depth = w_in.shape[0]
    L = S + N_META
    Lp = -(-L // SEQ_TILE) * SEQ_TILE
    T = B * Lp
    top_k = min(TOPK_MAX, S // 4)
    tm = _largest_tile(T, 512)
    tm_wide = _largest_tile(T, 1024)
    tm_ffn = _largest_multiple(T, SEQ_TILE, FFN_ROWS_MAX)
    d_ff = ffn1_w_down.shape[1]
    tf = _largest_tile(d_ff, 512)

    meta = jnp.broadcast_to(meta_tokens.astype(x.dtype)[None], (B, N_META, D))
    h = jnp.concatenate([meta, x, jnp.zeros((B, Lp - L, D), x.dtype)], axis=1).reshape(T, D)

    lbs = jnp.cumsum(jax.nn.softmax(hgrn_lb.astype(F32), axis=0), axis=0)
    lbs = lbs - lbs[0:1]
    near_c = _bias_tiles(rel_bias[:, :N_HEADS])
    near_d = _bias_tiles(rel_bias[:, N_HEADS:])

    for l in range(depth):
        h = _ffn(h, ffn1_norm[l], ffn1_w_gu[l].astype(BF16), ffn1_w_down[l].astype(BF16), tm_ffn, tf)

        w_main, w_gate = _pack_w_in(w_in[l])
        xn, P = _proj(h, mix_norm[l], w_main, tm_wide, 1024)
        P3 = P.reshape(B, Lp, P_COLS)
        br_a = _hgrn(P3, lbs[l], hgrn_gnorm[l])
        br_b = _conv(P3, conv_w[l])
        cqn, ckn, cvt, dqn, dkn, dvt, qib, kk, wq = _prep(
            P, diff_q_norm[l], diff_k_norm[l], dsa_q_norm[l], dsa_k_norm[l], tm)
        r3 = lambda a: a.reshape(B, Lp, a.shape[-1])
        lp = diff_lambda[l].astype(F32)
        lam_init = 0.8 - 0.6 * math.exp(-0.3 * l)
        lam = jnp.exp(jnp.sum(lp[0] * lp[1])) - jnp.exp(jnp.sum(lp[2] * lp[3])) + lam_init
        tiles = lambda a: a.reshape(B, Lp // SEQ_TILE, a.shape[-2], SEQ_TILE)
        br_c = _diff(r3(cqn), r3(ckn), tiles(cvt), near_c,
                     jnp.broadcast_to(lam, (1, SEQ_TILE)).astype(F32),
                     jnp.broadcast_to(diff_subln[l].astype(F32)[:, None], (HEAD_DIM, SEQ_TILE)),
                     1.0 - lam_init)
        br_d = _dsa(r3(qib), r3(kk), r3(wq), r3(dqn), r3(dkn), tiles(dvt), near_d, top_k)
        branches = [a.reshape(T, MIX_WIDTH) for a in (br_a, br_b, br_c, br_d)]
        h = _merge(h, xn, branches, w_gate, w_branch[l].astype(BF16), w_out[l].astype(BF16), tm, 512)

        last = l == depth - 1
        h = _ffn(h, ffn2_norm[l], ffn2_w_gu[l].astype(BF16), ffn2_w_down[l].astype(BF16),
                 _largest_multiple(S, SUBLANES, FFN_ROWS_MAX) if last else tm_ffn, tf,
                 keep=(B, Lp, N_META, S) if last else None)

    return h.reshape(B, S, D)
```

```python
import functools
import math

import jax
import jax.numpy as jnp
import numpy as np
from jax import lax
from jax.experimental import pallas as pl
from jax.experimental.pallas import tpu as pltpu

F32 = jnp.float32
BF16 = jnp.bfloat16
I32 = jnp.int32

EPS = 1e-6
N_META = 16
HEAD_DIM = 128
HALF = 64
N_HEADS = 4
MIX_WIDTH = N_HEADS * HEAD_DIM
IDX_HEADS = 16
IDX_DIM = 64
TOPK_MAX = 256
N_BRANCH = 4
N_BUCKETS = 32
MAX_DISTANCE = 128
CONV_W = 3
SEQ_TILE = 128
HGRN_CHUNK = 8
HGRN_GROUP = 8
DIFF_QUERY_TILES = 1
DIFF_HEADS_PER_STEP = 4
DSA_QUERY_TILES = 2
CONV_ROWS_MAX = 1408
FFN_ROWS_MAX = 1024
MASKED = -1e30
LOG2E = math.log2(math.e)
INT_MIN = -(2 ** 31)
KEY_NEG_INF = -2139095041
VMEM_LIMIT_BYTES = 56 * 1024 * 1024

COL_A = 0
COL_QI = 2048
COL_B = 3072
COL_C = 4608
COL_DQ = 6144
COL_DK = 6656
COL_DV = 6784
COL_KIW = 6912
P_COLS = 7168


def _cparams(*sem):
    return pltpu.CompilerParams(dimension_semantics=sem, vmem_limit_bytes=VMEM_LIMIT_BYTES)


def _dot(a, b):
    return jnp.dot(a, b, preferred_element_type=F32)


def _dot_nt(a, b):
    return lax.dot_general(a, b, (((1,), (1,)), ((), ())), preferred_element_type=F32)


def _dot_tn(a, b):
    return lax.dot_general(a, b, (((0,), (0,)), ((), ())), preferred_element_type=F32)


def _rms_inv(x):
    return lax.rsqrt(jnp.mean(x * x, axis=-1, keepdims=True) + EPS)


def _ffn_body(h_ref, g_ref, wg_ref, wu_ref, wd_ref, o_ref, xn_ref):
    j = pl.program_id(1)

    @pl.when(j == 0)
    def _():
        x = h_ref[...]
        xn_ref[...] = ((x * _rms_inv(x)) * g_ref[...]).astype(BF16)
        o_ref[...] = x

    xn = xn_ref[...]
    g = _dot(xn, wg_ref[...])
    u = _dot(xn, wu_ref[...])
    a = (g * jax.nn.sigmoid(g) * (0.5 * u)).astype(BF16)
    o_ref[...] += _dot(a, wd_ref[...])


def _ffn(h, gain, w_gu, w_down, tm, tf, keep=None):
    T, D = h.shape
    F = w_down.shape[0]
    nf = F // tf
    if keep is None:
        n_row_tiles = T // tm
        row_in = pl.BlockSpec((tm, D), lambda i, j: (i, 0))
        t_out = T
    else:
        B, Lp, first, n_rows = keep
        per_seq = n_rows // tm
        n_row_tiles = B * per_seq
        row_in = pl.BlockSpec((pl.Element(tm), pl.Element(D)),
                              lambda i, j: (pl.multiple_of(
                                  (i // per_seq) * Lp + first + (i % per_seq) * tm, SUBLANES), 0))
        t_out = B * n_rows
    return pl.pallas_call(
        _ffn_body,
        grid=(n_row_tiles, nf),
        in_specs=[
            row_in,
            pl.BlockSpec((1, D), lambda i, j: (0, 0)),
            pl.BlockSpec((D, tf), lambda i, j: (0, j)),
            pl.BlockSpec((D, tf), lambda i, j: (0, j + nf)),
            pl.BlockSpec((tf, D), lambda i, j: (j, 0)),
        ],
        out_specs=pl.BlockSpec((tm, D), lambda i, j: (i, 0)),
        out_shape=jax.ShapeDtypeStruct((t_out, D), F32),
        scratch_shapes=[pltpu.VMEM((tm, D), BF16)],
        compiler_params=_cparams("parallel", "arbitrary"),
        name="ffn",
    )(h, gain.reshape(1, D), w_gu, w_gu, w_down)


def _proj_body(h_ref, g_ref, w_ref, xn_ref, p_ref):
    @pl.when(pl.program_id(1) == 0)
    def _():
        x = h_ref[...]
        xn_ref[...] = ((x * _rms_inv(x)) * g_ref[...]).astype(BF16)

    p_ref[...] = _dot(xn_ref[...], w_ref[...])


def _proj(h, gain, w, tm, tn):
    T, D = h.shape
    N = w.shape[1]
    return pl.pallas_call(
        _proj_body,
        grid=(T // tm, N // tn),
        in_specs=[
            pl.BlockSpec((tm, D), lambda i, j: (i, 0)),
            pl.BlockSpec((1, D), lambda i, j: (0, 0)),
            pl.BlockSpec((D, tn), lambda i, j: (0, j)),
        ],
        out_specs=[
            pl.BlockSpec((tm, D), lambda i, j: (i, 0)),
            pl.BlockSpec((tm, tn), lambda i, j: (i, j)),
        ],
        out_shape=[jax.ShapeDtypeStruct((T, D), BF16), jax.ShapeDtypeStruct((T, N), F32)],
        compiler_params=_cparams("parallel", "arbitrary"),
        name="proj",
    )(h, gain.reshape(1, D), w)


def _hgrn_body(q_ref, f_ref, i_ref, g_ref, loglb_ref, log1mlb_ref, omlb_ref, gn_ref, o_ref,
               st_ref, qs_ref, ks_ref, bs_ref, os_ref):
    G, R, _ = q_ref.shape
    C = HGRN_CHUNK

    @pl.when(pl.program_id(1) == 0)
    def _():
        st_ref[...] = jnp.zeros_like(st_ref)

    row = lax.broadcasted_iota(I32, (R, MIX_WIDTH), 0) % C
    for g in range(G):
        z = f_ref[g]
        log_sig = jnp.minimum(z, 0.0) - jnp.log(1.0 + jnp.exp(-jnp.abs(z)))
        y = log1mlb_ref[...] + log_sig
        a = loglb_ref[...]
        log_f = jnp.maximum(a, y) + jnp.log(1.0 + jnp.exp(-jnp.abs(a - y)))
        b = log_f
        for sh in [1 << e for e in range(C.bit_length() - 1)]:
            b = b + jnp.where(row >= sh, pltpu.roll(b, sh, axis=0), 0.0)
        bs_ref[g] = b
        ks_ref[g] = omlb_ref[...] * jax.nn.sigmoid(-z)
        qr = q_ref[g]
        qs_ref[g] = qr * jax.nn.sigmoid(qr)

    t_iota = lax.broadcasted_iota(I32, (C, 1), 0)

    def chunk(c, carry):
        r0 = pl.multiple_of(c * C, C)
        for g in range(G):
            for h in range(N_HEADS):
                hs = slice(h * HEAD_DIM, (h + 1) * HEAD_DIM)
                qc = qs_ref[g, pl.ds(r0, C), hs]
                kc = ks_ref[g, pl.ds(r0, C), hs]
                bc = bs_ref[g, pl.ds(r0, C), hs]
                vc = i_ref[g, pl.ds(r0, C), hs]
                st = st_ref[g, h]
                o = _dot_nt((qc * jnp.exp(bc)).astype(BF16), st.astype(BF16))
                for s in range(C):
                    b_s = bc[s:s + 1, :]
                    k_s = kc[s:s + 1, :]
                    v_s = vc[s:s + 1, :]
                    e = jnp.exp(bc - b_s)
                    col = jnp.sum(qc * k_s * e, axis=-1, keepdims=True)
                    col = jnp.where(t_iota >= s, col, 0.0)
                    o = o + col * v_s
                os_ref[g, pl.ds(r0, C), hs] = o
                b_last = bc[C - 1:C, :]
                kd = kc * jnp.exp(b_last - bc)
                st_ref[g, h] = st * jnp.exp(b_last) + _dot_tn(vc.astype(BF16), kd.astype(BF16))
        return carry

    lax.fori_loop(0, R // C, chunk, 0)

    for g in range(G):
        for h in range(N_HEADS):
            hs = slice(h * HEAD_DIM, (h + 1) * HEAD_DIM)
            o = os_ref[g, :, hs]
            on = (o * _rms_inv(o)) * gn_ref[:, hs]
            gr = g_ref[g, :, hs]
            o_ref[g, :, hs] = (on * (gr * jax.nn.sigmoid(gr))).astype(o_ref.dtype)


def _hgrn(P3, lb, gnorm):
    B, Lp, _ = P3.shape
    R = SEQ_TILE
    W = MIX_WIDTH
    G = _largest_tile(B, HGRN_GROUP)
    lb = lb.reshape(1, W).astype(F32)
    vec = pl.BlockSpec((1, W), lambda b, t: (0, 0))
    c0 = COL_A // W
    return pl.pallas_call(
        _hgrn_body,
        grid=(B // G, Lp // R),
        in_specs=[pl.BlockSpec((G, R, W), functools.partial(lambda b, t, c: (b, t, c), c=c0 + k))
                  for k in range(4)] + [vec, vec, vec, vec],
        out_specs=pl.BlockSpec((G, R, W), lambda b, t: (b, t, 0)),
        out_shape=jax.ShapeDtypeStruct((B, Lp, W), BF16),
        scratch_shapes=[pltpu.VMEM((G, N_HEADS, HEAD_DIM, HEAD_DIM), F32)]
        + [pltpu.VMEM((G, R, W), F32)] * 4,
        compiler_params=_cparams("parallel", "arbitrary"),
        name="hgrn",
    )(P3, P3, P3, P3, jnp.log(lb), jnp.log1p(-lb), 1.0 - lb, gnorm.reshape(1, W).astype(F32))


def _conv_body(b_ref, c_ref, u_ref, w_ref, o_ref, carry_ref):
    @pl.when(pl.program_id(1) == 0)
    def _():
        carry_ref[...] = jnp.zeros_like(carry_ref)

    zc = c_ref[...] * u_ref[...]
    R = zc.shape[0]
    row = lax.broadcasted_iota(I32, zc.shape, 0)
    last = carry_ref[7:8, :]
    last2 = carry_ref[6:7, :]
    z1 = jnp.where(row == 0, last, pltpu.roll(zc, 1, axis=0))
    z2 = jnp.where(row == 0, last2, jnp.where(row == 1, last, pltpu.roll(zc, 2, axis=0)))
    y = w_ref[0:1, :] * zc + w_ref[1:2, :] * z1 + w_ref[2:3, :] * z2
    o_ref[...] = (b_ref[...] * y).astype(o_ref.dtype)
    carry_ref[...] = zc[R - 8:, :]


def _conv(P3, conv_w):
    B, Lp, _ = P3.shape
    R = _largest_multiple(Lp, SEQ_TILE, CONV_ROWS_MAX)
    W = MIX_WIDTH
    c0 = COL_B // W
    w8 = jnp.zeros((8, W), F32).at[:CONV_W].set(conv_w.astype(F32))
    return pl.pallas_call(
        _conv_body,
        grid=(B, Lp // R),
        in_specs=[pl.BlockSpec((None, R, W), functools.partial(lambda b, t, c: (b, t, c), c=c0 + k))
                  for k in range(3)] + [pl.BlockSpec((8, W), lambda b, t: (0, 0))],
        out_specs=pl.BlockSpec((None, R, W), lambda b, t: (b, t, 0)),
        out_shape=jax.ShapeDtypeStruct((B, Lp, W), BF16),
        scratch_shapes=[pltpu.VMEM((8, W), F32)],
        compiler_params=_cparams("parallel", "arbitrary"),
        name="conv",
    )(P3, P3, P3, w8)


def _prep_body(cq_ref, ck_ref, cv_ref, dq_ref, dk_ref, dv_ref, kiw_ref, qi_ref,
               cqg_ref, ckg_ref, dqg_ref, dkg_ref,
               cqn_ref, ckn_ref, cvb_ref, dqn_ref, dkn_ref, dvb_ref, qib_ref, kk_ref, wq_ref):
    lane = lax.broadcasted_iota(I32, (1, HEAD_DIM), 1)
    lo = lane < HALF

    def half_norm(x, g, scale):
        sq = x * x
        ms_lo = jnp.sum(jnp.where(lo, sq, 0.0), axis=-1, keepdims=True) * (1.0 / HALF)
        ms_hi = jnp.sum(jnp.where(lo, 0.0, sq), axis=-1, keepdims=True) * (1.0 / HALF)
        inv = jnp.where(lo, lax.rsqrt(ms_lo + EPS), lax.rsqrt(ms_hi + EPS))
        return ((x * inv) * g) * scale

    for h in range(N_HEADS):
        hs = slice(h * HEAD_DIM, (h + 1) * HEAD_DIM)
        cqn_ref[:, hs] = half_norm(cq_ref[:, hs], cqg_ref[...], HALF ** -0.5 * LOG2E).astype(BF16)
        ckn_ref[:, hs] = half_norm(ck_ref[:, hs], ckg_ref[...], 1.0).astype(BF16)
        x = dq_ref[:, hs]
        dqn_ref[:, hs] = (((x * _rms_inv(x)) * dqg_ref[...]) * (HEAD_DIM ** -0.5 * LOG2E)).astype(BF16)
    x = dk_ref[...]
    dkn_ref[...] = ((x * _rms_inv(x)) * dkg_ref[...]).astype(BF16)
    for t in range(cv_ref.shape[0] // SEQ_TILE):
        rs = slice(t * SEQ_TILE, (t + 1) * SEQ_TILE)
        for h in range(N_HEADS):
            hs = slice(h * HEAD_DIM, (h + 1) * HEAD_DIM)
            cvb_ref[t, hs, :] = cv_ref[rs, hs].T.astype(BF16)
        dvb_ref[t] = dv_ref[rs, :].T.astype(BF16)
    qib_ref[...] = (qi_ref[...] * (IDX_DIM ** -0.5)).astype(BF16)
    kiw = kiw_ref[...]
    swapped = pltpu.roll(kiw, HALF, axis=1)
    kk_ref[...] = jnp.where(lo, kiw, swapped).astype(BF16)
    wq_ref[...] = swapped * (IDX_HEADS ** -0.5)


def _prep(P, cq_g, ck_g, dq_g, dk_g, tm):
    T = P.shape[0]
    W = MIX_WIDTH
    H = HEAD_DIM

    def col(width, start):
        return pl.BlockSpec((tm, width), functools.partial(lambda i, c: (i, c), c=start // width))

    vec = pl.BlockSpec((1, H), lambda i: (0, 0))
    out = lambda width: pl.BlockSpec((tm, width), lambda i: (i, 0))
    tiles = lambda width: pl.BlockSpec((tm // SEQ_TILE, width, SEQ_TILE), lambda i: (i, 0, 0))
    return pl.pallas_call(
        _prep_body,
        grid=(T // tm,),
        in_specs=[col(W, COL_C), col(W, COL_C + W), col(W, COL_C + 2 * W), col(W, COL_DQ),
                  col(H, COL_DK), col(H, COL_DV), col(H, COL_KIW), col(IDX_HEADS * IDX_DIM, COL_QI),
                  vec, vec, vec, vec],
        out_specs=[out(W), out(W), tiles(W), out(W), out(H), tiles(H), out(IDX_HEADS * IDX_DIM), out(H), out(H)],
        out_shape=[jax.ShapeDtypeStruct((T, W), BF16)] * 2
        + [jax.ShapeDtypeStruct((T // SEQ_TILE, W, SEQ_TILE), BF16), jax.ShapeDtypeStruct((T, W), BF16),
           jax.ShapeDtypeStruct((T, H), BF16), jax.ShapeDtypeStruct((T // SEQ_TILE, H, SEQ_TILE), BF16)]
        + [jax.ShapeDtypeStruct((T, IDX_HEADS * IDX_DIM), BF16),
           jax.ShapeDtypeStruct((T, H), BF16), jax.ShapeDtypeStruct((T, H), F32)],
        compiler_params=_cparams("parallel"),
        name="prep",
    )(P, P, P, P, P, P, P, P,
      jnp.tile(cq_g.astype(F32), 2).reshape(1, H), jnp.tile(ck_g.astype(F32), 2).reshape(1, H),
      dq_g.astype(F32).reshape(1, H), dk_g.astype(F32).reshape(1, H))


SUBLANES = 8
KEY_CHUNK = 8


def _max8(x):
    return jnp.max(x.reshape(x.shape[0] // SUBLANES, SUBLANES, x.shape[1]), axis=0)


def _sum8(x):
    return jnp.sum(x.reshape(x.shape[0] // SUBLANES, SUBLANES, x.shape[1]), axis=0)


def _bit_transpose32(rows):
    a = list(rows)
    j = 16
    m = 0x0000FFFF
    while j:
        k = 0
        while k < 32:
            t = (a[k] ^ lax.shift_right_logical(a[k + j], jnp.int32(j))) & jnp.int32(m)
            a[k] = a[k] ^ t
            a[k + j] = a[k + j] ^ (t << j)
            k = (k + j + 1) & ~j
        j >>= 1
        m ^= (m << j) & 0xFFFFFFFF
    return a


def _for_tiles(n_tiles, body, chunk=KEY_CHUNK):
    n_chunks = n_tiles // chunk

    def chunked(c, carry):
        body(c * chunk, chunk)
        return carry

    lax.fori_loop(0, n_chunks, chunked, 0)
    base = n_chunks * chunk
    size = chunk // 2
    while size >= 1:
        take = (n_tiles - base) // size

        def part(_, carry, base=base, size=size):
            body(base, size)
            return carry

        lax.fori_loop(0, take, part, 0)
        base = base + take * size
        size //= 2


def _diff_body(q_ref, k_ref, vt_ref, near_ref, lam_ref, sub_ref, o_ref,
               q2_ref, s_ref, mx_ref, l_ref, acc_ref, *, out_scale, n_tiles):
    blk = pl.program_id(2)
    KT = SEQ_TILE
    H = HEAD_DIM
    QT = q_ref.shape[0]
    QB = QT // KT
    HP = q_ref.shape[1] // H
    W2 = 2 * QT
    lane = lax.broadcasted_iota(I32, (1, H), 1)
    for h in range(HP):
        q = q_ref[:, h * H:(h + 1) * H]
        zero = jnp.zeros_like(q)
        q2_ref[h, 0:QT, :] = jnp.where(lane < HALF, q, zero)
        q2_ref[h, QT:, :] = jnp.where(lane < HALF, zero, q)
    mx_ref[...] = jnp.full_like(mx_ref, MASKED)

    def put(j0, n, near_idx):
        k0 = pl.multiple_of(j0 * KT, KT)
        parts = []
        for h in range(HP):
            s = _dot_nt(k_ref[pl.ds(k0, n * KT), h * H:(h + 1) * H], q2_ref[h])
            if near_idx is None:
                parts.append(s)
            else:
                b = near_ref[h, near_idx]
                parts.append(s + jnp.concatenate([b, b], axis=1))
        s = jnp.concatenate(parts, axis=1)
        for t in range(n):
            s_ref[j0 + t] = s[t * KT:(t + 1) * KT, :]
        mx_ref[...] = jnp.maximum(mx_ref[...], _max8(s))

    first_near = QB * blk - 1
    n_far = jnp.maximum(first_near, 0)
    newest = jnp.minimum(QB * blk + QB - 1, n_tiles - 1)
    _for_tiles(n_far, lambda j0, n: put(j0, n, None))

    def near(j, carry):
        put(j, 1, j - first_near)
        return carry

    lax.fori_loop(n_far, newest + 1, near, 0)

    m = jnp.max(mx_ref[...], axis=0, keepdims=True)
    l_ref[...] = jnp.zeros_like(l_ref)
    acc_ref[...] = jnp.zeros_like(acc_ref)

    def pv(j0, n):
        p = jnp.exp2(s_ref[pl.ds(j0, n)].reshape(n * KT, HP * W2) - m)
        l_ref[...] += _sum8(p)
        pb = p.astype(BF16)
        for h in range(HP):
            acc = acc_ref[:, h * W2:(h + 1) * W2]
            for t in range(n):
                acc = acc + _dot(vt_ref[j0 + t, h * H:(h + 1) * H, :],
                                 pb[t * KT:(t + 1) * KT, h * W2:(h + 1) * W2])
            acc_ref[:, h * W2:(h + 1) * W2] = acc

    _for_tiles(newest + 1, pv)

    ot = acc_ref[...] * (1.0 / jnp.sum(l_ref[...], axis=0, keepdims=True))
    for h in range(HP):
        od = ot[:, h * W2:h * W2 + QT] - lam_ref[...] * ot[:, h * W2 + QT:(h + 1) * W2]
        inv = lax.rsqrt(jnp.mean(od * od, axis=0, keepdims=True) + EPS)
        o_ref[:, h * H:(h + 1) * H] = (((od * inv) * sub_ref[...]) * out_scale).T.astype(o_ref.dtype)


def _diff(cqn, ckn, cvt, near_t, lam, subln, out_scale):
    B, Lp, W = cqn.shape
    KT = SEQ_TILE
    H = HEAD_DIM
    nt = Lp // KT
    QB = near_t.shape[1] - 1
    QT = QB * KT
    HP = DIFF_HEADS_PER_STEP
    lanes = HP * 2 * QT
    return pl.pallas_call(
        functools.partial(_diff_body, out_scale=out_scale, n_tiles=nt),
        grid=(B, N_HEADS // HP, pl.cdiv(nt, QB)),
        in_specs=[
            pl.BlockSpec((None, QT, HP * H), lambda b, g, i: (b, i, g)),
            pl.BlockSpec((None, Lp, HP * H), lambda b, g, i: (b, 0, g)),
            pl.BlockSpec((None, nt, HP * H, KT), lambda b, g, i: (b, 0, g, 0)),
            pl.BlockSpec((HP, QB + 1, KT, QT), lambda b, g, i: (g, 0, 0, 0)),
            pl.BlockSpec((1, QT), lambda b, g, i: (0, 0)),
            pl.BlockSpec((H, QT), lambda b, g, i: (0, 0)),
        ],
        out_specs=pl.BlockSpec((None, QT, HP * H), lambda b, g, i: (b, i, g)),
        out_shape=jax.ShapeDtypeStruct((B, Lp, W), BF16),
        scratch_shapes=[pltpu.VMEM((HP, 2 * QT, H), BF16),
                        pltpu.VMEM((nt, KT, lanes), F32),
                        pltpu.VMEM((SUBLANES, lanes), F32),
                        pltpu.VMEM((SUBLANES, lanes), F32),
                        pltpu.VMEM((H, lanes), F32)],
        compiler_params=_cparams("parallel", "parallel", "arbitrary"),
        name="diff",
    )(cqn, ckn, cvt, near_t, lam, subln)


def _block_first_tile(blk, n_blocks, n_tiles, tiles_per_block):
    return jnp.maximum(n_tiles - tiles_per_block * (n_blocks - blk), 0)


def _dsa_body(qi_ref, kk_ref, wq_ref, q_ref, k_ref, vt_ref, near_ref, o_ref,
              qi2_ref, w_ref, key_ref, kq_ref, plane_ref, alive_ref, q4_ref, thr_ref, s_ref, mx_ref,
              l_ref, acc_ref, *, top_k, n_tiles):
    blk = pl.program_id(1)
    KT = SEQ_TILE
    H = HEAD_DIM
    QT = q_ref.shape[0]
    QB = QT // KT
    first = _block_first_tile(blk, pl.num_programs(1), n_tiles, QB)
    newest = first + QB - 1
    lane = lax.broadcasted_iota(I32, (1, H), 1)

    for p in range(IDX_HEADS // 2):
        x = qi_ref[:, p * H:(p + 1) * H]
        zero = jnp.zeros_like(x)
        qi2_ref[p, 0:QT, :] = jnp.where(lane < HALF, x, zero)
        qi2_ref[p, QT:, :] = jnp.where(lane < HALF, zero, x)
    w_ref[...] = wq_ref[...].T
    for h in range(N_HEADS):
        q4_ref[h * QT:(h + 1) * QT, :] = q_ref[:, h * H:(h + 1) * H]
    kq_ref[...] = lax.broadcasted_iota(I32, (KT, QT), 0) - lax.broadcasted_iota(I32, (KT, QT), 1)

    def index(j0, n):
        k0 = pl.multiple_of(j0 * KT, KT)
        kt = kk_ref[pl.ds(k0, n * KT), :]
        acc = jnp.zeros((n * KT, QT), F32)
        for p in range(IDX_HEADS // 2):
            s = jnp.maximum(_dot_nt(kt, qi2_ref[p]), 0.0)
            acc = acc + s[:, 0:QT] * w_ref[2 * p:2 * p + 1, :] + s[:, QT:] * w_ref[2 * p + 1:2 * p + 2, :]
        bits = pltpu.bitcast(acc, I32)
        key = bits ^ ((bits >> 31) & jnp.int32(0x7FFFFFFF))
        tiles = []
        for t in range(n):
            kt_ = key[t * KT:(t + 1) * KT, :]
            if t >= n - QB:
                kt_ = jnp.where(kq_ref[...] <= (first - (j0 + t)) * KT, kt_, jnp.int32(KEY_NEG_INF))
            key_ref[j0 + t] = kt_
            tiles.append(kt_)
        if n == 1:
            tiles.append(None)
        for u in range(len(tiles) // 2):
            rows = []
            for tile in tiles[2 * u:2 * u + 2]:
                for r in range(KT // SUBLANES):
                    if tile is None:
                        rows.append(jnp.full((SUBLANES, QT), INT_MIN, I32))
                    else:
                        rows.append(tile[r * SUBLANES:(r + 1) * SUBLANES, :])
            planes = _bit_transpose32(rows)
            planes[0] = ~planes[0]
            pair = j0 // 2 + u
            for b in range(32):
                plane_ref[pair, b] = planes[b]

    @pl.when(blk == 0)
    def _():
        plane_ref[...] = jnp.zeros_like(plane_ref)

    _for_tiles(newest + 1, index)

    n_pairs = plane_ref.shape[0]
    for tp in range(n_pairs):
        alive_ref[tp] = jnp.full((SUBLANES, QT), -1, I32)

    def bit_step(t, carry):
        above, thr = carry
        ones = [alive_ref[tp] & plane_ref[tp, t] for tp in range(n_pairs)]
        cnt = lax.population_count(ones[0])
        for tp in range(1, n_pairs):
            cnt = cnt + lax.population_count(ones[tp])
        hit = above + jnp.sum(cnt, axis=0, keepdims=True)
        take = hit >= top_k
        for tp in range(n_pairs):
            alive_ref[tp] = jnp.where(take, ones[tp], alive_ref[tp] ^ ones[tp])
        thr = thr | jnp.where(take, jnp.left_shift(jnp.int32(1), 31 - t), 0)
        return jnp.where(take, above, hit), thr

    zero8 = jnp.zeros((SUBLANES, QT), I32)
    _, thr_u = lax.fori_loop(0, 32, bit_step, (zero8, zero8))
    thr_ref[...] = thr_u ^ jnp.int32(INT_MIN)

    mx_ref[...] = jnp.full_like(mx_ref, MASKED)

    def put(j0, n, near_idx):
        k0 = pl.multiple_of(j0 * KT, KT)
        s = _dot_nt(k_ref[pl.ds(k0, n * KT), :], q4_ref[...])
        thr1 = thr_ref[0:1, :]
        mx = mx_ref[...]
        for t in range(n):
            sel = key_ref[j0 + t] >= thr1
            parts = []
            for h in range(N_HEADS):
                sh = s[t * KT:(t + 1) * KT, h * QT:(h + 1) * QT]
                if near_idx is not None:
                    sh = sh + near_ref[h, near_idx]
                parts.append(jnp.where(sel, sh, MASKED))
            row = jnp.concatenate(parts, axis=1)
            s_ref[j0 + t] = row
            mx = jnp.maximum(mx, _max8(row))
        mx_ref[...] = mx

    first_near = first - 1
    n_far = jnp.maximum(first_near, 0)
    _for_tiles(n_far, lambda j0, n: put(j0, n, None))

    def near(j, carry):
        put(j, 1, j - first_near)
        return carry

    lax.fori_loop(n_far, newest + 1, near, 0)

    m = jnp.max(mx_ref[...], axis=0, keepdims=True)
    l_ref[...] = jnp.zeros_like(l_ref)
    acc_ref[...] = jnp.zeros_like(acc_ref)

    def pv(j0, n):
        p = jnp.exp2(s_ref[pl.ds(j0, n)].reshape(n * KT, N_HEADS * QT) - m)
        l_ref[...] += _sum8(p)
        pb = p.astype(BF16)
        acc = acc_ref[...]
        for t in range(n):
            acc = acc + _dot(vt_ref[j0 + t], pb[t * KT:(t + 1) * KT, :])
        acc_ref[...] = acc

    _for_tiles(newest + 1, pv)

    ot = acc_ref[...] * (1.0 / jnp.sum(l_ref[...], axis=0, keepdims=True))
    for h in range(N_HEADS):
        o_ref[:, h * H:(h + 1) * H] = ot[:, h * QT:(h + 1) * QT].T.astype(o_ref.dtype)


def _dsa(qib, kk, wq, dqn, dkn, dvt, near_t, top_k):
    B, Lp, H = dkn.shape
    W = dqn.shape[1]
    KT = SEQ_TILE
    NI = IDX_HEADS * IDX_DIM
    nt = Lp // KT
    QB = near_t.shape[1] - 1
    QT = QB * KT
    n_blocks = pl.cdiv(nt, QB)

    def rows(width):
        return pl.BlockSpec(
            (pl.Element(QT), pl.Element(width)),
            lambda b, i: (pl.multiple_of(b * Lp + _block_first_tile(i, n_blocks, nt, QB) * KT, KT), 0))

    seq = lambda width: pl.BlockSpec((None, Lp, width), lambda b, i: (b, 0, 0))
    n_pairs = (nt + 1) // 2
    return pl.pallas_call(
        functools.partial(_dsa_body, top_k=top_k, n_tiles=nt),
        grid=(B, n_blocks),
        in_specs=[rows(NI), seq(H), rows(H), rows(W), seq(H),
                  pl.BlockSpec((None, nt, H, KT), lambda b, i: (b, 0, 0, 0)),
                  pl.BlockSpec((N_HEADS, QB + 1, KT, QT), lambda b, i: (0, 0, 0, 0))],
        out_specs=rows(W),
        out_shape=jax.ShapeDtypeStruct((B * Lp, W), BF16),
        scratch_shapes=[
            pltpu.VMEM((IDX_HEADS // 2, 2 * QT, H), BF16),
            pltpu.VMEM((H, QT), F32),
            pltpu.VMEM((nt, KT, QT), I32),
            pltpu.VMEM((KT, QT), I32),
            pltpu.VMEM((n_pairs, 32, SUBLANES, QT), I32),
            pltpu.VMEM((n_pairs, SUBLANES, QT), I32),
            pltpu.VMEM((N_HEADS * QT, H), BF16),
            pltpu.VMEM((SUBLANES, QT), I32),
            pltpu.VMEM((nt, KT, N_HEADS * QT), F32),
            pltpu.VMEM((SUBLANES, N_HEADS * QT), F32),
            pltpu.VMEM((SUBLANES, N_HEADS * QT), F32),
            pltpu.VMEM((H, N_HEADS * QT), F32),
        ],
        compiler_params=_cparams("parallel", "arbitrary"),
        name="dsa",
    )(qib, kk, wq, dqn, dkn, dvt, near_t)


def _merge_body(h_ref, xn_ref, b0_ref, b1_ref, b2_ref, b3_ref, g0_ref, g1_ref, g2_ref, g3_ref,
                wb_ref, wo_ref, o_ref):
    @pl.when(pl.program_id(1) == 0)
    def _():
        o_ref[...] = h_ref[...]

    xn = xn_ref[...]
    merged = None
    for m, (b_ref, g_ref) in enumerate(((b0_ref, g0_ref), (b1_ref, g1_ref), (b2_ref, g2_ref), (b3_ref, g3_ref))):
        term = jax.nn.sigmoid(_dot(xn, g_ref[...])) * _dot(b_ref[...], wb_ref[m])
        merged = term if merged is None else merged + term
    o_ref[...] += _dot(merged.astype(BF16), wo_ref[...])


def _merge(h, xn, branches, w_gate, w_branch, w_out, tm, tn):
    T, D = h.shape
    W = MIX_WIDTH
    nn = D // tn
    row = lambda width: pl.BlockSpec((tm, width), lambda i, n: (i, 0))
    gate = lambda m: pl.BlockSpec((D, tn), functools.partial(lambda i, n, m: (0, m * nn + n), m=m))
    return pl.pallas_call(
        _merge_body,
        grid=(T // tm, nn),
        in_specs=[row(D), row(D), row(W), row(W), row(W), row(W),
                  gate(0), gate(1), gate(2), gate(3),
                  pl.BlockSpec((N_BRANCH, W, tn), lambda i, n: (0, 0, n)),
                  pl.BlockSpec((tn, D), lambda i, n: (n, 0))],
        out_specs=row(D),
        out_shape=jax.ShapeDtypeStruct((T, D), F32),
        compiler_params=_cparams("parallel", "arbitrary"),
        name="merge",
    )(h, xn, *branches, w_gate, w_gate, w_gate, w_gate, w_branch, w_out)


def _rel_bucket(n):
    max_exact = N_BUCKETS // 2
    nf = jnp.maximum(n, 1).astype(F32)
    large = max_exact + (jnp.log(nf / max_exact) / math.log(MAX_DISTANCE / max_exact)
                         * (N_BUCKETS - max_exact)).astype(I32)
    large = jnp.minimum(large, N_BUCKETS - 1)
    return jnp.where(n < max_exact, n, large)


def _bias_tiles(table, query_tiles):
    KT = SEQ_TILE
    kl = jnp.arange(KT, dtype=I32)[:, None]
    ql = jnp.arange(query_tiles * KT, dtype=I32)[None, :]
    tbl = table.astype(F32).T

    def lookup(bucket):
        out = jnp.zeros((tbl.shape[0],) + bucket.shape, F32)
        for b in range(N_BUCKETS):
            out = jnp.where(bucket[None] == b, tbl[:, b][:, None, None], out)
        return out

    far = tbl[:, N_BUCKETS - 1][:, None, None]
    tiles = []
    for t in range(query_tiles + 1):
        d = ql - kl + (1 - t) * KT
        bias = (lookup(_rel_bucket(jnp.maximum(d, 0))) - far) * LOG2E
        tiles.append(jnp.where(d >= 0, bias, MASKED))
    return jnp.stack(tiles, axis=1)


def _pack_w_in(w):
    offs = np.cumsum([0, 512, 512, 512, 512, 512, 512, 512, 512, 512, 512, 512, 128, 128,
                      IDX_HEADS * IDX_DIM, IDX_DIM, IDX_HEADS]).tolist()
    seg = lambda a, b: w[:, offs[a]:offs[b]]
    D = w.shape[0]
    used = COL_KIW + IDX_DIM + IDX_HEADS
    main = jnp.concatenate([
        seg(0, 4),
        seg(13, 14),
        seg(4, 7),
        seg(7, 10),
        seg(10, 11),
        seg(11, 12),
        seg(12, 13),
        seg(14, 16),
        jnp.zeros((D, P_COLS - used), w.dtype),
    ], axis=1).astype(BF16)
    gates = w[:, offs[16]:].astype(BF16)
    return main, gates


def _largest_tile(n, cap):
    t = cap
    while n % t:
        t //= 2
    return t


def _largest_multiple(n, unit, cap):
    return max(t for t in range(unit, cap + 1, unit) if n % t == 0)


def kernel(x, meta_tokens, rel_bias, ffn1_norm, ffn1_w_gu, ffn1_w_down, mix_norm, w_in, hgrn_lb, hgrn_gnorm, conv_w, diff_q_norm, diff_k_norm, diff_lambda, diff_subln, dsa_q_norm, dsa_k_norm, w_branch, w_out, ffn2_norm, ffn2_w_gu, ffn2_w_down):
    B, S, D = x.shape
    depth = w_in.shape[0]
    L = S + N_META
    Lp = -(-L // SEQ_TILE) * SEQ_TILE
    T = B * Lp
    top_k = min(TOPK_MAX, S // 4)
    tm = _largest_tile(T, 512)
    tm_wide = _largest_tile(T, 1024)
    tm_ffn = _largest_multiple(T, SEQ_TILE, FFN_ROWS_MAX)
    d_ff = ffn1_w_down.shape[1]
    tf = _largest_tile(d_ff, 512)

    meta = jnp.broadcast_to(meta_tokens.astype(x.dtype)[None], (B, N_META, D))
    h = jnp.concatenate([meta, x, jnp.zeros((B, Lp - L, D), x.dtype)], axis=1).reshape(T, D)

    lbs = jnp.cumsum(jax.nn.softmax(hgrn_lb.astype(F32), axis=0), axis=0)
    lbs = lbs - lbs[0:1]
    near_c = _bias_tiles(rel_bias[:, :N_HEADS], DIFF_QUERY_TILES)
    near_d = _bias_tiles(rel_bias[:, N_HEADS:], DSA_QUERY_TILES)
    diff_qt = DIFF_QUERY_TILES * SEQ_TILE

    for l in range(depth):
        h = _ffn(h, ffn1_norm[l], ffn1_w_gu[l].astype(BF16), ffn1_w_down[l].astype(BF16), tm_ffn, tf)

        w_main, w_gate = _pack_w_in(w_in[l])
        xn, P = _proj(h, mix_norm[l], w_main, tm_wide, 1024)
        P3 = P.reshape(B, Lp, P_COLS)
        br_a = _hgrn(P3, lbs[l], hgrn_gnorm[l])
        br_b = _conv(P3, conv_w[l])
        cqn, ckn, cvt, dqn, dkn, dvt, qib, kk, wq = _prep(
            P, diff_q_norm[l], diff_k_norm[l], dsa_q_norm[l], dsa_k_norm[l], tm)
        r3 = lambda a: a.reshape(B, Lp, a.shape[-1])
        lp = diff_lambda[l].astype(F32)
        lam_init = 0.8 - 0.6 * math.exp(-0.3 * l)
        lam = jnp.exp(jnp.sum(lp[0] * lp[1])) - jnp.exp(jnp.sum(lp[2] * lp[3])) + lam_init
        tiles = lambda a: a.reshape(B, Lp // SEQ_TILE, a.shape[-2], SEQ_TILE)
        br_c = _diff(r3(cqn), r3(ckn), tiles(cvt), near_c,
                     jnp.broadcast_to(lam, (1, diff_qt)).astype(F32),
                     jnp.broadcast_to(diff_subln[l].astype(F32)[:, None], (HEAD_DIM, diff_qt)),
                     1.0 - lam_init)
        br_d = _dsa(qib, r3(kk), wq, dqn, r3(dkn), tiles(dvt), near_d, top_k)
        branches = [a.reshape(T, MIX_WIDTH) for a in (br_a, br_b, br_c, br_d)]
        h = _merge(h, xn, branches, w_gate, w_branch[l].astype(BF16), w_out[l].astype(BF16), tm, 512)

        last = l == depth - 1
        h = _ffn(h, ffn2_norm[l], ffn2_w_gu[l].astype(BF16), ffn2_w_down[l].astype(BF16),
                 _largest_multiple(S, SUBLANES, FFN_ROWS_MAX) if last else tm_ffn, tf,
                 keep=(B, Lp, N_META, S) if last else None)

    return h.reshape(B, S, D)
```

```python
import functools
import math

import jax
import jax.numpy as jnp
import numpy as np
from jax import lax
from jax.experimental import pallas as pl
from jax.experimental.pallas import tpu as pltpu

F32 = jnp.float32
BF16 = jnp.bfloat16
I32 = jnp.int32

EPS = 1e-6
N_META = 16
HEAD_DIM = 128
HALF = 64
N_HEADS = 4
MIX_WIDTH = N_HEADS * HEAD_DIM
IDX_HEADS = 16
IDX_DIM = 64
TOPK_MAX = 256
N_BRANCH = 4
N_BUCKETS = 32
MAX_DISTANCE = 128
CONV_W = 3
SEQ_TILE = 128
HGRN_CHUNK = 8
HGRN_GROUP = 8
DIFF_QUERY_TILES = 1
DIFF_HEADS_PER_STEP = 4
DSA_QUERY_TILES = 2
CONV_ROWS_MAX = 1408
CAST_BLOCK_BYTES = 8 * 1024 * 1024
FFN_ROWS_MAX = 1024
MASKED = -1e30
LOG2E = math.log2(math.e)
INT_MIN = -(2 ** 31)
KEY_NEG_INF = -2139095041
VMEM_LIMIT_BYTES = 56 * 1024 * 1024

COL_A = 0
COL_QI = 2048
COL_B = 3072
COL_C = 4608
COL_DQ = 6144
COL_DK = 6656
COL_DV = 6784
COL_KIW = 6912
P_COLS = 7168


def _cparams(*sem):
    return pltpu.CompilerParams(dimension_semantics=sem, vmem_limit_bytes=VMEM_LIMIT_BYTES)


def _dot(a, b):
    return jnp.dot(a, b, preferred_element_type=F32)


def _dot_nt(a, b):
    return lax.dot_general(a, b, (((1,), (1,)), ((), ())), preferred_element_type=F32)


def _dot_tn(a, b):
    return lax.dot_general(a, b, (((0,), (0,)), ((), ())), preferred_element_type=F32)


def _rms_inv(x):
    return lax.rsqrt(jnp.mean(x * x, axis=-1, keepdims=True) + EPS)


def _cast_body(x_ref, o_ref):
    o_ref[...] = x_ref[...].astype(o_ref.dtype)


def _to_bf16(w):
    cols = w.shape[-1]
    w2 = w.reshape(-1, cols)
    n_rows = w2.shape[0]
    rows = _largest_multiple(n_rows, SUBLANES, max(SUBLANES, CAST_BLOCK_BYTES // (4 * cols)))
    out = pl.pallas_call(
        _cast_body,
        grid=(n_rows // rows,),
        in_specs=[pl.BlockSpec((rows, cols), lambda i: (i, 0))],
        out_specs=pl.BlockSpec((rows, cols), lambda i: (i, 0)),
        out_shape=jax.ShapeDtypeStruct((n_rows, cols), BF16),
        compiler_params=_cparams("parallel"),
        name="cast",
    )(w2)
    return out.reshape(w.shape)


def _ffn_body(h_ref, g_ref, wg_ref, wu_ref, wd_ref, o_ref, xn_ref):
    j = pl.program_id(1)

    @pl.when(j == 0)
    def _():
        x = h_ref[...]
        xn_ref[...] = ((x * _rms_inv(x)) * g_ref[...]).astype(BF16)
        o_ref[...] = x

    xn = xn_ref[...]
    g = _dot(xn, wg_ref[...])
    u = _dot(xn, wu_ref[...])
    a = (g * jax.nn.sigmoid(g) * (0.5 * u)).astype(BF16)
    o_ref[...] += _dot(a, wd_ref[...])


def _ffn(h, gain, w_gu, w_down, layer, tm, tf, keep=None):
    T, D = h.shape
    F = w_down.shape[1]
    nf = F // tf
    if keep is None:
        n_row_tiles = T // tm
        row_in = pl.BlockSpec((tm, D), lambda i, j: (i, 0))
        t_out = T
    else:
        B, Lp, first, n_rows = keep
        per_seq = n_rows // tm
        n_row_tiles = B * per_seq
        row_in = pl.BlockSpec((pl.Element(tm), pl.Element(D)),
                              lambda i, j: (pl.multiple_of(
                                  (i // per_seq) * Lp + first + (i % per_seq) * tm, SUBLANES), 0))
        t_out = B * n_rows
    return pl.pallas_call(
        _ffn_body,
        grid=(n_row_tiles, nf),
        in_specs=[
            row_in,
            pl.BlockSpec((1, D), lambda i, j: (0, 0)),
            pl.BlockSpec((None, D, tf), lambda i, j: (layer, 0, j)),
            pl.BlockSpec((None, D, tf), lambda i, j: (layer, 0, j + nf)),
            pl.BlockSpec((None, tf, D), lambda i, j: (layer, j, 0)),
        ],
        out_specs=pl.BlockSpec((tm, D), lambda i, j: (i, 0)),
        out_shape=jax.ShapeDtypeStruct((t_out, D), F32),
        scratch_shapes=[pltpu.VMEM((tm, D), BF16)],
        compiler_params=_cparams("parallel", "arbitrary"),
        name="ffn",
    )(h, gain.reshape(1, D), w_gu, w_gu, w_down)


def _proj_body(h_ref, g_ref, w_ref, xn_ref, p_ref):
    @pl.when(pl.program_id(1) == 0)
    def _():
        x = h_ref[...]
        xn_ref[...] = ((x * _rms_inv(x)) * g_ref[...]).astype(BF16)

    p_ref[...] = _dot(xn_ref[...], w_ref[...])


def _proj(h, gain, w, tm, tn):
    T, D = h.shape
    N = w.shape[1]
    return pl.pallas_call(
        _proj_body,
        grid=(T // tm, N // tn),
        in_specs=[
            pl.BlockSpec((tm, D), lambda i, j: (i, 0)),
            pl.BlockSpec((1, D), lambda i, j: (0, 0)),
            pl.BlockSpec((D, tn), lambda i, j: (0, j)),
        ],
        out_specs=[
            pl.BlockSpec((tm, D), lambda i, j: (i, 0)),
            pl.BlockSpec((tm, tn), lambda i, j: (i, j)),
        ],
        out_shape=[jax.ShapeDtypeStruct((T, D), BF16), jax.ShapeDtypeStruct((T, N), F32)],
        compiler_params=_cparams("parallel", "arbitrary"),
        name="proj",
    )(h, gain.reshape(1, D), w)


def _hgrn_body(q_ref, f_ref, i_ref, g_ref, loglb_ref, log1mlb_ref, omlb_ref, gn_ref, o_ref,
               st_ref, qs_ref, ks_ref, bs_ref, os_ref):
    G, R, _ = q_ref.shape
    C = HGRN_CHUNK

    @pl.when(pl.program_id(1) == 0)
    def _():
        st_ref[...] = jnp.zeros_like(st_ref)

    row = lax.broadcasted_iota(I32, (R, MIX_WIDTH), 0) % C
    for g in range(G):
        z = f_ref[g]
        log_sig = jnp.minimum(z, 0.0) - jnp.log(1.0 + jnp.exp(-jnp.abs(z)))
        y = log1mlb_ref[...] + log_sig
        a = loglb_ref[...]
        log_f = jnp.maximum(a, y) + jnp.log(1.0 + jnp.exp(-jnp.abs(a - y)))
        b = log_f * LOG2E
        for sh in [1 << e for e in range(C.bit_length() - 1)]:
            b = b + jnp.where(row >= sh, pltpu.roll(b, sh, axis=0), 0.0)
        bs_ref[g] = b
        ks_ref[g] = omlb_ref[...] * jax.nn.sigmoid(-z)
        qr = q_ref[g]
        qs_ref[g] = qr * jax.nn.sigmoid(qr)

    t_iota = lax.broadcasted_iota(I32, (C, 1), 0)

    def chunk(c, carry):
        r0 = pl.multiple_of(c * C, C)
        for g in range(G):
            for h in range(N_HEADS):
                hs = slice(h * HEAD_DIM, (h + 1) * HEAD_DIM)
                qc = qs_ref[g, pl.ds(r0, C), hs]
                kc = ks_ref[g, pl.ds(r0, C), hs]
                bc = bs_ref[g, pl.ds(r0, C), hs]
                vc = i_ref[g, pl.ds(r0, C), hs]
                st = st_ref[g, h]
                o = _dot_nt((qc * jnp.exp2(bc)).astype(BF16), st.astype(BF16))
                for s in range(C):
                    b_s = bc[s:s + 1, :]
                    k_s = kc[s:s + 1, :]
                    v_s = vc[s:s + 1, :]
                    e = jnp.exp2(bc - b_s)
                    col = jnp.sum(qc * k_s * e, axis=-1, keepdims=True)
                    col = jnp.where(t_iota >= s, col, 0.0)
                    o = o + col * v_s
                os_ref[g, pl.ds(r0, C), hs] = o
                b_last = bc[C - 1:C, :]
                kd = kc * jnp.exp2(b_last - bc)
                st_ref[g, h] = st * jnp.exp2(b_last) + _dot_tn(vc.astype(BF16), kd.astype(BF16))
        return carry

    lax.fori_loop(0, R // C, chunk, 0)

    for g in range(G):
        for h in range(N_HEADS):
            hs = slice(h * HEAD_DIM, (h + 1) * HEAD_DIM)
            o = os_ref[g, :, hs]
            on = (o * _rms_inv(o)) * gn_ref[:, hs]
            gr = g_ref[g, :, hs]
            o_ref[g, :, hs] = (on * (gr * jax.nn.sigmoid(gr))).astype(o_ref.dtype)


def _hgrn(P3, lb, gnorm):
    B, Lp, _ = P3.shape
    R = SEQ_TILE
    W = MIX_WIDTH
    G = _largest_tile(B, HGRN_GROUP)
    lb = lb.reshape(1, W).astype(F32)
    vec = pl.BlockSpec((1, W), lambda b, t: (0, 0))
    c0 = COL_A // W
    return pl.pallas_call(
        _hgrn_body,
        grid=(B // G, Lp // R),
        in_specs=[pl.BlockSpec((G, R, W), functools.partial(lambda b, t, c: (b, t, c), c=c0 + k))
                  for k in range(4)] + [vec, vec, vec, vec],
        out_specs=pl.BlockSpec((G, R, W), lambda b, t: (b, t, 0)),
        out_shape=jax.ShapeDtypeStruct((B, Lp, W), BF16),
        scratch_shapes=[pltpu.VMEM((G, N_HEADS, HEAD_DIM, HEAD_DIM), F32)]
        + [pltpu.VMEM((G, R, W), F32)] * 4,
        compiler_params=_cparams("parallel", "arbitrary"),
        name="hgrn",
    )(P3, P3, P3, P3, jnp.log(lb), jnp.log1p(-lb), 1.0 - lb, gnorm.reshape(1, W).astype(F32))


def _conv_body(b_ref, c_ref, u_ref, w_ref, o_ref, carry_ref):
    @pl.when(pl.program_id(1) == 0)
    def _():
        carry_ref[...] = jnp.zeros_like(carry_ref)

    zc = c_ref[...] * u_ref[...]
    R = zc.shape[0]
    row = lax.broadcasted_iota(I32, zc.shape, 0)
    last = carry_ref[7:8, :]
    last2 = carry_ref[6:7, :]
    z1 = jnp.where(row == 0, last, pltpu.roll(zc, 1, axis=0))
    z2 = jnp.where(row == 0, last2, jnp.where(row == 1, last, pltpu.roll(zc, 2, axis=0)))
    y = w_ref[0:1, :] * zc + w_ref[1:2, :] * z1 + w_ref[2:3, :] * z2
    o_ref[...] = (b_ref[...] * y).astype(o_ref.dtype)
    carry_ref[...] = zc[R - 8:, :]


def _conv(P3, conv_w):
    B, Lp, _ = P3.shape
    R = _largest_multiple(Lp, SEQ_TILE, CONV_ROWS_MAX)
    W = MIX_WIDTH
    c0 = COL_B // W
    w8 = jnp.zeros((8, W), F32).at[:CONV_W].set(conv_w.astype(F32))
    return pl.pallas_call(
        _conv_body,
        grid=(B, Lp // R),
        in_specs=[pl.BlockSpec((None, R, W), functools.partial(lambda b, t, c: (b, t, c), c=c0 + k))
                  for k in range(3)] + [pl.BlockSpec((8, W), lambda b, t: (0, 0))],
        out_specs=pl.BlockSpec((None, R, W), lambda b, t: (b, t, 0)),
        out_shape=jax.ShapeDtypeStruct((B, Lp, W), BF16),
        scratch_shapes=[pltpu.VMEM((8, W), F32)],
        compiler_params=_cparams("parallel", "arbitrary"),
        name="conv",
    )(P3, P3, P3, w8)


def _prep_body(cq_ref, ck_ref, cv_ref, dq_ref, dk_ref, dv_ref, kiw_ref, qi_ref,
               cqg_ref, ckg_ref, dqg_ref, dkg_ref,
               cqn_ref, ckn_ref, cvb_ref, dqn_ref, dkn_ref, dvb_ref, qib_ref, kk_ref, wq_ref):
    lane = lax.broadcasted_iota(I32, (1, HEAD_DIM), 1)
    lo = lane < HALF

    def half_norm(x, g, scale):
        sq = x * x
        ms_lo = jnp.sum(jnp.where(lo, sq, 0.0), axis=-1, keepdims=True) * (1.0 / HALF)
        ms_hi = jnp.sum(jnp.where(lo, 0.0, sq), axis=-1, keepdims=True) * (1.0 / HALF)
        inv = jnp.where(lo, lax.rsqrt(ms_lo + EPS), lax.rsqrt(ms_hi + EPS))
        return ((x * inv) * g) * scale

    for h in range(N_HEADS):
        hs = slice(h * HEAD_DIM, (h + 1) * HEAD_DIM)
        cqn_ref[:, hs] = half_norm(cq_ref[:, hs], cqg_ref[...], HALF ** -0.5 * LOG2E).astype(BF16)
        ckn_ref[:, hs] = half_norm(ck_ref[:, hs], ckg_ref[...], 1.0).astype(BF16)
        x = dq_ref[:, hs]
        dqn_ref[:, hs] = (((x * _rms_inv(x)) * dqg_ref[...]) * (HEAD_DIM ** -0.5 * LOG2E)).astype(BF16)
    x = dk_ref[...]
    dkn_ref[...] = ((x * _rms_inv(x)) * dkg_ref[...]).astype(BF16)
    for t in range(cv_ref.shape[0] // SEQ_TILE):
        rs = slice(t * SEQ_TILE, (t + 1) * SEQ_TILE)
        for h in range(N_HEADS):
            hs = slice(h * HEAD_DIM, (h + 1) * HEAD_DIM)
            cvb_ref[t, hs, :] = cv_ref[rs, hs].T.astype(BF16)
        dvb_ref[t] = dv_ref[rs, :].T.astype(BF16)
    qib_ref[...] = (qi_ref[...] * (IDX_DIM ** -0.5)).astype(BF16)
    kiw = kiw_ref[...]
    swapped = pltpu.roll(kiw, HALF, axis=1)
    kk_ref[...] = jnp.where(lo, kiw, swapped).astype(BF16)
    wq_ref[...] = swapped * (IDX_HEADS ** -0.5)


def _prep(P, cq_g, ck_g, dq_g, dk_g, tm):
    T = P.shape[0]
    W = MIX_WIDTH
    H = HEAD_DIM

    def col(width, start):
        return pl.BlockSpec((tm, width), functools.partial(lambda i, c: (i, c), c=start // width))

    vec = pl.BlockSpec((1, H), lambda i: (0, 0))
    out = lambda width: pl.BlockSpec((tm, width), lambda i: (i, 0))
    tiles = lambda width: pl.BlockSpec((tm // SEQ_TILE, width, SEQ_TILE), lambda i: (i, 0, 0))
    return pl.pallas_call(
        _prep_body,
        grid=(T // tm,),
        in_specs=[col(W, COL_C), col(W, COL_C + W), col(W, COL_C + 2 * W), col(W, COL_DQ),
                  col(H, COL_DK), col(H, COL_DV), col(H, COL_KIW), col(IDX_HEADS * IDX_DIM, COL_QI),
                  vec, vec, vec, vec],
        out_specs=[out(W), out(W), tiles(W), out(W), out(H), tiles(H), out(IDX_HEADS * IDX_DIM), out(H), out(H)],
        out_shape=[jax.ShapeDtypeStruct((T, W), BF16)] * 2
        + [jax.ShapeDtypeStruct((T // SEQ_TILE, W, SEQ_TILE), BF16), jax.ShapeDtypeStruct((T, W), BF16),
           jax.ShapeDtypeStruct((T, H), BF16), jax.ShapeDtypeStruct((T // SEQ_TILE, H, SEQ_TILE), BF16)]
        + [jax.ShapeDtypeStruct((T, IDX_HEADS * IDX_DIM), BF16),
           jax.ShapeDtypeStruct((T, H), BF16), jax.ShapeDtypeStruct((T, H), F32)],
        compiler_params=_cparams("parallel"),
        name="prep",
    )(P, P, P, P, P, P, P, P,
      jnp.tile(cq_g.astype(F32), 2).reshape(1, H), jnp.tile(ck_g.astype(F32), 2).reshape(1, H),
      dq_g.astype(F32).reshape(1, H), dk_g.astype(F32).reshape(1, H))


SUBLANES = 8
KEY_CHUNK = 8


def _max8(x):
    return jnp.max(x.reshape(x.shape[0] // SUBLANES, SUBLANES, x.shape[1]), axis=0)


def _sum8(x):
    return jnp.sum(x.reshape(x.shape[0] // SUBLANES, SUBLANES, x.shape[1]), axis=0)


def _bit_transpose32(rows):
    a = list(rows)
    j = 16
    m = 0x0000FFFF
    while j:
        k = 0
        while k < 32:
            t = (a[k] ^ lax.shift_right_logical(a[k + j], jnp.int32(j))) & jnp.int32(m)
            a[k] = a[k] ^ t
            a[k + j] = a[k + j] ^ (t << j)
            k = (k + j + 1) & ~j
        j >>= 1
        m ^= (m << j) & 0xFFFFFFFF
    return a


def _for_tiles(n_tiles, body, chunk=KEY_CHUNK):
    n_chunks = n_tiles // chunk

    def chunked(c, carry):
        body(c * chunk, chunk)
        return carry

    lax.fori_loop(0, n_chunks, chunked, 0)
    base = n_chunks * chunk
    size = chunk // 2
    while size >= 1:
        take = (n_tiles - base) // size

        def part(_, carry, base=base, size=size):
            body(base, size)
            return carry

        lax.fori_loop(0, take, part, 0)
        base = base + take * size
        size //= 2


def _diff_body(q_ref, k_ref, vt_ref, near_ref, lam_ref, sub_ref, o_ref,
               q2_ref, s_ref, mx_ref, l_ref, acc_ref, *, out_scale, n_tiles):
    blk = pl.program_id(2)
    KT = SEQ_TILE
    H = HEAD_DIM
    QT = q_ref.shape[0]
    QB = QT // KT
    HP = q_ref.shape[1] // H
    W2 = 2 * QT
    lane = lax.broadcasted_iota(I32, (1, H), 1)
    for h in range(HP):
        q = q_ref[:, h * H:(h + 1) * H]
        zero = jnp.zeros_like(q)
        q2_ref[h, 0:QT, :] = jnp.where(lane < HALF, q, zero)
        q2_ref[h, QT:, :] = jnp.where(lane < HALF, zero, q)
    mx_ref[...] = jnp.full_like(mx_ref, MASKED)

    def put(j0, n, near_idx):
        k0 = pl.multiple_of(j0 * KT, KT)
        parts = []
        for h in range(HP):
            s = _dot_nt(k_ref[pl.ds(k0, n * KT), h * H:(h + 1) * H], q2_ref[h])
            if near_idx is None:
                parts.append(s)
            else:
                b = near_ref[h, near_idx]
                parts.append(s + jnp.concatenate([b, b], axis=1))
        s = jnp.concatenate(parts, axis=1)
        for t in range(n):
            s_ref[j0 + t] = s[t * KT:(t + 1) * KT, :]
        mx_ref[...] = jnp.maximum(mx_ref[...], _max8(s))

    first_near = QB * blk - 1
    n_far = jnp.maximum(first_near, 0)
    newest = jnp.minimum(QB * blk + QB - 1, n_tiles - 1)
    _for_tiles(n_far, lambda j0, n: put(j0, n, None))

    def near(j, carry):
        put(j, 1, j - first_near)
        return carry

    lax.fori_loop(n_far, newest + 1, near, 0)

    m = jnp.max(mx_ref[...], axis=0, keepdims=True)
    l_ref[...] = jnp.zeros_like(l_ref)
    acc_ref[...] = jnp.zeros_like(acc_ref)

    def pv(j0, n):
        p = jnp.exp2(s_ref[pl.ds(j0, n)].reshape(n * KT, HP * W2) - m)
        l_ref[...] += _sum8(p)
        pb = p.astype(BF16)
        for h in range(HP):
            acc = acc_ref[:, h * W2:(h + 1) * W2]
            for t in range(n):
                acc = acc + _dot(vt_ref[j0 + t, h * H:(h + 1) * H, :],
                                 pb[t * KT:(t + 1) * KT, h * W2:(h + 1) * W2])
            acc_ref[:, h * W2:(h + 1) * W2] = acc

    _for_tiles(newest + 1, pv)

    ot = acc_ref[...] * (1.0 / jnp.sum(l_ref[...], axis=0, keepdims=True))
    for h in range(HP):
        od = ot[:, h * W2:h * W2 + QT] - lam_ref[...] * ot[:, h * W2 + QT:(h + 1) * W2]
        inv = lax.rsqrt(jnp.mean(od * od, axis=0, keepdims=True) + EPS)
        o_ref[:, h * H:(h + 1) * H] = (((od * inv) * sub_ref[...]) * out_scale).T.astype(o_ref.dtype)


def _diff(cqn, ckn, cvt, near_t, lam, subln, out_scale):
    B, Lp, W = cqn.shape
    KT = SEQ_TILE
    H = HEAD_DIM
    nt = Lp // KT
    QB = near_t.shape[1] - 1
    QT = QB * KT
    HP = DIFF_HEADS_PER_STEP
    lanes = HP * 2 * QT
    return pl.pallas_call(
        functools.partial(_diff_body, out_scale=out_scale, n_tiles=nt),
        grid=(B, N_HEADS // HP, pl.cdiv(nt, QB)),
        in_specs=[
            pl.BlockSpec((None, QT, HP * H), lambda b, g, i: (b, i, g)),
            pl.BlockSpec((None, Lp, HP * H), lambda b, g, i: (b, 0, g)),
            pl.BlockSpec((None, nt, HP * H, KT), lambda b, g, i: (b, 0, g, 0)),
            pl.BlockSpec((HP, QB + 1, KT, QT), lambda b, g, i: (g, 0, 0, 0)),
            pl.BlockSpec((1, QT), lambda b, g, i: (0, 0)),
            pl.BlockSpec((H, QT), lambda b, g, i: (0, 0)),
        ],
        out_specs=pl.BlockSpec((None, QT, HP * H), lambda b, g, i: (b, i, g)),
        out_shape=jax.ShapeDtypeStruct((B, Lp, W), BF16),
        scratch_shapes=[pltpu.VMEM((HP, 2 * QT, H), BF16),
                        pltpu.VMEM((nt, KT, lanes), F32),
                        pltpu.VMEM((SUBLANES, lanes), F32),
                        pltpu.VMEM((SUBLANES, lanes), F32),
                        pltpu.VMEM((H, lanes), F32)],
        compiler_params=_cparams("parallel", "parallel", "arbitrary"),
        name="diff",
    )(cqn, ckn, cvt, near_t, lam, subln)


def _block_first_tile(blk, n_blocks, n_tiles, tiles_per_block):
    return jnp.maximum(n_tiles - tiles_per_block * (n_blocks - blk), 0)


def _dsa_body(qi_ref, kk_ref, wq_ref, q_ref, k_ref, vt_ref, near_ref, o_ref,
              qi2_ref, w_ref, key_ref, kq_ref, plane_ref, alive_ref, q4_ref, thr_ref, s_ref, mx_ref,
              l_ref, acc_ref, *, top_k, n_tiles):
    blk = pl.program_id(1)
    KT = SEQ_TILE
    H = HEAD_DIM
    QT = q_ref.shape[0]
    QB = QT // KT
    first = _block_first_tile(blk, pl.num_programs(1), n_tiles, QB)
    newest = first + QB - 1
    lane = lax.broadcasted_iota(I32, (1, H), 1)

    for p in range(IDX_HEADS // 2):
        x = qi_ref[:, p * H:(p + 1) * H]
        zero = jnp.zeros_like(x)
        qi2_ref[p, 0:QT, :] = jnp.where(lane < HALF, x, zero)
        qi2_ref[p, QT:, :] = jnp.where(lane < HALF, zero, x)
    w_ref[...] = wq_ref[...].T
    for h in range(N_HEADS):
        q4_ref[h * QT:(h + 1) * QT, :] = q_ref[:, h * H:(h + 1) * H]
    kq_ref[...] = lax.broadcasted_iota(I32, (KT, QT), 0) - lax.broadcasted_iota(I32, (KT, QT), 1)

    def index(j0, n):
        k0 = pl.multiple_of(j0 * KT, KT)
        kt = kk_ref[pl.ds(k0, n * KT), :]
        acc = jnp.zeros((n * KT, QT), F32)
        for p in range(IDX_HEADS // 2):
            s = jnp.maximum(_dot_nt(kt, qi2_ref[p]), 0.0)
            acc = acc + s[:, 0:QT] * w_ref[2 * p:2 * p + 1, :] + s[:, QT:] * w_ref[2 * p + 1:2 * p + 2, :]
        bits = pltpu.bitcast(acc, I32)
        key = bits ^ ((bits >> 31) & jnp.int32(0x7FFFFFFF))
        tiles = []
        for t in range(n):
            kt_ = key[t * KT:(t + 1) * KT, :]
            if t >= n - QB:
                kt_ = jnp.where(kq_ref[...] <= (first - (j0 + t)) * KT, kt_, jnp.int32(KEY_NEG_INF))
            key_ref[j0 + t] = kt_
            tiles.append(kt_)
        if n == 1:
            tiles.append(None)
        for u in range(len(tiles) // 2):
            rows = []
            for tile in tiles[2 * u:2 * u + 2]:
                for r in range(KT // SUBLANES):
                    if tile is None:
                        rows.append(jnp.full((SUBLANES, QT), INT_MIN, I32))
                    else:
                        rows.append(tile[r * SUBLANES:(r + 1) * SUBLANES, :])
            planes = _bit_transpose32(rows)
            planes[0] = ~planes[0]
            pair = j0 // 2 + u
            for b in range(32):
                plane_ref[pair, b] = planes[b]

    @pl.when(blk == 0)
    def _():
        plane_ref[...] = jnp.zeros_like(plane_ref)

    _for_tiles(newest + 1, index)

    n_pairs = plane_ref.shape[0]
    for tp in range(n_pairs):
        alive_ref[tp] = jnp.full((SUBLANES, QT), -1, I32)

    def bit_step(t, carry):
        above, thr = carry
        ones = [alive_ref[tp] & plane_ref[tp, t] for tp in range(n_pairs)]
        cnt = lax.population_count(ones[0])
        for tp in range(1, n_pairs):
            cnt = cnt + lax.population_count(ones[tp])
        hit = above + jnp.sum(cnt, axis=0, keepdims=True)
        take = hit >= top_k
        for tp in range(n_pairs):
            alive_ref[tp] = jnp.where(take, ones[tp], alive_ref[tp] ^ ones[tp])
        thr = thr | jnp.where(take, jnp.left_shift(jnp.int32(1), 31 - t), 0)
        return jnp.where(take, above, hit), thr

    zero8 = jnp.zeros((SUBLANES, QT), I32)
    _, thr_u = lax.fori_loop(0, 32, bit_step, (zero8, zero8))
    thr_ref[...] = thr_u ^ jnp.int32(INT_MIN)

    mx_ref[...] = jnp.full_like(mx_ref, MASKED)

    def put(j0, n, near_idx):
        k0 = pl.multiple_of(j0 * KT, KT)
        s = _dot_nt(k_ref[pl.ds(k0, n * KT), :], q4_ref[...])
        thr1 = thr_ref[0:1, :]
        mx = mx_ref[...]
        for t in range(n):
            sel = key_ref[j0 + t] >= thr1
            parts = []
            for h in range(N_HEADS):
                sh = s[t * KT:(t + 1) * KT, h * QT:(h + 1) * QT]
                if near_idx is not None:
                    sh = sh + near_ref[h, near_idx]
                parts.append(jnp.where(sel, sh, MASKED))
            row = jnp.concatenate(parts, axis=1)
            s_ref[j0 + t] = row
            mx = jnp.maximum(mx, _max8(row))
        mx_ref[...] = mx

    first_near = first - 1
    n_far = jnp.maximum(first_near, 0)
    _for_tiles(n_far, lambda j0, n: put(j0, n, None))

    def near(j, carry):
        put(j, 1, j - first_near)
        return carry

    lax.fori_loop(n_far, newest + 1, near, 0)

    m = jnp.max(mx_ref[...], axis=0, keepdims=True)
    l_ref[...] = jnp.zeros_like(l_ref)
    acc_ref[...] = jnp.zeros_like(acc_ref)

    def pv(j0, n):
        p = jnp.exp2(s_ref[pl.ds(j0, n)].reshape(n * KT, N_HEADS * QT) - m)
        l_ref[...] += _sum8(p)
        pb = p.astype(BF16)
        acc = acc_ref[...]
        for t in range(n):
            acc = acc + _dot(vt_ref[j0 + t], pb[t * KT:(t + 1) * KT, :])
        acc_ref[...] = acc

    _for_tiles(newest + 1, pv)

    ot = acc_ref[...] * (1.0 / jnp.sum(l_ref[...], axis=0, keepdims=True))
    for h in range(N_HEADS):
        o_ref[:, h * H:(h + 1) * H] = ot[:, h * QT:(h + 1) * QT].T.astype(o_ref.dtype)


def _dsa(qib, kk, wq, dqn, dkn, dvt, near_t, top_k):
    B, Lp, H = dkn.shape
    W = dqn.shape[1]
    KT = SEQ_TILE
    NI = IDX_HEADS * IDX_DIM
    nt = Lp // KT
    QB = near_t.shape[1] - 1
    QT = QB * KT
    n_blocks = pl.cdiv(nt, QB)

    def rows(width):
        return pl.BlockSpec(
            (pl.Element(QT), pl.Element(width)),
            lambda b, i: (pl.multiple_of(b * Lp + _block_first_tile(i, n_blocks, nt, QB) * KT, KT), 0))

    seq = lambda width: pl.BlockSpec((None, Lp, width), lambda b, i: (b, 0, 0))
    n_pairs = (nt + 1) // 2
    return pl.pallas_call(
        functools.partial(_dsa_body, top_k=top_k, n_tiles=nt),
        grid=(B, n_blocks),
        in_specs=[rows(NI), seq(H), rows(H), rows(W), seq(H),
                  pl.BlockSpec((None, nt, H, KT), lambda b, i: (b, 0, 0, 0)),
                  pl.BlockSpec((N_HEADS, QB + 1, KT, QT), lambda b, i: (0, 0, 0, 0))],
        out_specs=rows(W),
        out_shape=jax.ShapeDtypeStruct((B * Lp, W), BF16),
        scratch_shapes=[
            pltpu.VMEM((IDX_HEADS // 2, 2 * QT, H), BF16),
            pltpu.VMEM((H, QT), F32),
            pltpu.VMEM((nt, KT, QT), I32),
            pltpu.VMEM((KT, QT), I32),
            pltpu.VMEM((n_pairs, 32, SUBLANES, QT), I32),
            pltpu.VMEM((n_pairs, SUBLANES, QT), I32),
            pltpu.VMEM((N_HEADS * QT, H), BF16),
            pltpu.VMEM((SUBLANES, QT), I32),
            pltpu.VMEM((nt, KT, N_HEADS * QT), F32),
            pltpu.VMEM((SUBLANES, N_HEADS * QT), F32),
            pltpu.VMEM((SUBLANES, N_HEADS * QT), F32),
            pltpu.VMEM((H, N_HEADS * QT), F32),
        ],
        compiler_params=_cparams("parallel", "arbitrary"),
        name="dsa",
    )(qib, kk, wq, dqn, dkn, dvt, near_t)


def _merge_body(h_ref, xn_ref, b0_ref, b1_ref, b2_ref, b3_ref, g0_ref, g1_ref, g2_ref, g3_ref,
                wb_ref, wo_ref, o_ref):
    @pl.when(pl.program_id(1) == 0)
    def _():
        o_ref[...] = h_ref[...]

    xn = xn_ref[...]
    merged = None
    for m, (b_ref, g_ref) in enumerate(((b0_ref, g0_ref), (b1_ref, g1_ref), (b2_ref, g2_ref), (b3_ref, g3_ref))):
        term = jax.nn.sigmoid(_dot(xn, g_ref[...])) * _dot(b_ref[...], wb_ref[m])
        merged = term if merged is None else merged + term
    o_ref[...] += _dot(merged.astype(BF16), wo_ref[...])


def _merge(h, xn, branches, w_gate, w_branch, w_out, layer, tm, tn):
    T, D = h.shape
    W = MIX_WIDTH
    nn = D // tn
    row = lambda width: pl.BlockSpec((tm, width), lambda i, n: (i, 0))
    gate = lambda m: pl.BlockSpec((D, tn), functools.partial(lambda i, n, m: (0, m * nn + n), m=m))
    return pl.pallas_call(
        _merge_body,
        grid=(T // tm, nn),
        in_specs=[row(D), row(D), row(W), row(W), row(W), row(W),
                  gate(0), gate(1), gate(2), gate(3),
                  pl.BlockSpec((None, N_BRANCH, W, tn), lambda i, n: (layer, 0, 0, n)),
                  pl.BlockSpec((None, tn, D), lambda i, n: (layer, n, 0))],
        out_specs=row(D),
        out_shape=jax.ShapeDtypeStruct((T, D), F32),
        compiler_params=_cparams("parallel", "arbitrary"),
        name="merge",
    )(h, xn, *branches, w_gate, w_gate, w_gate, w_gate, w_branch, w_out)


def _rel_bucket(n):
    max_exact = N_BUCKETS // 2
    nf = jnp.maximum(n, 1).astype(F32)
    large = max_exact + (jnp.log(nf / max_exact) / math.log(MAX_DISTANCE / max_exact)
                         * (N_BUCKETS - max_exact)).astype(I32)
    large = jnp.minimum(large, N_BUCKETS - 1)
    return jnp.where(n < max_exact, n, large)


def _bias_tiles(table, query_tiles):
    KT = SEQ_TILE
    kl = jnp.arange(KT, dtype=I32)[:, None]
    ql = jnp.arange(query_tiles * KT, dtype=I32)[None, :]
    tbl = table.astype(F32).T

    def lookup(bucket):
        out = jnp.zeros((tbl.shape[0],) + bucket.shape, F32)
        for b in range(N_BUCKETS):
            out = jnp.where(bucket[None] == b, tbl[:, b][:, None, None], out)
        return out

    far = tbl[:, N_BUCKETS - 1][:, None, None]
    tiles = []
    for t in range(query_tiles + 1):
        d = ql - kl + (1 - t) * KT
        bias = (lookup(_rel_bucket(jnp.maximum(d, 0))) - far) * LOG2E
        tiles.append(jnp.where(d >= 0, bias, MASKED))
    return jnp.stack(tiles, axis=1)


def _pack_w_in(w):
    offs = np.cumsum([0, 512, 512, 512, 512, 512, 512, 512, 512, 512, 512, 512, 128, 128,
                      IDX_HEADS * IDX_DIM, IDX_DIM, IDX_HEADS]).tolist()
    seg = lambda a, b: w[:, offs[a]:offs[b]]
    D = w.shape[0]
    used = COL_KIW + IDX_DIM + IDX_HEADS
    main = jnp.concatenate([
        seg(0, 4),
        seg(13, 14),
        seg(4, 7),
        seg(7, 10),
        seg(10, 11),
        seg(11, 12),
        seg(12, 13),
        seg(14, 16),
        jnp.zeros((D, P_COLS - used), w.dtype),
    ], axis=1).astype(BF16)
    gates = w[:, offs[16]:].astype(BF16)
    return main, gates


def _largest_tile(n, cap):
    t = cap
    while n % t:
        t //= 2
    return t


def _largest_multiple(n, unit, cap):
    return max(t for t in range(unit, cap + 1, unit) if n % t == 0)


def kernel(x, meta_tokens, rel_bias, ffn1_norm, ffn1_w_gu, ffn1_w_down, mix_norm, w_in, hgrn_lb, hgrn_gnorm, conv_w, diff_q_norm, diff_k_norm, diff_lambda, diff_subln, dsa_q_norm, dsa_k_norm, w_branch, w_out, ffn2_norm, ffn2_w_gu, ffn2_w_down):
    B, S, D = x.shape
    depth = w_in.shape[0]
    L = S + N_META
    Lp = -(-L // SEQ_TILE) * SEQ_TILE
    T = B * Lp
    top_k = min(TOPK_MAX, S // 4)
    tm = _largest_tile(T, 512)
    tm_wide = _largest_tile(T, 1024)
    tm_ffn = _largest_multiple(T, SEQ_TILE, FFN_ROWS_MAX)
    d_ff = ffn1_w_down.shape[1]
    tf = _largest_tile(d_ff, 512)

    meta = jnp.broadcast_to(meta_tokens.astype(x.dtype)[None], (B, N_META, D))
    h = jnp.concatenate([meta, x, jnp.zeros((B, Lp - L, D), x.dtype)], axis=1).reshape(T, D)

    lbs = jnp.cumsum(jax.nn.softmax(hgrn_lb.astype(F32), axis=0), axis=0)
    lbs = lbs - lbs[0:1]
    near_c = _bias_tiles(rel_bias[:, :N_HEADS], DIFF_QUERY_TILES)
    near_d = _bias_tiles(rel_bias[:, N_HEADS:], DSA_QUERY_TILES)
    diff_qt = DIFF_QUERY_TILES * SEQ_TILE

    w1_gu, w1_down, w2_gu, w2_down, wb, wo = (
        _to_bf16(w) for w in (ffn1_w_gu, ffn1_w_down, ffn2_w_gu, ffn2_w_down, w_branch, w_out))

    for l in range(depth):
        h = _ffn(h, ffn1_norm[l], w1_gu, w1_down, l, tm_ffn, tf)

        w_main, w_gate = _pack_w_in(w_in[l])
        xn, P = _proj(h, mix_norm[l], w_main, tm_wide, 1024)
        P3 = P.reshape(B, Lp, P_COLS)
        br_a = _hgrn(P3, lbs[l], hgrn_gnorm[l])
        br_b = _conv(P3, conv_w[l])
        cqn, ckn, cvt, dqn, dkn, dvt, qib, kk, wq = _prep(
            P, diff_q_norm[l], diff_k_norm[l], dsa_q_norm[l], dsa_k_norm[l], tm)
        r3 = lambda a: a.reshape(B, Lp, a.shape[-1])
        lp = diff_lambda[l].astype(F32)
        lam_init = 0.8 - 0.6 * math.exp(-0.3 * l)
        lam = jnp.exp(jnp.sum(lp[0] * lp[1])) - jnp.exp(jnp.sum(lp[2] * lp[3])) + lam_init
        tiles = lambda a: a.reshape(B, Lp // SEQ_TILE, a.shape[-2], SEQ_TILE)
        br_c = _diff(r3(cqn), r3(ckn), tiles(cvt), near_c,
                     jnp.broadcast_to(lam, (1, diff_qt)).astype(F32),
                     jnp.broadcast_to(diff_subln[l].astype(F32)[:, None], (HEAD_DIM, diff_qt)),
                     1.0 - lam_init)
        br_d = _dsa(qib, r3(kk), wq, dqn, r3(dkn), tiles(dvt), near_d, top_k)
        branches = [a.reshape(T, MIX_WIDTH) for a in (br_a, br_b, br_c, br_d)]
        h = _merge(h, xn, branches, w_gate, wb, wo, l, tm, 512)

        last = l == depth - 1
        h = _ffn(h, ffn2_norm[l], w2_gu, w2_down, l,
                 _largest_multiple(S, SUBLANES, FFN_ROWS_MAX) if last else tm_ffn, tf,
                 keep=(B, Lp, N_META, S) if last else None)

    return h.reshape(B, S, D)
```

```python
import functools
import math

import jax
import jax.numpy as jnp
import numpy as np
from jax import lax
from jax.experimental import pallas as pl
from jax.experimental.pallas import tpu as pltpu

F32 = jnp.float32
BF16 = jnp.bfloat16
I32 = jnp.int32

EPS = 1e-6
N_META = 16
HEAD_DIM = 128
HALF = 64
N_HEADS = 4
MIX_WIDTH = N_HEADS * HEAD_DIM
IDX_HEADS = 16
IDX_DIM = 64
TOPK_MAX = 256
N_BRANCH = 4
N_BUCKETS = 32
MAX_DISTANCE = 128
CONV_W = 3
SEQ_TILE = 128
HGRN_CHUNK = 8
HGRN_GROUP = 8
DIFF_QUERY_TILES = 1
DIFF_HEADS_PER_STEP = 4
DSA_QUERY_TILES = 2
CONV_ROWS_MAX = 1408
CAST_BLOCK_BYTES = 8 * 1024 * 1024
FFN_ROWS_MAX = 1024
MASKED = -1e30
LOG2E = math.log2(math.e)
INT_MIN = -(2 ** 31)
KEY_NEG_INF = -2139095041
VMEM_LIMIT_BYTES = 56 * 1024 * 1024

COL_A = 0
COL_QI = 2048
COL_B = 3072
COL_C = 4608
COL_DQ = 6144
COL_DK = 6656
COL_DV = 6784
COL_KIW = 6912
P_COLS = 7168


def _cparams(*sem):
    return pltpu.CompilerParams(dimension_semantics=sem, vmem_limit_bytes=VMEM_LIMIT_BYTES)


def _dot(a, b):
    return jnp.dot(a, b, preferred_element_type=F32)


def _dot_nt(a, b):
    return lax.dot_general(a, b, (((1,), (1,)), ((), ())), preferred_element_type=F32)


def _dot_tn(a, b):
    return lax.dot_general(a, b, (((0,), (0,)), ((), ())), preferred_element_type=F32)


def _rms_inv(x):
    return lax.rsqrt(jnp.mean(x * x, axis=-1, keepdims=True) + EPS)


def _cast_body(x_ref, o_ref):
    o_ref[...] = x_ref[...].astype(o_ref.dtype)


def _to_bf16(w):
    cols = w.shape[-1]
    w2 = w.reshape(-1, cols)
    n_rows = w2.shape[0]
    rows = _largest_multiple(n_rows, SUBLANES, max(SUBLANES, CAST_BLOCK_BYTES // (4 * cols)))
    out = pl.pallas_call(
        _cast_body,
        grid=(n_rows // rows,),
        in_specs=[pl.BlockSpec((rows, cols), lambda i: (i, 0))],
        out_specs=pl.BlockSpec((rows, cols), lambda i: (i, 0)),
        out_shape=jax.ShapeDtypeStruct((n_rows, cols), BF16),
        compiler_params=_cparams("parallel"),
        name="cast",
    )(w2)
    return out.reshape(w.shape)


def _ffn_body(h_ref, g_ref, wg_ref, wu_ref, wd_ref, o_ref, xn_ref):
    j = pl.program_id(1)

    @pl.when(j == 0)
    def _():
        x = h_ref[...]
        xn_ref[...] = ((x * _rms_inv(x)) * g_ref[...]).astype(BF16)
        o_ref[...] = x

    xn = xn_ref[...]
    g = _dot(xn, wg_ref[...])
    u = _dot(xn, wu_ref[...])
    a = (g * jax.nn.sigmoid(g) * (0.5 * u)).astype(BF16)
    o_ref[...] += _dot(a, wd_ref[...])


def _ffn(h, gain, w_gu, w_down, layer, tm, tf, keep=None):
    T, D = h.shape
    F = w_down.shape[1]
    nf = F // tf
    if keep is None:
        n_row_tiles = T // tm
        row_in = pl.BlockSpec((tm, D), lambda i, j: (i, 0))
        t_out = T
    else:
        B, Lp, first, n_rows = keep
        per_seq = n_rows // tm
        n_row_tiles = B * per_seq
        row_in = pl.BlockSpec((pl.Element(tm), pl.Element(D)),
                              lambda i, j: (pl.multiple_of(
                                  (i // per_seq) * Lp + first + (i % per_seq) * tm, SUBLANES), 0))
        t_out = B * n_rows
    return pl.pallas_call(
        _ffn_body,
        grid=(n_row_tiles, nf),
        in_specs=[
            row_in,
            pl.BlockSpec((1, D), lambda i, j: (0, 0)),
            pl.BlockSpec((None, D, tf), lambda i, j: (layer, 0, j)),
            pl.BlockSpec((None, D, tf), lambda i, j: (layer, 0, j + nf)),
            pl.BlockSpec((None, tf, D), lambda i, j: (layer, j, 0)),
        ],
        out_specs=pl.BlockSpec((tm, D), lambda i, j: (i, 0)),
        out_shape=jax.ShapeDtypeStruct((t_out, D), F32),
        scratch_shapes=[pltpu.VMEM((tm, D), BF16)],
        compiler_params=_cparams("parallel", "arbitrary"),
        name="ffn",
    )(h, gain.reshape(1, D), w_gu, w_gu, w_down)


def _proj_body(h_ref, g_ref, w_ref, p_ref, xn_ref):
    @pl.when(pl.program_id(1) == 0)
    def _():
        x = h_ref[...]
        xn_ref[...] = ((x * _rms_inv(x)) * g_ref[...]).astype(BF16)

    p_ref[...] = _dot(xn_ref[...], w_ref[...])


def _proj(h, gain, w, tm, tn):
    T, D = h.shape
    N = w.shape[1]
    return pl.pallas_call(
        _proj_body,
        grid=(T // tm, N // tn),
        in_specs=[
            pl.BlockSpec((tm, D), lambda i, j: (i, 0)),
            pl.BlockSpec((1, D), lambda i, j: (0, 0)),
            pl.BlockSpec((D, tn), lambda i, j: (0, j)),
        ],
        out_specs=pl.BlockSpec((tm, tn), lambda i, j: (i, j)),
        out_shape=jax.ShapeDtypeStruct((T, N), F32),
        scratch_shapes=[pltpu.VMEM((tm, D), BF16)],
        compiler_params=_cparams("parallel", "arbitrary"),
        name="proj",
    )(h, gain.reshape(1, D), w)


def _hgrn_body(q_ref, f_ref, i_ref, g_ref, loglb_ref, log1mlb_ref, omlb_ref, gn_ref, o_ref,
               st_ref, qs_ref, ks_ref, bs_ref, os_ref):
    G, R, _ = q_ref.shape
    C = HGRN_CHUNK

    @pl.when(pl.program_id(1) == 0)
    def _():
        st_ref[...] = jnp.zeros_like(st_ref)

    row = lax.broadcasted_iota(I32, (R, MIX_WIDTH), 0) % C
    for g in range(G):
        z = f_ref[g]
        log_sig = jnp.minimum(z, 0.0) - jnp.log(1.0 + jnp.exp(-jnp.abs(z)))
        y = log1mlb_ref[...] + log_sig
        a = loglb_ref[...]
        log_f = jnp.maximum(a, y) + jnp.log(1.0 + jnp.exp(-jnp.abs(a - y)))
        b = log_f * LOG2E
        for sh in [1 << e for e in range(C.bit_length() - 1)]:
            b = b + jnp.where(row >= sh, pltpu.roll(b, sh, axis=0), 0.0)
        bs_ref[g] = b
        ks_ref[g] = omlb_ref[...] * jax.nn.sigmoid(-z)
        qr = q_ref[g]
        qs_ref[g] = qr * jax.nn.sigmoid(qr)

    t_iota = lax.broadcasted_iota(I32, (C, 1), 0)

    def chunk(c, carry):
        r0 = pl.multiple_of(c * C, C)
        for g in range(G):
            for h in range(N_HEADS):
                hs = slice(h * HEAD_DIM, (h + 1) * HEAD_DIM)
                qc = qs_ref[g, pl.ds(r0, C), hs]
                kc = ks_ref[g, pl.ds(r0, C), hs]
                bc = bs_ref[g, pl.ds(r0, C), hs]
                vc = i_ref[g, pl.ds(r0, C), hs]
                st = st_ref[g, h]
                o = _dot_nt((qc * jnp.exp2(bc)).astype(BF16), st.astype(BF16))
                for s in range(C):
                    b_s = bc[s:s + 1, :]
                    k_s = kc[s:s + 1, :]
                    v_s = vc[s:s + 1, :]
                    e = jnp.exp2(bc - b_s)
                    col = jnp.sum(qc * k_s * e, axis=-1, keepdims=True)
                    col = jnp.where(t_iota >= s, col, 0.0)
                    o = o + col * v_s
                os_ref[g, pl.ds(r0, C), hs] = o
                b_last = bc[C - 1:C, :]
                kd = kc * jnp.exp2(b_last - bc)
                st_ref[g, h] = st * jnp.exp2(b_last) + _dot_tn(vc.astype(BF16), kd.astype(BF16))
        return carry

    lax.fori_loop(0, R // C, chunk, 0)

    for g in range(G):
        for h in range(N_HEADS):
            hs = slice(h * HEAD_DIM, (h + 1) * HEAD_DIM)
            o = os_ref[g, :, hs]
            on = (o * _rms_inv(o)) * gn_ref[:, hs]
            gr = g_ref[g, :, hs]
            o_ref[g, :, hs] = (on * (gr * jax.nn.sigmoid(gr))).astype(o_ref.dtype)


def _hgrn(P3, lb, gnorm):
    B, Lp, _ = P3.shape
    R = SEQ_TILE
    W = MIX_WIDTH
    G = _largest_tile(B, HGRN_GROUP)
    lb = lb.reshape(1, W).astype(F32)
    vec = pl.BlockSpec((1, W), lambda b, t: (0, 0))
    c0 = COL_A // W
    return pl.pallas_call(
        _hgrn_body,
        grid=(B // G, Lp // R),
        in_specs=[pl.BlockSpec((G, R, W), functools.partial(lambda b, t, c: (b, t, c), c=c0 + k))
                  for k in range(4)] + [vec, vec, vec, vec],
        out_specs=pl.BlockSpec((G, R, W), lambda b, t: (b, t, 0)),
        out_shape=jax.ShapeDtypeStruct((B, Lp, W), BF16),
        scratch_shapes=[pltpu.VMEM((G, N_HEADS, HEAD_DIM, HEAD_DIM), F32)]
        + [pltpu.VMEM((G, R, W), F32)] * 4,
        compiler_params=_cparams("parallel", "arbitrary"),
        name="hgrn",
    )(P3, P3, P3, P3, jnp.log(lb), jnp.log1p(-lb), 1.0 - lb, gnorm.reshape(1, W).astype(F32))


def _conv_body(b_ref, c_ref, u_ref, w_ref, o_ref, carry_ref):
    @pl.when(pl.program_id(1) == 0)
    def _():
        carry_ref[...] = jnp.zeros_like(carry_ref)

    zc = c_ref[...] * u_ref[...]
    R = zc.shape[0]
    row = lax.broadcasted_iota(I32, zc.shape, 0)
    last = carry_ref[7:8, :]
    last2 = carry_ref[6:7, :]
    z1 = jnp.where(row == 0, last, pltpu.roll(zc, 1, axis=0))
    z2 = jnp.where(row == 0, last2, jnp.where(row == 1, last, pltpu.roll(zc, 2, axis=0)))
    y = w_ref[0:1, :] * zc + w_ref[1:2, :] * z1 + w_ref[2:3, :] * z2
    o_ref[...] = (b_ref[...] * y).astype(o_ref.dtype)
    carry_ref[...] = zc[R - 8:, :]


def _conv(P3, conv_w):
    B, Lp, _ = P3.shape
    R = _largest_multiple(Lp, SEQ_TILE, CONV_ROWS_MAX)
    W = MIX_WIDTH
    c0 = COL_B // W
    w8 = jnp.zeros((8, W), F32).at[:CONV_W].set(conv_w.astype(F32))
    return pl.pallas_call(
        _conv_body,
        grid=(B, Lp // R),
        in_specs=[pl.BlockSpec((None, R, W), functools.partial(lambda b, t, c: (b, t, c), c=c0 + k))
                  for k in range(3)] + [pl.BlockSpec((8, W), lambda b, t: (0, 0))],
        out_specs=pl.BlockSpec((None, R, W), lambda b, t: (b, t, 0)),
        out_shape=jax.ShapeDtypeStruct((B, Lp, W), BF16),
        scratch_shapes=[pltpu.VMEM((8, W), F32)],
        compiler_params=_cparams("parallel", "arbitrary"),
        name="conv",
    )(P3, P3, P3, w8)


def _prep_body(cq_ref, ck_ref, cv_ref, dq_ref, dk_ref, dv_ref, kiw_ref, qi_ref,
               cqg_ref, ckg_ref, dqg_ref, dkg_ref,
               cqn_ref, ckn_ref, cvb_ref, dqn_ref, dkn_ref, dvb_ref, qib_ref, kk_ref, wq_ref):
    lane = lax.broadcasted_iota(I32, (1, HEAD_DIM), 1)
    lo = lane < HALF

    def half_norm(x, g, scale):
        sq = x * x
        ms_lo = jnp.sum(jnp.where(lo, sq, 0.0), axis=-1, keepdims=True) * (1.0 / HALF)
        ms_hi = jnp.sum(jnp.where(lo, 0.0, sq), axis=-1, keepdims=True) * (1.0 / HALF)
        inv = jnp.where(lo, lax.rsqrt(ms_lo + EPS), lax.rsqrt(ms_hi + EPS))
        return ((x * inv) * g) * scale

    for h in range(N_HEADS):
        hs = slice(h * HEAD_DIM, (h + 1) * HEAD_DIM)
        cqn_ref[:, hs] = half_norm(cq_ref[:, hs], cqg_ref[...], HALF ** -0.5 * LOG2E).astype(BF16)
        ckn_ref[:, hs] = half_norm(ck_ref[:, hs], ckg_ref[...], 1.0).astype(BF16)
        x = dq_ref[:, hs]
        dqn_ref[:, hs] = (((x * _rms_inv(x)) * dqg_ref[...]) * (HEAD_DIM ** -0.5 * LOG2E)).astype(BF16)
    x = dk_ref[...]
    dkn_ref[...] = ((x * _rms_inv(x)) * dkg_ref[...]).astype(BF16)
    for t in range(cv_ref.shape[0] // SEQ_TILE):
        rs = slice(t * SEQ_TILE, (t + 1) * SEQ_TILE)
        for h in range(N_HEADS):
            hs = slice(h * HEAD_DIM, (h + 1) * HEAD_DIM)
            cvb_ref[t, hs, :] = cv_ref[rs, hs].T.astype(BF16)
        dvb_ref[t] = dv_ref[rs, :].T.astype(BF16)
    qib_ref[...] = (qi_ref[...] * (IDX_DIM ** -0.5)).astype(BF16)
    kiw = kiw_ref[...]
    swapped = pltpu.roll(kiw, HALF, axis=1)
    kk_ref[...] = jnp.where(lo, kiw, swapped).astype(BF16)
    wq_ref[...] = swapped * (IDX_HEADS ** -0.5)


def _prep(P, cq_g, ck_g, dq_g, dk_g, tm):
    T = P.shape[0]
    W = MIX_WIDTH
    H = HEAD_DIM

    def col(width, start):
        return pl.BlockSpec((tm, width), functools.partial(lambda i, c: (i, c), c=start // width))

    vec = pl.BlockSpec((1, H), lambda i: (0, 0))
    out = lambda width: pl.BlockSpec((tm, width), lambda i: (i, 0))
    tiles = lambda width: pl.BlockSpec((tm // SEQ_TILE, width, SEQ_TILE), lambda i: (i, 0, 0))
    return pl.pallas_call(
        _prep_body,
        grid=(T // tm,),
        in_specs=[col(W, COL_C), col(W, COL_C + W), col(W, COL_C + 2 * W), col(W, COL_DQ),
                  col(H, COL_DK), col(H, COL_DV), col(H, COL_KIW), col(IDX_HEADS * IDX_DIM, COL_QI),
                  vec, vec, vec, vec],
        out_specs=[out(W), out(W), tiles(W), out(W), out(H), tiles(H), out(IDX_HEADS * IDX_DIM), out(H), out(H)],
        out_shape=[jax.ShapeDtypeStruct((T, W), BF16)] * 2
        + [jax.ShapeDtypeStruct((T // SEQ_TILE, W, SEQ_TILE), BF16), jax.ShapeDtypeStruct((T, W), BF16),
           jax.ShapeDtypeStruct((T, H), BF16), jax.ShapeDtypeStruct((T // SEQ_TILE, H, SEQ_TILE), BF16)]
        + [jax.ShapeDtypeStruct((T, IDX_HEADS * IDX_DIM), BF16),
           jax.ShapeDtypeStruct((T, H), BF16), jax.ShapeDtypeStruct((T, H), F32)],
        compiler_params=_cparams("parallel"),
        name="prep",
    )(P, P, P, P, P, P, P, P,
      jnp.tile(cq_g.astype(F32), 2).reshape(1, H), jnp.tile(ck_g.astype(F32), 2).reshape(1, H),
      dq_g.astype(F32).reshape(1, H), dk_g.astype(F32).reshape(1, H))


SUBLANES = 8
KEY_CHUNK = 8


def _max8(x):
    return jnp.max(x.reshape(x.shape[0] // SUBLANES, SUBLANES, x.shape[1]), axis=0)


def _sum8(x):
    return jnp.sum(x.reshape(x.shape[0] // SUBLANES, SUBLANES, x.shape[1]), axis=0)


def _bit_transpose32(rows):
    a = list(rows)
    j = 16
    m = 0x0000FFFF
    while j:
        k = 0
        while k < 32:
            t = (a[k] ^ lax.shift_right_logical(a[k + j], jnp.int32(j))) & jnp.int32(m)
            a[k] = a[k] ^ t
            a[k + j] = a[k + j] ^ (t << j)
            k = (k + j + 1) & ~j
        j >>= 1
        m ^= (m << j) & 0xFFFFFFFF
    return a


def _for_tiles(n_tiles, body, chunk=KEY_CHUNK):
    n_chunks = n_tiles // chunk

    def chunked(c, carry):
        body(c * chunk, chunk)
        return carry

    lax.fori_loop(0, n_chunks, chunked, 0)
    base = n_chunks * chunk
    size = chunk // 2
    while size >= 1:
        take = (n_tiles - base) // size

        def part(_, carry, base=base, size=size):
            body(base, size)
            return carry

        lax.fori_loop(0, take, part, 0)
        base = base + take * size
        size //= 2


def _diff_body(q_ref, k_ref, vt_ref, near_ref, lam_ref, sub_ref, o_ref,
               q2_ref, s_ref, mx_ref, l_ref, acc_ref, *, out_scale, n_tiles):
    blk = pl.program_id(2)
    KT = SEQ_TILE
    H = HEAD_DIM
    QT = q_ref.shape[0]
    QB = QT // KT
    HP = q_ref.shape[1] // H
    W2 = 2 * QT
    lane = lax.broadcasted_iota(I32, (1, H), 1)
    for h in range(HP):
        q = q_ref[:, h * H:(h + 1) * H]
        zero = jnp.zeros_like(q)
        q2_ref[h, 0:QT, :] = jnp.where(lane < HALF, q, zero)
        q2_ref[h, QT:, :] = jnp.where(lane < HALF, zero, q)
    mx_ref[...] = jnp.full_like(mx_ref, MASKED)

    def put(j0, n, near_idx):
        k0 = pl.multiple_of(j0 * KT, KT)
        parts = []
        for h in range(HP):
            s = _dot_nt(k_ref[pl.ds(k0, n * KT), h * H:(h + 1) * H], q2_ref[h])
            if near_idx is None:
                parts.append(s)
            else:
                b = near_ref[h, near_idx]
                parts.append(s + jnp.concatenate([b, b], axis=1))
        s = jnp.concatenate(parts, axis=1)
        for t in range(n):
            s_ref[j0 + t] = s[t * KT:(t + 1) * KT, :]
        mx_ref[...] = jnp.maximum(mx_ref[...], _max8(s))

    first_near = QB * blk - 1
    n_far = jnp.maximum(first_near, 0)
    newest = jnp.minimum(QB * blk + QB - 1, n_tiles - 1)
    _for_tiles(n_far, lambda j0, n: put(j0, n, None))

    def near(j, carry):
        put(j, 1, j - first_near)
        return carry

    lax.fori_loop(n_far, newest + 1, near, 0)

    m = jnp.max(mx_ref[...], axis=0, keepdims=True)
    l_ref[...] = jnp.zeros_like(l_ref)
    acc_ref[...] = jnp.zeros_like(acc_ref)

    def pv(j0, n):
        p = jnp.exp2(s_ref[pl.ds(j0, n)].reshape(n * KT, HP * W2) - m)
        l_ref[...] += _sum8(p)
        pb = p.astype(BF16)
        for h in range(HP):
            acc = acc_ref[:, h * W2:(h + 1) * W2]
            for t in range(n):
                acc = acc + _dot(vt_ref[j0 + t, h * H:(h + 1) * H, :],
                                 pb[t * KT:(t + 1) * KT, h * W2:(h + 1) * W2])
            acc_ref[:, h * W2:(h + 1) * W2] = acc

    _for_tiles(newest + 1, pv)

    ot = acc_ref[...] * (1.0 / jnp.sum(l_ref[...], axis=0, keepdims=True))
    for h in range(HP):
        od = ot[:, h * W2:h * W2 + QT] - lam_ref[...] * ot[:, h * W2 + QT:(h + 1) * W2]
        inv = lax.rsqrt(jnp.mean(od * od, axis=0, keepdims=True) + EPS)
        o_ref[:, h * H:(h + 1) * H] = (((od * inv) * sub_ref[...]) * out_scale).T.astype(o_ref.dtype)


def _diff(cqn, ckn, cvt, near_t, lam, subln, out_scale):
    B, Lp, W = cqn.shape
    KT = SEQ_TILE
    H = HEAD_DIM
    nt = Lp // KT
    QB = near_t.shape[1] - 1
    QT = QB * KT
    HP = DIFF_HEADS_PER_STEP
    lanes = HP * 2 * QT
    return pl.pallas_call(
        functools.partial(_diff_body, out_scale=out_scale, n_tiles=nt),
        grid=(B, N_HEADS // HP, pl.cdiv(nt, QB)),
        in_specs=[
            pl.BlockSpec((None, QT, HP * H), lambda b, g, i: (b, i, g)),
            pl.BlockSpec((None, Lp, HP * H), lambda b, g, i: (b, 0, g)),
            pl.BlockSpec((None, nt, HP * H, KT), lambda b, g, i: (b, 0, g, 0)),
            pl.BlockSpec((HP, QB + 1, KT, QT), lambda b, g, i: (g, 0, 0, 0)),
            pl.BlockSpec((1, QT), lambda b, g, i: (0, 0)),
            pl.BlockSpec((H, QT), lambda b, g, i: (0, 0)),
        ],
        out_specs=pl.BlockSpec((None, QT, HP * H), lambda b, g, i: (b, i, g)),
        out_shape=jax.ShapeDtypeStruct((B, Lp, W), BF16),
        scratch_shapes=[pltpu.VMEM((HP, 2 * QT, H), BF16),
                        pltpu.VMEM((nt, KT, lanes), F32),
                        pltpu.VMEM((SUBLANES, lanes), F32),
                        pltpu.VMEM((SUBLANES, lanes), F32),
                        pltpu.VMEM((H, lanes), F32)],
        compiler_params=_cparams("parallel", "parallel", "arbitrary"),
        name="diff",
    )(cqn, ckn, cvt, near_t, lam, subln)


def _block_first_tile(blk, n_blocks, n_tiles, tiles_per_block):
    return jnp.maximum(n_tiles - tiles_per_block * (n_blocks - blk), 0)


def _dsa_body(qi_ref, kk_ref, wq_ref, q_ref, k_ref, vt_ref, near_ref, o_ref,
              qi2_ref, w_ref, key_ref, kq_ref, plane_ref, alive_ref, q4_ref, thr_ref, s_ref, mx_ref,
              l_ref, acc_ref, *, top_k, n_tiles):
    blk = pl.program_id(1)
    KT = SEQ_TILE
    H = HEAD_DIM
    QT = q_ref.shape[0]
    QB = QT // KT
    first = _block_first_tile(blk, pl.num_programs(1), n_tiles, QB)
    newest = first + QB - 1
    lane = lax.broadcasted_iota(I32, (1, H), 1)

    for p in range(IDX_HEADS // 2):
        x = qi_ref[:, p * H:(p + 1) * H]
        zero = jnp.zeros_like(x)
        qi2_ref[p, 0:QT, :] = jnp.where(lane < HALF, x, zero)
        qi2_ref[p, QT:, :] = jnp.where(lane < HALF, zero, x)
    w_ref[...] = wq_ref[...].T
    for h in range(N_HEADS):
        q4_ref[h * QT:(h + 1) * QT, :] = q_ref[:, h * H:(h + 1) * H]
    kq_ref[...] = lax.broadcasted_iota(I32, (KT, QT), 0) - lax.broadcasted_iota(I32, (KT, QT), 1)

    def index(j0, n):
        k0 = pl.multiple_of(j0 * KT, KT)
        kt = kk_ref[pl.ds(k0, n * KT), :]
        acc = jnp.zeros((n * KT, QT), F32)
        for p in range(IDX_HEADS // 2):
            s = jnp.maximum(_dot_nt(kt, qi2_ref[p]), 0.0)
            acc = acc + s[:, 0:QT] * w_ref[2 * p:2 * p + 1, :] + s[:, QT:] * w_ref[2 * p + 1:2 * p + 2, :]
        bits = pltpu.bitcast(acc, I32)
        key = bits ^ ((bits >> 31) & jnp.int32(0x7FFFFFFF))
        tiles = []
        for t in range(n):
            kt_ = key[t * KT:(t + 1) * KT, :]
            if t >= n - QB:
                kt_ = jnp.where(kq_ref[...] <= (first - (j0 + t)) * KT, kt_, jnp.int32(KEY_NEG_INF))
            key_ref[j0 + t] = kt_
            tiles.append(kt_)
        if n == 1:
            tiles.append(None)
        for u in range(len(tiles) // 2):
            rows = []
            for tile in tiles[2 * u:2 * u + 2]:
                for r in range(KT // SUBLANES):
                    if tile is None:
                        rows.append(jnp.full((SUBLANES, QT), INT_MIN, I32))
                    else:
                        rows.append(tile[r * SUBLANES:(r + 1) * SUBLANES, :])
            planes = _bit_transpose32(rows)
            planes[0] = ~planes[0]
            pair = j0 // 2 + u
            for b in range(32):
                plane_ref[pair, b] = planes[b]

    @pl.when(blk == 0)
    def _():
        plane_ref[...] = jnp.zeros_like(plane_ref)

    _for_tiles(newest + 1, index)

    n_pairs = plane_ref.shape[0]
    for tp in range(n_pairs):
        alive_ref[tp] = jnp.full((SUBLANES, QT), -1, I32)

    def bit_step(t, carry):
        above, thr = carry
        ones = [alive_ref[tp] & plane_ref[tp, t] for tp in range(n_pairs)]
        cnt = lax.population_count(ones[0])
        for tp in range(1, n_pairs):
            cnt = cnt + lax.population_count(ones[tp])
        hit = above + jnp.sum(cnt, axis=0, keepdims=True)
        take = hit >= top_k
        for tp in range(n_pairs):
            alive_ref[tp] = jnp.where(take, ones[tp], alive_ref[tp] ^ ones[tp])
        thr = thr | jnp.where(take, jnp.left_shift(jnp.int32(1), 31 - t), 0)
        return jnp.where(take, above, hit), thr

    zero8 = jnp.zeros((SUBLANES, QT), I32)
    _, thr_u = lax.fori_loop(0, 32, bit_step, (zero8, zero8))
    thr_ref[...] = thr_u ^ jnp.int32(INT_MIN)

    mx_ref[...] = jnp.full_like(mx_ref, MASKED)

    def put(j0, n, near_idx):
        k0 = pl.multiple_of(j0 * KT, KT)
        s = _dot_nt(k_ref[pl.ds(k0, n * KT), :], q4_ref[...])
        thr1 = thr_ref[0:1, :]
        mx = mx_ref[...]
        for t in range(n):
            sel = key_ref[j0 + t] >= thr1
            parts = []
            for h in range(N_HEADS):
                sh = s[t * KT:(t + 1) * KT, h * QT:(h + 1) * QT]
                if near_idx is not None:
                    sh = sh + near_ref[h, near_idx]
                parts.append(jnp.where(sel, sh, MASKED))
            row = jnp.concatenate(parts, axis=1)
            s_ref[j0 + t] = row
            mx = jnp.maximum(mx, _max8(row))
        mx_ref[...] = mx

    first_near = first - 1
    n_far = jnp.maximum(first_near, 0)
    _for_tiles(n_far, lambda j0, n: put(j0, n, None))

    def near(j, carry):
        put(j, 1, j - first_near)
        return carry

    lax.fori_loop(n_far, newest + 1, near, 0)

    m = jnp.max(mx_ref[...], axis=0, keepdims=True)
    l_ref[...] = jnp.zeros_like(l_ref)
    acc_ref[...] = jnp.zeros_like(acc_ref)

    def pv(j0, n):
        p = jnp.exp2(s_ref[pl.ds(j0, n)].reshape(n * KT, N_HEADS * QT) - m)
        l_ref[...] += _sum8(p)
        pb = p.astype(BF16)
        acc = acc_ref[...]
        for t in range(n):
            acc = acc + _dot(vt_ref[j0 + t], pb[t * KT:(t + 1) * KT, :])
        acc_ref[...] = acc

    _for_tiles(newest + 1, pv)

    ot = acc_ref[...] * (1.0 / jnp.sum(l_ref[...], axis=0, keepdims=True))
    for h in range(N_HEADS):
        o_ref[:, h * H:(h + 1) * H] = ot[:, h * QT:(h + 1) * QT].T.astype(o_ref.dtype)


def _dsa(qib, kk, wq, dqn, dkn, dvt, near_t, top_k):
    B, Lp, H = dkn.shape
    W = dqn.shape[1]
    KT = SEQ_TILE
    NI = IDX_HEADS * IDX_DIM
    nt = Lp // KT
    QB = near_t.shape[1] - 1
    QT = QB * KT
    n_blocks = pl.cdiv(nt, QB)

    def rows(width):
        return pl.BlockSpec(
            (pl.Element(QT), pl.Element(width)),
            lambda b, i: (pl.multiple_of(b * Lp + _block_first_tile(i, n_blocks, nt, QB) * KT, KT), 0))

    seq = lambda width: pl.BlockSpec((None, Lp, width), lambda b, i: (b, 0, 0))
    n_pairs = (nt + 1) // 2
    return pl.pallas_call(
        functools.partial(_dsa_body, top_k=top_k, n_tiles=nt),
        grid=(B, n_blocks),
        in_specs=[rows(NI), seq(H), rows(H), rows(W), seq(H),
                  pl.BlockSpec((None, nt, H, KT), lambda b, i: (b, 0, 0, 0)),
                  pl.BlockSpec((N_HEADS, QB + 1, KT, QT), lambda b, i: (0, 0, 0, 0))],
        out_specs=rows(W),
        out_shape=jax.ShapeDtypeStruct((B * Lp, W), BF16),
        scratch_shapes=[
            pltpu.VMEM((IDX_HEADS // 2, 2 * QT, H), BF16),
            pltpu.VMEM((H, QT), F32),
            pltpu.VMEM((nt, KT, QT), I32),
            pltpu.VMEM((KT, QT), I32),
            pltpu.VMEM((n_pairs, 32, SUBLANES, QT), I32),
            pltpu.VMEM((n_pairs, SUBLANES, QT), I32),
            pltpu.VMEM((N_HEADS * QT, H), BF16),
            pltpu.VMEM((SUBLANES, QT), I32),
            pltpu.VMEM((nt, KT, N_HEADS * QT), F32),
            pltpu.VMEM((SUBLANES, N_HEADS * QT), F32),
            pltpu.VMEM((SUBLANES, N_HEADS * QT), F32),
            pltpu.VMEM((H, N_HEADS * QT), F32),
        ],
        compiler_params=_cparams("parallel", "arbitrary"),
        name="dsa",
    )(qib, kk, wq, dqn, dkn, dvt, near_t)


def _merge_body(h_ref, gain_ref, b0_ref, b1_ref, b2_ref, b3_ref, g0_ref, g1_ref, g2_ref, g3_ref,
                wb_ref, wo_ref, o_ref, xn_ref):
    @pl.when(pl.program_id(1) == 0)
    def _():
        x = h_ref[...]
        xn_ref[...] = ((x * _rms_inv(x)) * gain_ref[...]).astype(BF16)
        o_ref[...] = x

    xn = xn_ref[...]
    merged = None
    for m, (b_ref, g_ref) in enumerate(((b0_ref, g0_ref), (b1_ref, g1_ref), (b2_ref, g2_ref), (b3_ref, g3_ref))):
        term = jax.nn.sigmoid(_dot(xn, g_ref[...])) * _dot(b_ref[...], wb_ref[m])
        merged = term if merged is None else merged + term
    o_ref[...] += _dot(merged.astype(BF16), wo_ref[...])


def _merge(h, gain, branches, w_gate, w_branch, w_out, layer, tm, tn):
    T, D = h.shape
    W = MIX_WIDTH
    nn = D // tn
    row = lambda width: pl.BlockSpec((tm, width), lambda i, n: (i, 0))
    gate = lambda m: pl.BlockSpec((D, tn), functools.partial(lambda i, n, m: (0, m * nn + n), m=m))
    return pl.pallas_call(
        _merge_body,
        grid=(T // tm, nn),
        in_specs=[row(D), pl.BlockSpec((1, D), lambda i, n: (0, 0)), row(W), row(W), row(W), row(W),
                  gate(0), gate(1), gate(2), gate(3),
                  pl.BlockSpec((None, N_BRANCH, W, tn), lambda i, n: (layer, 0, 0, n)),
                  pl.BlockSpec((None, tn, D), lambda i, n: (layer, n, 0))],
        out_specs=row(D),
        out_shape=jax.ShapeDtypeStruct((T, D), F32),
        scratch_shapes=[pltpu.VMEM((tm, D), BF16)],
        compiler_params=_cparams("parallel", "arbitrary"),
        name="merge",
    )(h, gain.reshape(1, D), *branches, w_gate, w_gate, w_gate, w_gate, w_branch, w_out)


def _rel_bucket(n):
    max_exact = N_BUCKETS // 2
    nf = jnp.maximum(n, 1).astype(F32)
    large = max_exact + (jnp.log(nf / max_exact) / math.log(MAX_DISTANCE / max_exact)
                         * (N_BUCKETS - max_exact)).astype(I32)
    large = jnp.minimum(large, N_BUCKETS - 1)
    return jnp.where(n < max_exact, n, large)


def _bias_tiles(table, query_tiles):
    KT = SEQ_TILE
    kl = jnp.arange(KT, dtype=I32)[:, None]
    ql = jnp.arange(query_tiles * KT, dtype=I32)[None, :]
    tbl = table.astype(F32).T

    def lookup(bucket):
        out = jnp.zeros((tbl.shape[0],) + bucket.shape, F32)
        for b in range(N_BUCKETS):
            out = jnp.where(bucket[None] == b, tbl[:, b][:, None, None], out)
        return out

    far = tbl[:, N_BUCKETS - 1][:, None, None]
    tiles = []
    for t in range(query_tiles + 1):
        d = ql - kl + (1 - t) * KT
        bias = (lookup(_rel_bucket(jnp.maximum(d, 0))) - far) * LOG2E
        tiles.append(jnp.where(d >= 0, bias, MASKED))
    return jnp.stack(tiles, axis=1)


def _pack_w_in(w):
    offs = np.cumsum([0, 512, 512, 512, 512, 512, 512, 512, 512, 512, 512, 512, 128, 128,
                      IDX_HEADS * IDX_DIM, IDX_DIM, IDX_HEADS]).tolist()
    seg = lambda a, b: w[:, offs[a]:offs[b]]
    D = w.shape[0]
    used = COL_KIW + IDX_DIM + IDX_HEADS
    main = jnp.concatenate([
        seg(0, 4),
        seg(13, 14),
        seg(4, 7),
        seg(7, 10),
        seg(10, 11),
        seg(11, 12),
        seg(12, 13),
        seg(14, 16),
        jnp.zeros((D, P_COLS - used), w.dtype),
    ], axis=1).astype(BF16)
    gates = w[:, offs[16]:].astype(BF16)
    return main, gates


def _largest_tile(n, cap):
    t = cap
    while n % t:
        t //= 2
    return t


def _largest_multiple(n, unit, cap):
    return max(t for t in range(unit, cap + 1, unit) if n % t == 0)


def kernel(x, meta_tokens, rel_bias, ffn1_norm, ffn1_w_gu, ffn1_w_down, mix_norm, w_in, hgrn_lb, hgrn_gnorm, conv_w, diff_q_norm, diff_k_norm, diff_lambda, diff_subln, dsa_q_norm, dsa_k_norm, w_branch, w_out, ffn2_norm, ffn2_w_gu, ffn2_w_down):
    B, S, D = x.shape
    depth = w_in.shape[0]
    L = S + N_META
    Lp = -(-L // SEQ_TILE) * SEQ_TILE
    T = B * Lp
    top_k = min(TOPK_MAX, S // 4)
    tm = _largest_tile(T, 512)
    tm_wide = _largest_tile(T, 1024)
    tm_ffn = _largest_multiple(T, SEQ_TILE, FFN_ROWS_MAX)
    d_ff = ffn1_w_down.shape[1]
    tf = _largest_tile(d_ff, 512)

    meta = jnp.broadcast_to(meta_tokens.astype(x.dtype)[None], (B, N_META, D))
    h = jnp.concatenate([meta, x, jnp.zeros((B, Lp - L, D), x.dtype)], axis=1).reshape(T, D)

    lbs = jnp.cumsum(jax.nn.softmax(hgrn_lb.astype(F32), axis=0), axis=0)
    lbs = lbs - lbs[0:1]
    near_c = _bias_tiles(rel_bias[:, :N_HEADS], DIFF_QUERY_TILES)
    near_d = _bias_tiles(rel_bias[:, N_HEADS:], DSA_QUERY_TILES)
    diff_qt = DIFF_QUERY_TILES * SEQ_TILE

    w1_gu, w1_down, w2_gu, w2_down, wb, wo = (
        _to_bf16(w) for w in (ffn1_w_gu, ffn1_w_down, ffn2_w_gu, ffn2_w_down, w_branch, w_out))

    for l in range(depth):
        h = _ffn(h, ffn1_norm[l], w1_gu, w1_down, l, tm_ffn, tf)

        w_main, w_gate = _pack_w_in(w_in[l])
        P = _proj(h, mix_norm[l], w_main, tm_wide, 1024)
        P3 = P.reshape(B, Lp, P_COLS)
        br_a = _hgrn(P3, lbs[l], hgrn_gnorm[l])
        br_b = _conv(P3, conv_w[l])
        cqn, ckn, cvt, dqn, dkn, dvt, qib, kk, wq = _prep(
            P, diff_q_norm[l], diff_k_norm[l], dsa_q_norm[l], dsa_k_norm[l], tm)
        r3 = lambda a: a.reshape(B, Lp, a.shape[-1])
        lp = diff_lambda[l].astype(F32)
        lam_init = 0.8 - 0.6 * math.exp(-0.3 * l)
        lam = jnp.exp(jnp.sum(lp[0] * lp[1])) - jnp.exp(jnp.sum(lp[2] * lp[3])) + lam_init
        tiles = lambda a: a.reshape(B, Lp // SEQ_TILE, a.shape[-2], SEQ_TILE)
        br_c = _diff(r3(cqn), r3(ckn), tiles(cvt), near_c,
                     jnp.broadcast_to(lam, (1, diff_qt)).astype(F32),
                     jnp.broadcast_to(diff_subln[l].astype(F32)[:, None], (HEAD_DIM, diff_qt)),
                     1.0 - lam_init)
        br_d = _dsa(qib, r3(kk), wq, dqn, r3(dkn), tiles(dvt), near_d, top_k)
        branches = [a.reshape(T, MIX_WIDTH) for a in (br_a, br_b, br_c, br_d)]
        h = _merge(h, mix_norm[l], branches, w_gate, wb, wo, l, tm, 512)

        last = l == depth - 1
        h = _ffn(h, ffn2_norm[l], w2_gu, w2_down, l,
                 _largest_multiple(S, SUBLANES, FFN_ROWS_MAX) if last else tm_ffn, tf,
                 keep=(B, Lp, N_META, S) if last else None)

    return h.reshape(B, S, D)
```

```python
import functools
import math

import jax
import jax.numpy as jnp
import numpy as np
from jax import lax
from jax.experimental import pallas as pl
from jax.experimental.pallas import tpu as pltpu

F32 = jnp.float32
BF16 = jnp.bfloat16
I32 = jnp.int32

EPS = 1e-6
N_META = 16
HEAD_DIM = 128
HALF = 64
N_HEADS = 4
MIX_WIDTH = N_HEADS * HEAD_DIM
IDX_HEADS = 16
IDX_DIM = 64
TOPK_MAX = 256
N_BRANCH = 4
N_BUCKETS = 32
MAX_DISTANCE = 128
CONV_W = 3
SEQ_TILE = 128
HGRN_CHUNK = 8
HGRN_GROUP = 8
DIFF_QUERY_TILES = 1
DIFF_HEADS_PER_STEP = 4
DSA_QUERY_TILES = 2
CONV_ROWS_MAX = 1408
CAST_BLOCK_BYTES = 8 * 1024 * 1024
FFN_ROWS_MAX = 1024
MASKED = -1e30
LOG2E = math.log2(math.e)
INT_MIN = -(2 ** 31)
KEY_NEG_INF = -2139095041
VMEM_LIMIT_BYTES = 56 * 1024 * 1024

COL_A = 0
COL_QI = 2048
COL_B = 3072
COL_C = 4608
COL_DQ = 6144
COL_DK = 6656
COL_DV = 6784
COL_KIW = 6912
P_COLS = 7168
PROJ_COLS_TILE = 1792


def _cparams(*sem):
    return pltpu.CompilerParams(dimension_semantics=sem, vmem_limit_bytes=VMEM_LIMIT_BYTES)


def _dot(a, b):
    return jnp.dot(a, b, preferred_element_type=F32)


def _dot_nt(a, b):
    return lax.dot_general(a, b, (((1,), (1,)), ((), ())), preferred_element_type=F32)


def _dot_tn(a, b):
    return lax.dot_general(a, b, (((0,), (0,)), ((), ())), preferred_element_type=F32)


def _rms_inv(x):
    return lax.rsqrt(jnp.mean(x * x, axis=-1, keepdims=True) + EPS)


def _cast_body(x_ref, o_ref):
    o_ref[...] = x_ref[...].astype(o_ref.dtype)


def _to_bf16(w):
    cols = w.shape[-1]
    w2 = w.reshape(-1, cols)
    n_rows = w2.shape[0]
    rows = _largest_multiple(n_rows, SUBLANES, max(SUBLANES, CAST_BLOCK_BYTES // (4 * cols)))
    out = pl.pallas_call(
        _cast_body,
        grid=(n_rows // rows,),
        in_specs=[pl.BlockSpec((rows, cols), lambda i: (i, 0))],
        out_specs=pl.BlockSpec((rows, cols), lambda i: (i, 0)),
        out_shape=jax.ShapeDtypeStruct((n_rows, cols), BF16),
        compiler_params=_cparams("parallel"),
        name="cast",
    )(w2)
    return out.reshape(w.shape)


def _ffn_body(h_ref, g_ref, wg_ref, wu_ref, wd_ref, o_ref, xn_ref):
    j = pl.program_id(1)

    @pl.when(j == 0)
    def _():
        x = h_ref[...]
        xn_ref[...] = ((x * _rms_inv(x)) * g_ref[...]).astype(BF16)
        o_ref[...] = x

    xn = xn_ref[...]
    g = _dot(xn, wg_ref[...])
    u = _dot(xn, wu_ref[...])
    a = (g * jax.nn.sigmoid(g) * (0.5 * u)).astype(BF16)
    o_ref[...] += _dot(a, wd_ref[...])


def _ffn(h, gain, w_gu, w_down, layer, tm, tf, keep=None):
    T, D = h.shape
    F = w_down.shape[1]
    nf = F // tf
    if keep is None:
        n_row_tiles = T // tm
        row_in = pl.BlockSpec((tm, D), lambda i, j: (i, 0))
        t_out = T
    else:
        B, Lp, first, n_rows = keep
        per_seq = n_rows // tm
        n_row_tiles = B * per_seq
        row_in = pl.BlockSpec((pl.Element(tm), pl.Element(D)),
                              lambda i, j: (pl.multiple_of(
                                  (i // per_seq) * Lp + first + (i % per_seq) * tm, SUBLANES), 0))
        t_out = B * n_rows
    return pl.pallas_call(
        _ffn_body,
        grid=(n_row_tiles, nf),
        in_specs=[
            row_in,
            pl.BlockSpec((1, D), lambda i, j: (0, 0)),
            pl.BlockSpec((None, D, tf), lambda i, j: (layer, 0, j)),
            pl.BlockSpec((None, D, tf), lambda i, j: (layer, 0, j + nf)),
            pl.BlockSpec((None, tf, D), lambda i, j: (layer, j, 0)),
        ],
        out_specs=pl.BlockSpec((tm, D), lambda i, j: (i, 0)),
        out_shape=jax.ShapeDtypeStruct((t_out, D), F32),
        scratch_shapes=[pltpu.VMEM((tm, D), BF16)],
        compiler_params=_cparams("parallel", "arbitrary"),
        name="ffn",
    )(h, gain.reshape(1, D), w_gu, w_gu, w_down)


def _proj_body(h_ref, g_ref, w_ref, p_ref, xn_ref):
    @pl.when(pl.program_id(1) == 0)
    def _():
        x = h_ref[...]
        xn_ref[...] = ((x * _rms_inv(x)) * g_ref[...]).astype(BF16)

    p_ref[...] = _dot(xn_ref[...], w_ref[...])


def _proj(h, gain, w, tm, tn):
    T, D = h.shape
    N = w.shape[1]
    return pl.pallas_call(
        _proj_body,
        grid=(T // tm, N // tn),
        in_specs=[
            pl.BlockSpec((tm, D), lambda i, j: (i, 0)),
            pl.BlockSpec((1, D), lambda i, j: (0, 0)),
            pl.BlockSpec((D, tn), lambda i, j: (0, j)),
        ],
        out_specs=pl.BlockSpec((tm, tn), lambda i, j: (i, j)),
        out_shape=jax.ShapeDtypeStruct((T, N), F32),
        scratch_shapes=[pltpu.VMEM((tm, D), BF16)],
        compiler_params=_cparams("parallel", "arbitrary"),
        name="proj",
    )(h, gain.reshape(1, D), w)


def _hgrn_body(q_ref, f_ref, i_ref, g_ref, loglb_ref, log1mlb_ref, omlb_ref, gn_ref, o_ref,
               st_ref, qs_ref, ks_ref, bs_ref, os_ref):
    G, R, _ = q_ref.shape
    C = HGRN_CHUNK

    @pl.when(pl.program_id(1) == 0)
    def _():
        st_ref[...] = jnp.zeros_like(st_ref)

    row = lax.broadcasted_iota(I32, (R, MIX_WIDTH), 0) % C
    for g in range(G):
        z = f_ref[g]
        log_sig = jnp.minimum(z, 0.0) - jnp.log(1.0 + jnp.exp(-jnp.abs(z)))
        y = log1mlb_ref[...] + log_sig
        a = loglb_ref[...]
        log_f = jnp.maximum(a, y) + jnp.log(1.0 + jnp.exp(-jnp.abs(a - y)))
        b = log_f * LOG2E
        for sh in [1 << e for e in range(C.bit_length() - 1)]:
            b = b + jnp.where(row >= sh, pltpu.roll(b, sh, axis=0), 0.0)
        bs_ref[g] = b
        ks_ref[g] = omlb_ref[...] * jax.nn.sigmoid(-z)
        qr = q_ref[g]
        qs_ref[g] = qr * jax.nn.sigmoid(qr)

    t_iota = lax.broadcasted_iota(I32, (C, 1), 0)

    def chunk(c, carry):
        r0 = pl.multiple_of(c * C, C)
        for g in range(G):
            for h in range(N_HEADS):
                hs = slice(h * HEAD_DIM, (h + 1) * HEAD_DIM)
                qc = qs_ref[g, pl.ds(r0, C), hs]
                kc = ks_ref[g, pl.ds(r0, C), hs]
                bc = bs_ref[g, pl.ds(r0, C), hs]
                vc = i_ref[g, pl.ds(r0, C), hs]
                st = st_ref[g, h]
                o = _dot_nt((qc * jnp.exp2(bc)).astype(BF16), st.astype(BF16))
                for s in range(C):
                    b_s = bc[s:s + 1, :]
                    k_s = kc[s:s + 1, :]
                    v_s = vc[s:s + 1, :]
                    e = jnp.exp2(bc - b_s)
                    col = jnp.sum(qc * k_s * e, axis=-1, keepdims=True)
                    col = jnp.where(t_iota >= s, col, 0.0)
                    o = o + col * v_s
                os_ref[g, pl.ds(r0, C), hs] = o
                b_last = bc[C - 1:C, :]
                kd = kc * jnp.exp2(b_last - bc)
                st_ref[g, h] = st * jnp.exp2(b_last) + _dot_tn(vc.astype(BF16), kd.astype(BF16))
        return carry

    lax.fori_loop(0, R // C, chunk, 0)

    for g in range(G):
        for h in range(N_HEADS):
            hs = slice(h * HEAD_DIM, (h + 1) * HEAD_DIM)
            o = os_ref[g, :, hs]
            on = (o * _rms_inv(o)) * gn_ref[:, hs]
            gr = g_ref[g, :, hs]
            o_ref[g, :, hs] = (on * (gr * jax.nn.sigmoid(gr))).astype(o_ref.dtype)


def _hgrn(P3, lb, gnorm):
    B, Lp, _ = P3.shape
    R = SEQ_TILE
    W = MIX_WIDTH
    G = _largest_tile(B, HGRN_GROUP)
    lb = lb.reshape(1, W).astype(F32)
    vec = pl.BlockSpec((1, W), lambda b, t: (0, 0))
    c0 = COL_A // W
    return pl.pallas_call(
        _hgrn_body,
        grid=(B // G, Lp // R),
        in_specs=[pl.BlockSpec((G, R, W), functools.partial(lambda b, t, c: (b, t, c), c=c0 + k))
                  for k in range(4)] + [vec, vec, vec, vec],
        out_specs=pl.BlockSpec((G, R, W), lambda b, t: (b, t, 0)),
        out_shape=jax.ShapeDtypeStruct((B, Lp, W), BF16),
        scratch_shapes=[pltpu.VMEM((G, N_HEADS, HEAD_DIM, HEAD_DIM), F32)]
        + [pltpu.VMEM((G, R, W), F32)] * 4,
        compiler_params=_cparams("parallel", "arbitrary"),
        name="hgrn",
    )(P3, P3, P3, P3, jnp.log(lb), jnp.log1p(-lb), 1.0 - lb, gnorm.reshape(1, W).astype(F32))


def _conv_body(b_ref, c_ref, u_ref, w_ref, o_ref, carry_ref):
    @pl.when(pl.program_id(1) == 0)
    def _():
        carry_ref[...] = jnp.zeros_like(carry_ref)

    zc = c_ref[...] * u_ref[...]
    R = zc.shape[0]
    row = lax.broadcasted_iota(I32, zc.shape, 0)
    last = carry_ref[7:8, :]
    last2 = carry_ref[6:7, :]
    z1 = jnp.where(row == 0, last, pltpu.roll(zc, 1, axis=0))
    z2 = jnp.where(row == 0, last2, jnp.where(row == 1, last, pltpu.roll(zc, 2, axis=0)))
    y = w_ref[0:1, :] * zc + w_ref[1:2, :] * z1 + w_ref[2:3, :] * z2
    o_ref[...] = (b_ref[...] * y).astype(o_ref.dtype)
    carry_ref[...] = zc[R - 8:, :]


def _conv(P3, conv_w):
    B, Lp, _ = P3.shape
    R = _largest_multiple(Lp, SEQ_TILE, CONV_ROWS_MAX)
    W = MIX_WIDTH
    c0 = COL_B // W
    w8 = jnp.zeros((8, W), F32).at[:CONV_W].set(conv_w.astype(F32))
    return pl.pallas_call(
        _conv_body,
        grid=(B, Lp // R),
        in_specs=[pl.BlockSpec((None, R, W), functools.partial(lambda b, t, c: (b, t, c), c=c0 + k))
                  for k in range(3)] + [pl.BlockSpec((8, W), lambda b, t: (0, 0))],
        out_specs=pl.BlockSpec((None, R, W), lambda b, t: (b, t, 0)),
        out_shape=jax.ShapeDtypeStruct((B, Lp, W), BF16),
        scratch_shapes=[pltpu.VMEM((8, W), F32)],
        compiler_params=_cparams("parallel", "arbitrary"),
        name="conv",
    )(P3, P3, P3, w8)


def _prep_body(cq_ref, ck_ref, cv_ref, dq_ref, dk_ref, dv_ref, kiw_ref, qi_ref,
               cqg_ref, ckg_ref, dqg_ref, dkg_ref,
               cqn_ref, ckn_ref, cvb_ref, dqn_ref, dkn_ref, dvb_ref, qib_ref, kk_ref, wq_ref):
    lane = lax.broadcasted_iota(I32, (1, HEAD_DIM), 1)
    lo = lane < HALF

    def half_norm(x, g, scale):
        sq = x * x
        ms_lo = jnp.sum(jnp.where(lo, sq, 0.0), axis=-1, keepdims=True) * (1.0 / HALF)
        ms_hi = jnp.sum(jnp.where(lo, 0.0, sq), axis=-1, keepdims=True) * (1.0 / HALF)
        inv = jnp.where(lo, lax.rsqrt(ms_lo + EPS), lax.rsqrt(ms_hi + EPS))
        return ((x * inv) * g) * scale

    for h in range(N_HEADS):
        hs = slice(h * HEAD_DIM, (h + 1) * HEAD_DIM)
        cqn_ref[:, hs] = half_norm(cq_ref[:, hs], cqg_ref[...], HALF ** -0.5 * LOG2E).astype(BF16)
        ckn_ref[:, hs] = half_norm(ck_ref[:, hs], ckg_ref[...], 1.0).astype(BF16)
        x = dq_ref[:, hs]
        dqn_ref[:, hs] = (((x * _rms_inv(x)) * dqg_ref[...]) * (HEAD_DIM ** -0.5 * LOG2E)).astype(BF16)
    x = dk_ref[...]
    dkn_ref[...] = ((x * _rms_inv(x)) * dkg_ref[...]).astype(BF16)
    for t in range(cv_ref.shape[0] // SEQ_TILE):
        rs = slice(t * SEQ_TILE, (t + 1) * SEQ_TILE)
        for h in range(N_HEADS):
            hs = slice(h * HEAD_DIM, (h + 1) * HEAD_DIM)
            cvb_ref[t, hs, :] = cv_ref[rs, hs].T.astype(BF16)
        dvb_ref[t] = dv_ref[rs, :].T.astype(BF16)
    qib_ref[...] = (qi_ref[...] * (IDX_DIM ** -0.5)).astype(BF16)
    kiw = kiw_ref[...]
    swapped = pltpu.roll(kiw, HALF, axis=1)
    kk_ref[...] = jnp.where(lo, kiw, swapped).astype(BF16)
    wq_ref[...] = swapped * (IDX_HEADS ** -0.5)


def _prep(P, cq_g, ck_g, dq_g, dk_g, tm):
    T = P.shape[0]
    W = MIX_WIDTH
    H = HEAD_DIM

    def col(width, start):
        return pl.BlockSpec((tm, width), functools.partial(lambda i, c: (i, c), c=start // width))

    vec = pl.BlockSpec((1, H), lambda i: (0, 0))
    out = lambda width: pl.BlockSpec((tm, width), lambda i: (i, 0))
    tiles = lambda width: pl.BlockSpec((tm // SEQ_TILE, width, SEQ_TILE), lambda i: (i, 0, 0))
    return pl.pallas_call(
        _prep_body,
        grid=(T // tm,),
        in_specs=[col(W, COL_C), col(W, COL_C + W), col(W, COL_C + 2 * W), col(W, COL_DQ),
                  col(H, COL_DK), col(H, COL_DV), col(H, COL_KIW), col(IDX_HEADS * IDX_DIM, COL_QI),
                  vec, vec, vec, vec],
        out_specs=[out(W), out(W), tiles(W), out(W), out(H), tiles(H), out(IDX_HEADS * IDX_DIM), out(H), out(H)],
        out_shape=[jax.ShapeDtypeStruct((T, W), BF16)] * 2
        + [jax.ShapeDtypeStruct((T // SEQ_TILE, W, SEQ_TILE), BF16), jax.ShapeDtypeStruct((T, W), BF16),
           jax.ShapeDtypeStruct((T, H), BF16), jax.ShapeDtypeStruct((T // SEQ_TILE, H, SEQ_TILE), BF16)]
        + [jax.ShapeDtypeStruct((T, IDX_HEADS * IDX_DIM), BF16),
           jax.ShapeDtypeStruct((T, H), BF16), jax.ShapeDtypeStruct((T, H), F32)],
        compiler_params=_cparams("parallel"),
        name="prep",
    )(P, P, P, P, P, P, P, P,
      jnp.tile(cq_g.astype(F32), 2).reshape(1, H), jnp.tile(ck_g.astype(F32), 2).reshape(1, H),
      dq_g.astype(F32).reshape(1, H), dk_g.astype(F32).reshape(1, H))


SUBLANES = 8
KEY_CHUNK = 8


def _max8(x):
    return jnp.max(x.reshape(x.shape[0] // SUBLANES, SUBLANES, x.shape[1]), axis=0)


def _sum8(x):
    return jnp.sum(x.reshape(x.shape[0] // SUBLANES, SUBLANES, x.shape[1]), axis=0)


def _bit_transpose32(rows):
    a = list(rows)
    j = 16
    m = 0x0000FFFF
    while j:
        k = 0
        while k < 32:
            t = (a[k] ^ lax.shift_right_logical(a[k + j], jnp.int32(j))) & jnp.int32(m)
            a[k] = a[k] ^ t
            a[k + j] = a[k + j] ^ (t << j)
            k = (k + j + 1) & ~j
        j >>= 1
        m ^= (m << j) & 0xFFFFFFFF
    return a


def _for_tiles(n_tiles, body, chunk=KEY_CHUNK):
    n_chunks = n_tiles // chunk

    def chunked(c, carry):
        body(c * chunk, chunk)
        return carry

    lax.fori_loop(0, n_chunks, chunked, 0)
    base = n_chunks * chunk
    size = chunk // 2
    while size >= 1:
        take = (n_tiles - base) // size

        def part(_, carry, base=base, size=size):
            body(base, size)
            return carry

        lax.fori_loop(0, take, part, 0)
        base = base + take * size
        size //= 2


def _diff_body(q_ref, k_ref, vt_ref, near_ref, lam_ref, sub_ref, o_ref,
               q2_ref, s_ref, mx_ref, l_ref, acc_ref, *, out_scale, n_tiles):
    blk = pl.program_id(2)
    KT = SEQ_TILE
    H = HEAD_DIM
    QT = q_ref.shape[0]
    QB = QT // KT
    HP = q_ref.shape[1] // H
    W2 = 2 * QT
    lane = lax.broadcasted_iota(I32, (1, H), 1)
    for h in range(HP):
        q = q_ref[:, h * H:(h + 1) * H]
        zero = jnp.zeros_like(q)
        q2_ref[h, 0:QT, :] = jnp.where(lane < HALF, q, zero)
        q2_ref[h, QT:, :] = jnp.where(lane < HALF, zero, q)
    mx_ref[...] = jnp.full_like(mx_ref, MASKED)

    def put(j0, n, near_idx):
        k0 = pl.multiple_of(j0 * KT, KT)
        parts = []
        for h in range(HP):
            s = _dot_nt(k_ref[pl.ds(k0, n * KT), h * H:(h + 1) * H], q2_ref[h])
            if near_idx is None:
                parts.append(s)
            else:
                b = near_ref[h, near_idx]
                parts.append(s + jnp.concatenate([b, b], axis=1))
        s = jnp.concatenate(parts, axis=1)
        for t in range(n):
            s_ref[j0 + t] = s[t * KT:(t + 1) * KT, :]
        mx_ref[...] = jnp.maximum(mx_ref[...], _max8(s))

    first_near = QB * blk - 1
    n_far = jnp.maximum(first_near, 0)
    newest = jnp.minimum(QB * blk + QB - 1, n_tiles - 1)
    _for_tiles(n_far, lambda j0, n: put(j0, n, None))

    def near(j, carry):
        put(j, 1, j - first_near)
        return carry

    lax.fori_loop(n_far, newest + 1, near, 0)

    m = jnp.max(mx_ref[...], axis=0, keepdims=True)
    l_ref[...] = jnp.zeros_like(l_ref)
    acc_ref[...] = jnp.zeros_like(acc_ref)

    def pv(j0, n):
        p = jnp.exp2(s_ref[pl.ds(j0, n)].reshape(n * KT, HP * W2) - m)
        l_ref[...] += _sum8(p)
        pb = p.astype(BF16)
        for h in range(HP):
            acc = acc_ref[:, h * W2:(h + 1) * W2]
            for t in range(n):
                acc = acc + _dot(vt_ref[j0 + t, h * H:(h + 1) * H, :],
                                 pb[t * KT:(t + 1) * KT, h * W2:(h + 1) * W2])
            acc_ref[:, h * W2:(h + 1) * W2] = acc

    _for_tiles(newest + 1, pv)

    ot = acc_ref[...] * (1.0 / jnp.sum(l_ref[...], axis=0, keepdims=True))
    for h in range(HP):
        od = ot[:, h * W2:h * W2 + QT] - lam_ref[...] * ot[:, h * W2 + QT:(h + 1) * W2]
        inv = lax.rsqrt(jnp.mean(od * od, axis=0, keepdims=True) + EPS)
        o_ref[:, h * H:(h + 1) * H] = (((od * inv) * sub_ref[...]) * out_scale).T.astype(o_ref.dtype)


def _diff(cqn, ckn, cvt, near_t, lam, subln, out_scale):
    B, Lp, W = cqn.shape
    KT = SEQ_TILE
    H = HEAD_DIM
    nt = Lp // KT
    QB = near_t.shape[1] - 1
    QT = QB * KT
    HP = DIFF_HEADS_PER_STEP
    lanes = HP * 2 * QT
    return pl.pallas_call(
        functools.partial(_diff_body, out_scale=out_scale, n_tiles=nt),
        grid=(B, N_HEADS // HP, pl.cdiv(nt, QB)),
        in_specs=[
            pl.BlockSpec((None, QT, HP * H), lambda b, g, i: (b, i, g)),
            pl.BlockSpec((None, Lp, HP * H), lambda b, g, i: (b, 0, g)),
            pl.BlockSpec((None, nt, HP * H, KT), lambda b, g, i: (b, 0, g, 0)),
            pl.BlockSpec((HP, QB + 1, KT, QT), lambda b, g, i: (g, 0, 0, 0)),
            pl.BlockSpec((1, QT), lambda b, g, i: (0, 0)),
            pl.BlockSpec((H, QT), lambda b, g, i: (0, 0)),
        ],
        out_specs=pl.BlockSpec((None, QT, HP * H), lambda b, g, i: (b, i, g)),
        out_shape=jax.ShapeDtypeStruct((B, Lp, W), BF16),
        scratch_shapes=[pltpu.VMEM((HP, 2 * QT, H), BF16),
                        pltpu.VMEM((nt, KT, lanes), F32),
                        pltpu.VMEM((SUBLANES, lanes), F32),
                        pltpu.VMEM((SUBLANES, lanes), F32),
                        pltpu.VMEM((H, lanes), F32)],
        compiler_params=_cparams("parallel", "parallel", "arbitrary"),
        name="diff",
    )(cqn, ckn, cvt, near_t, lam, subln)


def _block_first_tile(blk, n_blocks, n_tiles, tiles_per_block):
    return jnp.maximum(n_tiles - tiles_per_block * (n_blocks - blk), 0)


def _dsa_body(qi_ref, kk_ref, wq_ref, q_ref, k_ref, vt_ref, near_ref, o_ref,
              qi2_ref, w_ref, key_ref, kq_ref, plane_ref, alive_ref, q4_ref, thr_ref, s_ref, mx_ref,
              l_ref, acc_ref, *, top_k, n_tiles):
    blk = pl.program_id(1)
    KT = SEQ_TILE
    H = HEAD_DIM
    QT = q_ref.shape[0]
    QB = QT // KT
    first = _block_first_tile(blk, pl.num_programs(1), n_tiles, QB)
    newest = first + QB - 1
    lane = lax.broadcasted_iota(I32, (1, H), 1)

    for p in range(IDX_HEADS // 2):
        x = qi_ref[:, p * H:(p + 1) * H]
        zero = jnp.zeros_like(x)
        qi2_ref[p, 0:QT, :] = jnp.where(lane < HALF, x, zero)
        qi2_ref[p, QT:, :] = jnp.where(lane < HALF, zero, x)
    w_ref[...] = wq_ref[...].T
    for h in range(N_HEADS):
        q4_ref[h * QT:(h + 1) * QT, :] = q_ref[:, h * H:(h + 1) * H]
    kq_ref[...] = lax.broadcasted_iota(I32, (KT, QT), 0) - lax.broadcasted_iota(I32, (KT, QT), 1)

    def index(j0, n):
        k0 = pl.multiple_of(j0 * KT, KT)
        kt = kk_ref[pl.ds(k0, n * KT), :]
        acc = jnp.zeros((n * KT, QT), F32)
        for p in range(IDX_HEADS // 2):
            s = jnp.maximum(_dot_nt(kt, qi2_ref[p]), 0.0)
            acc = acc + s[:, 0:QT] * w_ref[2 * p:2 * p + 1, :] + s[:, QT:] * w_ref[2 * p + 1:2 * p + 2, :]
        bits = pltpu.bitcast(acc, I32)
        key = bits ^ ((bits >> 31) & jnp.int32(0x7FFFFFFF))
        tiles = []
        for t in range(n):
            kt_ = key[t * KT:(t + 1) * KT, :]
            if t >= n - QB:
                kt_ = jnp.where(kq_ref[...] <= (first - (j0 + t)) * KT, kt_, jnp.int32(KEY_NEG_INF))
            key_ref[j0 + t] = kt_
            tiles.append(kt_)
        if n == 1:
            tiles.append(None)
        for u in range(len(tiles) // 2):
            rows = []
            for tile in tiles[2 * u:2 * u + 2]:
                for r in range(KT // SUBLANES):
                    if tile is None:
                        rows.append(jnp.full((SUBLANES, QT), INT_MIN, I32))
                    else:
                        rows.append(tile[r * SUBLANES:(r + 1) * SUBLANES, :])
            planes = _bit_transpose32(rows)
            planes[0] = ~planes[0]
            pair = j0 // 2 + u
            for b in range(32):
                plane_ref[pair, b] = planes[b]

    @pl.when(blk == 0)
    def _():
        plane_ref[...] = jnp.zeros_like(plane_ref)

    _for_tiles(newest + 1, index)

    n_pairs = plane_ref.shape[0]
    for tp in range(n_pairs):
        alive_ref[tp] = jnp.full((SUBLANES, QT), -1, I32)

    def bit_step(t, carry):
        above, thr = carry
        ones = [alive_ref[tp] & plane_ref[tp, t] for tp in range(n_pairs)]
        cnt = lax.population_count(ones[0])
        for tp in range(1, n_pairs):
            cnt = cnt + lax.population_count(ones[tp])
        hit = above + jnp.sum(cnt, axis=0, keepdims=True)
        take = hit >= top_k
        for tp in range(n_pairs):
            alive_ref[tp] = jnp.where(take, ones[tp], alive_ref[tp] ^ ones[tp])
        thr = thr | jnp.where(take, jnp.left_shift(jnp.int32(1), 31 - t), 0)
        return jnp.where(take, above, hit), thr

    zero8 = jnp.zeros((SUBLANES, QT), I32)
    _, thr_u = lax.fori_loop(0, 32, bit_step, (zero8, zero8))
    thr_ref[...] = thr_u ^ jnp.int32(INT_MIN)

    mx_ref[...] = jnp.full_like(mx_ref, MASKED)

    def put(j0, n, near_idx):
        k0 = pl.multiple_of(j0 * KT, KT)
        s = _dot_nt(k_ref[pl.ds(k0, n * KT), :], q4_ref[...])
        thr1 = thr_ref[0:1, :]
        mx = mx_ref[...]
        for t in range(n):
            sel = key_ref[j0 + t] >= thr1
            parts = []
            for h in range(N_HEADS):
                sh = s[t * KT:(t + 1) * KT, h * QT:(h + 1) * QT]
                if near_idx is not None:
                    sh = sh + near_ref[h, near_idx]
                parts.append(jnp.where(sel, sh, MASKED))
            row = jnp.concatenate(parts, axis=1)
            s_ref[j0 + t] = row
            mx = jnp.maximum(mx, _max8(row))
        mx_ref[...] = mx

    first_near = first - 1
    n_far = jnp.maximum(first_near, 0)
    _for_tiles(n_far, lambda j0, n: put(j0, n, None))

    def near(j, carry):
        put(j, 1, j - first_near)
        return carry

    lax.fori_loop(n_far, newest + 1, near, 0)

    m = jnp.max(mx_ref[...], axis=0, keepdims=True)
    l_ref[...] = jnp.zeros_like(l_ref)
    acc_ref[...] = jnp.zeros_like(acc_ref)

    def pv(j0, n):
        p = jnp.exp2(s_ref[pl.ds(j0, n)].reshape(n * KT, N_HEADS * QT) - m)
        l_ref[...] += _sum8(p)
        pb = p.astype(BF16)
        acc = acc_ref[...]
        for t in range(n):
            acc = acc + _dot(vt_ref[j0 + t], pb[t * KT:(t + 1) * KT, :])
        acc_ref[...] = acc

    _for_tiles(newest + 1, pv)

    ot = acc_ref[...] * (1.0 / jnp.sum(l_ref[...], axis=0, keepdims=True))
    for h in range(N_HEADS):
        o_ref[:, h * H:(h + 1) * H] = ot[:, h * QT:(h + 1) * QT].T.astype(o_ref.dtype)


def _dsa(qib, kk, wq, dqn, dkn, dvt, near_t, top_k):
    B, Lp, H = dkn.shape
    W = dqn.shape[1]
    KT = SEQ_TILE
    NI = IDX_HEADS * IDX_DIM
    nt = Lp // KT
    QB = near_t.shape[1] - 1
    QT = QB * KT
    n_blocks = pl.cdiv(nt, QB)

    def rows(width):
        return pl.BlockSpec(
            (pl.Element(QT), pl.Element(width)),
            lambda b, i: (pl.multiple_of(b * Lp + _block_first_tile(i, n_blocks, nt, QB) * KT, KT), 0))

    seq = lambda width: pl.BlockSpec((None, Lp, width), lambda b, i: (b, 0, 0))
    n_pairs = (nt + 1) // 2
    return pl.pallas_call(
        functools.partial(_dsa_body, top_k=top_k, n_tiles=nt),
        grid=(B, n_blocks),
        in_specs=[rows(NI), seq(H), rows(H), rows(W), seq(H),
                  pl.BlockSpec((None, nt, H, KT), lambda b, i: (b, 0, 0, 0)),
                  pl.BlockSpec((N_HEADS, QB + 1, KT, QT), lambda b, i: (0, 0, 0, 0))],
        out_specs=rows(W),
        out_shape=jax.ShapeDtypeStruct((B * Lp, W), BF16),
        scratch_shapes=[
            pltpu.VMEM((IDX_HEADS // 2, 2 * QT, H), BF16),
            pltpu.VMEM((H, QT), F32),
            pltpu.VMEM((nt, KT, QT), I32),
            pltpu.VMEM((KT, QT), I32),
            pltpu.VMEM((n_pairs, 32, SUBLANES, QT), I32),
            pltpu.VMEM((n_pairs, SUBLANES, QT), I32),
            pltpu.VMEM((N_HEADS * QT, H), BF16),
            pltpu.VMEM((SUBLANES, QT), I32),
            pltpu.VMEM((nt, KT, N_HEADS * QT), F32),
            pltpu.VMEM((SUBLANES, N_HEADS * QT), F32),
            pltpu.VMEM((SUBLANES, N_HEADS * QT), F32),
            pltpu.VMEM((H, N_HEADS * QT), F32),
        ],
        compiler_params=_cparams("parallel", "arbitrary"),
        name="dsa",
    )(qib, kk, wq, dqn, dkn, dvt, near_t)


def _merge_body(h_ref, gain_ref, b0_ref, b1_ref, b2_ref, b3_ref, g0_ref, g1_ref, g2_ref, g3_ref,
                wb_ref, wo_ref, o_ref, xn_ref):
    @pl.when(pl.program_id(1) == 0)
    def _():
        x = h_ref[...]
        xn_ref[...] = ((x * _rms_inv(x)) * gain_ref[...]).astype(BF16)
        o_ref[...] = x

    xn = xn_ref[...]
    merged = None
    for m, (b_ref, g_ref) in enumerate(((b0_ref, g0_ref), (b1_ref, g1_ref), (b2_ref, g2_ref), (b3_ref, g3_ref))):
        term = jax.nn.sigmoid(_dot(xn, g_ref[...])) * _dot(b_ref[...], wb_ref[m])
        merged = term if merged is None else merged + term
    o_ref[...] += _dot(merged.astype(BF16), wo_ref[...])


def _merge(h, gain, branches, w_gate, w_branch, w_out, layer, tm, tn):
    T, D = h.shape
    W = MIX_WIDTH
    nn = D // tn
    row = lambda width: pl.BlockSpec((tm, width), lambda i, n: (i, 0))
    gate = lambda m: pl.BlockSpec((D, tn), functools.partial(lambda i, n, m: (0, m * nn + n), m=m))
    return pl.pallas_call(
        _merge_body,
        grid=(T // tm, nn),
        in_specs=[row(D), pl.BlockSpec((1, D), lambda i, n: (0, 0)), row(W), row(W), row(W), row(W),
                  gate(0), gate(1), gate(2), gate(3),
                  pl.BlockSpec((None, N_BRANCH, W, tn), lambda i, n: (layer, 0, 0, n)),
                  pl.BlockSpec((None, tn, D), lambda i, n: (layer, n, 0))],
        out_specs=row(D),
        out_shape=jax.ShapeDtypeStruct((T, D), F32),
        scratch_shapes=[pltpu.VMEM((tm, D), BF16)],
        compiler_params=_cparams("parallel", "arbitrary"),
        name="merge",
    )(h, gain.reshape(1, D), *branches, w_gate, w_gate, w_gate, w_gate, w_branch, w_out)


def _rel_bucket(n):
    max_exact = N_BUCKETS // 2
    nf = jnp.maximum(n, 1).astype(F32)
    large = max_exact + (jnp.log(nf / max_exact) / math.log(MAX_DISTANCE / max_exact)
                         * (N_BUCKETS - max_exact)).astype(I32)
    large = jnp.minimum(large, N_BUCKETS - 1)
    return jnp.where(n < max_exact, n, large)


def _bias_tiles(table, query_tiles):
    KT = SEQ_TILE
    kl = jnp.arange(KT, dtype=I32)[:, None]
    ql = jnp.arange(query_tiles * KT, dtype=I32)[None, :]
    tbl = table.astype(F32).T

    def lookup(bucket):
        out = jnp.zeros((tbl.shape[0],) + bucket.shape, F32)
        for b in range(N_BUCKETS):
            out = jnp.where(bucket[None] == b, tbl[:, b][:, None, None], out)
        return out

    far = tbl[:, N_BUCKETS - 1][:, None, None]
    tiles = []
    for t in range(query_tiles + 1):
        d = ql - kl + (1 - t) * KT
        bias = (lookup(_rel_bucket(jnp.maximum(d, 0))) - far) * LOG2E
        tiles.append(jnp.where(d >= 0, bias, MASKED))
    return jnp.stack(tiles, axis=1)


def _pack_w_in(w):
    offs = np.cumsum([0, 512, 512, 512, 512, 512, 512, 512, 512, 512, 512, 512, 128, 128,
                      IDX_HEADS * IDX_DIM, IDX_DIM, IDX_HEADS]).tolist()
    seg = lambda a, b: w[:, offs[a]:offs[b]]
    D = w.shape[0]
    used = COL_KIW + IDX_DIM + IDX_HEADS
    main = jnp.concatenate([
        seg(0, 4),
        seg(13, 14),
        seg(4, 7),
        seg(7, 10),
        seg(10, 11),
        seg(11, 12),
        seg(12, 13),
        seg(14, 16),
        jnp.zeros((D, P_COLS - used), w.dtype),
    ], axis=1).astype(BF16)
    gates = w[:, offs[16]:].astype(BF16)
    return main, gates


def _largest_tile(n, cap):
    t = cap
    while n % t:
        t //= 2
    return t


def _largest_multiple(n, unit, cap):
    return max(t for t in range(unit, cap + 1, unit) if n % t == 0)


def kernel(x, meta_tokens, rel_bias, ffn1_norm, ffn1_w_gu, ffn1_w_down, mix_norm, w_in, hgrn_lb, hgrn_gnorm, conv_w, diff_q_norm, diff_k_norm, diff_lambda, diff_subln, dsa_q_norm, dsa_k_norm, w_branch, w_out, ffn2_norm, ffn2_w_gu, ffn2_w_down):
    B, S, D = x.shape
    depth = w_in.shape[0]
    L = S + N_META
    Lp = -(-L // SEQ_TILE) * SEQ_TILE
    T = B * Lp
    top_k = min(TOPK_MAX, S // 4)
    tm = _largest_tile(T, 512)
    tm_wide = _largest_tile(T, 1024)
    tm_ffn = _largest_multiple(T, SEQ_TILE, FFN_ROWS_MAX)
    d_ff = ffn1_w_down.shape[1]
    tf = _largest_tile(d_ff, 512)

    meta = jnp.broadcast_to(meta_tokens.astype(x.dtype)[None], (B, N_META, D))
    h = jnp.concatenate([meta, x, jnp.zeros((B, Lp - L, D), x.dtype)], axis=1).reshape(T, D)

    lbs = jnp.cumsum(jax.nn.softmax(hgrn_lb.astype(F32), axis=0), axis=0)
    lbs = lbs - lbs[0:1]
    near_c = _bias_tiles(rel_bias[:, :N_HEADS], DIFF_QUERY_TILES)
    near_d = _bias_tiles(rel_bias[:, N_HEADS:], DSA_QUERY_TILES)
    diff_qt = DIFF_QUERY_TILES * SEQ_TILE

    w1_gu, w1_down, w2_gu, w2_down, wb, wo = (
        _to_bf16(w) for w in (ffn1_w_gu, ffn1_w_down, ffn2_w_gu, ffn2_w_down, w_branch, w_out))

    for l in range(depth):
        h = _ffn(h, ffn1_norm[l], w1_gu, w1_down, l, tm_ffn, tf)

        w_main, w_gate = _pack_w_in(w_in[l])
        P = _proj(h, mix_norm[l], w_main, tm_wide, PROJ_COLS_TILE)
        P3 = P.reshape(B, Lp, P_COLS)
        br_a = _hgrn(P3, lbs[l], hgrn_gnorm[l])
        br_b = _conv(P3, conv_w[l])
        cqn, ckn, cvt, dqn, dkn, dvt, qib, kk, wq = _prep(
            P, diff_q_norm[l], diff_k_norm[l], dsa_q_norm[l], dsa_k_norm[l], tm)
        r3 = lambda a: a.reshape(B, Lp, a.shape[-1])
        lp = diff_lambda[l].astype(F32)
        lam_init = 0.8 - 0.6 * math.exp(-0.3 * l)
        lam = jnp.exp(jnp.sum(lp[0] * lp[1])) - jnp.exp(jnp.sum(lp[2] * lp[3])) + lam_init
        tiles = lambda a: a.reshape(B, Lp // SEQ_TILE, a.shape[-2], SEQ_TILE)
        br_c = _diff(r3(cqn), r3(ckn), tiles(cvt), near_c,
                     jnp.broadcast_to(lam, (1, diff_qt)).astype(F32),
                     jnp.broadcast_to(diff_subln[l].astype(F32)[:, None], (HEAD_DIM, diff_qt)),
                     1.0 - lam_init)
        br_d = _dsa(qib, r3(kk), wq, dqn, r3(dkn), tiles(dvt), near_d, top_k)
        branches = [a.reshape(T, MIX_WIDTH) for a in (br_a, br_b, br_c, br_d)]
        h = _merge(h, mix_norm[l], branches, w_gate, wb, wo, l, tm, 512)

        last = l == depth - 1
        h = _ffn(h, ffn2_norm[l], w2_gu, w2_down, l,
                 _largest_multiple(S, SUBLANES, FFN_ROWS_MAX) if last else tm_ffn, tf,
                 keep=(B, Lp, N_META, S) if last else None)

    return h.reshape(B, S, D)
```

```python
import functools
import math

import jax
import jax.numpy as jnp
import numpy as np
from jax import lax
from jax.experimental import pallas as pl
from jax.experimental.pallas import tpu as pltpu

F32 = jnp.float32
BF16 = jnp.bfloat16
I32 = jnp.int32

EPS = 1e-6
N_META = 16
HEAD_DIM = 128
HALF = 64
N_HEADS = 4
MIX_WIDTH = N_HEADS * HEAD_DIM
IDX_HEADS = 16
IDX_DIM = 64
TOPK_MAX = 256
N_BRANCH = 4
N_BUCKETS = 32
MAX_DISTANCE = 128
CONV_W = 3
SEQ_TILE = 128
HGRN_CHUNK = 8
HGRN_GROUP = 8
DIFF_QUERY_TILES = 1
DIFF_HEADS_PER_STEP = 4
DSA_QUERY_TILES = 2
CONV_ROWS_MAX = 1408
MERGE_ROWS_MAX = 768
CAST_BLOCK_BYTES =8 * 1024 * 1024
FFN_ROWS_MAX = 1024
MASKED = -1e30
LOG2E = math.log2(math.e)
INT_MIN = -(2 ** 31)
KEY_NEG_INF = -2139095041
VMEM_LIMIT_BYTES = 56 * 1024 * 1024

COL_A = 0
COL_QI = 2048
COL_B = 3072
COL_C = 4608
COL_DQ = 6144
COL_DK = 6656
COL_DV = 6784
COL_KIW = 6912
P_COLS = 7168
PROJ_COLS_TILE = 1792


def _cparams(*sem):
    return pltpu.CompilerParams(dimension_semantics=sem, vmem_limit_bytes=VMEM_LIMIT_BYTES)


def _dot(a, b):
    return jnp.dot(a, b, preferred_element_type=F32)


def _dot_nt(a, b):
    return lax.dot_general(a, b, (((1,), (1,)), ((), ())), preferred_element_type=F32)


def _dot_tn(a, b):
    return lax.dot_general(a, b, (((0,), (0,)), ((), ())), preferred_element_type=F32)


def _rms_inv(x):
    return lax.rsqrt(jnp.mean(x * x, axis=-1, keepdims=True) + EPS)


def _cast_body(x_ref, o_ref):
    o_ref[...] = x_ref[...].astype(o_ref.dtype)


def _to_bf16(w):
    cols = w.shape[-1]
    w2 = w.reshape(-1, cols)
    n_rows = w2.shape[0]
    rows = _largest_multiple(n_rows, SUBLANES, max(SUBLANES, CAST_BLOCK_BYTES // (4 * cols)))
    out = pl.pallas_call(
        _cast_body,
        grid=(n_rows // rows,),
        in_specs=[pl.BlockSpec((rows, cols), lambda i: (i, 0))],
        out_specs=pl.BlockSpec((rows, cols), lambda i: (i, 0)),
        out_shape=jax.ShapeDtypeStruct((n_rows, cols), BF16),
        compiler_params=_cparams("parallel"),
        name="cast",
    )(w2)
    return out.reshape(w.shape)


def _ffn_body(h_ref, g_ref, wg_ref, wu_ref, wd_ref, o_ref, xn_ref):
    j = pl.program_id(1)

    @pl.when(j == 0)
    def _():
        x = h_ref[...]
        xn_ref[...] = ((x * _rms_inv(x)) * g_ref[...]).astype(BF16)
        o_ref[...] = x

    xn = xn_ref[...]
    g = _dot(xn, wg_ref[...])
    u = _dot(xn, wu_ref[...])
    a = (g * jax.nn.sigmoid(g) * (0.5 * u)).astype(BF16)
    o_ref[...] += _dot(a, wd_ref[...])


def _ffn(h, gain, w_gu, w_down, layer, tm, tf, keep=None):
    T, D = h.shape
    F = w_down.shape[1]
    nf = F // tf
    if keep is None:
        n_row_tiles = T // tm
        row_in = pl.BlockSpec((tm, D), lambda i, j: (i, 0))
        t_out = T
    else:
        B, Lp, first, n_rows = keep
        per_seq = n_rows // tm
        n_row_tiles = B * per_seq
        row_in = pl.BlockSpec((pl.Element(tm), pl.Element(D)),
                              lambda i, j: (pl.multiple_of(
                                  (i // per_seq) * Lp + first + (i % per_seq) * tm, SUBLANES), 0))
        t_out = B * n_rows
    return pl.pallas_call(
        _ffn_body,
        grid=(n_row_tiles, nf),
        in_specs=[
            row_in,
            pl.BlockSpec((1, D), lambda i, j: (0, 0)),
            pl.BlockSpec((None, D, tf), lambda i, j: (layer, 0, j)),
            pl.BlockSpec((None, D, tf), lambda i, j: (layer, 0, j + nf)),
            pl.BlockSpec((None, tf, D), lambda i, j: (layer, j, 0)),
        ],
        out_specs=pl.BlockSpec((tm, D), lambda i, j: (i, 0)),
        out_shape=jax.ShapeDtypeStruct((t_out, D), F32),
        scratch_shapes=[pltpu.VMEM((tm, D), BF16)],
        compiler_params=_cparams("parallel", "arbitrary"),
        name="ffn",
    )(h, gain.reshape(1, D), w_gu, w_gu, w_down)


def _proj_body(h_ref, g_ref, w_ref, p_ref, xn_ref):
    @pl.when(pl.program_id(1) == 0)
    def _():
        x = h_ref[...]
        xn_ref[...] = ((x * _rms_inv(x)) * g_ref[...]).astype(BF16)

    p_ref[...] = _dot(xn_ref[...], w_ref[...])


def _proj(h, gain, w, tm, tn):
    T, D = h.shape
    N = w.shape[1]
    return pl.pallas_call(
        _proj_body,
        grid=(T // tm, N // tn),
        in_specs=[
            pl.BlockSpec((tm, D), lambda i, j: (i, 0)),
            pl.BlockSpec((1, D), lambda i, j: (0, 0)),
            pl.BlockSpec((D, tn), lambda i, j: (0, j)),
        ],
        out_specs=pl.BlockSpec((tm, tn), lambda i, j: (i, j)),
        out_shape=jax.ShapeDtypeStruct((T, N), F32),
        scratch_shapes=[pltpu.VMEM((tm, D), BF16)],
        compiler_params=_cparams("parallel", "arbitrary"),
        name="proj",
    )(h, gain.reshape(1, D), w)


def _hgrn_body(q_ref, f_ref, i_ref, g_ref, loglb_ref, log1mlb_ref, omlb_ref, gn_ref, o_ref,
               st_ref, qs_ref, ks_ref, bs_ref, os_ref):
    G, R, _ = q_ref.shape
    C = HGRN_CHUNK

    @pl.when(pl.program_id(1) == 0)
    def _():
        st_ref[...] = jnp.zeros_like(st_ref)

    row = lax.broadcasted_iota(I32, (R, MIX_WIDTH), 0) % C
    for g in range(G):
        z = f_ref[g]
        log_sig = jnp.minimum(z, 0.0) - jnp.log(1.0 + jnp.exp(-jnp.abs(z)))
        y = log1mlb_ref[...] + log_sig
        a = loglb_ref[...]
        log_f = jnp.maximum(a, y) + jnp.log(1.0 + jnp.exp(-jnp.abs(a - y)))
        b = log_f * LOG2E
        for sh in [1 << e for e in range(C.bit_length() - 1)]:
            b = b + jnp.where(row >= sh, pltpu.roll(b, sh, axis=0), 0.0)
        bs_ref[g] = b
        ks_ref[g] = omlb_ref[...] * jax.nn.sigmoid(-z)
        qr = q_ref[g]
        qs_ref[g] = qr * jax.nn.sigmoid(qr)

    t_iota = lax.broadcasted_iota(I32, (C, 1), 0)

    def chunk(c, carry):
        r0 = pl.multiple_of(c * C, C)
        for g in range(G):
            for h in range(N_HEADS):
                hs = slice(h * HEAD_DIM, (h + 1) * HEAD_DIM)
                qc = qs_ref[g, pl.ds(r0, C), hs]
                kc = ks_ref[g, pl.ds(r0, C), hs]
                bc = bs_ref[g, pl.ds(r0, C), hs]
                vc = i_ref[g, pl.ds(r0, C), hs]
                st = st_ref[g, h]
                o = _dot_nt((qc * jnp.exp2(bc)).astype(BF16), st.astype(BF16))
                for s in range(C):
                    b_s = bc[s:s + 1, :]
                    k_s = kc[s:s + 1, :]
                    v_s = vc[s:s + 1, :]
                    e = jnp.exp2(bc - b_s)
                    col = jnp.sum(qc * k_s * e, axis=-1, keepdims=True)
                    col = jnp.where(t_iota >= s, col, 0.0)
                    o = o + col * v_s
                os_ref[g, pl.ds(r0, C), hs] = o
                b_last = bc[C - 1:C, :]
                kd = kc * jnp.exp2(b_last - bc)
                st_ref[g, h] = st * jnp.exp2(b_last) + _dot_tn(vc.astype(BF16), kd.astype(BF16))
        return carry

    lax.fori_loop(0, R // C, chunk, 0)

    for g in range(G):
        for h in range(N_HEADS):
            hs = slice(h * HEAD_DIM, (h + 1) * HEAD_DIM)
            o = os_ref[g, :, hs]
            on = (o * _rms_inv(o)) * gn_ref[:, hs]
            gr = g_ref[g, :, hs]
            o_ref[g, :, hs] = (on * (gr * jax.nn.sigmoid(gr))).astype(o_ref.dtype)


def _hgrn(P3, lb, gnorm):
    B, Lp, _ = P3.shape
    R = SEQ_TILE
    W = MIX_WIDTH
    G = _largest_tile(B, HGRN_GROUP)
    lb = lb.reshape(1, W).astype(F32)
    vec = pl.BlockSpec((1, W), lambda b, t: (0, 0))
    c0 = COL_A // W
    return pl.pallas_call(
        _hgrn_body,
        grid=(B // G, Lp // R),
        in_specs=[pl.BlockSpec((G, R, W), functools.partial(lambda b, t, c: (b, t, c), c=c0 + k))
                  for k in range(4)] + [vec, vec, vec, vec],
        out_specs=pl.BlockSpec((G, R, W), lambda b, t: (b, t, 0)),
        out_shape=jax.ShapeDtypeStruct((B, Lp, W), BF16),
        scratch_shapes=[pltpu.VMEM((G, N_HEADS, HEAD_DIM, HEAD_DIM), F32)]
        + [pltpu.VMEM((G, R, W), F32)] * 4,
        compiler_params=_cparams("parallel", "arbitrary"),
        name="hgrn",
    )(P3, P3, P3, P3, jnp.log(lb), jnp.log1p(-lb), 1.0 - lb, gnorm.reshape(1, W).astype(F32))


def _conv_body(b_ref, c_ref, u_ref, w_ref, o_ref, carry_ref):
    @pl.when(pl.program_id(1) == 0)
    def _():
        carry_ref[...] = jnp.zeros_like(carry_ref)

    zc = c_ref[...] * u_ref[...]
    R = zc.shape[0]
    row = lax.broadcasted_iota(I32, zc.shape, 0)
    last = carry_ref[7:8, :]
    last2 = carry_ref[6:7, :]
    z1 = jnp.where(row == 0, last, pltpu.roll(zc, 1, axis=0))
    z2 = jnp.where(row == 0, last2, jnp.where(row == 1, last, pltpu.roll(zc, 2, axis=0)))
    y = w_ref[0:1, :] * zc + w_ref[1:2, :] * z1 + w_ref[2:3, :] * z2
    o_ref[...] = (b_ref[...] * y).astype(o_ref.dtype)
    carry_ref[...] = zc[R - 8:, :]


def _conv(P3, conv_w):
    B, Lp, _ = P3.shape
    R = _largest_multiple(Lp, SEQ_TILE, CONV_ROWS_MAX)
    W = MIX_WIDTH
    c0 = COL_B // W
    w8 = jnp.zeros((8, W), F32).at[:CONV_W].set(conv_w.astype(F32))
    return pl.pallas_call(
        _conv_body,
        grid=(B, Lp // R),
        in_specs=[pl.BlockSpec((None, R, W), functools.partial(lambda b, t, c: (b, t, c), c=c0 + k))
                  for k in range(3)] + [pl.BlockSpec((8, W), lambda b, t: (0, 0))],
        out_specs=pl.BlockSpec((None, R, W), lambda b, t: (b, t, 0)),
        out_shape=jax.ShapeDtypeStruct((B, Lp, W), BF16),
        scratch_shapes=[pltpu.VMEM((8, W), F32)],
        compiler_params=_cparams("parallel", "arbitrary"),
        name="conv",
    )(P3, P3, P3, w8)


def _prep_body(cq_ref, ck_ref, cv_ref, dq_ref, dk_ref, dv_ref, kiw_ref, qi_ref,
               cqg_ref, ckg_ref, dqg_ref, dkg_ref,
               cqn_ref, ckn_ref, cvb_ref, dqn_ref, dkn_ref, dvb_ref, qib_ref, kk_ref, wq_ref):
    lane = lax.broadcasted_iota(I32, (1, HEAD_DIM), 1)
    lo = lane < HALF

    def half_norm(x, g, scale):
        sq = x * x
        ms_lo = jnp.sum(jnp.where(lo, sq, 0.0), axis=-1, keepdims=True) * (1.0 / HALF)
        ms_hi = jnp.sum(jnp.where(lo, 0.0, sq), axis=-1, keepdims=True) * (1.0 / HALF)
        inv = jnp.where(lo, lax.rsqrt(ms_lo + EPS), lax.rsqrt(ms_hi + EPS))
        return ((x * inv) * g) * scale

    for h in range(N_HEADS):
        hs = slice(h * HEAD_DIM, (h + 1) * HEAD_DIM)
        cqn_ref[:, hs] = half_norm(cq_ref[:, hs], cqg_ref[...], HALF ** -0.5 * LOG2E).astype(BF16)
        ckn_ref[:, hs] = half_norm(ck_ref[:, hs], ckg_ref[...], 1.0).astype(BF16)
        x = dq_ref[:, hs]
        dqn_ref[:, hs] = (((x * _rms_inv(x)) * dqg_ref[...]) * (HEAD_DIM ** -0.5 * LOG2E)).astype(BF16)
    x = dk_ref[...]
    dkn_ref[...] = ((x * _rms_inv(x)) * dkg_ref[...]).astype(BF16)
    for t in range(cv_ref.shape[0] // SEQ_TILE):
        rs = slice(t * SEQ_TILE, (t + 1) * SEQ_TILE)
        for h in range(N_HEADS):
            hs = slice(h * HEAD_DIM, (h + 1) * HEAD_DIM)
            cvb_ref[t, hs, :] = cv_ref[rs, hs].T.astype(BF16)
        dvb_ref[t] = dv_ref[rs, :].T.astype(BF16)
    qib_ref[...] = (qi_ref[...] * (IDX_DIM ** -0.5)).astype(BF16)
    kiw = kiw_ref[...]
    swapped = pltpu.roll(kiw, HALF, axis=1)
    kk_ref[...] = jnp.where(lo, kiw, swapped).astype(BF16)
    wq_ref[...] = swapped * (IDX_HEADS ** -0.5)


def _prep(P, cq_g, ck_g, dq_g, dk_g, tm):
    T = P.shape[0]
    W = MIX_WIDTH
    H = HEAD_DIM

    def col(width, start):
        return pl.BlockSpec((tm, width), functools.partial(lambda i, c: (i, c), c=start // width))

    vec = pl.BlockSpec((1, H), lambda i: (0, 0))
    out = lambda width: pl.BlockSpec((tm, width), lambda i: (i, 0))
    tiles = lambda width: pl.BlockSpec((tm // SEQ_TILE, width, SEQ_TILE), lambda i: (i, 0, 0))
    return pl.pallas_call(
        _prep_body,
        grid=(T // tm,),
        in_specs=[col(W, COL_C), col(W, COL_C + W), col(W, COL_C + 2 * W), col(W, COL_DQ),
                  col(H, COL_DK), col(H, COL_DV), col(H, COL_KIW), col(IDX_HEADS * IDX_DIM, COL_QI),
                  vec, vec, vec, vec],
        out_specs=[out(W), out(W), tiles(W), out(W), out(H), tiles(H), out(IDX_HEADS * IDX_DIM), out(H), out(H)],
        out_shape=[jax.ShapeDtypeStruct((T, W), BF16)] * 2
        + [jax.ShapeDtypeStruct((T // SEQ_TILE, W, SEQ_TILE), BF16), jax.ShapeDtypeStruct((T, W), BF16),
           jax.ShapeDtypeStruct((T, H), BF16), jax.ShapeDtypeStruct((T // SEQ_TILE, H, SEQ_TILE), BF16)]
        + [jax.ShapeDtypeStruct((T, IDX_HEADS * IDX_DIM), BF16),
           jax.ShapeDtypeStruct((T, H), BF16), jax.ShapeDtypeStruct((T, H), F32)],
        compiler_params=_cparams("parallel"),
        name="prep",
    )(P, P, P, P, P, P, P, P,
      jnp.tile(cq_g.astype(F32), 2).reshape(1, H), jnp.tile(ck_g.astype(F32), 2).reshape(1, H),
      dq_g.astype(F32).reshape(1, H), dk_g.astype(F32).reshape(1, H))


SUBLANES = 8
KEY_CHUNK = 8


def _max8(x):
    return jnp.max(x.reshape(x.shape[0] // SUBLANES, SUBLANES, x.shape[1]), axis=0)


def _sum8(x):
    return jnp.sum(x.reshape(x.shape[0] // SUBLANES, SUBLANES, x.shape[1]), axis=0)


def _bit_transpose32(rows):
    a = list(rows)
    j = 16
    m = 0x0000FFFF
    while j:
        k = 0
        while k < 32:
            t = (a[k] ^ lax.shift_right_logical(a[k + j], jnp.int32(j))) & jnp.int32(m)
            a[k] = a[k] ^ t
            a[k + j] = a[k + j] ^ (t << j)
            k = (k + j + 1) & ~j
        j >>= 1
        m ^= (m << j) & 0xFFFFFFFF
    return a


def _for_tiles(n_tiles, body, chunk=KEY_CHUNK):
    n_chunks = n_tiles // chunk

    def chunked(c, carry):
        body(c * chunk, chunk)
        return carry

    lax.fori_loop(0, n_chunks, chunked, 0)
    base = n_chunks * chunk
    size = chunk // 2
    while size >= 1:
        take = (n_tiles - base) // size

        def part(_, carry, base=base, size=size):
            body(base, size)
            return carry

        lax.fori_loop(0, take, part, 0)
        base = base + take * size
        size //= 2


def _diff_body(q_ref, k_ref, vt_ref, near_ref, lam_ref, sub_ref, o_ref,
               q2_ref, s_ref, mx_ref, l_ref, acc_ref, *, out_scale, n_tiles):
    blk = pl.program_id(2)
    KT = SEQ_TILE
    H = HEAD_DIM
    QT = q_ref.shape[0]
    QB = QT // KT
    HP = q_ref.shape[1] // H
    W2 = 2 * QT
    lane = lax.broadcasted_iota(I32, (1, H), 1)
    for h in range(HP):
        q = q_ref[:, h * H:(h + 1) * H]
        zero = jnp.zeros_like(q)
        q2_ref[h, 0:QT, :] = jnp.where(lane < HALF, q, zero)
        q2_ref[h, QT:, :] = jnp.where(lane < HALF, zero, q)
    mx_ref[...] = jnp.full_like(mx_ref, MASKED)

    def put(j0, n, near_idx):
        k0 = pl.multiple_of(j0 * KT, KT)
        parts = []
        for h in range(HP):
            s = _dot_nt(k_ref[pl.ds(k0, n * KT), h * H:(h + 1) * H], q2_ref[h])
            if near_idx is None:
                parts.append(s)
            else:
                b = near_ref[h, near_idx]
                parts.append(s + jnp.concatenate([b, b], axis=1))
        s = jnp.concatenate(parts, axis=1)
        for t in range(n):
            s_ref[j0 + t] = s[t * KT:(t + 1) * KT, :]
        mx_ref[...] = jnp.maximum(mx_ref[...], _max8(s))

    first_near = QB * blk - 1
    n_far = jnp.maximum(first_near, 0)
    newest = jnp.minimum(QB * blk + QB - 1, n_tiles - 1)
    _for_tiles(n_far, lambda j0, n: put(j0, n, None))

    def near(j, carry):
        put(j, 1, j - first_near)
        return carry

    lax.fori_loop(n_far, newest + 1, near, 0)

    m = jnp.max(mx_ref[...], axis=0, keepdims=True)
    l_ref[...] = jnp.zeros_like(l_ref)
    acc_ref[...] = jnp.zeros_like(acc_ref)

    def pv(j0, n):
        p = jnp.exp2(s_ref[pl.ds(j0, n)].reshape(n * KT, HP * W2) - m)
        l_ref[...] += _sum8(p)
        pb = p.astype(BF16)
        for h in range(HP):
            acc = acc_ref[:, h * W2:(h + 1) * W2]
            for t in range(n):
                acc = acc + _dot(vt_ref[j0 + t, h * H:(h + 1) * H, :],
                                 pb[t * KT:(t + 1) * KT, h * W2:(h + 1) * W2])
            acc_ref[:, h * W2:(h + 1) * W2] = acc

    _for_tiles(newest + 1, pv)

    ot = acc_ref[...] * (1.0 / jnp.sum(l_ref[...], axis=0, keepdims=True))
    for h in range(HP):
        od = ot[:, h * W2:h * W2 + QT] - lam_ref[...] * ot[:, h * W2 + QT:(h + 1) * W2]
        inv = lax.rsqrt(jnp.mean(od * od, axis=0, keepdims=True) + EPS)
        o_ref[:, h * H:(h + 1) * H] = (((od * inv) * sub_ref[...]) * out_scale).T.astype(o_ref.dtype)


def _diff(cqn, ckn, cvt, near_t, lam, subln, out_scale):
    B, Lp, W = cqn.shape
    KT = SEQ_TILE
    H = HEAD_DIM
    nt = Lp // KT
    QB = near_t.shape[1] - 1
    QT = QB * KT
    HP = DIFF_HEADS_PER_STEP
    lanes = HP * 2 * QT
    return pl.pallas_call(
        functools.partial(_diff_body, out_scale=out_scale, n_tiles=nt),
        grid=(B, N_HEADS // HP, pl.cdiv(nt, QB)),
        in_specs=[
            pl.BlockSpec((None, QT, HP * H), lambda b, g, i: (b, i, g)),
            pl.BlockSpec((None, Lp, HP * H), lambda b, g, i: (b, 0, g)),
            pl.BlockSpec((None, nt, HP * H, KT), lambda b, g, i: (b, 0, g, 0)),
            pl.BlockSpec((HP, QB + 1, KT, QT), lambda b, g, i: (g, 0, 0, 0)),
            pl.BlockSpec((1, QT), lambda b, g, i: (0, 0)),
            pl.BlockSpec((H, QT), lambda b, g, i: (0, 0)),
        ],
        out_specs=pl.BlockSpec((None, QT, HP * H), lambda b, g, i: (b, i, g)),
        out_shape=jax.ShapeDtypeStruct((B, Lp, W), BF16),
        scratch_shapes=[pltpu.VMEM((HP, 2 * QT, H), BF16),
                        pltpu.VMEM((nt, KT, lanes), F32),
                        pltpu.VMEM((SUBLANES, lanes), F32),
                        pltpu.VMEM((SUBLANES, lanes), F32),
                        pltpu.VMEM((H, lanes), F32)],
        compiler_params=_cparams("parallel", "parallel", "arbitrary"),
        name="diff",
    )(cqn, ckn, cvt, near_t, lam, subln)


def _block_first_tile(blk, n_blocks, n_tiles, tiles_per_block):
    return jnp.maximum(n_tiles - tiles_per_block * (n_blocks - blk), 0)


def _dsa_body(qi_ref, kk_ref, wq_ref, q_ref, k_ref, vt_ref, near_ref, o_ref,
              qi2_ref, w_ref, key_ref, kq_ref, plane_ref, alive_ref, q4_ref, thr_ref, s_ref, mx_ref,
              l_ref, acc_ref, *, top_k, n_tiles):
    blk = pl.program_id(1)
    KT = SEQ_TILE
    H = HEAD_DIM
    QT = q_ref.shape[0]
    QB = QT // KT
    first = _block_first_tile(blk, pl.num_programs(1), n_tiles, QB)
    newest = first + QB - 1
    lane = lax.broadcasted_iota(I32, (1, H), 1)

    for p in range(IDX_HEADS // 2):
        x = qi_ref[:, p * H:(p + 1) * H]
        zero = jnp.zeros_like(x)
        qi2_ref[p, 0:QT, :] = jnp.where(lane < HALF, x, zero)
        qi2_ref[p, QT:, :] = jnp.where(lane < HALF, zero, x)
    w_ref[...] = wq_ref[...].T
    for h in range(N_HEADS):
        q4_ref[h * QT:(h + 1) * QT, :] = q_ref[:, h * H:(h + 1) * H]
    kq_ref[...] = lax.broadcasted_iota(I32, (KT, QT), 0) - lax.broadcasted_iota(I32, (KT, QT), 1)

    def index(j0, n):
        k0 = pl.multiple_of(j0 * KT, KT)
        kt = kk_ref[pl.ds(k0, n * KT), :]
        acc = jnp.zeros((n * KT, QT), F32)
        for p in range(IDX_HEADS // 2):
            s = jnp.maximum(_dot_nt(kt, qi2_ref[p]), 0.0)
            acc = acc + s[:, 0:QT] * w_ref[2 * p:2 * p + 1, :] + s[:, QT:] * w_ref[2 * p + 1:2 * p + 2, :]
        bits = pltpu.bitcast(acc, I32)
        key = bits ^ ((bits >> 31) & jnp.int32(0x7FFFFFFF))
        tiles = []
        for t in range(n):
            kt_ = key[t * KT:(t + 1) * KT, :]
            if t >= n - QB:
                kt_ = jnp.where(kq_ref[...] <= (first - (j0 + t)) * KT, kt_, jnp.int32(KEY_NEG_INF))
            key_ref[j0 + t] = kt_
            tiles.append(kt_)
        if n == 1:
            tiles.append(None)
        for u in range(len(tiles) // 2):
            rows = []
            for tile in tiles[2 * u:2 * u + 2]:
                for r in range(KT // SUBLANES):
                    if tile is None:
                        rows.append(jnp.full((SUBLANES, QT), INT_MIN, I32))
                    else:
                        rows.append(tile[r * SUBLANES:(r + 1) * SUBLANES, :])
            planes = _bit_transpose32(rows)
            planes[0] = ~planes[0]
            pair = j0 // 2 + u
            for b in range(32):
                plane_ref[pair, b] = planes[b]

    @pl.when(blk == 0)
    def _():
        plane_ref[...] = jnp.zeros_like(plane_ref)

    _for_tiles(newest + 1, index)

    n_pairs = plane_ref.shape[0]
    for tp in range(n_pairs):
        alive_ref[tp] = jnp.full((SUBLANES, QT), -1, I32)

    def bit_step(t, carry):
        above, thr = carry
        ones = [alive_ref[tp] & plane_ref[tp, t] for tp in range(n_pairs)]
        cnt = lax.population_count(ones[0])
        for tp in range(1, n_pairs):
            cnt = cnt + lax.population_count(ones[tp])
        hit = above + jnp.sum(cnt, axis=0, keepdims=True)
        take = hit >= top_k
        for tp in range(n_pairs):
            alive_ref[tp] = jnp.where(take, ones[tp], alive_ref[tp] ^ ones[tp])
        thr = thr | jnp.where(take, jnp.left_shift(jnp.int32(1), 31 - t), 0)
        return jnp.where(take, above, hit), thr

    zero8 = jnp.zeros((SUBLANES, QT), I32)
    _, thr_u = lax.fori_loop(0, 32, bit_step, (zero8, zero8))
    thr_ref[...] = thr_u ^ jnp.int32(INT_MIN)

    mx_ref[...] = jnp.full_like(mx_ref, MASKED)

    def put(j0, n, near_idx):
        k0 = pl.multiple_of(j0 * KT, KT)
        s = _dot_nt(k_ref[pl.ds(k0, n * KT), :], q4_ref[...])
        thr1 = thr_ref[0:1, :]
        mx = mx_ref[...]
        for t in range(n):
            sel = key_ref[j0 + t] >= thr1
            parts = []
            for h in range(N_HEADS):
                sh = s[t * KT:(t + 1) * KT, h * QT:(h + 1) * QT]
                if near_idx is not None:
                    sh = sh + near_ref[h, near_idx]
                parts.append(jnp.where(sel, sh, MASKED))
            row = jnp.concatenate(parts, axis=1)
            s_ref[j0 + t] = row
            mx = jnp.maximum(mx, _max8(row))
        mx_ref[...] = mx

    first_near = first - 1
    n_far = jnp.maximum(first_near, 0)
    _for_tiles(n_far, lambda j0, n: put(j0, n, None))

    def near(j, carry):
        put(j, 1, j - first_near)
        return carry

    lax.fori_loop(n_far, newest + 1, near, 0)

    m = jnp.max(mx_ref[...], axis=0, keepdims=True)
    l_ref[...] = jnp.zeros_like(l_ref)
    acc_ref[...] = jnp.zeros_like(acc_ref)

    def pv(j0, n):
        p = jnp.exp2(s_ref[pl.ds(j0, n)].reshape(n * KT, N_HEADS * QT) - m)
        l_ref[...] += _sum8(p)
        pb = p.astype(BF16)
        acc = acc_ref[...]
        for t in range(n):
            acc = acc + _dot(vt_ref[j0 + t], pb[t * KT:(t + 1) * KT, :])
        acc_ref[...] = acc

    _for_tiles(newest + 1, pv)

    ot = acc_ref[...] * (1.0 / jnp.sum(l_ref[...], axis=0, keepdims=True))
    for h in range(N_HEADS):
        o_ref[:, h * H:(h + 1) * H] = ot[:, h * QT:(h + 1) * QT].T.astype(o_ref.dtype)


def _dsa(qib, kk, wq, dqn, dkn, dvt, near_t, top_k):
    B, Lp, H = dkn.shape
    W = dqn.shape[1]
    KT = SEQ_TILE
    NI = IDX_HEADS * IDX_DIM
    nt = Lp // KT
    QB = near_t.shape[1] - 1
    QT = QB * KT
    n_blocks = pl.cdiv(nt, QB)

    def rows(width):
        return pl.BlockSpec(
            (pl.Element(QT), pl.Element(width)),
            lambda b, i: (pl.multiple_of(b * Lp + _block_first_tile(i, n_blocks, nt, QB) * KT, KT), 0))

    seq = lambda width: pl.BlockSpec((None, Lp, width), lambda b, i: (b, 0, 0))
    n_pairs = (nt + 1) // 2
    return pl.pallas_call(
        functools.partial(_dsa_body, top_k=top_k, n_tiles=nt),
        grid=(B, n_blocks),
        in_specs=[rows(NI), seq(H), rows(H), rows(W), seq(H),
                  pl.BlockSpec((None, nt, H, KT), lambda b, i: (b, 0, 0, 0)),
                  pl.BlockSpec((N_HEADS, QB + 1, KT, QT), lambda b, i: (0, 0, 0, 0))],
        out_specs=rows(W),
        out_shape=jax.ShapeDtypeStruct((B * Lp, W), BF16),
        scratch_shapes=[
            pltpu.VMEM((IDX_HEADS // 2, 2 * QT, H), BF16),
            pltpu.VMEM((H, QT), F32),
            pltpu.VMEM((nt, KT, QT), I32),
            pltpu.VMEM((KT, QT), I32),
            pltpu.VMEM((n_pairs, 32, SUBLANES, QT), I32),
            pltpu.VMEM((n_pairs, SUBLANES, QT), I32),
            pltpu.VMEM((N_HEADS * QT, H), BF16),
            pltpu.VMEM((SUBLANES, QT), I32),
            pltpu.VMEM((nt, KT, N_HEADS * QT), F32),
            pltpu.VMEM((SUBLANES, N_HEADS * QT), F32),
            pltpu.VMEM((SUBLANES, N_HEADS * QT), F32),
            pltpu.VMEM((H, N_HEADS * QT), F32),
        ],
        compiler_params=_cparams("parallel", "arbitrary"),
        name="dsa",
    )(qib, kk, wq, dqn, dkn, dvt, near_t)


def _merge_body(h_ref, gain_ref, b0_ref, b1_ref, b2_ref, b3_ref, g0_ref, g1_ref, g2_ref, g3_ref,
                wb_ref, wo_ref, o_ref, xn_ref):
    @pl.when(pl.program_id(1) == 0)
    def _():
        x = h_ref[...]
        xn_ref[...] = ((x * _rms_inv(x)) * gain_ref[...]).astype(BF16)
        o_ref[...] = x

    xn = xn_ref[...]
    merged = None
    for m, (b_ref, g_ref) in enumerate(((b0_ref, g0_ref), (b1_ref, g1_ref), (b2_ref, g2_ref), (b3_ref, g3_ref))):
        term = jax.nn.sigmoid(_dot(xn, g_ref[...])) * _dot(b_ref[...], wb_ref[m])
        merged = term if merged is None else merged + term
    o_ref[...] += _dot(merged.astype(BF16), wo_ref[...])


def _merge(h, gain, branches, w_gate, w_branch, w_out, layer, tm, tn):
    T, D = h.shape
    W = MIX_WIDTH
    nn = D // tn
    row = lambda width: pl.BlockSpec((tm, width), lambda i, n: (i, 0))
    gate = lambda m: pl.BlockSpec((D, tn), functools.partial(lambda i, n, m: (0, m * nn + n), m=m))
    return pl.pallas_call(
        _merge_body,
        grid=(T // tm, nn),
        in_specs=[row(D), pl.BlockSpec((1, D), lambda i, n: (0, 0)), row(W), row(W), row(W), row(W),
                  gate(0), gate(1), gate(2), gate(3),
                  pl.BlockSpec((None, N_BRANCH, W, tn), lambda i, n: (layer, 0, 0, n)),
                  pl.BlockSpec((None, tn, D), lambda i, n: (layer, n, 0))],
        out_specs=row(D),
        out_shape=jax.ShapeDtypeStruct((T, D), F32),
        scratch_shapes=[pltpu.VMEM((tm, D), BF16)],
        compiler_params=_cparams("parallel", "arbitrary"),
        name="merge",
    )(h, gain.reshape(1, D), *branches, w_gate, w_gate, w_gate, w_gate, w_branch, w_out)


def _rel_bucket(n):
    max_exact = N_BUCKETS // 2
    nf = jnp.maximum(n, 1).astype(F32)
    large = max_exact + (jnp.log(nf / max_exact) / math.log(MAX_DISTANCE / max_exact)
                         * (N_BUCKETS - max_exact)).astype(I32)
    large = jnp.minimum(large, N_BUCKETS - 1)
    return jnp.where(n < max_exact, n, large)


def _bias_tiles(table, query_tiles):
    KT = SEQ_TILE
    kl = jnp.arange(KT, dtype=I32)[:, None]
    ql = jnp.arange(query_tiles * KT, dtype=I32)[None, :]
    tbl = table.astype(F32).T

    def lookup(bucket):
        out = jnp.zeros((tbl.shape[0],) + bucket.shape, F32)
        for b in range(N_BUCKETS):
            out = jnp.where(bucket[None] == b, tbl[:, b][:, None, None], out)
        return out

    far = tbl[:, N_BUCKETS - 1][:, None, None]
    tiles = []
    for t in range(query_tiles + 1):
        d = ql - kl + (1 - t) * KT
        bias = (lookup(_rel_bucket(jnp.maximum(d, 0))) - far) * LOG2E
        tiles.append(jnp.where(d >= 0, bias, MASKED))
    return jnp.stack(tiles, axis=1)


def _pack_w_in(w):
    offs = np.cumsum([0, 512, 512, 512, 512, 512, 512, 512, 512, 512, 512, 512, 128, 128,
                      IDX_HEADS * IDX_DIM, IDX_DIM, IDX_HEADS]).tolist()
    seg = lambda a, b: w[:, offs[a]:offs[b]]
    D = w.shape[0]
    used = COL_KIW + IDX_DIM + IDX_HEADS
    main = jnp.concatenate([
        seg(0, 4),
        seg(13, 14),
        seg(4, 7),
        seg(7, 10),
        seg(10, 11),
        seg(11, 12),
        seg(12, 13),
        seg(14, 16),
        jnp.zeros((D, P_COLS - used), w.dtype),
    ], axis=1).astype(BF16)
    gates = w[:, offs[16]:].astype(BF16)
    return main, gates


def _largest_tile(n, cap):
    t = cap
    while n % t:
        t //= 2
    return t


def _largest_multiple(n, unit, cap):
    return max(t for t in range(unit, cap + 1, unit) if n % t == 0)


def kernel(x, meta_tokens, rel_bias, ffn1_norm, ffn1_w_gu, ffn1_w_down, mix_norm, w_in, hgrn_lb, hgrn_gnorm, conv_w, diff_q_norm, diff_k_norm, diff_lambda, diff_subln, dsa_q_norm, dsa_k_norm, w_branch, w_out, ffn2_norm, ffn2_w_gu, ffn2_w_down):
    B, S, D = x.shape
    depth = w_in.shape[0]
    L = S + N_META
    Lp = -(-L // SEQ_TILE) * SEQ_TILE
    T = B * Lp
    top_k = min(TOPK_MAX, S // 4)
    tm_wide = _largest_tile(T, 1024)
    tm_ffn = _largest_multiple(T, SEQ_TILE, FFN_ROWS_MAX)
    tm_merge = _largest_multiple(T, SEQ_TILE, MERGE_ROWS_MAX)
    d_ff = ffn1_w_down.shape[1]
    tf = _largest_tile(d_ff, 512)

    meta = jnp.broadcast_to(meta_tokens.astype(x.dtype)[None], (B, N_META, D))
    h = jnp.concatenate([meta, x, jnp.zeros((B, Lp - L, D), x.dtype)], axis=1).reshape(T, D)

    lbs = jnp.cumsum(jax.nn.softmax(hgrn_lb.astype(F32), axis=0), axis=0)
    lbs = lbs - lbs[0:1]
    near_c = _bias_tiles(rel_bias[:, :N_HEADS], DIFF_QUERY_TILES)
    near_d = _bias_tiles(rel_bias[:, N_HEADS:], DSA_QUERY_TILES)
    diff_qt = DIFF_QUERY_TILES * SEQ_TILE

    w1_gu, w1_down, w2_gu, w2_down, wb, wo = (
        _to_bf16(w) for w in (ffn1_w_gu, ffn1_w_down, ffn2_w_gu, ffn2_w_down, w_branch, w_out))

    for l in range(depth):
        h = _ffn(h, ffn1_norm[l], w1_gu, w1_down, l, tm_ffn, tf)

        w_main, w_gate = _pack_w_in(w_in[l])
        P = _proj(h, mix_norm[l], w_main, tm_wide, PROJ_COLS_TILE)
        P3 = P.reshape(B, Lp, P_COLS)
        br_a = _hgrn(P3, lbs[l], hgrn_gnorm[l])
        br_b = _conv(P3, conv_w[l])
        cqn, ckn, cvt, dqn, dkn, dvt, qib, kk, wq = _prep(
            P, diff_q_norm[l], diff_k_norm[l], dsa_q_norm[l], dsa_k_norm[l], tm_wide)
        r3 = lambda a: a.reshape(B, Lp, a.shape[-1])
        lp = diff_lambda[l].astype(F32)
        lam_init = 0.8 - 0.6 * math.exp(-0.3 * l)
        lam = jnp.exp(jnp.sum(lp[0] * lp[1])) - jnp.exp(jnp.sum(lp[2] * lp[3])) + lam_init
        tiles = lambda a: a.reshape(B, Lp // SEQ_TILE, a.shape[-2], SEQ_TILE)
        br_c = _diff(r3(cqn), r3(ckn), tiles(cvt), near_c,
                     jnp.broadcast_to(lam, (1, diff_qt)).astype(F32),
                     jnp.broadcast_to(diff_subln[l].astype(F32)[:, None], (HEAD_DIM, diff_qt)),
                     1.0 - lam_init)
        br_d = _dsa(qib, r3(kk), wq, dqn, r3(dkn), tiles(dvt), near_d, top_k)
        branches = [a.reshape(T, MIX_WIDTH) for a in (br_a, br_b, br_c, br_d)]
        h = _merge(h, mix_norm[l], branches, w_gate, wb, wo, l, tm_merge, 256)

        last = l == depth - 1
        h = _ffn(h, ffn2_norm[l], w2_gu, w2_down, l,
                 _largest_multiple(S, SUBLANES, FFN_ROWS_MAX) if last else tm_ffn, tf,
                 keep=(B, Lp, N_META, S) if last else None)

    return h.reshape(B, S, D)
```

```python
import functools
import math

import jax
import jax.numpy as jnp
import numpy as np
from jax import lax
from jax.experimental import pallas as pl
from jax.experimental.pallas import tpu as pltpu

F32 = jnp.float32
BF16 = jnp.bfloat16
I32 = jnp.int32

EPS = 1e-6
N_META = 16
HEAD_DIM = 128
HALF = 64
N_HEADS = 4
MIX_WIDTH = N_HEADS * HEAD_DIM
IDX_HEADS = 16
IDX_DIM = 64
TOPK_MAX = 256
N_BRANCH = 4
N_BUCKETS = 32
MAX_DISTANCE = 128
CONV_W = 3
SEQ_TILE = 128
HGRN_CHUNK = 8
HGRN_GROUP = 8
DIFF_QUERY_TILES = 1
DIFF_HEADS_PER_STEP = 4
DSA_QUERY_TILES = 2
CONV_ROWS_MAX = 1408
CAST_BLOCK_BYTES = 8 * 1024 * 1024
FFN_ROWS_MAX = 1024
MASKED = -1e30
LOG2E = math.log2(math.e)
INT_MIN = -(2 ** 31)
KEY_NEG_INF = -2139095041
VMEM_LIMIT_BYTES = 56 * 1024 * 1024

COL_A = 0
COL_QI = 2048
COL_B = 3072
COL_C = 4608
COL_DQ = 6144
COL_DK = 6656
COL_DV = 6784
COL_KIW = 6912
P_COLS = 7168
PROJ_COLS_TILE = 1792


def _cparams(*sem):
    return pltpu.CompilerParams(dimension_semantics=sem, vmem_limit_bytes=VMEM_LIMIT_BYTES)


def _dot(a, b):
    return jnp.dot(a, b, preferred_element_type=F32)


def _dot_nt(a, b):
    return lax.dot_general(a, b, (((1,), (1,)), ((), ())), preferred_element_type=F32)


def _dot_tn(a, b):
    return lax.dot_general(a, b, (((0,), (0,)), ((), ())), preferred_element_type=F32)


def _rms_inv(x):
    return lax.rsqrt(jnp.mean(x * x, axis=-1, keepdims=True) + EPS)


def _cast_body(x_ref, o_ref):
    o_ref[...] = x_ref[...].astype(o_ref.dtype)


def _to_bf16(w):
    cols = w.shape[-1]
    w2 = w.reshape(-1, cols)
    n_rows = w2.shape[0]
    rows = _largest_multiple(n_rows, SUBLANES, max(SUBLANES, CAST_BLOCK_BYTES // (4 * cols)))
    out = pl.pallas_call(
        _cast_body,
        grid=(n_rows // rows,),
        in_specs=[pl.BlockSpec((rows, cols), lambda i: (i, 0))],
        out_specs=pl.BlockSpec((rows, cols), lambda i: (i, 0)),
        out_shape=jax.ShapeDtypeStruct((n_rows, cols), BF16),
        compiler_params=_cparams("parallel"),
        name="cast",
    )(w2)
    return out.reshape(w.shape)


def _ffn_body(h_ref, g_ref, wg_ref, wu_ref, wd_ref, o_ref, xn_ref):
    j = pl.program_id(1)

    @pl.when(j == 0)
    def _():
        x = h_ref[...]
        xn_ref[...] = ((x * _rms_inv(x)) * g_ref[...]).astype(BF16)
        o_ref[...] = x

    xn = xn_ref[...]
    g = _dot(xn, wg_ref[...])
    u = _dot(xn, wu_ref[...])
    a = (g * jax.nn.sigmoid(g) * (0.5 * u)).astype(BF16)
    o_ref[...] += _dot(a, wd_ref[...])


def _ffn(h, gain, w_gu, w_down, layer, tm, tf, keep=None):
    T, D = h.shape
    F = w_down.shape[1]
    nf = F // tf
    if keep is None:
        n_row_tiles = T // tm
        row_in = pl.BlockSpec((tm, D), lambda i, j: (i, 0))
        t_out = T
    else:
        B, Lp, first, n_rows = keep
        per_seq = n_rows // tm
        n_row_tiles = B * per_seq
        row_in = pl.BlockSpec((pl.Element(tm), pl.Element(D)),
                              lambda i, j: (pl.multiple_of(
                                  (i // per_seq) * Lp + first + (i % per_seq) * tm, SUBLANES), 0))
        t_out = B * n_rows
    return pl.pallas_call(
        _ffn_body,
        grid=(n_row_tiles, nf),
        in_specs=[
            row_in,
            pl.BlockSpec((1, D), lambda i, j: (0, 0)),
            pl.BlockSpec((None, D, tf), lambda i, j: (layer, 0, j)),
            pl.BlockSpec((None, D, tf), lambda i, j: (layer, 0, j + nf)),
            pl.BlockSpec((None, tf, D), lambda i, j: (layer, j, 0)),
        ],
        out_specs=pl.BlockSpec((tm, D), lambda i, j: (i, 0)),
        out_shape=jax.ShapeDtypeStruct((t_out, D), F32),
        scratch_shapes=[pltpu.VMEM((tm, D), BF16)],
        compiler_params=_cparams("parallel", "arbitrary"),
        name="ffn",
    )(h, gain.reshape(1, D), w_gu, w_gu, w_down)


def _proj_body(h_ref, g_ref, w_ref, p_ref, xn_ref):
    @pl.when(pl.program_id(1) == 0)
    def _():
        x = h_ref[...]
        xn_ref[...] = ((x * _rms_inv(x)) * g_ref[...]).astype(BF16)

    p_ref[...] = _dot(xn_ref[...], w_ref[...])


def _proj(h, gain, w, tm, tn):
    T, D = h.shape
    N = w.shape[1]
    return pl.pallas_call(
        _proj_body,
        grid=(T // tm, N // tn),
        in_specs=[
            pl.BlockSpec((tm, D), lambda i, j: (i, 0)),
            pl.BlockSpec((1, D), lambda i, j: (0, 0)),
            pl.BlockSpec((D, tn), lambda i, j: (0, j)),
        ],
        out_specs=pl.BlockSpec((tm, tn), lambda i, j: (i, j)),
        out_shape=jax.ShapeDtypeStruct((T, N), F32),
        scratch_shapes=[pltpu.VMEM((tm, D), BF16)],
        compiler_params=_cparams("parallel", "arbitrary"),
        name="proj",
    )(h, gain.reshape(1, D), w)


def _hgrn_body(q_ref, f_ref, i_ref, g_ref, loglb_ref, log1mlb_ref, omlb_ref, gn_ref, o_ref,
               st_ref, qs_ref, ks_ref, bs_ref, os_ref):
    G, R, _ = q_ref.shape
    C = HGRN_CHUNK

    @pl.when(pl.program_id(1) == 0)
    def _():
        st_ref[...] = jnp.zeros_like(st_ref)

    row = lax.broadcasted_iota(I32, (R, MIX_WIDTH), 0) % C
    for g in range(G):
        z = f_ref[g]
        log_sig = jnp.minimum(z, 0.0) - jnp.log(1.0 + jnp.exp(-jnp.abs(z)))
        y = log1mlb_ref[...] + log_sig
        a = loglb_ref[...]
        log_f = jnp.maximum(a, y) + jnp.log(1.0 + jnp.exp(-jnp.abs(a - y)))
        b = log_f * LOG2E
        for sh in [1 << e for e in range(C.bit_length() - 1)]:
            b = b + jnp.where(row >= sh, pltpu.roll(b, sh, axis=0), 0.0)
        bs_ref[g] = b
        ks_ref[g] = omlb_ref[...] * jax.nn.sigmoid(-z)
        qr = q_ref[g]
        qs_ref[g] = qr * jax.nn.sigmoid(qr)

    t_iota = lax.broadcasted_iota(I32, (C, 1), 0)

    def chunk(c, carry):
        r0 = pl.multiple_of(c * C, C)
        for g in range(G):
            for h in range(N_HEADS):
                hs = slice(h * HEAD_DIM, (h + 1) * HEAD_DIM)
                qc = qs_ref[g, pl.ds(r0, C), hs]
                kc = ks_ref[g, pl.ds(r0, C), hs]
                bc = bs_ref[g, pl.ds(r0, C), hs]
                vc = i_ref[g, pl.ds(r0, C), hs]
                st = st_ref[g, h]
                o = _dot_nt((qc * jnp.exp2(bc)).astype(BF16), st.astype(BF16))
                for s in range(C):
                    b_s = bc[s:s + 1, :]
                    k_s = kc[s:s + 1, :]
                    v_s = vc[s:s + 1, :]
                    e = jnp.exp2(bc - b_s)
                    col = jnp.sum(qc * k_s * e, axis=-1, keepdims=True)
                    col = jnp.where(t_iota >= s, col, 0.0)
                    o = o + col * v_s
                os_ref[g, pl.ds(r0, C), hs] = o
                b_last = bc[C - 1:C, :]
                kd = kc * jnp.exp2(b_last - bc)
                st_ref[g, h] = st * jnp.exp2(b_last) + _dot_tn(vc.astype(BF16), kd.astype(BF16))
        return carry

    lax.fori_loop(0, R // C, chunk, 0)

    for g in range(G):
        for h in range(N_HEADS):
            hs = slice(h * HEAD_DIM, (h + 1) * HEAD_DIM)
            o = os_ref[g, :, hs]
            on = (o * _rms_inv(o)) * gn_ref[:, hs]
            gr = g_ref[g, :, hs]
            o_ref[g, :, hs] = (on * (gr * jax.nn.sigmoid(gr))).astype(o_ref.dtype)


def _hgrn(P3, lb, gnorm):
    B, Lp, _ = P3.shape
    R = SEQ_TILE
    W = MIX_WIDTH
    G = _largest_tile(B, HGRN_GROUP)
    lb = lb.reshape(1, W).astype(F32)
    vec = pl.BlockSpec((1, W), lambda b, t: (0, 0))
    c0 = COL_A // W
    return pl.pallas_call(
        _hgrn_body,
        grid=(B // G, Lp // R),
        in_specs=[pl.BlockSpec((G, R, W), functools.partial(lambda b, t, c: (b, t, c), c=c0 + k))
                  for k in range(4)] + [vec, vec, vec, vec],
        out_specs=pl.BlockSpec((G, R, W), lambda b, t: (b, t, 0)),
        out_shape=jax.ShapeDtypeStruct((B, Lp, W), BF16),
        scratch_shapes=[pltpu.VMEM((G, N_HEADS, HEAD_DIM, HEAD_DIM), F32)]
        + [pltpu.VMEM((G, R, W), F32)] * 4,
        compiler_params=_cparams("parallel", "arbitrary"),
        name="hgrn",
    )(P3, P3, P3, P3, jnp.log(lb), jnp.log1p(-lb), 1.0 - lb, gnorm.reshape(1, W).astype(F32))


def _conv_body(b_ref, c_ref, u_ref, w_ref, o_ref, carry_ref):
    @pl.when(pl.program_id(1) == 0)
    def _():
        carry_ref[...] = jnp.zeros_like(carry_ref)

    zc = c_ref[...] * u_ref[...]
    R = zc.shape[0]
    row = lax.broadcasted_iota(I32, zc.shape, 0)
    last = carry_ref[7:8, :]
    last2 = carry_ref[6:7, :]
    z1 = jnp.where(row == 0, last, pltpu.roll(zc, 1, axis=0))
    z2 = jnp.where(row == 0, last2, jnp.where(row == 1, last, pltpu.roll(zc, 2, axis=0)))
    y = w_ref[0:1, :] * zc + w_ref[1:2, :] * z1 + w_ref[2:3, :] * z2
    o_ref[...] = (b_ref[...] * y).astype(o_ref.dtype)
    carry_ref[...] = zc[R - 8:, :]


def _conv(P3, conv_w):
    B, Lp, _ = P3.shape
    R = _largest_multiple(Lp, SEQ_TILE, CONV_ROWS_MAX)
    W = MIX_WIDTH
    c0 = COL_B // W
    w8 = jnp.zeros((8, W), F32).at[:CONV_W].set(conv_w.astype(F32))
    return pl.pallas_call(
        _conv_body,
        grid=(B, Lp // R),
        in_specs=[pl.BlockSpec((None, R, W), functools.partial(lambda b, t, c: (b, t, c), c=c0 + k))
                  for k in range(3)] + [pl.BlockSpec((8, W), lambda b, t: (0, 0))],
        out_specs=pl.BlockSpec((None, R, W), lambda b, t: (b, t, 0)),
        out_shape=jax.ShapeDtypeStruct((B, Lp, W), BF16),
        scratch_shapes=[pltpu.VMEM((8, W), F32)],
        compiler_params=_cparams("parallel", "arbitrary"),
        name="conv",
    )(P3, P3, P3, w8)


def _prep_body(cq_ref, ck_ref, cv_ref, dq_ref, dk_ref, dv_ref, kiw_ref, qi_ref,
               cqg_ref, ckg_ref, dqg_ref, dkg_ref,
               cqn_ref, ckn_ref, cvb_ref, dqn_ref, dkn_ref, dvb_ref, qib_ref, kk_ref, wq_ref):
    lane = lax.broadcasted_iota(I32, (1, HEAD_DIM), 1)
    lo = lane < HALF

    def half_norm(x, g, scale):
        sq = x * x
        ms_lo = jnp.sum(jnp.where(lo, sq, 0.0), axis=-1, keepdims=True) * (1.0 / HALF)
        ms_hi = jnp.sum(jnp.where(lo, 0.0, sq), axis=-1, keepdims=True) * (1.0 / HALF)
        inv = jnp.where(lo, lax.rsqrt(ms_lo + EPS), lax.rsqrt(ms_hi + EPS))
        return ((x * inv) * g) * scale

    for h in range(N_HEADS):
        hs = slice(h * HEAD_DIM, (h + 1) * HEAD_DIM)
        cqn_ref[:, hs] = half_norm(cq_ref[:, hs], cqg_ref[...], HALF ** -0.5 * LOG2E).astype(BF16)
        ckn_ref[:, hs] = half_norm(ck_ref[:, hs], ckg_ref[...], 1.0).astype(BF16)
        x = dq_ref[:, hs]
        dqn_ref[:, hs] = (((x * _rms_inv(x)) * dqg_ref[...]) * (HEAD_DIM ** -0.5 * LOG2E)).astype(BF16)
    x = dk_ref[...]
    dkn_ref[...] = ((x * _rms_inv(x)) * dkg_ref[...]).astype(BF16)
    for t in range(cv_ref.shape[0] // SEQ_TILE):
        rs = slice(t * SEQ_TILE, (t + 1) * SEQ_TILE)
        for h in range(N_HEADS):
            hs = slice(h * HEAD_DIM, (h + 1) * HEAD_DIM)
            cvb_ref[t, hs, :] = cv_ref[rs, hs].T.astype(BF16)
        dvb_ref[t] = dv_ref[rs, :].T.astype(BF16)
    qib_ref[...] = (qi_ref[...] * (IDX_DIM ** -0.5)).astype(BF16)
    kiw = kiw_ref[...]
    swapped = pltpu.roll(kiw, HALF, axis=1)
    kk_ref[...] = jnp.where(lo, kiw, swapped).astype(BF16)
    wq_ref[...] = swapped * (IDX_HEADS ** -0.5)


def _prep(P, cq_g, ck_g, dq_g, dk_g, tm):
    T = P.shape[0]
    W = MIX_WIDTH
    H = HEAD_DIM

    def col(width, start):
        return pl.BlockSpec((tm, width), functools.partial(lambda i, c: (i, c), c=start // width))

    vec = pl.BlockSpec((1, H), lambda i: (0, 0))
    out = lambda width: pl.BlockSpec((tm, width), lambda i: (i, 0))
    tiles = lambda width: pl.BlockSpec((tm // SEQ_TILE, width, SEQ_TILE), lambda i: (i, 0, 0))
    return pl.pallas_call(
        _prep_body,
        grid=(T // tm,),
        in_specs=[col(W, COL_C), col(W, COL_C + W), col(W, COL_C + 2 * W), col(W, COL_DQ),
                  col(H, COL_DK), col(H, COL_DV), col(H, COL_KIW), col(IDX_HEADS * IDX_DIM, COL_QI),
                  vec, vec, vec, vec],
        out_specs=[out(W), out(W), tiles(W), out(W), out(H), tiles(H), out(IDX_HEADS * IDX_DIM), out(H), out(H)],
        out_shape=[jax.ShapeDtypeStruct((T, W), BF16)] * 2
        + [jax.ShapeDtypeStruct((T // SEQ_TILE, W, SEQ_TILE), BF16), jax.ShapeDtypeStruct((T, W), BF16),
           jax.ShapeDtypeStruct((T, H), BF16), jax.ShapeDtypeStruct((T // SEQ_TILE, H, SEQ_TILE), BF16)]
        + [jax.ShapeDtypeStruct((T, IDX_HEADS * IDX_DIM), BF16),
           jax.ShapeDtypeStruct((T, H), BF16), jax.ShapeDtypeStruct((T, H), F32)],
        compiler_params=_cparams("parallel"),
        name="prep",
    )(P, P, P, P, P, P, P, P,
      jnp.tile(cq_g.astype(F32), 2).reshape(1, H), jnp.tile(ck_g.astype(F32), 2).reshape(1, H),
      dq_g.astype(F32).reshape(1, H), dk_g.astype(F32).reshape(1, H))


SUBLANES = 8
KEY_CHUNK = 8


def _max8(x):
    return jnp.max(x.reshape(x.shape[0] // SUBLANES, SUBLANES, x.shape[1]), axis=0)


def _sum8(x):
    return jnp.sum(x.reshape(x.shape[0] // SUBLANES, SUBLANES, x.shape[1]), axis=0)


def _bit_transpose32(rows):
    a = list(rows)
    j = 16
    m = 0x0000FFFF
    while j:
        k = 0
        while k < 32:
            t = (a[k] ^ lax.shift_right_logical(a[k + j], jnp.int32(j))) & jnp.int32(m)
            a[k] = a[k] ^ t
            a[k + j] = a[k + j] ^ (t << j)
            k = (k + j + 1) & ~j
        j >>= 1
        m ^= (m << j) & 0xFFFFFFFF
    return a


def _for_tiles(n_tiles, body, chunk=KEY_CHUNK):
    n_chunks = n_tiles // chunk

    def chunked(c, carry):
        body(c * chunk, chunk)
        return carry

    lax.fori_loop(0, n_chunks, chunked, 0)
    base = n_chunks * chunk
    size = chunk // 2
    while size >= 1:
        take = (n_tiles - base) // size

        def part(_, carry, base=base, size=size):
            body(base, size)
            return carry

        lax.fori_loop(0, take, part, 0)
        base = base + take * size
        size //= 2


def _diff_body(q_ref, k_ref, vt_ref, near_ref, lam_ref, sub_ref, o_ref,
               q2_ref, s_ref, mx_ref, l_ref, acc_ref, *, out_scale):
    blk = pl.program_id(2)
    KT = SEQ_TILE
    H = HEAD_DIM
    QT = q_ref.shape[0]
    QB = QT // KT
    HP = q_ref.shape[1] // H
    W2 = 2 * QT
    lane = lax.broadcasted_iota(I32, (1, H), 1)
    for h in range(HP):
        q = q_ref[:, h * H:(h + 1) * H]
        zero = jnp.zeros_like(q)
        q2_ref[h, 0:QT, :] = jnp.where(lane < HALF, q, zero)
        q2_ref[h, QT:, :] = jnp.where(lane < HALF, zero, q)
    mx_ref[...] = jnp.full_like(mx_ref, MASKED)

    def put(j0, n, near_idx):
        k0 = pl.multiple_of(j0 * KT, KT)
        parts = []
        for h in range(HP):
            s = _dot_nt(k_ref[pl.ds(k0, n * KT), h * H:(h + 1) * H], q2_ref[h])
            if near_idx is None:
                parts.append(s)
            else:
                b = jnp.concatenate([near_ref[h, near_idx + t] for t in range(n)], axis=0)
                parts.append(s + jnp.concatenate([b, b], axis=1))
        s = jnp.concatenate(parts, axis=1)
        for t in range(n):
            s_ref[j0 + t] = s[t * KT:(t + 1) * KT, :]
        mx_ref[...] = jnp.maximum(mx_ref[...], _max8(s))

    first_near = QB * blk - 1
    n_far = jnp.maximum(first_near, 0)
    newest = QB * blk + QB - 1
    _for_tiles(n_far, lambda j0, n: put(j0, n, None))

    @pl.when(first_near >= 0)
    def _():
        put(first_near, QB + 1, 0)

    @pl.when(first_near < 0)
    def _():
        put(0, QB, 1)

    m = jnp.max(mx_ref[...], axis=0, keepdims=True)
    l_ref[...] = jnp.zeros_like(l_ref)
    acc_ref[...] = jnp.zeros_like(acc_ref)

    def pv(j0, n):
        p = jnp.exp2(s_ref[pl.ds(j0, n)].reshape(n * KT, HP * W2) - m)
        l_ref[...] += _sum8(p)
        pb = p.astype(BF16)
        for h in range(HP):
            acc = acc_ref[:, h * W2:(h + 1) * W2]
            for t in range(n):
                acc = acc + _dot(vt_ref[j0 + t, h * H:(h + 1) * H, :],
                                 pb[t * KT:(t + 1) * KT, h * W2:(h + 1) * W2])
            acc_ref[:, h * W2:(h + 1) * W2] = acc

    _for_tiles(newest + 1, pv)

    ot = acc_ref[...] * (1.0 / jnp.sum(l_ref[...], axis=0, keepdims=True))
    for h in range(HP):
        od = ot[:, h * W2:h * W2 + QT] - lam_ref[...] * ot[:, h * W2 + QT:(h + 1) * W2]
        inv = lax.rsqrt(jnp.mean(od * od, axis=0, keepdims=True) + EPS)
        o_ref[:, h * H:(h + 1) * H] = (((od * inv) * sub_ref[...]) * out_scale).T.astype(o_ref.dtype)


def _diff(cqn, ckn, cvt, near_t, lam, subln, out_scale):
    B, Lp, W = cqn.shape
    KT = SEQ_TILE
    H = HEAD_DIM
    nt = Lp // KT
    QB = near_t.shape[1] - 1
    assert nt % QB == 0
    QT = QB * KT
    HP = DIFF_HEADS_PER_STEP
    lanes = HP * 2 * QT
    return pl.pallas_call(
        functools.partial(_diff_body, out_scale=out_scale),
        grid=(B, N_HEADS // HP, nt // QB),
        in_specs=[
            pl.BlockSpec((None, QT, HP * H), lambda b, g, i: (b, i, g)),
            pl.BlockSpec((None, Lp, HP * H), lambda b, g, i: (b, 0, g)),
            pl.BlockSpec((None, nt, HP * H, KT), lambda b, g, i: (b, 0, g, 0)),
            pl.BlockSpec((HP, QB + 1, KT, QT), lambda b, g, i: (g, 0, 0, 0)),
            pl.BlockSpec((1, QT), lambda b, g, i: (0, 0)),
            pl.BlockSpec((H, QT), lambda b, g, i: (0, 0)),
        ],
        out_specs=pl.BlockSpec((None, QT, HP * H), lambda b, g, i: (b, i, g)),
        out_shape=jax.ShapeDtypeStruct((B, Lp, W), BF16),
        scratch_shapes=[pltpu.VMEM((HP, 2 * QT, H), BF16),
                        pltpu.VMEM((nt, KT, lanes), F32),
                        pltpu.VMEM((SUBLANES, lanes), F32),
                        pltpu.VMEM((SUBLANES, lanes), F32),
                        pltpu.VMEM((H, lanes), F32)],
        compiler_params=_cparams("parallel", "parallel", "arbitrary"),
        name="diff",
    )(cqn, ckn, cvt, near_t, lam, subln)


def _block_first_tile(blk, n_blocks, n_tiles, tiles_per_block):
    return jnp.maximum(n_tiles - tiles_per_block * (n_blocks - blk), 0)


def _dsa_body(qi_ref, kk_ref, wq_ref, q_ref, k_ref, vt_ref, near_ref, o_ref,
              qi2_ref, w_ref, key_ref, kq_ref, plane_ref, alive_ref, q4_ref, thr_ref, s_ref, mx_ref,
              l_ref, acc_ref, *, top_k, n_tiles):
    blk = pl.program_id(1)
    KT = SEQ_TILE
    H = HEAD_DIM
    QT = q_ref.shape[0]
    QB = QT // KT
    first = _block_first_tile(blk, pl.num_programs(1), n_tiles, QB)
    newest = first + QB - 1
    lane = lax.broadcasted_iota(I32, (1, H), 1)

    for p in range(IDX_HEADS // 2):
        x = qi_ref[:, p * H:(p + 1) * H]
        zero = jnp.zeros_like(x)
        qi2_ref[p, 0:QT, :] = jnp.where(lane < HALF, x, zero)
        qi2_ref[p, QT:, :] = jnp.where(lane < HALF, zero, x)
    w_ref[...] = wq_ref[...].T
    for h in range(N_HEADS):
        q4_ref[h * QT:(h + 1) * QT, :] = q_ref[:, h * H:(h + 1) * H]
    kq_ref[...] = lax.broadcasted_iota(I32, (KT, QT), 0) - lax.broadcasted_iota(I32, (KT, QT), 1)

    def index(j0, n):
        k0 = pl.multiple_of(j0 * KT, KT)
        kt = kk_ref[pl.ds(k0, n * KT), :]
        acc = jnp.zeros((n * KT, QT), F32)
        for p in range(IDX_HEADS // 2):
            s = jnp.maximum(_dot_nt(kt, qi2_ref[p]), 0.0)
            acc = acc + s[:, 0:QT] * w_ref[2 * p:2 * p + 1, :] + s[:, QT:] * w_ref[2 * p + 1:2 * p + 2, :]
        bits = pltpu.bitcast(acc, I32)
        key = bits ^ ((bits >> 31) & jnp.int32(0x7FFFFFFF))
        tiles = []
        for t in range(n):
            kt_ = key[t * KT:(t + 1) * KT, :]
            if t >= n - QB:
                kt_ = jnp.where(kq_ref[...] <= (first - (j0 + t)) * KT, kt_, jnp.int32(KEY_NEG_INF))
            key_ref[j0 + t] = kt_
            tiles.append(kt_)
        if n == 1:
            tiles.append(None)
        for u in range(len(tiles) // 2):
            rows = []
            for tile in tiles[2 * u:2 * u + 2]:
                for r in range(KT // SUBLANES):
                    if tile is None:
                        rows.append(jnp.full((SUBLANES, QT), INT_MIN, I32))
                    else:
                        rows.append(tile[r * SUBLANES:(r + 1) * SUBLANES, :])
            planes = _bit_transpose32(rows)
            planes[0] = ~planes[0]
            pair = j0 // 2 + u
            for b in range(32):
                plane_ref[pair, b] = planes[b]

    @pl.when(blk == 0)
    def _():
        plane_ref[...] = jnp.zeros_like(plane_ref)

    _for_tiles(newest + 1, index)

    n_pairs = plane_ref.shape[0]
    for tp in range(n_pairs):
        alive_ref[tp] = jnp.full((SUBLANES, QT), -1, I32)

    def bit_step(t, carry):
        above, thr = carry
        ones = [alive_ref[tp] & plane_ref[tp, t] for tp in range(n_pairs)]
        cnt = lax.population_count(ones[0])
        for tp in range(1, n_pairs):
            cnt = cnt + lax.population_count(ones[tp])
        hit = above + jnp.sum(cnt, axis=0, keepdims=True)
        take = hit >= top_k
        for tp in range(n_pairs):
            alive_ref[tp] = jnp.where(take, ones[tp], alive_ref[tp] ^ ones[tp])
        thr = thr | jnp.where(take, jnp.left_shift(jnp.int32(1), 31 - t), 0)
        return jnp.where(take, above, hit), thr

    zero8 = jnp.zeros((SUBLANES, QT), I32)
    _, thr_u = lax.fori_loop(0, 32, bit_step, (zero8, zero8))
    thr_ref[...] = thr_u ^ jnp.int32(INT_MIN)

    mx_ref[...] = jnp.full_like(mx_ref, MASKED)

    def put(j0, n, near_idx):
        k0 = pl.multiple_of(j0 * KT, KT)
        s = _dot_nt(k_ref[pl.ds(k0, n * KT), :], q4_ref[...])
        thr1 = thr_ref[0:1, :]
        mx = mx_ref[...]
        for t in range(n):
            sel = key_ref[j0 + t] >= thr1
            parts = []
            for h in range(N_HEADS):
                sh = s[t * KT:(t + 1) * KT, h * QT:(h + 1) * QT]
                if near_idx is not None:
                    sh = sh + near_ref[h, near_idx + t]
                parts.append(jnp.where(sel, sh, MASKED))
            row = jnp.concatenate(parts, axis=1)
            s_ref[j0 + t] = row
            mx = jnp.maximum(mx, _max8(row))
        mx_ref[...] = mx

    first_near = first - 1
    n_far = jnp.maximum(first_near, 0)
    _for_tiles(n_far, lambda j0, n: put(j0, n, None))

    @pl.when(first_near >= 0)
    def _():
        put(first_near, QB + 1, 0)

    @pl.when(first_near < 0)
    def _():
        put(0, QB, 1)

    m = jnp.max(mx_ref[...], axis=0, keepdims=True)
    l_ref[...] = jnp.zeros_like(l_ref)
    acc_ref[...] = jnp.zeros_like(acc_ref)

    def pv(j0, n):
        p = jnp.exp2(s_ref[pl.ds(j0, n)].reshape(n * KT, N_HEADS * QT) - m)
        l_ref[...] += _sum8(p)
        pb = p.astype(BF16)
        acc = acc_ref[...]
        for t in range(n):
            acc = acc + _dot(vt_ref[j0 + t], pb[t * KT:(t + 1) * KT, :])
        acc_ref[...] = acc

    _for_tiles(newest + 1, pv)

    ot = acc_ref[...] * (1.0 / jnp.sum(l_ref[...], axis=0, keepdims=True))
    for h in range(N_HEADS):
        o_ref[:, h * H:(h + 1) * H] = ot[:, h * QT:(h + 1) * QT].T.astype(o_ref.dtype)


def _dsa(qib, kk, wq, dqn, dkn, dvt, near_t, top_k):
    B, Lp, H = dkn.shape
    W = dqn.shape[1]
    KT = SEQ_TILE
    NI = IDX_HEADS * IDX_DIM
    nt = Lp // KT
    QB = near_t.shape[1] - 1
    QT = QB * KT
    n_blocks = pl.cdiv(nt, QB)

    def rows(width):
        return pl.BlockSpec(
            (pl.Element(QT), pl.Element(width)),
            lambda b, i: (pl.multiple_of(b * Lp + _block_first_tile(i, n_blocks, nt, QB) * KT, KT), 0))

    seq = lambda width: pl.BlockSpec((None, Lp, width), lambda b, i: (b, 0, 0))
    n_pairs = (nt + 1) // 2
    return pl.pallas_call(
        functools.partial(_dsa_body, top_k=top_k, n_tiles=nt),
        grid=(B, n_blocks),
        in_specs=[rows(NI), seq(H), rows(H), rows(W), seq(H),
                  pl.BlockSpec((None, nt, H, KT), lambda b, i: (b, 0, 0, 0)),
                  pl.BlockSpec((N_HEADS, QB + 1, KT, QT), lambda b, i: (0, 0, 0, 0))],
        out_specs=rows(W),
        out_shape=jax.ShapeDtypeStruct((B * Lp, W), BF16),
        scratch_shapes=[
            pltpu.VMEM((IDX_HEADS // 2, 2 * QT, H), BF16),
            pltpu.VMEM((H, QT), F32),
            pltpu.VMEM((nt, KT, QT), I32),
            pltpu.VMEM((KT, QT), I32),
            pltpu.VMEM((n_pairs, 32, SUBLANES, QT), I32),
            pltpu.VMEM((n_pairs, SUBLANES, QT), I32),
            pltpu.VMEM((N_HEADS * QT, H), BF16),
            pltpu.VMEM((SUBLANES, QT), I32),
            pltpu.VMEM((nt, KT, N_HEADS * QT), F32),
            pltpu.VMEM((SUBLANES, N_HEADS * QT), F32),
            pltpu.VMEM((SUBLANES, N_HEADS * QT), F32),
            pltpu.VMEM((H, N_HEADS * QT), F32),
        ],
        compiler_params=_cparams("parallel", "arbitrary"),
        name="dsa",
    )(qib, kk, wq, dqn, dkn, dvt, near_t)


def _merge_body(h_ref, gain_ref, b0_ref, b1_ref, b2_ref, b3_ref, g0_ref, g1_ref, g2_ref, g3_ref,
                wb_ref, wo_ref, o_ref, xn_ref):
    @pl.when(pl.program_id(1) == 0)
    def _():
        x = h_ref[...]
        xn_ref[...] = ((x * _rms_inv(x)) * gain_ref[...]).astype(BF16)
        o_ref[...] = x

    xn = xn_ref[...]
    merged = None
    for m, (b_ref, g_ref) in enumerate(((b0_ref, g0_ref), (b1_ref, g1_ref), (b2_ref, g2_ref), (b3_ref, g3_ref))):
        term = jax.nn.sigmoid(_dot(xn, g_ref[...])) * _dot(b_ref[...], wb_ref[m])
        merged = term if merged is None else merged + term
    o_ref[...] += _dot(merged.astype(BF16), wo_ref[...])


def _merge(h, gain, branches, w_gate, w_branch, w_out, layer, tm, tn):
    T, D = h.shape
    W = MIX_WIDTH
    nn = D // tn
    row = lambda width: pl.BlockSpec((tm, width), lambda i, n: (i, 0))
    gate = lambda m: pl.BlockSpec((D, tn), functools.partial(lambda i, n, m: (0, m * nn + n), m=m))
    return pl.pallas_call(
        _merge_body,
        grid=(T // tm, nn),
        in_specs=[row(D), pl.BlockSpec((1, D), lambda i, n: (0, 0)), row(W), row(W), row(W), row(W),
                  gate(0), gate(1), gate(2), gate(3),
                  pl.BlockSpec((None, N_BRANCH, W, tn), lambda i, n: (layer, 0, 0, n)),
                  pl.BlockSpec((None, tn, D), lambda i, n: (layer, n, 0))],
        out_specs=row(D),
        out_shape=jax.ShapeDtypeStruct((T, D), F32),
        scratch_shapes=[pltpu.VMEM((tm, D), BF16)],
        compiler_params=_cparams("parallel", "arbitrary"),
        name="merge",
    )(h, gain.reshape(1, D), *branches, w_gate, w_gate, w_gate, w_gate, w_branch, w_out)


def _rel_bucket(n):
    max_exact = N_BUCKETS // 2
    nf = jnp.maximum(n, 1).astype(F32)
    large = max_exact + (jnp.log(nf / max_exact) / math.log(MAX_DISTANCE / max_exact)
                         * (N_BUCKETS - max_exact)).astype(I32)
    large = jnp.minimum(large, N_BUCKETS - 1)
    return jnp.where(n < max_exact, n, large)


def _bias_tiles(table, query_tiles):
    KT = SEQ_TILE
    kl = jnp.arange(KT, dtype=I32)[:, None]
    ql = jnp.arange(query_tiles * KT, dtype=I32)[None, :]
    tbl = table.astype(F32).T

    def lookup(bucket):
        out = jnp.zeros((tbl.shape[0],) + bucket.shape, F32)
        for b in range(N_BUCKETS):
            out = jnp.where(bucket[None] == b, tbl[:, b][:, None, None], out)
        return out

    far = tbl[:, N_BUCKETS - 1][:, None, None]
    tiles = []
    for t in range(query_tiles + 1):
        d = ql - kl + (1 - t) * KT
        bias = (lookup(_rel_bucket(jnp.maximum(d, 0))) - far) * LOG2E
        tiles.append(jnp.where(d >= 0, bias, MASKED))
    return jnp.stack(tiles, axis=1)


def _pack_w_in(w):
    offs = np.cumsum([0, 512, 512, 512, 512, 512, 512, 512, 512, 512, 512, 512, 128, 128,
                      IDX_HEADS * IDX_DIM, IDX_DIM, IDX_HEADS]).tolist()
    seg = lambda a, b: w[:, offs[a]:offs[b]]
    D = w.shape[0]
    used = COL_KIW + IDX_DIM + IDX_HEADS
    main = jnp.concatenate([
        seg(0, 4),
        seg(13, 14),
        seg(4, 7),
        seg(7, 10),
        seg(10, 11),
        seg(11, 12),
        seg(12, 13),
        seg(14, 16),
        jnp.zeros((D, P_COLS - used), w.dtype),
    ], axis=1).astype(BF16)
    gates = w[:, offs[16]:].astype(BF16)
    return main, gates


def _largest_tile(n, cap):
    t = cap
    while n % t:
        t //= 2
    return t


def _largest_multiple(n, unit, cap):
    return max(t for t in range(unit, cap + 1, unit) if n % t == 0)


def kernel(x, meta_tokens, rel_bias, ffn1_norm, ffn1_w_gu, ffn1_w_down, mix_norm, w_in, hgrn_lb, hgrn_gnorm, conv_w, diff_q_norm, diff_k_norm, diff_lambda, diff_subln, dsa_q_norm, dsa_k_norm, w_branch, w_out, ffn2_norm, ffn2_w_gu, ffn2_w_down):
    B, S, D = x.shape
    depth = w_in.shape[0]
    L = S + N_META
    Lp = -(-L // SEQ_TILE) * SEQ_TILE
    T = B * Lp
    top_k = min(TOPK_MAX, S // 4)
    tm = _largest_tile(T, 512)
    tm_wide = _largest_tile(T, 1024)
    tm_ffn = _largest_multiple(T, SEQ_TILE, FFN_ROWS_MAX)
    d_ff = ffn1_w_down.shape[1]
    tf = _largest_tile(d_ff, 512)

    meta = jnp.broadcast_to(meta_tokens.astype(x.dtype)[None], (B, N_META, D))
    h = jnp.concatenate([meta, x, jnp.zeros((B, Lp - L, D), x.dtype)], axis=1).reshape(T, D)

    lbs = jnp.cumsum(jax.nn.softmax(hgrn_lb.astype(F32), axis=0), axis=0)
    lbs = lbs - lbs[0:1]
    near_c = _bias_tiles(rel_bias[:, :N_HEADS], DIFF_QUERY_TILES)
    near_d = _bias_tiles(rel_bias[:, N_HEADS:], min(DSA_QUERY_TILES, Lp // SEQ_TILE))
    diff_qt = DIFF_QUERY_TILES * SEQ_TILE

    w1_gu, w1_down, w2_gu, w2_down, wb, wo = (
        _to_bf16(w) for w in (ffn1_w_gu, ffn1_w_down, ffn2_w_gu, ffn2_w_down, w_branch, w_out))

    for l in range(depth):
        h = _ffn(h, ffn1_norm[l], w1_gu, w1_down, l, tm_ffn, tf)

        w_main, w_gate = _pack_w_in(w_in[l])
        P = _proj(h, mix_norm[l], w_main, tm_wide, PROJ_COLS_TILE)
        P3 = P.reshape(B, Lp, P_COLS)
        br_a = _hgrn(P3, lbs[l], hgrn_gnorm[l])
        br_b = _conv(P3, conv_w[l])
        cqn, ckn, cvt, dqn, dkn, dvt, qib, kk, wq = _prep(
            P, diff_q_norm[l], diff_k_norm[l], dsa_q_norm[l], dsa_k_norm[l], tm)
        r3 = lambda a: a.reshape(B, Lp, a.shape[-1])
        lp = diff_lambda[l].astype(F32)
        lam_init = 0.8 - 0.6 * math.exp(-0.3 * l)
        lam = jnp.exp(jnp.sum(lp[0] * lp[1])) - jnp.exp(jnp.sum(lp[2] * lp[3])) + lam_init
        tiles = lambda a: a.reshape(B, Lp // SEQ_TILE, a.shape[-2], SEQ_TILE)
        br_c = _diff(r3(cqn), r3(ckn), tiles(cvt), near_c,
                     jnp.broadcast_to(lam, (1, diff_qt)).astype(F32),
                     jnp.broadcast_to(diff_subln[l].astype(F32)[:, None], (HEAD_DIM, diff_qt)),
                     1.0 - lam_init)
        br_d = _dsa(qib, r3(kk), wq, dqn, r3(dkn), tiles(dvt), near_d, top_k)
        branches = [a.reshape(T, MIX_WIDTH) for a in (br_a, br_b, br_c, br_d)]
        h = _merge(h, mix_norm[l], branches, w_gate, wb, wo, l, tm, 512)

        last = l == depth - 1
        h = _ffn(h, ffn2_norm[l], w2_gu, w2_down, l,
                 _largest_multiple(S, SUBLANES, FFN_ROWS_MAX) if last else tm_ffn, tf,
                 keep=(B, Lp, N_META, S) if last else None)

    return h.reshape(B, S, D)
```

```python
import functools
import math

import jax
import jax.numpy as jnp
import numpy as np
from jax import lax
from jax.experimental import pallas as pl
from jax.experimental.pallas import tpu as pltpu

F32 = jnp.float32
BF16 = jnp.bfloat16
I32 = jnp.int32

EPS = 1e-6
N_META = 16
HEAD_DIM = 128
HALF = 64
N_HEADS = 4
MIX_WIDTH = N_HEADS * HEAD_DIM
IDX_HEADS = 16
IDX_DIM = 64
TOPK_MAX = 256
N_BRANCH = 4
N_BUCKETS = 32
MAX_DISTANCE = 128
CONV_W = 3
SEQ_TILE = 128
HGRN_CHUNK = 8
HGRN_GROUP = 8
DIFF_QUERY_TILES = 1
DIFF_HEADS_PER_STEP = 4
DSA_QUERY_TILES = 2
CONV_ROWS_MAX = 1408
CAST_BLOCK_BYTES = 8 * 1024 * 1024
FFN_ROWS_MAX = 1024
MASKED = -1e30
LOG2E = math.log2(math.e)
INT_MIN = -(2 ** 31)
KEY_NEG_INF = -2139095041
VMEM_LIMIT_BYTES = 56 * 1024 * 1024

COL_A = 0
COL_QI = 2048
COL_B = 3072
COL_C = 4608
COL_DQ = 6144
COL_DK = 6656
COL_DV = 6784
COL_KIW = 6912
P_COLS = 7168
PROJ_COLS_TILE = 1792


def _cparams(*sem):
    return pltpu.CompilerParams(dimension_semantics=sem, vmem_limit_bytes=VMEM_LIMIT_BYTES)


def _dot(a, b):
    return jnp.dot(a, b, preferred_element_type=F32)


def _dot_nt(a, b):
    return lax.dot_general(a, b, (((1,), (1,)), ((), ())), preferred_element_type=F32)


def _dot_tn(a, b):
    return lax.dot_general(a, b, (((0,), (0,)), ((), ())), preferred_element_type=F32)


def _rms_inv(x):
    return lax.rsqrt(jnp.mean(x * x, axis=-1, keepdims=True) + EPS)


def _cast_body(x_ref, o_ref):
    o_ref[...] = x_ref[...].astype(o_ref.dtype)


def _to_bf16(w):
    cols = w.shape[-1]
    w2 = w.reshape(-1, cols)
    n_rows = w2.shape[0]
    rows = _largest_multiple(n_rows, SUBLANES, max(SUBLANES, CAST_BLOCK_BYTES // (4 * cols)))
    out = pl.pallas_call(
        _cast_body,
        grid=(n_rows // rows,),
        in_specs=[pl.BlockSpec((rows, cols), lambda i: (i, 0))],
        out_specs=pl.BlockSpec((rows, cols), lambda i: (i, 0)),
        out_shape=jax.ShapeDtypeStruct((n_rows, cols), BF16),
        compiler_params=_cparams("parallel"),
        name="cast",
    )(w2)
    return out.reshape(w.shape)


def _ffn_body(h_ref, g_ref, wg_ref, wu_ref, wd_ref, o_ref, xn_ref):
    j = pl.program_id(1)

    @pl.when(j == 0)
    def _():
        x = h_ref[...]
        xn_ref[...] = ((x * _rms_inv(x)) * g_ref[...]).astype(BF16)
        o_ref[...] = x

    xn = xn_ref[...]
    g = _dot(xn, wg_ref[...])
    u = _dot(xn, wu_ref[...])
    a = (g * jax.nn.sigmoid(g) * (0.5 * u)).astype(BF16)
    o_ref[...] += _dot(a, wd_ref[...])


def _ffn(h, gain, w_gu, w_down, layer, tm, tf, keep=None):
    T, D = h.shape
    F = w_down.shape[1]
    nf = F // tf
    if keep is None:
        n_row_tiles = T // tm
        row_in = pl.BlockSpec((tm, D), lambda i, j: (i, 0))
        t_out = T
    else:
        B, Lp, first, n_rows = keep
        per_seq = n_rows // tm
        n_row_tiles = B * per_seq
        row_in = pl.BlockSpec((pl.Element(tm), pl.Element(D)),
                              lambda i, j: (pl.multiple_of(
                                  (i // per_seq) * Lp + first + (i % per_seq) * tm, SUBLANES), 0))
        t_out = B * n_rows
    return pl.pallas_call(
        _ffn_body,
        grid=(n_row_tiles, nf),
        in_specs=[
            row_in,
            pl.BlockSpec((1, D), lambda i, j: (0, 0)),
            pl.BlockSpec((None, D, tf), lambda i, j: (layer, 0, j)),
            pl.BlockSpec((None, D, tf), lambda i, j: (layer, 0, j + nf)),
            pl.BlockSpec((None, tf, D), lambda i, j: (layer, j, 0)),
        ],
        out_specs=pl.BlockSpec((tm, D), lambda i, j: (i, 0)),
        out_shape=jax.ShapeDtypeStruct((t_out, D), F32),
        scratch_shapes=[pltpu.VMEM((tm, D), BF16)],
        compiler_params=_cparams("parallel", "arbitrary"),
        name="ffn",
    )(h, gain.reshape(1, D), w_gu, w_gu, w_down)


def _proj_body(h_ref, g_ref, w_ref, p_ref, xn_ref):
    @pl.when(pl.program_id(1) == 0)
    def _():
        x = h_ref[...]
        xn_ref[...] = ((x * _rms_inv(x)) * g_ref[...]).astype(BF16)

    p_ref[...] = _dot(xn_ref[...], w_ref[...])


def _proj(h, gain, w, tm, tn):
    T, D = h.shape
    N = w.shape[1]
    return pl.pallas_call(
        _proj_body,
        grid=(T // tm, N // tn),
        in_specs=[
            pl.BlockSpec((tm, D), lambda i, j: (i, 0)),
            pl.BlockSpec((1, D), lambda i, j: (0, 0)),
            pl.BlockSpec((D, tn), lambda i, j: (0, j)),
        ],
        out_specs=pl.BlockSpec((tm, tn), lambda i, j: (i, j)),
        out_shape=jax.ShapeDtypeStruct((T, N), F32),
        scratch_shapes=[pltpu.VMEM((tm, D), BF16)],
        compiler_params=_cparams("parallel", "arbitrary"),
        name="proj",
    )(h, gain.reshape(1, D), w)


def _hgrn_body(q_ref, f_ref, i_ref, g_ref, loglb_ref, log1mlb_ref, omlb_ref, gn_ref, o_ref,
               st_ref, qs_ref, ks_ref, bs_ref, os_ref):
    G, R, _ = q_ref.shape
    C = HGRN_CHUNK

    @pl.when(pl.program_id(1) == 0)
    def _():
        st_ref[...] = jnp.zeros_like(st_ref)

    row = lax.broadcasted_iota(I32, (R, MIX_WIDTH), 0) % C
    for g in range(G):
        z = f_ref[g]
        log_sig = jnp.minimum(z, 0.0) - jnp.log(1.0 + jnp.exp(-jnp.abs(z)))
        y = log1mlb_ref[...] + log_sig
        a = loglb_ref[...]
        log_f = jnp.maximum(a, y) + jnp.log(1.0 + jnp.exp(-jnp.abs(a - y)))
        b = log_f * LOG2E
        for sh in [1 << e for e in range(C.bit_length() - 1)]:
            b = b + jnp.where(row >= sh, pltpu.roll(b, sh, axis=0), 0.0)
        bs_ref[g] = b
        ks_ref[g] = omlb_ref[...] * jax.nn.sigmoid(-z)
        qr = q_ref[g]
        qs_ref[g] = qr * jax.nn.sigmoid(qr)

    t_iota = lax.broadcasted_iota(I32, (C, 1), 0)

    def chunk(c, carry):
        r0 = pl.multiple_of(c * C, C)
        for g in range(G):
            for h in range(N_HEADS):
                hs = slice(h * HEAD_DIM, (h + 1) * HEAD_DIM)
                qc = qs_ref[g, pl.ds(r0, C), hs]
                kc = ks_ref[g, pl.ds(r0, C), hs]
                bc = bs_ref[g, pl.ds(r0, C), hs]
                vc = i_ref[g, pl.ds(r0, C), hs]
                st = st_ref[g, h]
                o = _dot_nt((qc * jnp.exp2(bc)).astype(BF16), st.astype(BF16))
                for s in range(C):
                    b_s = bc[s:s + 1, :]
                    k_s = kc[s:s + 1, :]
                    v_s = vc[s:s + 1, :]
                    e = jnp.exp2(bc - b_s)
                    col = jnp.sum(qc * k_s * e, axis=-1, keepdims=True)
                    col = jnp.where(t_iota >= s, col, 0.0)
                    o = o + col * v_s
                os_ref[g, pl.ds(r0, C), hs] = o
                b_last = bc[C - 1:C, :]
                kd = kc * jnp.exp2(b_last - bc)
                st_ref[g, h] = st * jnp.exp2(b_last) + _dot_tn(vc.astype(BF16), kd.astype(BF16))
        return carry

    lax.fori_loop(0, R // C, chunk, 0)

    for g in range(G):
        for h in range(N_HEADS):
            hs = slice(h * HEAD_DIM, (h + 1) * HEAD_DIM)
            o = os_ref[g, :, hs]
            on = (o * _rms_inv(o)) * gn_ref[:, hs]
            gr = g_ref[g, :, hs]
            o_ref[g, :, hs] = (on * (gr * jax.nn.sigmoid(gr))).astype(o_ref.dtype)


def _hgrn(P3, lb, gnorm):
    B, Lp, _ = P3.shape
    R = SEQ_TILE
    W = MIX_WIDTH
    G = _largest_tile(B, HGRN_GROUP)
    lb = lb.reshape(1, W).astype(F32)
    vec = pl.BlockSpec((1, W), lambda b, t: (0, 0))
    c0 = COL_A // W
    return pl.pallas_call(
        _hgrn_body,
        grid=(B // G, Lp // R),
        in_specs=[pl.BlockSpec((G, R, W), functools.partial(lambda b, t, c: (b, t, c), c=c0 + k))
                  for k in range(4)] + [vec, vec, vec, vec],
        out_specs=pl.BlockSpec((G, R, W), lambda b, t: (b, t, 0)),
        out_shape=jax.ShapeDtypeStruct((B, Lp, W), BF16),
        scratch_shapes=[pltpu.VMEM((G, N_HEADS, HEAD_DIM, HEAD_DIM), F32)]
        + [pltpu.VMEM((G, R, W), F32)] * 4,
        compiler_params=_cparams("parallel", "arbitrary"),
        name="hgrn",
    )(P3, P3, P3, P3, jnp.log(lb), jnp.log1p(-lb), 1.0 - lb, gnorm.reshape(1, W).astype(F32))


def _conv_body(b_ref, c_ref, u_ref, w_ref, o_ref, carry_ref):
    @pl.when(pl.program_id(1) == 0)
    def _():
        carry_ref[...] = jnp.zeros_like(carry_ref)

    zc = c_ref[...] * u_ref[...]
    R = zc.shape[0]
    row = lax.broadcasted_iota(I32, zc.shape, 0)
    last = carry_ref[7:8, :]
    last2 = carry_ref[6:7, :]
    z1 = jnp.where(row == 0, last, pltpu.roll(zc, 1, axis=0))
    z2 = jnp.where(row == 0, last2, jnp.where(row == 1, last, pltpu.roll(zc, 2, axis=0)))
    y = w_ref[0:1, :] * zc + w_ref[1:2, :] * z1 + w_ref[2:3, :] * z2
    o_ref[...] = (b_ref[...] * y).astype(o_ref.dtype)
    carry_ref[...] = zc[R - 8:, :]


def _conv(P3, conv_w):
    B, Lp, _ = P3.shape
    R = _largest_multiple(Lp, SEQ_TILE, CONV_ROWS_MAX)
    W = MIX_WIDTH
    c0 = COL_B // W
    w8 = jnp.zeros((8, W), F32).at[:CONV_W].set(conv_w.astype(F32))
    return pl.pallas_call(
        _conv_body,
        grid=(B, Lp // R),
        in_specs=[pl.BlockSpec((None, R, W), functools.partial(lambda b, t, c: (b, t, c), c=c0 + k))
                  for k in range(3)] + [pl.BlockSpec((8, W), lambda b, t: (0, 0))],
        out_specs=pl.BlockSpec((None, R, W), lambda b, t: (b, t, 0)),
        out_shape=jax.ShapeDtypeStruct((B, Lp, W), BF16),
        scratch_shapes=[pltpu.VMEM((8, W), F32)],
        compiler_params=_cparams("parallel", "arbitrary"),
        name="conv",
    )(P3, P3, P3, w8)


def _prep_body(cq_ref, ck_ref, cv_ref, dq_ref, dk_ref, dv_ref, kiw_ref, qi_ref,
               cqg_ref, ckg_ref, dqg_ref, dkg_ref,
               cqn_ref, ckn_ref, cvb_ref, dqn_ref, dkn_ref, dvb_ref, qib_ref, kk_ref, wq_ref):
    lane = lax.broadcasted_iota(I32, (1, HEAD_DIM), 1)
    lo = lane < HALF

    def half_norm(x, g, scale):
        sq = x * x
        ms_lo = jnp.sum(jnp.where(lo, sq, 0.0), axis=-1, keepdims=True) * (1.0 / HALF)
        ms_hi = jnp.sum(jnp.where(lo, 0.0, sq), axis=-1, keepdims=True) * (1.0 / HALF)
        inv = jnp.where(lo, lax.rsqrt(ms_lo + EPS), lax.rsqrt(ms_hi + EPS))
        return ((x * inv) * g) * scale

    for h in range(N_HEADS):
        hs = slice(h * HEAD_DIM, (h + 1) * HEAD_DIM)
        cqn_ref[:, hs] = half_norm(cq_ref[:, hs], cqg_ref[...], HALF ** -0.5 * LOG2E).astype(BF16)
        ckn_ref[:, hs] = half_norm(ck_ref[:, hs], ckg_ref[...], 1.0).astype(BF16)
        x = dq_ref[:, hs]
        dqn_ref[:, hs] = (((x * _rms_inv(x)) * dqg_ref[...]) * (HEAD_DIM ** -0.5 * LOG2E)).astype(BF16)
    x = dk_ref[...]
    dkn_ref[...] = ((x * _rms_inv(x)) * dkg_ref[...]).astype(BF16)
    for t in range(cv_ref.shape[0] // SEQ_TILE):
        rs = slice(t * SEQ_TILE, (t + 1) * SEQ_TILE)
        for h in range(N_HEADS):
            hs = slice(h * HEAD_DIM, (h + 1) * HEAD_DIM)
            cvb_ref[t, hs, :] = cv_ref[rs, hs].T.astype(BF16)
        dvb_ref[t] = dv_ref[rs, :].T.astype(BF16)
    qib_ref[...] = (qi_ref[...] * (IDX_DIM ** -0.5)).astype(BF16)
    kiw = kiw_ref[...]
    swapped = pltpu.roll(kiw, HALF, axis=1)
    kk_ref[...] = jnp.where(lo, kiw, swapped).astype(BF16)
    wq_ref[...] = swapped * (IDX_HEADS ** -0.5)


def _prep(P, cq_g, ck_g, dq_g, dk_g, tm):
    T = P.shape[0]
    W = MIX_WIDTH
    H = HEAD_DIM

    def col(width, start):
        return pl.BlockSpec((tm, width), functools.partial(lambda i, c: (i, c), c=start // width))

    vec = pl.BlockSpec((1, H), lambda i: (0, 0))
    out = lambda width: pl.BlockSpec((tm, width), lambda i: (i, 0))
    tiles = lambda width: pl.BlockSpec((tm // SEQ_TILE, width, SEQ_TILE), lambda i: (i, 0, 0))
    return pl.pallas_call(
        _prep_body,
        grid=(T // tm,),
        in_specs=[col(W, COL_C), col(W, COL_C + W), col(W, COL_C + 2 * W), col(W, COL_DQ),
                  col(H, COL_DK), col(H, COL_DV), col(H, COL_KIW), col(IDX_HEADS * IDX_DIM, COL_QI),
                  vec, vec, vec, vec],
        out_specs=[out(W), out(W), tiles(W), out(W), out(H), tiles(H), out(IDX_HEADS * IDX_DIM), out(H), out(H)],
        out_shape=[jax.ShapeDtypeStruct((T, W), BF16)] * 2
        + [jax.ShapeDtypeStruct((T // SEQ_TILE, W, SEQ_TILE), BF16), jax.ShapeDtypeStruct((T, W), BF16),
           jax.ShapeDtypeStruct((T, H), BF16), jax.ShapeDtypeStruct((T // SEQ_TILE, H, SEQ_TILE), BF16)]
        + [jax.ShapeDtypeStruct((T, IDX_HEADS * IDX_DIM), BF16),
           jax.ShapeDtypeStruct((T, H), BF16), jax.ShapeDtypeStruct((T, H), F32)],
        compiler_params=_cparams("parallel"),
        name="prep",
    )(P, P, P, P, P, P, P, P,
      jnp.tile(cq_g.astype(F32), 2).reshape(1, H), jnp.tile(ck_g.astype(F32), 2).reshape(1, H),
      dq_g.astype(F32).reshape(1, H), dk_g.astype(F32).reshape(1, H))


SUBLANES = 8
KEY_CHUNK = 16
INDEX_CHUNK = 8


def _max8(x):
    return jnp.max(x.reshape(x.shape[0] // SUBLANES, SUBLANES, x.shape[1]), axis=0)


def _sum8(x):
    return jnp.sum(x.reshape(x.shape[0] // SUBLANES, SUBLANES, x.shape[1]), axis=0)


def _bit_transpose32(rows):
    a = list(rows)
    j = 16
    m = 0x0000FFFF
    while j:
        k = 0
        while k < 32:
            t = (a[k] ^ lax.shift_right_logical(a[k + j], jnp.int32(j))) & jnp.int32(m)
            a[k] = a[k] ^ t
            a[k + j] = a[k + j] ^ (t << j)
            k = (k + j + 1) & ~j
        j >>= 1
        m ^= (m << j) & 0xFFFFFFFF
    return a


def _for_tiles(n_tiles, body, chunk=KEY_CHUNK):
    n_chunks = n_tiles // chunk

    def chunked(c, carry):
        body(c * chunk, chunk)
        return carry

    lax.fori_loop(0, n_chunks, chunked, 0)
    base = n_chunks * chunk
    size = chunk // 2
    while size >= 1:
        take = (n_tiles - base) // size

        def part(_, carry, base=base, size=size):
            body(base, size)
            return carry

        lax.fori_loop(0, take, part, 0)
        base = base + take * size
        size //= 2


def _diff_body(q_ref, k_ref, vt_ref, near_ref, lam_ref, sub_ref, o_ref,
               q2_ref, s_ref, mx_ref, l_ref, acc_ref, *, out_scale):
    blk = pl.program_id(2)
    KT = SEQ_TILE
    H = HEAD_DIM
    QT = q_ref.shape[0]
    QB = QT // KT
    HP = q_ref.shape[1] // H
    W2 = 2 * QT
    lane = lax.broadcasted_iota(I32, (1, H), 1)
    for h in range(HP):
        q = q_ref[:, h * H:(h + 1) * H]
        zero = jnp.zeros_like(q)
        q2_ref[h, 0:QT, :] = jnp.where(lane < HALF, q, zero)
        q2_ref[h, QT:, :] = jnp.where(lane < HALF, zero, q)
    mx_ref[...] = jnp.full_like(mx_ref, MASKED)

    def put(j0, n, near_idx):
        k0 = pl.multiple_of(j0 * KT, KT)
        parts = []
        for h in range(HP):
            s = _dot_nt(k_ref[pl.ds(k0, n * KT), h * H:(h + 1) * H], q2_ref[h])
            if near_idx is None:
                parts.append(s)
            else:
                b = jnp.concatenate([near_ref[h, near_idx + t] for t in range(n)], axis=0)
                parts.append(s + jnp.concatenate([b, b], axis=1))
        s = jnp.concatenate(parts, axis=1)
        for t in range(n):
            s_ref[j0 + t] = s[t * KT:(t + 1) * KT, :]
        mx_ref[...] = jnp.maximum(mx_ref[...], _max8(s))

    first_near = QB * blk - 1
    n_far = jnp.maximum(first_near, 0)
    newest = QB * blk + QB - 1
    _for_tiles(n_far, lambda j0, n: put(j0, n, None))

    @pl.when(first_near >= 0)
    def _():
        put(first_near, QB + 1, 0)

    @pl.when(first_near < 0)
    def _():
        put(0, QB, 1)

    m = jnp.max(mx_ref[...], axis=0, keepdims=True)
    l_ref[...] = jnp.zeros_like(l_ref)
    acc_ref[...] = jnp.zeros_like(acc_ref)

    def pv(j0, n):
        p = jnp.exp2(s_ref[pl.ds(j0, n)].reshape(n * KT, HP * W2) - m)
        l_ref[...] += _sum8(p)
        pb = p.astype(BF16)
        for h in range(HP):
            acc = acc_ref[:, h * W2:(h + 1) * W2]
            for t in range(n):
                acc = acc + _dot(vt_ref[j0 + t, h * H:(h + 1) * H, :],
                                 pb[t * KT:(t + 1) * KT, h * W2:(h + 1) * W2])
            acc_ref[:, h * W2:(h + 1) * W2] = acc

    _for_tiles(newest + 1, pv)

    ot = acc_ref[...] * (1.0 / jnp.sum(l_ref[...], axis=0, keepdims=True))
    for h in range(HP):
        od = ot[:, h * W2:h * W2 + QT] - lam_ref[...] * ot[:, h * W2 + QT:(h + 1) * W2]
        inv = lax.rsqrt(jnp.mean(od * od, axis=0, keepdims=True) + EPS)
        o_ref[:, h * H:(h + 1) * H] = (((od * inv) * sub_ref[...]) * out_scale).T.astype(o_ref.dtype)


def _diff(cqn, ckn, cvt, near_t, lam, subln, out_scale):
    B, Lp, W = cqn.shape
    KT = SEQ_TILE
    H = HEAD_DIM
    nt = Lp // KT
    QB = near_t.shape[1] - 1
    assert nt % QB == 0
    QT = QB * KT
    HP = DIFF_HEADS_PER_STEP
    lanes = HP * 2 * QT
    return pl.pallas_call(
        functools.partial(_diff_body, out_scale=out_scale),
        grid=(B, N_HEADS // HP, nt // QB),
        in_specs=[
            pl.BlockSpec((None, QT, HP * H), lambda b, g, i: (b, i, g)),
            pl.BlockSpec((None, Lp, HP * H), lambda b, g, i: (b, 0, g)),
            pl.BlockSpec((None, nt, HP * H, KT), lambda b, g, i: (b, 0, g, 0)),
            pl.BlockSpec((HP, QB + 1, KT, QT), lambda b, g, i: (g, 0, 0, 0)),
            pl.BlockSpec((1, QT), lambda b, g, i: (0, 0)),
            pl.BlockSpec((H, QT), lambda b, g, i: (0, 0)),
        ],
        out_specs=pl.BlockSpec((None, QT, HP * H), lambda b, g, i: (b, i, g)),
        out_shape=jax.ShapeDtypeStruct((B, Lp, W), BF16),
        scratch_shapes=[pltpu.VMEM((HP, 2 * QT, H), BF16),
                        pltpu.VMEM((nt, KT, lanes), F32),
                        pltpu.VMEM((SUBLANES, lanes), F32),
                        pltpu.VMEM((SUBLANES, lanes), F32),
                        pltpu.VMEM((H, lanes), F32)],
        compiler_params=_cparams("parallel", "parallel", "arbitrary"),
        name="diff",
    )(cqn, ckn, cvt, near_t, lam, subln)


def _block_first_tile(blk, n_blocks, n_tiles, tiles_per_block):
    return jnp.maximum(n_tiles - tiles_per_block * (n_blocks - blk), 0)


def _dsa_body(qi_ref, kk_ref, wq_ref, q_ref, k_ref, vt_ref, near_ref, o_ref,
              qi2_ref, w_ref, key_ref, kq_ref, plane_ref, alive_ref, q4_ref, thr_ref, s_ref, mx_ref,
              l_ref, acc_ref, *, top_k, n_tiles):
    blk = pl.program_id(1)
    KT = SEQ_TILE
    H = HEAD_DIM
    QT = q_ref.shape[0]
    QB = QT // KT
    first = _block_first_tile(blk, pl.num_programs(1), n_tiles, QB)
    newest = first + QB - 1
    lane = lax.broadcasted_iota(I32, (1, H), 1)

    for p in range(IDX_HEADS // 2):
        x = qi_ref[:, p * H:(p + 1) * H]
        zero = jnp.zeros_like(x)
        qi2_ref[p, 0:QT, :] = jnp.where(lane < HALF, x, zero)
        qi2_ref[p, QT:, :] = jnp.where(lane < HALF, zero, x)
    w_ref[...] = wq_ref[...].T
    for h in range(N_HEADS):
        q4_ref[h * QT:(h + 1) * QT, :] = q_ref[:, h * H:(h + 1) * H]
    kq_ref[...] = lax.broadcasted_iota(I32, (KT, QT), 0) - lax.broadcasted_iota(I32, (KT, QT), 1)

    def index(j0, n):
        k0 = pl.multiple_of(j0 * KT, KT)
        kt = kk_ref[pl.ds(k0, n * KT), :]
        acc = jnp.zeros((n * KT, QT), F32)
        for p in range(IDX_HEADS // 2):
            s = jnp.maximum(_dot_nt(kt, qi2_ref[p]), 0.0)
            acc = acc + s[:, 0:QT] * w_ref[2 * p:2 * p + 1, :] + s[:, QT:] * w_ref[2 * p + 1:2 * p + 2, :]
        bits = pltpu.bitcast(acc, I32)
        key = bits ^ ((bits >> 31) & jnp.int32(0x7FFFFFFF))
        tiles = []
        for t in range(n):
            kt_ = key[t * KT:(t + 1) * KT, :]
            if t >= n - QB:
                kt_ = jnp.where(kq_ref[...] <= (first - (j0 + t)) * KT, kt_, jnp.int32(KEY_NEG_INF))
            key_ref[j0 + t] = kt_
            tiles.append(kt_)
        if n == 1:
            tiles.append(None)
        for u in range(len(tiles) // 2):
            rows = []
            for tile in tiles[2 * u:2 * u + 2]:
                for r in range(KT // SUBLANES):
                    if tile is None:
                        rows.append(jnp.full((SUBLANES, QT), INT_MIN, I32))
                    else:
                        rows.append(tile[r * SUBLANES:(r + 1) * SUBLANES, :])
            planes = _bit_transpose32(rows)
            planes[0] = ~planes[0]
            pair = j0 // 2 + u
            for b in range(32):
                plane_ref[pair, b] = planes[b]

    @pl.when(blk == 0)
    def _():
        plane_ref[...] = jnp.zeros_like(plane_ref)

    _for_tiles(newest + 1, index, chunk=INDEX_CHUNK)

    n_pairs = plane_ref.shape[0]
    for tp in range(n_pairs):
        alive_ref[tp] = jnp.full((SUBLANES, QT), -1, I32)

    def bit_step(t, carry):
        above, thr = carry
        ones = [alive_ref[tp] & plane_ref[tp, t] for tp in range(n_pairs)]
        cnt = lax.population_count(ones[0])
        for tp in range(1, n_pairs):
            cnt = cnt + lax.population_count(ones[tp])
        hit = above + jnp.sum(cnt, axis=0, keepdims=True)
        take = hit >= top_k
        for tp in range(n_pairs):
            alive_ref[tp] = jnp.where(take, ones[tp], alive_ref[tp] ^ ones[tp])
        thr = thr | jnp.where(take, jnp.left_shift(jnp.int32(1), 31 - t), 0)
        return jnp.where(take, above, hit), thr

    zero8 = jnp.zeros((SUBLANES, QT), I32)
    _, thr_u = lax.fori_loop(0, 32, bit_step, (zero8, zero8))
    thr_ref[...] = thr_u ^ jnp.int32(INT_MIN)

    mx_ref[...] = jnp.full_like(mx_ref, MASKED)

    def put(j0, n, near_idx):
        k0 = pl.multiple_of(j0 * KT, KT)
        s = _dot_nt(k_ref[pl.ds(k0, n * KT), :], q4_ref[...])
        thr1 = thr_ref[0:1, :]
        mx = mx_ref[...]
        for t in range(n):
            sel = key_ref[j0 + t] >= thr1
            parts = []
            for h in range(N_HEADS):
                sh = s[t * KT:(t + 1) * KT, h * QT:(h + 1) * QT]
                if near_idx is not None:
                    sh = sh + near_ref[h, near_idx + t]
                parts.append(jnp.where(sel, sh, MASKED))
            row = jnp.concatenate(parts, axis=1)
            s_ref[j0 + t] = row
            mx = jnp.maximum(mx, _max8(row))
        mx_ref[...] = mx

    first_near = first - 1
    n_far = jnp.maximum(first_near, 0)
    _for_tiles(n_far, lambda j0, n: put(j0, n, None))

    @pl.when(first_near >= 0)
    def _():
        put(first_near, QB + 1, 0)

    @pl.when(first_near < 0)
    def _():
        put(0, QB, 1)

    m = jnp.max(mx_ref[...], axis=0, keepdims=True)
    l_ref[...] = jnp.zeros_like(l_ref)
    acc_ref[...] = jnp.zeros_like(acc_ref)

    def pv(j0, n):
        p = jnp.exp2(s_ref[pl.ds(j0, n)].reshape(n * KT, N_HEADS * QT) - m)
        l_ref[...] += _sum8(p)
        pb = p.astype(BF16)
        acc = acc_ref[...]
        for t in range(n):
            acc = acc + _dot(vt_ref[j0 + t], pb[t * KT:(t + 1) * KT, :])
        acc_ref[...] = acc

    _for_tiles(newest + 1, pv)

    ot = acc_ref[...] * (1.0 / jnp.sum(l_ref[...], axis=0, keepdims=True))
    for h in range(N_HEADS):
        o_ref[:, h * H:(h + 1) * H] = ot[:, h * QT:(h + 1) * QT].T.astype(o_ref.dtype)


def _dsa(qib, kk, wq, dqn, dkn, dvt, near_t, top_k):
    B, Lp, H = dkn.shape
    W = dqn.shape[1]
    KT = SEQ_TILE
    NI = IDX_HEADS * IDX_DIM
    nt = Lp // KT
    QB = near_t.shape[1] - 1
    QT = QB * KT
    n_blocks = pl.cdiv(nt, QB)

    def rows(width):
        return pl.BlockSpec(
            (pl.Element(QT), pl.Element(width)),
            lambda b, i: (pl.multiple_of(b * Lp + _block_first_tile(i, n_blocks, nt, QB) * KT, KT), 0))

    seq = lambda width: pl.BlockSpec((None, Lp, width), lambda b, i: (b, 0, 0))
    n_pairs = (nt + 1) // 2
    return pl.pallas_call(
        functools.partial(_dsa_body, top_k=top_k, n_tiles=nt),
        grid=(B, n_blocks),
        in_specs=[rows(NI), seq(H), rows(H), rows(W), seq(H),
                  pl.BlockSpec((None, nt, H, KT), lambda b, i: (b, 0, 0, 0)),
                  pl.BlockSpec((N_HEADS, QB + 1, KT, QT), lambda b, i: (0, 0, 0, 0))],
        out_specs=rows(W),
        out_shape=jax.ShapeDtypeStruct((B * Lp, W), BF16),
        scratch_shapes=[
            pltpu.VMEM((IDX_HEADS // 2, 2 * QT, H), BF16),
            pltpu.VMEM((H, QT), F32),
            pltpu.VMEM((nt, KT, QT), I32),
            pltpu.VMEM((KT, QT), I32),
            pltpu.VMEM((n_pairs, 32, SUBLANES, QT), I32),
            pltpu.VMEM((n_pairs, SUBLANES, QT), I32),
            pltpu.VMEM((N_HEADS * QT, H), BF16),
            pltpu.VMEM((SUBLANES, QT), I32),
            pltpu.VMEM((nt, KT, N_HEADS * QT), F32),
            pltpu.VMEM((SUBLANES, N_HEADS * QT), F32),
            pltpu.VMEM((SUBLANES, N_HEADS * QT), F32),
            pltpu.VMEM((H, N_HEADS * QT), F32),
        ],
        compiler_params=_cparams("parallel", "arbitrary"),
        name="dsa",
    )(qib, kk, wq, dqn, dkn, dvt, near_t)


def _merge_body(h_ref, gain_ref, b0_ref, b1_ref, b2_ref, b3_ref, g0_ref, g1_ref, g2_ref, g3_ref,
                wb_ref, wo_ref, o_ref, xn_ref):
    @pl.when(pl.program_id(1) == 0)
    def _():
        x = h_ref[...]
        xn_ref[...] = ((x * _rms_inv(x)) * gain_ref[...]).astype(BF16)
        o_ref[...] = x

    xn = xn_ref[...]
    merged = None
    for m, (b_ref, g_ref) in enumerate(((b0_ref, g0_ref), (b1_ref, g1_ref), (b2_ref, g2_ref), (b3_ref, g3_ref))):
        term = jax.nn.sigmoid(_dot(xn, g_ref[...])) * _dot(b_ref[...], wb_ref[m])
        merged = term if merged is None else merged + term
    o_ref[...] += _dot(merged.astype(BF16), wo_ref[...])


def _merge(h, gain, branches, w_gate, w_branch, w_out, layer, tm, tn):
    T, D = h.shape
    W = MIX_WIDTH
    nn = D // tn
    row = lambda width: pl.BlockSpec((tm, width), lambda i, n: (i, 0))
    gate = lambda m: pl.BlockSpec((D, tn), functools.partial(lambda i, n, m: (0, m * nn + n), m=m))
    return pl.pallas_call(
        _merge_body,
        grid=(T // tm, nn),
        in_specs=[row(D), pl.BlockSpec((1, D), lambda i, n: (0, 0)), row(W), row(W), row(W), row(W),
                  gate(0), gate(1), gate(2), gate(3),
                  pl.BlockSpec((None, N_BRANCH, W, tn), lambda i, n: (layer, 0, 0, n)),
                  pl.BlockSpec((None, tn, D), lambda i, n: (layer, n, 0))],
        out_specs=row(D),
        out_shape=jax.ShapeDtypeStruct((T, D), F32),
        scratch_shapes=[pltpu.VMEM((tm, D), BF16)],
        compiler_params=_cparams("parallel", "arbitrary"),
        name="merge",
    )(h, gain.reshape(1, D), *branches, w_gate, w_gate, w_gate, w_gate, w_branch, w_out)


def _rel_bucket(n):
    max_exact = N_BUCKETS // 2
    nf = jnp.maximum(n, 1).astype(F32)
    large = max_exact + (jnp.log(nf / max_exact) / math.log(MAX_DISTANCE / max_exact)
                         * (N_BUCKETS - max_exact)).astype(I32)
    large = jnp.minimum(large, N_BUCKETS - 1)
    return jnp.where(n < max_exact, n, large)


def _bias_tiles(table, query_tiles):
    KT = SEQ_TILE
    kl = jnp.arange(KT, dtype=I32)[:, None]
    ql = jnp.arange(query_tiles * KT, dtype=I32)[None, :]
    tbl = table.astype(F32).T

    def lookup(bucket):
        out = jnp.zeros((tbl.shape[0],) + bucket.shape, F32)
        for b in range(N_BUCKETS):
            out = jnp.where(bucket[None] == b, tbl[:, b][:, None, None], out)
        return out

    far = tbl[:, N_BUCKETS - 1][:, None, None]
    tiles = []
    for t in range(query_tiles + 1):
        d = ql - kl + (1 - t) * KT
        bias = (lookup(_rel_bucket(jnp.maximum(d, 0))) - far) * LOG2E
        tiles.append(jnp.where(d >= 0, bias, MASKED))
    return jnp.stack(tiles, axis=1)


def _pack_w_in(w):
    offs = np.cumsum([0, 512, 512, 512, 512, 512, 512, 512, 512, 512, 512, 512, 128, 128,
                      IDX_HEADS * IDX_DIM, IDX_DIM, IDX_HEADS]).tolist()
    seg = lambda a, b: w[:, offs[a]:offs[b]]
    D = w.shape[0]
    used = COL_KIW + IDX_DIM + IDX_HEADS
    main = jnp.concatenate([
        seg(0, 4),
        seg(13, 14),
        seg(4, 7),
        seg(7, 10),
        seg(10, 11),
        seg(11, 12),
        seg(12, 13),
        seg(14, 16),
        jnp.zeros((D, P_COLS - used), w.dtype),
    ], axis=1).astype(BF16)
    gates = w[:, offs[16]:].astype(BF16)
    return main, gates


def _largest_tile(n, cap):
    t = cap
    while n % t:
        t //= 2
    return t


def _largest_multiple(n, unit, cap):
    return max(t for t in range(unit, cap + 1, unit) if n % t == 0)


def kernel(x, meta_tokens, rel_bias, ffn1_norm, ffn1_w_gu, ffn1_w_down, mix_norm, w_in, hgrn_lb, hgrn_gnorm, conv_w, diff_q_norm, diff_k_norm, diff_lambda, diff_subln, dsa_q_norm, dsa_k_norm, w_branch, w_out, ffn2_norm, ffn2_w_gu, ffn2_w_down):
    B, S, D = x.shape
    depth = w_in.shape[0]
    L = S + N_META
    Lp = -(-L // SEQ_TILE) * SEQ_TILE
    T = B * Lp
    top_k = min(TOPK_MAX, S // 4)
    tm = _largest_tile(T, 512)
    tm_wide = _largest_tile(T, 1024)
    tm_ffn = _largest_multiple(T, SEQ_TILE, FFN_ROWS_MAX)
    d_ff = ffn1_w_down.shape[1]
    tf = _largest_tile(d_ff, 512)

    meta = jnp.broadcast_to(meta_tokens.astype(x.dtype)[None], (B, N_META, D))
    h = jnp.concatenate([meta, x, jnp.zeros((B, Lp - L, D), x.dtype)], axis=1).reshape(T, D)

    lbs = jnp.cumsum(jax.nn.softmax(hgrn_lb.astype(F32), axis=0), axis=0)
    lbs = lbs - lbs[0:1]
    near_c = _bias_tiles(rel_bias[:, :N_HEADS], DIFF_QUERY_TILES)
    near_d = _bias_tiles(rel_bias[:, N_HEADS:], min(DSA_QUERY_TILES, Lp // SEQ_TILE))
    diff_qt = DIFF_QUERY_TILES * SEQ_TILE

    w1_gu, w1_down, w2_gu, w2_down, wb, wo = (
        _to_bf16(w) for w in (ffn1_w_gu, ffn1_w_down, ffn2_w_gu, ffn2_w_down, w_branch, w_out))

    for l in range(depth):
        h = _ffn(h, ffn1_norm[l], w1_gu, w1_down, l, tm_ffn, tf)

        w_main, w_gate = _pack_w_in(w_in[l])
        P = _proj(h, mix_norm[l], w_main, tm_wide, PROJ_COLS_TILE)
        P3 = P.reshape(B, Lp, P_COLS)
        br_a = _hgrn(P3, lbs[l], hgrn_gnorm[l])
        br_b = _conv(P3, conv_w[l])
        cqn, ckn, cvt, dqn, dkn, dvt, qib, kk, wq = _prep(
            P, diff_q_norm[l], diff_k_norm[l], dsa_q_norm[l], dsa_k_norm[l], tm)
        r3 = lambda a: a.reshape(B, Lp, a.shape[-1])
        lp = diff_lambda[l].astype(F32)
        lam_init = 0.8 - 0.6 * math.exp(-0.3 * l)
        lam = jnp.exp(jnp.sum(lp[0] * lp[1])) - jnp.exp(jnp.sum(lp[2] * lp[3])) + lam_init
        tiles = lambda a: a.reshape(B, Lp // SEQ_TILE, a.shape[-2], SEQ_TILE)
        br_c = _diff(r3(cqn), r3(ckn), tiles(cvt), near_c,
                     jnp.broadcast_to(lam, (1, diff_qt)).astype(F32),
                     jnp.broadcast_to(diff_subln[l].astype(F32)[:, None], (HEAD_DIM, diff_qt)),
                     1.0 - lam_init)
        br_d = _dsa(qib, r3(kk), wq, dqn, r3(dkn), tiles(dvt), near_d, top_k)
        branches = [a.reshape(T, MIX_WIDTH) for a in (br_a, br_b, br_c, br_d)]
        h = _merge(h, mix_norm[l], branches, w_gate, wb, wo, l, tm, 512)

        last = l == depth - 1
        h = _ffn(h, ffn2_norm[l], w2_gu, w2_down, l,
                 _largest_multiple(S, SUBLANES, FFN_ROWS_MAX) if last else tm_ffn, tf,
                 keep=(B, Lp, N_META, S) if last else None)

    return h.reshape(B, S, D)
```

```python
import functools
import math

import jax
import jax.numpy as jnp
import numpy as np
from jax import lax
from jax.experimental import pallas as pl
from jax.experimental.pallas import tpu as pltpu

F32 = jnp.float32
BF16 = jnp.bfloat16
I32 = jnp.int32

EPS = 1e-6
N_META = 16
HEAD_DIM = 128
HALF = 64
N_HEADS = 4
MIX_WIDTH = N_HEADS * HEAD_DIM
IDX_HEADS = 16
IDX_DIM = 64
TOPK_MAX = 256
N_BRANCH = 4
N_BUCKETS = 32
MAX_DISTANCE = 128
CONV_W = 3
SEQ_TILE = 128
HGRN_CHUNK = 8
HGRN_GROUP = 8
DIFF_QUERY_TILES = 1
DIFF_HEADS_PER_STEP = 4
DSA_QUERY_TILES = 2
CONV_ROWS_MAX = 1408
CAST_BLOCK_BYTES = 8 * 1024 * 1024
FFN_ROWS_MAX = 1024
MASKED = -1e30
LOG2E = math.log2(math.e)
INT_MIN = -(2 ** 31)
KEY_NEG_INF = -2139095041
VMEM_LIMIT_BYTES = 56 * 1024 * 1024

COL_A = 0
COL_QI = 2048
COL_B = 3072
COL_C = 4608
COL_DQ = 6144
COL_DK = 6656
COL_DV = 6784
COL_KIW = 6912
P_COLS = 7168
PROJ_COLS_TILE = 1792


def _cparams(*sem):
    return pltpu.CompilerParams(dimension_semantics=sem, vmem_limit_bytes=VMEM_LIMIT_BYTES)


def _dot(a, b):
    return jnp.dot(a, b, preferred_element_type=F32)


def _dot_nt(a, b):
    return lax.dot_general(a, b, (((1,), (1,)), ((), ())), preferred_element_type=F32)


def _dot_tn(a, b):
    return lax.dot_general(a, b, (((0,), (0,)), ((), ())), preferred_element_type=F32)


def _rms_inv(x):
    return lax.rsqrt(jnp.mean(x * x, axis=-1, keepdims=True) + EPS)


def _cast_body(x_ref, o_ref):
    o_ref[...] = x_ref[...].astype(o_ref.dtype)


def _to_bf16(w, interleave_tile=None):
    cols = w.shape[-1]
    w2 = w.reshape(-1, cols)
    n_rows = w2.shape[0]
    if interleave_tile is not None:
        t = interleave_tile
        half = cols // (2 * t)
        rows = _largest_multiple(n_rows, SUBLANES, max(SUBLANES, CAST_BLOCK_BYTES // (4 * t)))
        out = pl.pallas_call(
            _cast_body,
            grid=(n_rows // rows, 2 * half),
            in_specs=[pl.BlockSpec((rows, t), lambda i, c: (i, c))],
            out_specs=pl.BlockSpec((rows, t), lambda i, c: (i, jnp.where(c < half, 2 * c, 2 * (c - half) + 1))),
            out_shape=jax.ShapeDtypeStruct((n_rows, cols), BF16),
            compiler_params=_cparams("parallel", "parallel"),
            name="cast",
        )(w2)
        return out.reshape(w.shape)
    rows = _largest_multiple(n_rows, SUBLANES, max(SUBLANES, CAST_BLOCK_BYTES // (4 * cols)))
    out = pl.pallas_call(
        _cast_body,
        grid=(n_rows // rows,),
        in_specs=[pl.BlockSpec((rows, cols), lambda i: (i, 0))],
        out_specs=pl.BlockSpec((rows, cols), lambda i: (i, 0)),
        out_shape=jax.ShapeDtypeStruct((n_rows, cols), BF16),
        compiler_params=_cparams("parallel"),
        name="cast",
    )(w2)
    return out.reshape(w.shape)


def _ffn_body(h_ref, g_ref, wgu_ref, wd_ref, o_ref, xn_ref):
    j = pl.program_id(1)
    tf = wd_ref.shape[0]

    @pl.when(j == 0)
    def _():
        x = h_ref[...]
        xn_ref[...] = ((x * _rms_inv(x)) * g_ref[...]).astype(BF16)
        o_ref[...] = x

    xn = xn_ref[...]
    gu = _dot(xn, wgu_ref[...])
    g = gu[:, :tf]
    u = gu[:, tf:]
    a = (g * jax.nn.sigmoid(g) * (0.5 * u)).astype(BF16)
    o_ref[...] += _dot(a, wd_ref[...])


def _ffn(h, gain, w_gu, w_down, layer, tm, tf, keep=None):
    T, D = h.shape
    F = w_down.shape[1]
    nf = F // tf
    if keep is None:
        n_row_tiles = T // tm
        row_in = pl.BlockSpec((tm, D), lambda i, j: (i, 0))
        t_out = T
    else:
        B, Lp, first, n_rows = keep
        per_seq = n_rows // tm
        n_row_tiles = B * per_seq
        row_in = pl.BlockSpec((pl.Element(tm), pl.Element(D)),
                              lambda i, j: (pl.multiple_of(
                                  (i // per_seq) * Lp + first + (i % per_seq) * tm, SUBLANES), 0))
        t_out = B * n_rows
    return pl.pallas_call(
        _ffn_body,
        grid=(n_row_tiles, nf),
        in_specs=[
            row_in,
            pl.BlockSpec((1, D), lambda i, j: (0, 0)),
            pl.BlockSpec((None, D, 2 * tf), lambda i, j: (layer, 0, j)),
            pl.BlockSpec((None, tf, D), lambda i, j: (layer, j, 0)),
        ],
        out_specs=pl.BlockSpec((tm, D), lambda i, j: (i, 0)),
        out_shape=jax.ShapeDtypeStruct((t_out, D), F32),
        scratch_shapes=[pltpu.VMEM((tm, D), BF16)],
        compiler_params=_cparams("parallel", "arbitrary"),
        name="ffn",
    )(h, gain.reshape(1, D), w_gu, w_down)


def _proj_body(h_ref, g_ref, w_ref, p_ref, xn_ref):
    @pl.when(pl.program_id(1) == 0)
    def _():
        x = h_ref[...]
        xn_ref[...] = ((x * _rms_inv(x)) * g_ref[...]).astype(BF16)

    p_ref[...] = _dot(xn_ref[...], w_ref[...])


def _proj(h, gain, w, tm, tn):
    T, D = h.shape
    N = w.shape[1]
    return pl.pallas_call(
        _proj_body,
        grid=(T // tm, N // tn),
        in_specs=[
            pl.BlockSpec((tm, D), lambda i, j: (i, 0)),
            pl.BlockSpec((1, D), lambda i, j: (0, 0)),
            pl.BlockSpec((D, tn), lambda i, j: (0, j)),
        ],
        out_specs=pl.BlockSpec((tm, tn), lambda i, j: (i, j)),
        out_shape=jax.ShapeDtypeStruct((T, N), F32),
        scratch_shapes=[pltpu.VMEM((tm, D), BF16)],
        compiler_params=_cparams("parallel", "arbitrary"),
        name="proj",
    )(h, gain.reshape(1, D), w)


def _hgrn_body(q_ref, f_ref, i_ref, g_ref, loglb_ref, log1mlb_ref, omlb_ref, gn_ref, o_ref,
               st_ref, qs_ref, ks_ref, bs_ref, os_ref):
    G, R, _ = q_ref.shape
    C = HGRN_CHUNK

    @pl.when(pl.program_id(1) == 0)
    def _():
        st_ref[...] = jnp.zeros_like(st_ref)

    row = lax.broadcasted_iota(I32, (R, MIX_WIDTH), 0) % C
    for g in range(G):
        z = f_ref[g]
        log_sig = jnp.minimum(z, 0.0) - jnp.log(1.0 + jnp.exp(-jnp.abs(z)))
        y = log1mlb_ref[...] + log_sig
        a = loglb_ref[...]
        log_f = jnp.maximum(a, y) + jnp.log(1.0 + jnp.exp(-jnp.abs(a - y)))
        b = log_f * LOG2E
        for sh in [1 << e for e in range(C.bit_length() - 1)]:
            b = b + jnp.where(row >= sh, pltpu.roll(b, sh, axis=0), 0.0)
        bs_ref[g] = b
        ks_ref[g] = omlb_ref[...] * jax.nn.sigmoid(-z)
        qr = q_ref[g]
        qs_ref[g] = qr * jax.nn.sigmoid(qr)

    t_iota = lax.broadcasted_iota(I32, (C, 1), 0)

    def chunk(c, carry):
        r0 = pl.multiple_of(c * C, C)
        for g in range(G):
            for h in range(N_HEADS):
                hs = slice(h * HEAD_DIM, (h + 1) * HEAD_DIM)
                qc = qs_ref[g, pl.ds(r0, C), hs]
                kc = ks_ref[g, pl.ds(r0, C), hs]
                bc = bs_ref[g, pl.ds(r0, C), hs]
                vc = i_ref[g, pl.ds(r0, C), hs]
                st = st_ref[g, h]
                o = _dot_nt((qc * jnp.exp2(bc)).astype(BF16), st.astype(BF16))
                for s in range(C):
                    b_s = bc[s:s + 1, :]
                    k_s = kc[s:s + 1, :]
                    v_s = vc[s:s + 1, :]
                    e = jnp.exp2(bc - b_s)
                    col = jnp.sum(qc * k_s * e, axis=-1, keepdims=True)
                    col = jnp.where(t_iota >= s, col, 0.0)
                    o = o + col * v_s
                os_ref[g, pl.ds(r0, C), hs] = o
                b_last = bc[C - 1:C, :]
                kd = kc * jnp.exp2(b_last - bc)
                st_ref[g, h] = st * jnp.exp2(b_last) + _dot_tn(vc.astype(BF16), kd.astype(BF16))
        return carry

    lax.fori_loop(0, R // C, chunk, 0)

    for g in range(G):
        for h in range(N_HEADS):
            hs = slice(h * HEAD_DIM, (h + 1) * HEAD_DIM)
            o = os_ref[g, :, hs]
            on = (o * _rms_inv(o)) * gn_ref[:, hs]
            gr = g_ref[g, :, hs]
            o_ref[g, :, hs] = (on * (gr * jax.nn.sigmoid(gr))).astype(o_ref.dtype)


def _hgrn(P3, lb, gnorm):
    B, Lp, _ = P3.shape
    R = SEQ_TILE
    W = MIX_WIDTH
    G = _largest_tile(B, HGRN_GROUP)
    lb = lb.reshape(1, W).astype(F32)
    vec = pl.BlockSpec((1, W), lambda b, t: (0, 0))
    c0 = COL_A // W
    return pl.pallas_call(
        _hgrn_body,
        grid=(B // G, Lp // R),
        in_specs=[pl.BlockSpec((G, R, W), functools.partial(lambda b, t, c: (b, t, c), c=c0 + k))
                  for k in range(4)] + [vec, vec, vec, vec],
        out_specs=pl.BlockSpec((G, R, W), lambda b, t: (b, t, 0)),
        out_shape=jax.ShapeDtypeStruct((B, Lp, W), BF16),
        scratch_shapes=[pltpu.VMEM((G, N_HEADS, HEAD_DIM, HEAD_DIM), F32)]
        + [pltpu.VMEM((G, R, W), F32)] * 4,
        compiler_params=_cparams("parallel", "arbitrary"),
        name="hgrn",
    )(P3, P3, P3, P3, jnp.log(lb), jnp.log1p(-lb), 1.0 - lb, gnorm.reshape(1, W).astype(F32))


def _conv_body(b_ref, c_ref, u_ref, w_ref, o_ref, carry_ref):
    @pl.when(pl.program_id(1) == 0)
    def _():
        carry_ref[...] = jnp.zeros_like(carry_ref)

    zc = c_ref[...] * u_ref[...]
    R = zc.shape[0]
    row = lax.broadcasted_iota(I32, zc.shape, 0)
    last = carry_ref[7:8, :]
    last2 = carry_ref[6:7, :]
    z1 = jnp.where(row == 0, last, pltpu.roll(zc, 1, axis=0))
    z2 = jnp.where(row == 0, last2, jnp.where(row == 1, last, pltpu.roll(zc, 2, axis=0)))
    y = w_ref[0:1, :] * zc + w_ref[1:2, :] * z1 + w_ref[2:3, :] * z2
    o_ref[...] = (b_ref[...] * y).astype(o_ref.dtype)
    carry_ref[...] = zc[R - 8:, :]


def _conv(P3, conv_w):
    B, Lp, _ = P3.shape
    R = _largest_multiple(Lp, SEQ_TILE, CONV_ROWS_MAX)
    W = MIX_WIDTH
    c0 = COL_B // W
    w8 = jnp.zeros((8, W), F32).at[:CONV_W].set(conv_w.astype(F32))
    return pl.pallas_call(
        _conv_body,
        grid=(B, Lp // R),
        in_specs=[pl.BlockSpec((None, R, W), functools.partial(lambda b, t, c: (b, t, c), c=c0 + k))
                  for k in range(3)] + [pl.BlockSpec((8, W), lambda b, t: (0, 0))],
        out_specs=pl.BlockSpec((None, R, W), lambda b, t: (b, t, 0)),
        out_shape=jax.ShapeDtypeStruct((B, Lp, W), BF16),
        scratch_shapes=[pltpu.VMEM((8, W), F32)],
        compiler_params=_cparams("parallel", "arbitrary"),
        name="conv",
    )(P3, P3, P3, w8)


def _prep_body(cq_ref, ck_ref, cv_ref, dq_ref, dk_ref, dv_ref, kiw_ref, qi_ref,
               cqg_ref, ckg_ref, dqg_ref, dkg_ref,
               cqn_ref, ckn_ref, cvb_ref, dqn_ref, dkn_ref, dvb_ref, qib_ref, kk_ref, wq_ref):
    lane = lax.broadcasted_iota(I32, (1, HEAD_DIM), 1)
    lo = lane < HALF

    def half_norm(x, g, scale):
        sq = x * x
        ms_lo = jnp.sum(jnp.where(lo, sq, 0.0), axis=-1, keepdims=True) * (1.0 / HALF)
        ms_hi = jnp.sum(jnp.where(lo, 0.0, sq), axis=-1, keepdims=True) * (1.0 / HALF)
        inv = jnp.where(lo, lax.rsqrt(ms_lo + EPS), lax.rsqrt(ms_hi + EPS))
        return ((x * inv) * g) * scale

    for h in range(N_HEADS):
        hs = slice(h * HEAD_DIM, (h + 1) * HEAD_DIM)
        cqn_ref[:, hs] = half_norm(cq_ref[:, hs], cqg_ref[...], HALF ** -0.5 * LOG2E).astype(BF16)
        ckn_ref[:, hs] = half_norm(ck_ref[:, hs], ckg_ref[...], 1.0).astype(BF16)
        x = dq_ref[:, hs]
        dqn_ref[:, hs] = (((x * _rms_inv(x)) * dqg_ref[...]) * (HEAD_DIM ** -0.5 * LOG2E)).astype(BF16)
    x = dk_ref[...]
    dkn_ref[...] = ((x * _rms_inv(x)) * dkg_ref[...]).astype(BF16)
    for t in range(cv_ref.shape[0] // SEQ_TILE):
        rs = slice(t * SEQ_TILE, (t + 1) * SEQ_TILE)
        for h in range(N_HEADS):
            hs = slice(h * HEAD_DIM, (h + 1) * HEAD_DIM)
            cvb_ref[t, hs, :] = cv_ref[rs, hs].T.astype(BF16)
        dvb_ref[t] = dv_ref[rs, :].T.astype(BF16)
    qib_ref[...] = (qi_ref[...] * (IDX_DIM ** -0.5)).astype(BF16)
    kiw = kiw_ref[...]
    swapped = pltpu.roll(kiw, HALF, axis=1)
    kk_ref[...] = jnp.where(lo, kiw, swapped).astype(BF16)
    wq_ref[...] = swapped * (IDX_HEADS ** -0.5)


def _prep(P, cq_g, ck_g, dq_g, dk_g, tm):
    T = P.shape[0]
    W = MIX_WIDTH
    H = HEAD_DIM

    def col(width, start):
        return pl.BlockSpec((tm, width), functools.partial(lambda i, c: (i, c), c=start // width))

    vec = pl.BlockSpec((1, H), lambda i: (0, 0))
    out = lambda width: pl.BlockSpec((tm, width), lambda i: (i, 0))
    tiles = lambda width: pl.BlockSpec((tm // SEQ_TILE, width, SEQ_TILE), lambda i: (i, 0, 0))
    return pl.pallas_call(
        _prep_body,
        grid=(T // tm,),
        in_specs=[col(W, COL_C), col(W, COL_C + W), col(W, COL_C + 2 * W), col(W, COL_DQ),
                  col(H, COL_DK), col(H, COL_DV), col(H, COL_KIW), col(IDX_HEADS * IDX_DIM, COL_QI),
                  vec, vec, vec, vec],
        out_specs=[out(W), out(W), tiles(W), out(W), out(H), tiles(H), out(IDX_HEADS * IDX_DIM), out(H), out(H)],
        out_shape=[jax.ShapeDtypeStruct((T, W), BF16)] * 2
        + [jax.ShapeDtypeStruct((T // SEQ_TILE, W, SEQ_TILE), BF16), jax.ShapeDtypeStruct((T, W), BF16),
           jax.ShapeDtypeStruct((T, H), BF16), jax.ShapeDtypeStruct((T // SEQ_TILE, H, SEQ_TILE), BF16)]
        + [jax.ShapeDtypeStruct((T, IDX_HEADS * IDX_DIM), BF16),
           jax.ShapeDtypeStruct((T, H), BF16), jax.ShapeDtypeStruct((T, H), F32)],
        compiler_params=_cparams("parallel"),
        name="prep",
    )(P, P, P, P, P, P, P, P,
      jnp.tile(cq_g.astype(F32), 2).reshape(1, H), jnp.tile(ck_g.astype(F32), 2).reshape(1, H),
      dq_g.astype(F32).reshape(1, H), dk_g.astype(F32).reshape(1, H))


SUBLANES = 8
KEY_CHUNK = 16
INDEX_CHUNK = 8


def _max8(x):
    return jnp.max(x.reshape(x.shape[0] // SUBLANES, SUBLANES, x.shape[1]), axis=0)


def _sum8(x):
    return jnp.sum(x.reshape(x.shape[0] // SUBLANES, SUBLANES, x.shape[1]), axis=0)


def _bit_transpose32(rows):
    a = list(rows)
    j = 16
    m = 0x0000FFFF
    while j:
        k = 0
        while k < 32:
            t = (a[k] ^ lax.shift_right_logical(a[k + j], jnp.int32(j))) & jnp.int32(m)
            a[k] = a[k] ^ t
            a[k + j] = a[k + j] ^ (t << j)
            k = (k + j + 1) & ~j
        j >>= 1
        m ^= (m << j) & 0xFFFFFFFF
    return a


def _for_tiles(n_tiles, body, chunk=KEY_CHUNK):
    n_chunks = n_tiles // chunk

    def chunked(c, carry):
        body(c * chunk, chunk)
        return carry

    lax.fori_loop(0, n_chunks, chunked, 0)
    base = n_chunks * chunk
    size = chunk // 2
    while size >= 1:
        take = (n_tiles - base) // size

        def part(_, carry, base=base, size=size):
            body(base, size)
            return carry

        lax.fori_loop(0, take, part, 0)
        base = base + take * size
        size //= 2


def _diff_body(q_ref, k_ref, vt_ref, near_ref, lam_ref, sub_ref, o_ref,
               q2_ref, s_ref, mx_ref, l_ref, acc_ref, *, out_scale):
    blk = pl.program_id(2)
    KT = SEQ_TILE
    H = HEAD_DIM
    QT = q_ref.shape[0]
    QB = QT // KT
    HP = q_ref.shape[1] // H
    W2 = 2 * QT
    lane = lax.broadcasted_iota(I32, (1, H), 1)
    for h in range(HP):
        q = q_ref[:, h * H:(h + 1) * H]
        zero = jnp.zeros_like(q)
        q2_ref[h, 0:QT, :] = jnp.where(lane < HALF, q, zero)
        q2_ref[h, QT:, :] = jnp.where(lane < HALF, zero, q)
    mx_ref[...] = jnp.full_like(mx_ref, MASKED)

    def put(j0, n, near_idx):
        k0 = pl.multiple_of(j0 * KT, KT)
        parts = []
        for h in range(HP):
            s = _dot_nt(k_ref[pl.ds(k0, n * KT), h * H:(h + 1) * H], q2_ref[h])
            if near_idx is None:
                parts.append(s)
            else:
                b = jnp.concatenate([near_ref[h, near_idx + t] for t in range(n)], axis=0)
                parts.append(s + jnp.concatenate([b, b], axis=1))
        s = jnp.concatenate(parts, axis=1)
        for t in range(n):
            s_ref[j0 + t] = s[t * KT:(t + 1) * KT, :]
        mx_ref[...] = jnp.maximum(mx_ref[...], _max8(s))

    first_near = QB * blk - 1
    n_far = jnp.maximum(first_near, 0)
    newest = QB * blk + QB - 1
    _for_tiles(n_far, lambda j0, n: put(j0, n, None))

    @pl.when(first_near >= 0)
    def _():
        put(first_near, QB + 1, 0)

    @pl.when(first_near < 0)
    def _():
        put(0, QB, 1)

    m = jnp.max(mx_ref[...], axis=0, keepdims=True)
    l_ref[...] = jnp.zeros_like(l_ref)
    acc_ref[...] = jnp.zeros_like(acc_ref)

    def pv(j0, n):
        p = jnp.exp2(s_ref[pl.ds(j0, n)].reshape(n * KT, HP * W2) - m)
        l_ref[...] += _sum8(p)
        pb = p.astype(BF16)
        for h in range(HP):
            acc = acc_ref[:, h * W2:(h + 1) * W2]
            for t in range(n):
                acc = acc + _dot(vt_ref[j0 + t, h * H:(h + 1) * H, :],
                                 pb[t * KT:(t + 1) * KT, h * W2:(h + 1) * W2])
            acc_ref[:, h * W2:(h + 1) * W2] = acc

    _for_tiles(newest + 1, pv)

    ot = acc_ref[...] * (1.0 / jnp.sum(l_ref[...], axis=0, keepdims=True))
    for h in range(HP):
        od = ot[:, h * W2:h * W2 + QT] - lam_ref[...] * ot[:, h * W2 + QT:(h + 1) * W2]
        inv = lax.rsqrt(jnp.mean(od * od, axis=0, keepdims=True) + EPS)
        o_ref[:, h * H:(h + 1) * H] = (((od * inv) * sub_ref[...]) * out_scale).T.astype(o_ref.dtype)


def _diff(cqn, ckn, cvt, near_t, lam, subln, out_scale):
    B, Lp, W = cqn.shape
    KT = SEQ_TILE
    H = HEAD_DIM
    nt = Lp // KT
    QB = near_t.shape[1] - 1
    assert nt % QB == 0
    QT = QB * KT
    HP = DIFF_HEADS_PER_STEP
    lanes = HP * 2 * QT
    return pl.pallas_call(
        functools.partial(_diff_body, out_scale=out_scale),
        grid=(B, N_HEADS // HP, nt // QB),
        in_specs=[
            pl.BlockSpec((None, QT, HP * H), lambda b, g, i: (b, i, g)),
            pl.BlockSpec((None, Lp, HP * H), lambda b, g, i: (b, 0, g)),
            pl.BlockSpec((None, nt, HP * H, KT), lambda b, g, i: (b, 0, g, 0)),
            pl.BlockSpec((HP, QB + 1, KT, QT), lambda b, g, i: (g, 0, 0, 0)),
            pl.BlockSpec((1, QT), lambda b, g, i: (0, 0)),
            pl.BlockSpec((H, QT), lambda b, g, i: (0, 0)),
        ],
        out_specs=pl.BlockSpec((None, QT, HP * H), lambda b, g, i: (b, i, g)),
        out_shape=jax.ShapeDtypeStruct((B, Lp, W), BF16),
        scratch_shapes=[pltpu.VMEM((HP, 2 * QT, H), BF16),
                        pltpu.VMEM((nt, KT, lanes), F32),
                        pltpu.VMEM((SUBLANES, lanes), F32),
                        pltpu.VMEM((SUBLANES, lanes), F32),
                        pltpu.VMEM((H, lanes), F32)],
        compiler_params=_cparams("parallel", "parallel", "arbitrary"),
        name="diff",
    )(cqn, ckn, cvt, near_t, lam, subln)


def _block_first_tile(blk, n_blocks, n_tiles, tiles_per_block):
    return jnp.maximum(n_tiles - tiles_per_block * (n_blocks - blk), 0)


def _dsa_body(qi_ref, kk_ref, wq_ref, q_ref, k_ref, vt_ref, near_ref, o_ref,
              qi2_ref, w_ref, key_ref, kq_ref, plane_ref, alive_ref, q4_ref, thr_ref, s_ref, mx_ref,
              l_ref, acc_ref, *, top_k, n_tiles):
    blk = pl.program_id(1)
    KT = SEQ_TILE
    H = HEAD_DIM
    QT = q_ref.shape[0]
    QB = QT // KT
    first = _block_first_tile(blk, pl.num_programs(1), n_tiles, QB)
    newest = first + QB - 1
    lane = lax.broadcasted_iota(I32, (1, H), 1)

    for p in range(IDX_HEADS // 2):
        x = qi_ref[:, p * H:(p + 1) * H]
        zero = jnp.zeros_like(x)
        qi2_ref[p, 0:QT, :] = jnp.where(lane < HALF, x, zero)
        qi2_ref[p, QT:, :] = jnp.where(lane < HALF, zero, x)
    w_ref[...] = wq_ref[...].T
    for h in range(N_HEADS):
        q4_ref[h * QT:(h + 1) * QT, :] = q_ref[:, h * H:(h + 1) * H]
    kq_ref[...] = lax.broadcasted_iota(I32, (KT, QT), 0) - lax.broadcasted_iota(I32, (KT, QT), 1)

    def index(j0, n):
        k0 = pl.multiple_of(j0 * KT, KT)
        kt = kk_ref[pl.ds(k0, n * KT), :]
        acc = jnp.zeros((n * KT, QT), F32)
        for p in range(IDX_HEADS // 2):
            s = jnp.maximum(_dot_nt(kt, qi2_ref[p]), 0.0)
            acc = acc + s[:, 0:QT] * w_ref[2 * p:2 * p + 1, :] + s[:, QT:] * w_ref[2 * p + 1:2 * p + 2, :]
        bits = pltpu.bitcast(acc, I32)
        key = bits ^ ((bits >> 31) & jnp.int32(0x7FFFFFFF))
        tiles = []
        for t in range(n):
            kt_ = key[t * KT:(t + 1) * KT, :]
            if t >= n - QB:
                kt_ = jnp.where(kq_ref[...] <= (first - (j0 + t)) * KT, kt_, jnp.int32(KEY_NEG_INF))
            key_ref[j0 + t] = kt_
            tiles.append(kt_)
        if n == 1:
            tiles.append(None)
        for u in range(len(tiles) // 2):
            rows = []
            for tile in tiles[2 * u:2 * u + 2]:
                for r in range(KT // SUBLANES):
                    if tile is None:
                        rows.append(jnp.full((SUBLANES, QT), INT_MIN, I32))
                    else:
                        rows.append(tile[r * SUBLANES:(r + 1) * SUBLANES, :])
            planes = _bit_transpose32(rows)
            planes[0] = ~planes[0]
            pair = j0 // 2 + u
            for b in range(32):
                plane_ref[pair, b] = planes[b]

    @pl.when(blk == 0)
    def _():
        plane_ref[...] = jnp.zeros_like(plane_ref)

    _for_tiles(newest + 1, index, chunk=INDEX_CHUNK)

    n_pairs = plane_ref.shape[0]
    for tp in range(n_pairs):
        alive_ref[tp] = jnp.full((SUBLANES, QT), -1, I32)

    def bit_step(t, carry):
        above, thr = carry
        ones = [alive_ref[tp] & plane_ref[tp, t] for tp in range(n_pairs)]
        cnt = lax.population_count(ones[0])
        for tp in range(1, n_pairs):
            cnt = cnt + lax.population_count(ones[tp])
        hit = above + jnp.sum(cnt, axis=0, keepdims=True)
        take = hit >= top_k
        for tp in range(n_pairs):
            alive_ref[tp] = jnp.where(take, ones[tp], alive_ref[tp] ^ ones[tp])
        thr = thr | jnp.where(take, jnp.left_shift(jnp.int32(1), 31 - t), 0)
        return jnp.where(take, above, hit), thr

    zero8 = jnp.zeros((SUBLANES, QT), I32)
    _, thr_u = lax.fori_loop(0, 32, bit_step, (zero8, zero8))
    thr_ref[...] = thr_u ^ jnp.int32(INT_MIN)

    mx_ref[...] = jnp.full_like(mx_ref, MASKED)

    def put(j0, n, near_idx):
        k0 = pl.multiple_of(j0 * KT, KT)
        s = _dot_nt(k_ref[pl.ds(k0, n * KT), :], q4_ref[...])
        thr1 = thr_ref[0:1, :]
        mx = mx_ref[...]
        for t in range(n):
            sel = key_ref[j0 + t] >= thr1
            parts = []
            for h in range(N_HEADS):
                sh = s[t * KT:(t + 1) * KT, h * QT:(h + 1) * QT]
                if near_idx is not None:
                    sh = sh + near_ref[h, near_idx + t]
                parts.append(jnp.where(sel, sh, MASKED))
            row = jnp.concatenate(parts, axis=1)
            s_ref[j0 + t] = row
            mx = jnp.maximum(mx, _max8(row))
        mx_ref[...] = mx

    first_near = first - 1
    n_far = jnp.maximum(first_near, 0)
    _for_tiles(n_far, lambda j0, n: put(j0, n, None))

    @pl.when(first_near >= 0)
    def _():
        put(first_near, QB + 1, 0)

    @pl.when(first_near < 0)
    def _():
        put(0, QB, 1)

    m = jnp.max(mx_ref[...], axis=0, keepdims=True)
    l_ref[...] = jnp.zeros_like(l_ref)
    acc_ref[...] = jnp.zeros_like(acc_ref)

    def pv(j0, n):
        p = jnp.exp2(s_ref[pl.ds(j0, n)].reshape(n * KT, N_HEADS * QT) - m)
        l_ref[...] += _sum8(p)
        pb = p.astype(BF16)
        acc = acc_ref[...]
        for t in range(n):
            acc = acc + _dot(vt_ref[j0 + t], pb[t * KT:(t + 1) * KT, :])
        acc_ref[...] = acc

    _for_tiles(newest + 1, pv)

    ot = acc_ref[...] * (1.0 / jnp.sum(l_ref[...], axis=0, keepdims=True))
    for h in range(N_HEADS):
        o_ref[:, h * H:(h + 1) * H] = ot[:, h * QT:(h + 1) * QT].T.astype(o_ref.dtype)


def _dsa(qib, kk, wq, dqn, dkn, dvt, near_t, top_k):
    B, Lp, H = dkn.shape
    W = dqn.shape[1]
    KT = SEQ_TILE
    NI = IDX_HEADS * IDX_DIM
    nt = Lp // KT
    QB = near_t.shape[1] - 1
    QT = QB * KT
    n_blocks = pl.cdiv(nt, QB)

    def rows(width):
        return pl.BlockSpec(
            (pl.Element(QT), pl.Element(width)),
            lambda b, i: (pl.multiple_of(b * Lp + _block_first_tile(i, n_blocks, nt, QB) * KT, KT), 0))

    seq = lambda width: pl.BlockSpec((None, Lp, width), lambda b, i: (b, 0, 0))
    n_pairs = (nt + 1) // 2
    return pl.pallas_call(
        functools.partial(_dsa_body, top_k=top_k, n_tiles=nt),
        grid=(B, n_blocks),
        in_specs=[rows(NI), seq(H), rows(H), rows(W), seq(H),
                  pl.BlockSpec((None, nt, H, KT), lambda b, i: (b, 0, 0, 0)),
                  pl.BlockSpec((N_HEADS, QB + 1, KT, QT), lambda b, i: (0, 0, 0, 0))],
        out_specs=rows(W),
        out_shape=jax.ShapeDtypeStruct((B * Lp, W), BF16),
        scratch_shapes=[
            pltpu.VMEM((IDX_HEADS // 2, 2 * QT, H), BF16),
            pltpu.VMEM((H, QT), F32),
            pltpu.VMEM((nt, KT, QT), I32),
            pltpu.VMEM((KT, QT), I32),
            pltpu.VMEM((n_pairs, 32, SUBLANES, QT), I32),
            pltpu.VMEM((n_pairs, SUBLANES, QT), I32),
            pltpu.VMEM((N_HEADS * QT, H), BF16),
            pltpu.VMEM((SUBLANES, QT), I32),
            pltpu.VMEM((nt, KT, N_HEADS * QT), F32),
            pltpu.VMEM((SUBLANES, N_HEADS * QT), F32),
            pltpu.VMEM((SUBLANES, N_HEADS * QT), F32),
            pltpu.VMEM((H, N_HEADS * QT), F32),
        ],
        compiler_params=_cparams("parallel", "arbitrary"),
        name="dsa",
    )(qib, kk, wq, dqn, dkn, dvt, near_t)


def _merge_body(h_ref, gain_ref, b0_ref, b1_ref, b2_ref, b3_ref, g0_ref, g1_ref, g2_ref, g3_ref,
                wb_ref, wo_ref, o_ref, xn_ref):
    @pl.when(pl.program_id(1) == 0)
    def _():
        x = h_ref[...]
        xn_ref[...] = ((x * _rms_inv(x)) * gain_ref[...]).astype(BF16)
        o_ref[...] = x

    xn = xn_ref[...]
    merged = None
    for m, (b_ref, g_ref) in enumerate(((b0_ref, g0_ref), (b1_ref, g1_ref), (b2_ref, g2_ref), (b3_ref, g3_ref))):
        term = jax.nn.sigmoid(_dot(xn, g_ref[...])) * _dot(b_ref[...], wb_ref[m])
        merged = term if merged is None else merged + term
    o_ref[...] += _dot(merged.astype(BF16), wo_ref[...])


def _merge(h, gain, branches, w_gate, w_branch, w_out, layer, tm, tn):
    T, D = h.shape
    W = MIX_WIDTH
    nn = D // tn
    row = lambda width: pl.BlockSpec((tm, width), lambda i, n: (i, 0))
    gate = lambda m: pl.BlockSpec((D, tn), functools.partial(lambda i, n, m: (0, m * nn + n), m=m))
    return pl.pallas_call(
        _merge_body,
        grid=(T // tm, nn),
        in_specs=[row(D), pl.BlockSpec((1, D), lambda i, n: (0, 0)), row(W), row(W), row(W), row(W),
                  gate(0), gate(1), gate(2), gate(3),
                  pl.BlockSpec((None, N_BRANCH, W, tn), lambda i, n: (layer, 0, 0, n)),
                  pl.BlockSpec((None, tn, D), lambda i, n: (layer, n, 0))],
        out_specs=row(D),
        out_shape=jax.ShapeDtypeStruct((T, D), F32),
        scratch_shapes=[pltpu.VMEM((tm, D), BF16)],
        compiler_params=_cparams("parallel", "arbitrary"),
        name="merge",
    )(h, gain.reshape(1, D), *branches, w_gate, w_gate, w_gate, w_gate, w_branch, w_out)


def _rel_bucket(n):
    max_exact = N_BUCKETS // 2
    nf = jnp.maximum(n, 1).astype(F32)
    large = max_exact + (jnp.log(nf / max_exact) / math.log(MAX_DISTANCE / max_exact)
                         * (N_BUCKETS - max_exact)).astype(I32)
    large = jnp.minimum(large, N_BUCKETS - 1)
    return jnp.where(n < max_exact, n, large)


def _bias_tiles(table, query_tiles):
    KT = SEQ_TILE
    kl = jnp.arange(KT, dtype=I32)[:, None]
    ql = jnp.arange(query_tiles * KT, dtype=I32)[None, :]
    tbl = table.astype(F32).T

    def lookup(bucket):
        out = jnp.zeros((tbl.shape[0],) + bucket.shape, F32)
        for b in range(N_BUCKETS):
            out = jnp.where(bucket[None] == b, tbl[:, b][:, None, None], out)
        return out

    far = tbl[:, N_BUCKETS - 1][:, None, None]
    tiles = []
    for t in range(query_tiles + 1):
        d = ql - kl + (1 - t) * KT
        bias = (lookup(_rel_bucket(jnp.maximum(d, 0))) - far) * LOG2E
        tiles.append(jnp.where(d >= 0, bias, MASKED))
    return jnp.stack(tiles, axis=1)


def _pack_w_in(w):
    offs = np.cumsum([0, 512, 512, 512, 512, 512, 512, 512, 512, 512, 512, 512, 128, 128,
                      IDX_HEADS * IDX_DIM, IDX_DIM, IDX_HEADS]).tolist()
    seg = lambda a, b: w[:, offs[a]:offs[b]]
    D = w.shape[0]
    used = COL_KIW + IDX_DIM + IDX_HEADS
    main = jnp.concatenate([
        seg(0, 4),
        seg(13, 14),
        seg(4, 7),
        seg(7, 10),
        seg(10, 11),
        seg(11, 12),
        seg(12, 13),
        seg(14, 16),
        jnp.zeros((D, P_COLS - used), w.dtype),
    ], axis=1).astype(BF16)
    gates = w[:, offs[16]:].astype(BF16)
    return main, gates


def _largest_tile(n, cap):
    t = cap
    while n % t:
        t //= 2
    return t


def _largest_multiple(n, unit, cap):
    return max(t for t in range(unit, cap + 1, unit) if n % t == 0)


def kernel(x, meta_tokens, rel_bias, ffn1_norm, ffn1_w_gu, ffn1_w_down, mix_norm, w_in, hgrn_lb, hgrn_gnorm, conv_w, diff_q_norm, diff_k_norm, diff_lambda, diff_subln, dsa_q_norm, dsa_k_norm, w_branch, w_out, ffn2_norm, ffn2_w_gu, ffn2_w_down):
    B, S, D = x.shape
    depth = w_in.shape[0]
    L = S + N_META
    Lp = -(-L // SEQ_TILE) * SEQ_TILE
    T = B * Lp
    top_k = min(TOPK_MAX, S // 4)
    tm = _largest_tile(T, 512)
    tm_wide = _largest_tile(T, 1024)
    tm_ffn = _largest_multiple(T, SEQ_TILE, FFN_ROWS_MAX)
    d_ff = ffn1_w_down.shape[1]
    tf = _largest_tile(d_ff, 512)

    meta = jnp.broadcast_to(meta_tokens.astype(x.dtype)[None], (B, N_META, D))
    h = jnp.concatenate([meta, x, jnp.zeros((B, Lp - L, D), x.dtype)], axis=1).reshape(T, D)

    lbs = jnp.cumsum(jax.nn.softmax(hgrn_lb.astype(F32), axis=0), axis=0)
    lbs = lbs - lbs[0:1]
    near_c = _bias_tiles(rel_bias[:, :N_HEADS], DIFF_QUERY_TILES)
    near_d = _bias_tiles(rel_bias[:, N_HEADS:], min(DSA_QUERY_TILES, Lp // SEQ_TILE))
    diff_qt = DIFF_QUERY_TILES * SEQ_TILE

    w1_gu, w2_gu = (_to_bf16(w, interleave_tile=tf) for w in (ffn1_w_gu, ffn2_w_gu))
    w1_down, w2_down, wb, wo = (_to_bf16(w) for w in (ffn1_w_down, ffn2_w_down, w_branch, w_out))

    for l in range(depth):
        h = _ffn(h, ffn1_norm[l], w1_gu, w1_down, l, tm_ffn, tf)

        w_main, w_gate = _pack_w_in(w_in[l])
        P = _proj(h, mix_norm[l], w_main, tm_wide, PROJ_COLS_TILE)
        P3 = P.reshape(B, Lp, P_COLS)
        br_a = _hgrn(P3, lbs[l], hgrn_gnorm[l])
        br_b = _conv(P3, conv_w[l])
        cqn, ckn, cvt, dqn, dkn, dvt, qib, kk, wq = _prep(
            P, diff_q_norm[l], diff_k_norm[l], dsa_q_norm[l], dsa_k_norm[l], tm)
        r3 = lambda a: a.reshape(B, Lp, a.shape[-1])
        lp = diff_lambda[l].astype(F32)
        lam_init = 0.8 - 0.6 * math.exp(-0.3 * l)
        lam = jnp.exp(jnp.sum(lp[0] * lp[1])) - jnp.exp(jnp.sum(lp[2] * lp[3])) + lam_init
        tiles = lambda a: a.reshape(B, Lp // SEQ_TILE, a.shape[-2], SEQ_TILE)
        br_c = _diff(r3(cqn), r3(ckn), tiles(cvt), near_c,
                     jnp.broadcast_to(lam, (1, diff_qt)).astype(F32),
                     jnp.broadcast_to(diff_subln[l].astype(F32)[:, None], (HEAD_DIM, diff_qt)),
                     1.0 - lam_init)
        br_d = _dsa(qib, r3(kk), wq, dqn, r3(dkn), tiles(dvt), near_d, top_k)
        branches = [a.reshape(T, MIX_WIDTH) for a in (br_a, br_b, br_c, br_d)]
        h = _merge(h, mix_norm[l], branches, w_gate, wb, wo, l, tm, 512)

        last = l == depth - 1
        h = _ffn(h, ffn2_norm[l], w2_gu, w2_down, l,
                 _largest_multiple(S, SUBLANES, FFN_ROWS_MAX) if last else tm_ffn, tf,
                 keep=(B, Lp, N_META, S) if last else None)

    return h.reshape(B, S, D)
```

```python
import functools
import math

import jax
import jax.numpy as jnp
import numpy as np
from jax import lax
from jax.experimental import pallas as pl
from jax.experimental.pallas import tpu as pltpu

F32 = jnp.float32
BF16 = jnp.bfloat16
I32 = jnp.int32

EPS = 1e-6
N_META = 16
HEAD_DIM = 128
HALF = 64
N_HEADS = 4
MIX_WIDTH = N_HEADS * HEAD_DIM
IDX_HEADS = 16
IDX_DIM = 64
TOPK_MAX = 256
N_BRANCH = 4
N_BUCKETS = 32
MAX_DISTANCE = 128
CONV_W = 3
SEQ_TILE = 128
HGRN_CHUNK = 8
HGRN_GROUP = 8
DIFF_QUERY_TILES = 1
DIFF_HEADS_PER_STEP = 4
DSA_QUERY_TILES = 2
CONV_ROWS_MAX = 1408
CAST_BLOCK_BYTES = 8 * 1024 * 1024
FFN_ROWS_MAX = 1024
MASKED = -1e30
LOG2E = math.log2(math.e)
INT_MIN = -(2 ** 31)
KEY_NEG_INF = -2139095041
VMEM_LIMIT_BYTES = 56 * 1024 * 1024

COL_A = 0
COL_QI = 2048
COL_B = 3072
COL_C = 4608
COL_DQ = 6144
COL_DK = 6656
COL_DV = 6784
COL_KIW = 6912
P_COLS = 7168
PROJ_COLS_TILE = 1792


def _cparams(*sem):
    return pltpu.CompilerParams(dimension_semantics=sem, vmem_limit_bytes=VMEM_LIMIT_BYTES)


def _dot(a, b):
    return jnp.dot(a, b, preferred_element_type=F32)


def _dot_nt(a, b):
    return lax.dot_general(a, b, (((1,), (1,)), ((), ())), preferred_element_type=F32)


def _dot_tn(a, b):
    return lax.dot_general(a, b, (((0,), (0,)), ((), ())), preferred_element_type=F32)


def _rms_inv(x):
    return lax.rsqrt(jnp.mean(x * x, axis=-1, keepdims=True) + EPS)


def _cast_body(x_ref, o_ref):
    o_ref[...] = x_ref[...].astype(o_ref.dtype)


def _to_bf16(w, interleave_tile=None):
    cols = w.shape[-1]
    w2 = w.reshape(-1, cols)
    n_rows = w2.shape[0]
    if interleave_tile is not None:
        t = interleave_tile
        half = cols // (2 * t)
        rows = _largest_multiple(n_rows, SUBLANES, max(SUBLANES, CAST_BLOCK_BYTES // (4 * t)))
        out = pl.pallas_call(
            _cast_body,
            grid=(n_rows // rows, 2 * half),
            in_specs=[pl.BlockSpec((rows, t), lambda i, c: (i, c))],
            out_specs=pl.BlockSpec((rows, t), lambda i, c: (i, jnp.where(c < half, 2 * c, 2 * (c - half) + 1))),
            out_shape=jax.ShapeDtypeStruct((n_rows, cols), BF16),
            compiler_params=_cparams("parallel", "parallel"),
            name="cast",
        )(w2)
        return out.reshape(w.shape)
    rows = _largest_multiple(n_rows, SUBLANES, max(SUBLANES, CAST_BLOCK_BYTES // (4 * cols)))
    out = pl.pallas_call(
        _cast_body,
        grid=(n_rows // rows,),
        in_specs=[pl.BlockSpec((rows, cols), lambda i: (i, 0))],
        out_specs=pl.BlockSpec((rows, cols), lambda i: (i, 0)),
        out_shape=jax.ShapeDtypeStruct((n_rows, cols), BF16),
        compiler_params=_cparams("parallel"),
        name="cast",
    )(w2)
    return out.reshape(w.shape)


def _ffn_body(h_ref, g_ref, wgu_ref, wd_ref, o_ref, xn_ref):
    j = pl.program_id(1)
    tf = wd_ref.shape[0]

    @pl.when(j == 0)
    def _():
        x = h_ref[...]
        xn_ref[...] = ((x * _rms_inv(x)) * g_ref[...]).astype(BF16)
        o_ref[...] = x

    xn = xn_ref[...]
    gu = _dot(xn, wgu_ref[...])
    g = gu[:, :tf]
    u = gu[:, tf:]
    a = (g * jax.nn.sigmoid(g) * (0.5 * u)).astype(BF16)
    o_ref[...] += _dot(a, wd_ref[...])


def _ffn(h, gain, w_gu, w_down, layer, tm, tf, keep=None):
    T, D = h.shape
    F = w_down.shape[1]
    nf = F // tf
    if keep is None:
        n_row_tiles = T // tm
        row_in = pl.BlockSpec((tm, D), lambda i, j: (i, 0))
        t_out = T
    else:
        B, Lp, first, n_rows = keep
        per_seq = n_rows // tm
        n_row_tiles = B * per_seq
        row_in = pl.BlockSpec((pl.Element(tm), pl.Element(D)),
                              lambda i, j: (pl.multiple_of(
                                  (i // per_seq) * Lp + first + (i % per_seq) * tm, SUBLANES), 0))
        t_out = B * n_rows
    return pl.pallas_call(
        _ffn_body,
        grid=(n_row_tiles, nf),
        in_specs=[
            row_in,
            pl.BlockSpec((1, D), lambda i, j: (0, 0)),
            pl.BlockSpec((None, D, 2 * tf), lambda i, j: (layer, 0, j)),
            pl.BlockSpec((None, tf, D), lambda i, j: (layer, j, 0)),
        ],
        out_specs=pl.BlockSpec((tm, D), lambda i, j: (i, 0)),
        out_shape=jax.ShapeDtypeStruct((t_out, D), F32),
        scratch_shapes=[pltpu.VMEM((tm, D), BF16)],
        compiler_params=_cparams("parallel", "arbitrary"),
        name="ffn",
    )(h, gain.reshape(1, D), w_gu, w_down)


def _proj_body(h_ref, g_ref, w_ref, p_ref, xn_ref):
    @pl.when(pl.program_id(1) == 0)
    def _():
        x = h_ref[...]
        xn_ref[...] = ((x * _rms_inv(x)) * g_ref[...]).astype(BF16)

    p_ref[...] = _dot(xn_ref[...], w_ref[...])


def _proj(h, gain, w, tm, tn):
    T, D = h.shape
    N = w.shape[1]
    return pl.pallas_call(
        _proj_body,
        grid=(T // tm, N // tn),
        in_specs=[
            pl.BlockSpec((tm, D), lambda i, j: (i, 0)),
            pl.BlockSpec((1, D), lambda i, j: (0, 0)),
            pl.BlockSpec((D, tn), lambda i, j: (0, j)),
        ],
        out_specs=pl.BlockSpec((tm, tn), lambda i, j: (i, j)),
        out_shape=jax.ShapeDtypeStruct((T, N), F32),
        scratch_shapes=[pltpu.VMEM((tm, D), BF16)],
        compiler_params=_cparams("parallel", "arbitrary"),
        name="proj",
    )(h, gain.reshape(1, D), w)


def _hgrn_body(q_ref, f_ref, i_ref, g_ref, loglb_ref, log1mlb_ref, omlb_ref, gn_ref, o_ref,
               st_ref, qs_ref, ks_ref, bs_ref, os_ref):
    G, R, _ = q_ref.shape
    C = HGRN_CHUNK

    @pl.when(pl.program_id(1) == 0)
    def _():
        st_ref[...] = jnp.zeros_like(st_ref)

    row = lax.broadcasted_iota(I32, (R, MIX_WIDTH), 0) % C
    for g in range(G):
        z = f_ref[g]
        log_sig = jnp.minimum(z, 0.0) - jnp.log(1.0 + jnp.exp(-jnp.abs(z)))
        y = log1mlb_ref[...] + log_sig
        a = loglb_ref[...]
        log_f = jnp.maximum(a, y) + jnp.log(1.0 + jnp.exp(-jnp.abs(a - y)))
        b = log_f * LOG2E
        for sh in [1 << e for e in range(C.bit_length() - 1)]:
            b = b + jnp.where(row >= sh, pltpu.roll(b, sh, axis=0), 0.0)
        bs_ref[g] = b
        ks_ref[g] = omlb_ref[...] * jax.nn.sigmoid(-z)
        qr = q_ref[g]
        qs_ref[g] = qr * jax.nn.sigmoid(qr)

    t_iota = lax.broadcasted_iota(I32, (C, 1), 0)

    def chunk(c, carry):
        r0 = pl.multiple_of(c * C, C)
        for g in range(G):
            for h in range(N_HEADS):
                hs = slice(h * HEAD_DIM, (h + 1) * HEAD_DIM)
                qc = qs_ref[g, pl.ds(r0, C), hs]
                kc = ks_ref[g, pl.ds(r0, C), hs]
                bc = bs_ref[g, pl.ds(r0, C), hs]
                vc = i_ref[g, pl.ds(r0, C), hs]
                st = st_ref[g, h]
                o = _dot_nt((qc * jnp.exp2(bc)).astype(BF16), st.astype(BF16))
                for s in range(C):
                    b_s = bc[s:s + 1, :]
                    k_s = kc[s:s + 1, :]
                    v_s = vc[s:s + 1, :]
                    e = jnp.exp2(bc - b_s)
                    col = jnp.sum(qc * k_s * e, axis=-1, keepdims=True)
                    col = jnp.where(t_iota >= s, col, 0.0)
                    o = o + col * v_s
                os_ref[g, pl.ds(r0, C), hs] = o
                b_last = bc[C - 1:C, :]
                kd = kc * jnp.exp2(b_last - bc)
                st_ref[g, h] = st * jnp.exp2(b_last) + _dot_tn(vc.astype(BF16), kd.astype(BF16))
        return carry

    lax.fori_loop(0, R // C, chunk, 0)

    for g in range(G):
        for h in range(N_HEADS):
            hs = slice(h * HEAD_DIM, (h + 1) * HEAD_DIM)
            o = os_ref[g, :, hs]
            on = (o * _rms_inv(o)) * gn_ref[:, hs]
            gr = g_ref[g, :, hs]
            o_ref[g, :, hs] = (on * (gr * jax.nn.sigmoid(gr))).astype(o_ref.dtype)


def _hgrn(P3, lb, gnorm):
    B, Lp, _ = P3.shape
    R = SEQ_TILE
    W = MIX_WIDTH
    G = _largest_tile(B, HGRN_GROUP)
    lb = lb.reshape(1, W).astype(F32)
    vec = pl.BlockSpec((1, W), lambda b, t: (0, 0))
    c0 = COL_A // W
    return pl.pallas_call(
        _hgrn_body,
        grid=(B // G, Lp // R),
        in_specs=[pl.BlockSpec((G, R, W), functools.partial(lambda b, t, c: (b, t, c), c=c0 + k))
                  for k in range(4)] + [vec, vec, vec, vec],
        out_specs=pl.BlockSpec((G, R, W), lambda b, t: (b, t, 0)),
        out_shape=jax.ShapeDtypeStruct((B, Lp, W), BF16),
        scratch_shapes=[pltpu.VMEM((G, N_HEADS, HEAD_DIM, HEAD_DIM), F32)]
        + [pltpu.VMEM((G, R, W), F32)] * 4,
        compiler_params=_cparams("parallel", "arbitrary"),
        name="hgrn",
    )(P3, P3, P3, P3, jnp.log(lb), jnp.log1p(-lb), 1.0 - lb, gnorm.reshape(1, W).astype(F32))


def _conv_body(b_ref, c_ref, u_ref, w_ref, o_ref, carry_ref):
    @pl.when(pl.program_id(1) == 0)
    def _():
        carry_ref[...] = jnp.zeros_like(carry_ref)

    zc = c_ref[...] * u_ref[...]
    R = zc.shape[0]
    row = lax.broadcasted_iota(I32, zc.shape, 0)
    last = carry_ref[7:8, :]
    last2 = carry_ref[6:7, :]
    z1 = jnp.where(row == 0, last, pltpu.roll(zc, 1, axis=0))
    z2 = jnp.where(row == 0, last2, jnp.where(row == 1, last, pltpu.roll(zc, 2, axis=0)))
    y = w_ref[0:1, :] * zc + w_ref[1:2, :] * z1 + w_ref[2:3, :] * z2
    o_ref[...] = (b_ref[...] * y).astype(o_ref.dtype)
    carry_ref[...] = zc[R - 8:, :]


def _conv(P3, conv_w):
    B, Lp, _ = P3.shape
    R = _largest_multiple(Lp, SEQ_TILE, CONV_ROWS_MAX)
    W = MIX_WIDTH
    c0 = COL_B // W
    w8 = jnp.zeros((8, W), F32).at[:CONV_W].set(conv_w.astype(F32))
    return pl.pallas_call(
        _conv_body,
        grid=(B, Lp // R),
        in_specs=[pl.BlockSpec((None, R, W), functools.partial(lambda b, t, c: (b, t, c), c=c0 + k))
                  for k in range(3)] + [pl.BlockSpec((8, W), lambda b, t: (0, 0))],
        out_specs=pl.BlockSpec((None, R, W), lambda b, t: (b, t, 0)),
        out_shape=jax.ShapeDtypeStruct((B, Lp, W), BF16),
        scratch_shapes=[pltpu.VMEM((8, W), F32)],
        compiler_params=_cparams("parallel", "arbitrary"),
        name="conv",
    )(P3, P3, P3, w8)


def _prep_body(cq_ref, ck_ref, cv_ref, dq_ref, dk_ref, dv_ref, kiw_ref, qi_ref,
               cqg_ref, ckg_ref, dqg_ref, dkg_ref,
               cqn_ref, ckn_ref, cvb_ref, dqn_ref, dkn_ref, dvb_ref, qib_ref, kk_ref, wq_ref):
    lane = lax.broadcasted_iota(I32, (1, HEAD_DIM), 1)
    lo = lane < HALF

    def half_norm(x, g, scale):
        sq = x * x
        ms_lo = jnp.sum(jnp.where(lo, sq, 0.0), axis=-1, keepdims=True) * (1.0 / HALF)
        ms_hi = jnp.sum(jnp.where(lo, 0.0, sq), axis=-1, keepdims=True) * (1.0 / HALF)
        inv = jnp.where(lo, lax.rsqrt(ms_lo + EPS), lax.rsqrt(ms_hi + EPS))
        return ((x * inv) * g) * scale

    for h in range(N_HEADS):
        hs = slice(h * HEAD_DIM, (h + 1) * HEAD_DIM)
        cqn_ref[:, hs] = half_norm(cq_ref[:, hs], cqg_ref[...], HALF ** -0.5 * LOG2E).astype(BF16)
        ckn_ref[:, hs] = half_norm(ck_ref[:, hs], ckg_ref[...], 1.0).astype(BF16)
        x = dq_ref[:, hs]
        dqn_ref[:, hs] = (((x * _rms_inv(x)) * dqg_ref[...]) * (HEAD_DIM ** -0.5 * LOG2E)).astype(BF16)
    x = dk_ref[...]
    dkn_ref[...] = ((x * _rms_inv(x)) * dkg_ref[...]).astype(BF16)
    for t in range(cv_ref.shape[0] // SEQ_TILE):
        rs = slice(t * SEQ_TILE, (t + 1) * SEQ_TILE)
        for h in range(N_HEADS):
            hs = slice(h * HEAD_DIM, (h + 1) * HEAD_DIM)
            cvb_ref[t, hs, :] = cv_ref[rs, hs].T.astype(BF16)
        dvb_ref[t] = dv_ref[rs, :].T.astype(BF16)
    qib_ref[...] = (qi_ref[...] * (IDX_DIM ** -0.5)).astype(BF16)
    kiw = kiw_ref[...]
    swapped = pltpu.roll(kiw, HALF, axis=1)
    kk_ref[...] = jnp.where(lo, kiw, swapped).astype(BF16)
    wq_ref[...] = swapped * (IDX_HEADS ** -0.5)


def _prep(P, cq_g, ck_g, dq_g, dk_g, tm):
    T = P.shape[0]
    W = MIX_WIDTH
    H = HEAD_DIM

    def col(width, start):
        return pl.BlockSpec((tm, width), functools.partial(lambda i, c: (i, c), c=start // width))

    vec = pl.BlockSpec((1, H), lambda i: (0, 0))
    out = lambda width: pl.BlockSpec((tm, width), lambda i: (i, 0))
    tiles = lambda width: pl.BlockSpec((tm // SEQ_TILE, width, SEQ_TILE), lambda i: (i, 0, 0))
    return pl.pallas_call(
        _prep_body,
        grid=(T // tm,),
        in_specs=[col(W, COL_C), col(W, COL_C + W), col(W, COL_C + 2 * W), col(W, COL_DQ),
                  col(H, COL_DK), col(H, COL_DV), col(H, COL_KIW), col(IDX_HEADS * IDX_DIM, COL_QI),
                  vec, vec, vec, vec],
        out_specs=[out(W), out(W), tiles(W), out(W), out(H), tiles(H), out(IDX_HEADS * IDX_DIM), out(H), out(H)],
        out_shape=[jax.ShapeDtypeStruct((T, W), BF16)] * 2
        + [jax.ShapeDtypeStruct((T // SEQ_TILE, W, SEQ_TILE), BF16), jax.ShapeDtypeStruct((T, W), BF16),
           jax.ShapeDtypeStruct((T, H), BF16), jax.ShapeDtypeStruct((T // SEQ_TILE, H, SEQ_TILE), BF16)]
        + [jax.ShapeDtypeStruct((T, IDX_HEADS * IDX_DIM), BF16),
           jax.ShapeDtypeStruct((T, H), BF16), jax.ShapeDtypeStruct((T, H), F32)],
        compiler_params=_cparams("parallel"),
        name="prep",
    )(P, P, P, P, P, P, P, P,
      jnp.tile(cq_g.astype(F32), 2).reshape(1, H), jnp.tile(ck_g.astype(F32), 2).reshape(1, H),
      dq_g.astype(F32).reshape(1, H), dk_g.astype(F32).reshape(1, H))


SUBLANES = 8
KEY_CHUNK = 16
INDEX_CHUNK = 8


def _max8(x):
    return jnp.max(x.reshape(x.shape[0] // SUBLANES, SUBLANES, x.shape[1]), axis=0)


def _sum8(x):
    return jnp.sum(x.reshape(x.shape[0] // SUBLANES, SUBLANES, x.shape[1]), axis=0)


def _bit_transpose32(rows):
    a = list(rows)
    j = 16
    m = 0x0000FFFF
    while j:
        k = 0
        while k < 32:
            t = (a[k] ^ lax.shift_right_logical(a[k + j], jnp.int32(j))) & jnp.int32(m)
            a[k] = a[k] ^ t
            a[k + j] = a[k + j] ^ (t << j)
            k = (k + j + 1) & ~j
        j >>= 1
        m ^= (m << j) & 0xFFFFFFFF
    return a


def _for_tiles(n_tiles, body, chunk=KEY_CHUNK):
    n_chunks = n_tiles // chunk

    def chunked(c, carry):
        body(c * chunk, chunk)
        return carry

    lax.fori_loop(0, n_chunks, chunked, 0)
    base = n_chunks * chunk
    size = chunk // 2
    while size >= 1:
        take = (n_tiles - base) // size

        def part(_, carry, base=base, size=size):
            body(base, size)
            return carry

        lax.fori_loop(0, take, part, 0)
        base = base + take * size
        size //= 2


def _diff_body(q_ref, k_ref, vt_ref, near_ref, lam_ref, sub_ref, o_ref,
               q2_ref, s_ref, mx_ref, l_ref, acc_ref, *, out_scale):
    blk = pl.program_id(2)
    KT = SEQ_TILE
    H = HEAD_DIM
    QT = q_ref.shape[0]
    QB = QT // KT
    HP = q_ref.shape[1] // H
    W2 = 2 * QT
    lane = lax.broadcasted_iota(I32, (1, H), 1)
    for h in range(HP):
        q = q_ref[:, h * H:(h + 1) * H]
        zero = jnp.zeros_like(q)
        q2_ref[h, 0:QT, :] = jnp.where(lane < HALF, q, zero)
        q2_ref[h, QT:, :] = jnp.where(lane < HALF, zero, q)
    mx_ref[...] = jnp.full_like(mx_ref, MASKED)

    def put(j0, n, near_idx):
        k0 = pl.multiple_of(j0 * KT, KT)
        parts = []
        for h in range(HP):
            s = _dot_nt(k_ref[pl.ds(k0, n * KT), h * H:(h + 1) * H], q2_ref[h])
            if near_idx is None:
                parts.append(s)
            else:
                b = jnp.concatenate([near_ref[h, near_idx + t] for t in range(n)], axis=0)
                parts.append(s + jnp.concatenate([b, b], axis=1))
        s = jnp.concatenate(parts, axis=1)
        for t in range(n):
            s_ref[j0 + t] = s[t * KT:(t + 1) * KT, :]
        mx_ref[...] = jnp.maximum(mx_ref[...], _max8(s))

    first_near = QB * blk - 1
    n_far = jnp.maximum(first_near, 0)
    newest = QB * blk + QB - 1
    _for_tiles(n_far, lambda j0, n: put(j0, n, None))

    @pl.when(first_near >= 0)
    def _():
        put(first_near, QB + 1, 0)

    @pl.when(first_near < 0)
    def _():
        put(0, QB, 1)

    m = jnp.max(mx_ref[...], axis=0, keepdims=True)
    l_ref[...] = jnp.zeros_like(l_ref)
    acc_ref[...] = jnp.zeros_like(acc_ref)

    def pv(j0, n):
        p = jnp.exp2(s_ref[pl.ds(j0, n)].reshape(n * KT, HP * W2) - m)
        l_ref[...] += _sum8(p)
        pb = p.astype(BF16)
        for h in range(HP):
            acc = acc_ref[:, h * W2:(h + 1) * W2]
            for t in range(n):
                acc = acc + _dot(vt_ref[j0 + t, h * H:(h + 1) * H, :],
                                 pb[t * KT:(t + 1) * KT, h * W2:(h + 1) * W2])
            acc_ref[:, h * W2:(h + 1) * W2] = acc

    _for_tiles(newest + 1, pv)

    ot = acc_ref[...] * (1.0 / jnp.sum(l_ref[...], axis=0, keepdims=True))
    for h in range(HP):
        od = ot[:, h * W2:h * W2 + QT] - lam_ref[...] * ot[:, h * W2 + QT:(h + 1) * W2]
        inv = lax.rsqrt(jnp.mean(od * od, axis=0, keepdims=True) + EPS)
        o_ref[:, h * H:(h + 1) * H] = (((od * inv) * sub_ref[...]) * out_scale).T.astype(o_ref.dtype)


def _diff(cqn, ckn, cvt, near_t, lam, subln, out_scale):
    B, Lp, W = cqn.shape
    KT = SEQ_TILE
    H = HEAD_DIM
    nt = Lp // KT
    QB = near_t.shape[1] - 1
    assert nt % QB == 0
    QT = QB * KT
    HP = DIFF_HEADS_PER_STEP
    lanes = HP * 2 * QT
    return pl.pallas_call(
        functools.partial(_diff_body, out_scale=out_scale),
        grid=(B, N_HEADS // HP, nt // QB),
        in_specs=[
            pl.BlockSpec((None, QT, HP * H), lambda b, g, i: (b, i, g)),
            pl.BlockSpec((None, Lp, HP * H), lambda b, g, i: (b, 0, g)),
            pl.BlockSpec((None, nt, HP * H, KT), lambda b, g, i: (b, 0, g, 0)),
            pl.BlockSpec((HP, QB + 1, KT, QT), lambda b, g, i: (g, 0, 0, 0)),
            pl.BlockSpec((1, QT), lambda b, g, i: (0, 0)),
            pl.BlockSpec((H, QT), lambda b, g, i: (0, 0)),
        ],
        out_specs=pl.BlockSpec((None, QT, HP * H), lambda b, g, i: (b, i, g)),
        out_shape=jax.ShapeDtypeStruct((B, Lp, W), BF16),
        scratch_shapes=[pltpu.VMEM((HP, 2 * QT, H), BF16),
                        pltpu.VMEM((nt, KT, lanes), F32),
                        pltpu.VMEM((SUBLANES, lanes), F32),
                        pltpu.VMEM((SUBLANES, lanes), F32),
                        pltpu.VMEM((H, lanes), F32)],
        compiler_params=_cparams("parallel", "parallel", "arbitrary"),
        name="diff",
    )(cqn, ckn, cvt, near_t, lam, subln)


def _block_first_tile(blk, n_blocks, n_tiles, tiles_per_block):
    return jnp.maximum(n_tiles - tiles_per_block * (n_blocks - blk), 0)


def _dsa_body(qi_ref, kk_ref, wq_ref, q_ref, k_ref, vt_ref, near_ref, o_ref,
              qi2_ref, w_ref, key_ref, kq_ref, plane_ref, alive_ref, q4_ref, thr_ref, s_ref, mx_ref,
              l_ref, acc_ref, *, top_k, n_tiles):
    blk = pl.program_id(1)
    KT = SEQ_TILE
    H = HEAD_DIM
    QT = q_ref.shape[0]
    QB = QT // KT
    first = _block_first_tile(blk, pl.num_programs(1), n_tiles, QB)
    newest = first + QB - 1
    lane = lax.broadcasted_iota(I32, (1, H), 1)

    for p in range(IDX_HEADS // 2):
        x = qi_ref[:, p * H:(p + 1) * H]
        zero = jnp.zeros_like(x)
        qi2_ref[p, 0:QT, :] = jnp.where(lane < HALF, x, zero)
        qi2_ref[p, QT:, :] = jnp.where(lane < HALF, zero, x)
    w_ref[...] = wq_ref[...].T
    for h in range(N_HEADS):
        q4_ref[h * QT:(h + 1) * QT, :] = q_ref[:, h * H:(h + 1) * H]
    kq_ref[...] = lax.broadcasted_iota(I32, (KT, QT), 0) - lax.broadcasted_iota(I32, (KT, QT), 1)

    def index(j0, n):
        k0 = pl.multiple_of(j0 * KT, KT)
        kt = kk_ref[pl.ds(k0, n * KT), :]
        acc = jnp.zeros((n * KT, QT), F32)
        for p in range(IDX_HEADS // 2):
            s = jnp.maximum(_dot_nt(kt, qi2_ref[p]), 0.0)
            acc = acc + s[:, 0:QT] * w_ref[2 * p:2 * p + 1, :] + s[:, QT:] * w_ref[2 * p + 1:2 * p + 2, :]
        bits = pltpu.bitcast(acc, I32)
        key = bits ^ ((bits >> 31) & jnp.int32(0x7FFFFFFF))
        tiles = []
        for t in range(n):
            kt_ = key[t * KT:(t + 1) * KT, :]
            if t >= n - QB:
                kt_ = jnp.where(kq_ref[...] <= (first - (j0 + t)) * KT, kt_, jnp.int32(KEY_NEG_INF))
            key_ref[j0 + t] = kt_
            tiles.append(kt_)
        if n == 1:
            tiles.append(None)
        for u in range(len(tiles) // 2):
            rows = []
            for tile in tiles[2 * u:2 * u + 2]:
                for r in range(KT // SUBLANES):
                    if tile is None:
                        rows.append(jnp.full((SUBLANES, QT), INT_MIN, I32))
                    else:
                        rows.append(tile[r * SUBLANES:(r + 1) * SUBLANES, :])
            planes = _bit_transpose32(rows)
            planes[0] = ~planes[0]
            pair = j0 // 2 + u
            for b in range(32):
                plane_ref[pair, b] = planes[b]

    @pl.when(blk == 0)
    def _():
        plane_ref[...] = jnp.zeros_like(plane_ref)

    _for_tiles(newest + 1, index, chunk=INDEX_CHUNK)

    n_pairs = plane_ref.shape[0]
    for tp in range(n_pairs):
        alive_ref[tp] = jnp.full((SUBLANES, QT), -1, I32)

    def bit_step(t, carry):
        above, thr = carry
        ones = [alive_ref[tp] & plane_ref[tp, t] for tp in range(n_pairs)]
        cnt = lax.population_count(ones[0])
        for tp in range(1, n_pairs):
            cnt = cnt + lax.population_count(ones[tp])
        hit = above + jnp.sum(cnt, axis=0, keepdims=True)
        take = hit >= top_k
        for tp in range(n_pairs):
            alive_ref[tp] = jnp.where(take, ones[tp], alive_ref[tp] ^ ones[tp])
        thr = thr | jnp.where(take, jnp.left_shift(jnp.int32(1), 31 - t), 0)
        return jnp.where(take, above, hit), thr

    zero8 = jnp.zeros((SUBLANES, QT), I32)
    _, thr_u = lax.fori_loop(0, 32, bit_step, (zero8, zero8))
    thr_ref[...] = thr_u ^ jnp.int32(INT_MIN)

    mx_ref[...] = jnp.full_like(mx_ref, MASKED)

    def put(j0, n, near_idx):
        k0 = pl.multiple_of(j0 * KT, KT)
        s = _dot_nt(k_ref[pl.ds(k0, n * KT), :], q4_ref[...])
        thr1 = thr_ref[0:1, :]
        mx = mx_ref[...]
        for t in range(n):
            sel = key_ref[j0 + t] >= thr1
            parts = []
            for h in range(N_HEADS):
                sh = s[t * KT:(t + 1) * KT, h * QT:(h + 1) * QT]
                if near_idx is not None:
                    sh = sh + near_ref[h, near_idx + t]
                parts.append(jnp.where(sel, sh, MASKED))
            row = jnp.concatenate(parts, axis=1)
            s_ref[j0 + t] = row
            mx = jnp.maximum(mx, _max8(row))
        mx_ref[...] = mx

    first_near = first - 1
    n_far = jnp.maximum(first_near, 0)
    _for_tiles(n_far, lambda j0, n: put(j0, n, None))

    @pl.when(first_near >= 0)
    def _():
        put(first_near, QB + 1, 0)

    @pl.when(first_near < 0)
    def _():
        put(0, QB, 1)

    m = jnp.max(mx_ref[...], axis=0, keepdims=True)
    l_ref[...] = jnp.zeros_like(l_ref)
    acc_ref[...] = jnp.zeros_like(acc_ref)

    def pv(j0, n):
        p = jnp.exp2(s_ref[pl.ds(j0, n)].reshape(n * KT, N_HEADS * QT) - m)
        l_ref[...] += _sum8(p)
        pb = p.astype(BF16)
        acc = acc_ref[...]
        for t in range(n):
            acc = acc + _dot(vt_ref[j0 + t], pb[t * KT:(t + 1) * KT, :])
        acc_ref[...] = acc

    _for_tiles(newest + 1, pv)

    ot = acc_ref[...] * (1.0 / jnp.sum(l_ref[...], axis=0, keepdims=True))
    for h in range(N_HEADS):
        o_ref[:, h * H:(h + 1) * H] = ot[:, h * QT:(h + 1) * QT].T.astype(o_ref.dtype)


def _dsa(qib, kk, wq, dqn, dkn, dvt, near_t, top_k):
    B, Lp, H = dkn.shape
    W = dqn.shape[1]
    KT = SEQ_TILE
    NI = IDX_HEADS * IDX_DIM
    nt = Lp // KT
    QB = near_t.shape[1] - 1
    QT = QB * KT
    n_blocks = pl.cdiv(nt, QB)

    def rows(width):
        return pl.BlockSpec(
            (pl.Element(QT), pl.Element(width)),
            lambda b, i: (pl.multiple_of(b * Lp + _block_first_tile(i, n_blocks, nt, QB) * KT, KT), 0))

    seq = lambda width: pl.BlockSpec((None, Lp, width), lambda b, i: (b, 0, 0))
    n_pairs = (nt + 1) // 2
    return pl.pallas_call(
        functools.partial(_dsa_body, top_k=top_k, n_tiles=nt),
        grid=(B, n_blocks),
        in_specs=[rows(NI), seq(H), rows(H), rows(W), seq(H),
                  pl.BlockSpec((None, nt, H, KT), lambda b, i: (b, 0, 0, 0)),
                  pl.BlockSpec((N_HEADS, QB + 1, KT, QT), lambda b, i: (0, 0, 0, 0))],
        out_specs=rows(W),
        out_shape=jax.ShapeDtypeStruct((B * Lp, W), BF16),
        scratch_shapes=[
            pltpu.VMEM((IDX_HEADS // 2, 2 * QT, H), BF16),
            pltpu.VMEM((H, QT), F32),
            pltpu.VMEM((nt, KT, QT), I32),
            pltpu.VMEM((KT, QT), I32),
            pltpu.VMEM((n_pairs, 32, SUBLANES, QT), I32),
            pltpu.VMEM((n_pairs, SUBLANES, QT), I32),
            pltpu.VMEM((N_HEADS * QT, H), BF16),
            pltpu.VMEM((SUBLANES, QT), I32),
            pltpu.VMEM((nt, KT, N_HEADS * QT), F32),
            pltpu.VMEM((SUBLANES, N_HEADS * QT), F32),
            pltpu.VMEM((SUBLANES, N_HEADS * QT), F32),
            pltpu.VMEM((H, N_HEADS * QT), F32),
        ],
        compiler_params=_cparams("parallel", "arbitrary"),
        name="dsa",
    )(qib, kk, wq, dqn, dkn, dvt, near_t)


def _merge_body(h_ref, gain_ref, b0_ref, b1_ref, b2_ref, b3_ref, wg_ref, wb_ref, wo_ref, o_ref, xn_ref):
    tn = wo_ref.shape[0]
    @pl.when(pl.program_id(1) == 0)
    def _():
        x = h_ref[...]
        xn_ref[...] = ((x * _rms_inv(x)) * gain_ref[...]).astype(BF16)
        o_ref[...] = x

    xn = xn_ref[...]
    merged = None
    gates = _dot(xn, wg_ref[...])
    for m, b_ref in enumerate((b0_ref, b1_ref, b2_ref, b3_ref)):
        term = jax.nn.sigmoid(gates[:, m * tn:(m + 1) * tn]) * _dot(b_ref[...], wb_ref[m])
        merged = term if merged is None else merged + term
    o_ref[...] += _dot(merged.astype(BF16), wo_ref[...])


def _merge(h, gain, branches, w_gate, w_branch, w_out, layer, tm, tn):
    T, D = h.shape
    W = MIX_WIDTH
    nn = D // tn
    row = lambda width: pl.BlockSpec((tm, width), lambda i, n: (i, 0))
    gate = lambda m: pl.BlockSpec((D, tn), functools.partial(lambda i, n, m: (0, m * nn + n), m=m))
    return pl.pallas_call(
        _merge_body,
        grid=(T // tm, nn),
        in_specs=[row(D), pl.BlockSpec((1, D), lambda i, n: (0, 0)), row(W), row(W), row(W), row(W),
                  pl.BlockSpec((D, N_BRANCH * tn), lambda i, n: (0, n)),
                  pl.BlockSpec((None, N_BRANCH, W, tn), lambda i, n: (layer, 0, 0, n)),
                  pl.BlockSpec((None, tn, D), lambda i, n: (layer, n, 0))],
        out_specs=row(D),
        out_shape=jax.ShapeDtypeStruct((T, D), F32),
        scratch_shapes=[pltpu.VMEM((tm, D), BF16)],
        compiler_params=_cparams("parallel", "arbitrary"),
        name="merge",
    )(h, gain.reshape(1, D), *branches,
      w_gate.reshape(D, N_BRANCH, nn, tn).transpose(0, 2, 1, 3).reshape(D, N_BRANCH * D), w_branch, w_out)


def _rel_bucket(n):
    max_exact = N_BUCKETS // 2
    nf = jnp.maximum(n, 1).astype(F32)
    large = max_exact + (jnp.log(nf / max_exact) / math.log(MAX_DISTANCE / max_exact)
                         * (N_BUCKETS - max_exact)).astype(I32)
    large = jnp.minimum(large, N_BUCKETS - 1)
    return jnp.where(n < max_exact, n, large)


def _bias_tiles(table, query_tiles):
    KT = SEQ_TILE
    kl = jnp.arange(KT, dtype=I32)[:, None]
    ql = jnp.arange(query_tiles * KT, dtype=I32)[None, :]
    tbl = table.astype(F32).T

    def lookup(bucket):
        out = jnp.zeros((tbl.shape[0],) + bucket.shape, F32)
        for b in range(N_BUCKETS):
            out = jnp.where(bucket[None] == b, tbl[:, b][:, None, None], out)
        return out

    far = tbl[:, N_BUCKETS - 1][:, None, None]
    tiles = []
    for t in range(query_tiles + 1):
        d = ql - kl + (1 - t) * KT
        bias = (lookup(_rel_bucket(jnp.maximum(d, 0))) - far) * LOG2E
        tiles.append(jnp.where(d >= 0, bias, MASKED))
    return jnp.stack(tiles, axis=1)


def _pack_w_in(w):
    offs = np.cumsum([0, 512, 512, 512, 512, 512, 512, 512, 512, 512, 512, 512, 128, 128,
                      IDX_HEADS * IDX_DIM, IDX_DIM, IDX_HEADS]).tolist()
    seg = lambda a, b: w[:, offs[a]:offs[b]]
    D = w.shape[0]
    used = COL_KIW + IDX_DIM + IDX_HEADS
    main = jnp.concatenate([
        seg(0, 4),
        seg(13, 14),
        seg(4, 7),
        seg(7, 10),
        seg(10, 11),
        seg(11, 12),
        seg(12, 13),
        seg(14, 16),
        jnp.zeros((D, P_COLS - used), w.dtype),
    ], axis=1).astype(BF16)
    gates = w[:, offs[16]:].astype(BF16)
    return main, gates


def _largest_tile(n, cap):
    t = cap
    while n % t:
        t //= 2
    return t


def _largest_multiple(n, unit, cap):
    return max(t for t in range(unit, cap + 1, unit) if n % t == 0)


def kernel(x, meta_tokens, rel_bias, ffn1_norm, ffn1_w_gu, ffn1_w_down, mix_norm, w_in, hgrn_lb, hgrn_gnorm, conv_w, diff_q_norm, diff_k_norm, diff_lambda, diff_subln, dsa_q_norm, dsa_k_norm, w_branch, w_out, ffn2_norm, ffn2_w_gu, ffn2_w_down):
    B, S, D = x.shape
    depth = w_in.shape[0]
    L = S + N_META
    Lp = -(-L // SEQ_TILE) * SEQ_TILE
    T = B * Lp
    top_k = min(TOPK_MAX, S // 4)
    tm = _largest_tile(T, 512)
    tm_wide = _largest_tile(T, 1024)
    tm_ffn = _largest_multiple(T, SEQ_TILE, FFN_ROWS_MAX)
    d_ff = ffn1_w_down.shape[1]
    tf = _largest_tile(d_ff, 512)

    meta = jnp.broadcast_to(meta_tokens.astype(x.dtype)[None], (B, N_META, D))
    h = jnp.concatenate([meta, x, jnp.zeros((B, Lp - L, D), x.dtype)], axis=1).reshape(T, D)

    lbs = jnp.cumsum(jax.nn.softmax(hgrn_lb.astype(F32), axis=0), axis=0)
    lbs = lbs - lbs[0:1]
    near_c = _bias_tiles(rel_bias[:, :N_HEADS], DIFF_QUERY_TILES)
    near_d = _bias_tiles(rel_bias[:, N_HEADS:], min(DSA_QUERY_TILES, Lp // SEQ_TILE))
    diff_qt = DIFF_QUERY_TILES * SEQ_TILE

    w1_gu, w2_gu = (_to_bf16(w, interleave_tile=tf) for w in (ffn1_w_gu, ffn2_w_gu))
    w1_down, w2_down, wb, wo = (_to_bf16(w) for w in (ffn1_w_down, ffn2_w_down, w_branch, w_out))

    for l in range(depth):
        h = _ffn(h, ffn1_norm[l], w1_gu, w1_down, l, tm_ffn, tf)

        w_main, w_gate = _pack_w_in(w_in[l])
        P = _proj(h, mix_norm[l], w_main, tm_wide, PROJ_COLS_TILE)
        P3 = P.reshape(B, Lp, P_COLS)
        br_a = _hgrn(P3, lbs[l], hgrn_gnorm[l])
        br_b = _conv(P3, conv_w[l])
        cqn, ckn, cvt, dqn, dkn, dvt, qib, kk, wq = _prep(
            P, diff_q_norm[l], diff_k_norm[l], dsa_q_norm[l], dsa_k_norm[l], tm)
        r3 = lambda a: a.reshape(B, Lp, a.shape[-1])
        lp = diff_lambda[l].astype(F32)
        lam_init = 0.8 - 0.6 * math.exp(-0.3 * l)
        lam = jnp.exp(jnp.sum(lp[0] * lp[1])) - jnp.exp(jnp.sum(lp[2] * lp[3])) + lam_init
        tiles = lambda a: a.reshape(B, Lp // SEQ_TILE, a.shape[-2], SEQ_TILE)
        br_c = _diff(r3(cqn), r3(ckn), tiles(cvt), near_c,
                     jnp.broadcast_to(lam, (1, diff_qt)).astype(F32),
                     jnp.broadcast_to(diff_subln[l].astype(F32)[:, None], (HEAD_DIM, diff_qt)),
                     1.0 - lam_init)
        br_d = _dsa(qib, r3(kk), wq, dqn, r3(dkn), tiles(dvt), near_d, top_k)
        branches = [a.reshape(T, MIX_WIDTH) for a in (br_a, br_b, br_c, br_d)]
        h = _merge(h, mix_norm[l], branches, w_gate, wb, wo, l, tm, 512)

        last = l == depth - 1
        h = _ffn(h, ffn2_norm[l], w2_gu, w2_down, l,
                 _largest_multiple(S, SUBLANES, FFN_ROWS_MAX) if last else tm_ffn, tf,
                 keep=(B, Lp, N_META, S) if last else None)

    return h.reshape(B, S, D)
```
